```python
import math
import jax, jax.numpy as jnp
from jax import lax
import numpy as np

D_MODEL = 1024
BATCH = 2
SEQ = 8192
DEPTH = 4

N_A_LAYERS = DEPTH // 2
N_B_LAYERS = DEPTH - N_A_LAYERS
D_RNN = D_MODEL
LRU_HEADS = 4
LRU_BLOCK = D_RNN // LRU_HEADS
CONV_WIDTH = 4
LRU_C = 8.0
MLA_HEADS = 16
QK_NOPE = 64
QK_ROPE = 32
V_HEAD = 64
Q_LORA = 384
KV_LORA = 256
ROPE_THETA = 10000.0
Q_BLOCK = 128
ATTN_SCALE = 1.0 / math.sqrt(QK_NOPE + QK_ROPE)
PEER_HEADS = 8
N_KEYS = 128
N_EXPERTS = N_KEYS * N_KEYS
PEER_TOPK = 16
PEER_QDIM = 256
PEER_HALF = PEER_QDIM // 2
PEER_CHUNK = 128
RMS_EPS = 1e-6
NEG_INF = -1e30

kernel_name = 'yoco_hawk_mla_peer'


def rmsnorm(x, g):
    xf = x.astype(jnp.float32)
    y = xf * lax.rsqrt(jnp.mean(xf * xf, axis=-1, keepdims=True) + RMS_EPS)
    return (y * g.astype(jnp.float32)).astype(x.dtype)


def rope(x, positions):
    half = QK_ROPE // 2
    freqs = ROPE_THETA ** (-jnp.arange(half, dtype=jnp.float32) / half)
    ang = positions.astype(jnp.float32)[..., None] * freqs
    ang = ang.reshape(ang.shape[:2] + (1,) * (x.ndim - 3) + (half,))
    cos, sin = jnp.cos(ang), jnp.sin(ang)
    xf = x.astype(jnp.float32)
    x1, x2 = xf[..., :half], xf[..., half:]
    out = jnp.concatenate([x1 * cos - x2 * sin, x1 * sin + x2 * cos], axis=-1)
    return out.astype(x.dtype)


def _lru_combine(left, right):
    a1, b1 = left
    a2, b2 = right
    return a1 * a2, a2 * b1 + b2


def rglru_block(h, w_in, conv_w, conv_b, wa, ba, wx, bx, lam, w_out):
    B, S, _ = h.shape
    gate_in, rec_in = jnp.split(h @ w_in, 2, axis=-1)
    gate = jax.nn.gelu(gate_in, approximate=False)
    xp = jnp.pad(rec_in, ((0, 0), (CONV_WIDTH - 1, 0), (0, 0)))
    xc = conv_b
    for k in range(CONV_WIDTH):
        xc = xc + xp[:, k:k + S] * conv_w[k]
    xb = xc.reshape(B, S, LRU_HEADS, LRU_BLOCK)
    r = jax.nn.sigmoid(jnp.einsum('bshi,hij->bshj', xb, wa).reshape(B, S, D_RNN) + ba)
    i = jax.nn.sigmoid(jnp.einsum('bshi,hij->bshj', xb, wx).reshape(B, S, D_RNN) + bx)
    log_a = -LRU_C * r.astype(jnp.float32) * jax.nn.softplus(-lam.astype(jnp.float32))
    a = jnp.exp(log_a)
    b = jnp.sqrt(-jnp.expm1(2.0 * log_a)) * (i * xc).astype(jnp.float32)
    _, hs = lax.associative_scan(_lru_combine, (a, b), axis=1)
    return (gate * hs.astype(h.dtype)) @ w_out


def shared_kv(x, c, kv_ada_w, kv_ada_b, kv_norm_g, w_dkv, w_kr, kv_latent_g, w_uk, w_uv, positions):
    B, S, _ = x.shape
    shift, scale = jnp.split((jax.nn.silu(c) @ kv_ada_w + kv_ada_b)[:, None, :], 2, axis=-1)
    h = rmsnorm(x, kv_norm_g) * (1.0 + scale) + shift
    c_kv = rmsnorm(h @ w_dkv, kv_latent_g)
    k_rope = rope(h @ w_kr, positions)
    k_nope = (c_kv @ w_uk).reshape(B, S, MLA_HEADS, QK_NOPE)
    v = (c_kv @ w_uv).reshape(B, S, MLA_HEADS, V_HEAD)
    return k_nope, k_rope, v


def _to_blocks(t, block):
    B, S = t.shape[:2]
    return jnp.moveaxis(t.reshape((B, S // block, block) + t.shape[2:]), 1, 0)


def _from_blocks(t):
    nb, B, blk = t.shape[:3]
    return jnp.moveaxis(t, 0, 1).reshape((B, nb * blk) + t.shape[3:])


def mla_block(h, w_dq, q_latent_g, w_uq, w_o, k_nope, k_rope, v, positions):
    B, S, _ = h.shape
    q_lat = rmsnorm(h @ w_dq, q_latent_g)
    q = (q_lat @ w_uq).reshape(B, S, MLA_HEADS, QK_NOPE + QK_ROPE)
    q_nope = q[..., :QK_NOPE]
    q_rope = rope(q[..., QK_NOPE:], positions)

    def attend(args):
        qn, qr, qpos = args
        s = (jnp.einsum('bqhd,bkhd->bhqk', qn, k_nope)
             + jnp.einsum('bqhr,bkr->bhqk', qr, k_rope)).astype(jnp.float32) * ATTN_SCALE
        mask = positions[:, None, None, :] <= qpos[:, None, :, None]
        p = jax.nn.softmax(jnp.where(mask, s, NEG_INF), axis=-1).astype(v.dtype)
        return jnp.einsum('bhqk,bkhd->bqhd', p, v)

    out = lax.map(attend, (_to_blocks(q_nope, Q_BLOCK), _to_blocks(q_rope, Q_BLOCK),
                           _to_blocks(positions, Q_BLOCK)))
    out = _from_blocks(out).reshape(B, S, MLA_HEADS * V_HEAD)
    return out @ w_o


def peer(h, w_q, sub_keys, u_tab, v_tab):
    B, S, _ = h.shape
    q = (h @ w_q).reshape(B, S, PEER_HEADS, 2, PEER_HALF)
    s = jnp.einsum('bshpd,hpnd->bshpn', q, sub_keys).astype(jnp.float32)
    v1, i1 = lax.top_k(s[..., 0, :], PEER_TOPK)
    v2, i2 = lax.top_k(s[..., 1, :], PEER_TOPK)
    cand = (v1[..., :, None] + v2[..., None, :]).reshape(B, S, PEER_HEADS, PEER_TOPK * PEER_TOPK)
    cv, ci = lax.top_k(cand, PEER_TOPK)
    e1 = jnp.take_along_axis(i1, ci // PEER_TOPK, axis=-1)
    e2 = jnp.take_along_axis(i2, ci % PEER_TOPK, axis=-1)
    idx = e1 * N_KEYS + e2
    g = jax.nn.softmax(cv, axis=-1).astype(h.dtype)

    def experts(args):
        hc, ic, gc = args
        act = jax.nn.gelu(jnp.einsum('bcd,bchkd->bchk', hc, u_tab[ic]), approximate=False)
        return jnp.einsum('bchk,bchkd->bcd', gc * act, v_tab[ic])

    out = lax.map(experts, (_to_blocks(h, PEER_CHUNK), _to_blocks(idx, PEER_CHUNK),
                            _to_blocks(g, PEER_CHUNK)))
    return _from_blocks(out)


def _normal(key, shape, fan_in):
    return jax.random.normal(key, shape, jnp.float32) * (fan_in ** -0.5)


def _gain(key, shape):
    return 1.0 + 0.02 * jax.random.normal(key, shape, jnp.float32)


def _bias(key, shape):
    return 0.02 * jax.random.normal(key, shape, jnp.float32)


def setup_inputs(seed: int = 0) -> dict:
    key = jax.random.key(seed)
    ks = iter(jax.random.split(key, 40))
    D = D_MODEL
    x = jax.random.normal(next(ks), (BATCH, SEQ, D), jnp.float32)
    c = jax.random.normal(next(ks), (BATCH, D), jnp.float32)
    offset = jax.random.randint(next(ks), (BATCH, 1), 0, 1024, dtype=jnp.int32)
    positions = offset + jnp.arange(SEQ, dtype=jnp.int32)[None, :]
    u = jax.random.uniform(next(ks), (N_A_LAYERS, D_RNN), jnp.float32, 0.9, 0.999)
    a0 = u ** (1.0 / LRU_C)
    lru_lambda = jnp.log(a0) - jnp.log1p(-a0)
    return {
        'x': x, 'c': c, 'positions': positions,
        'ada_w': _normal(next(ks), (DEPTH, D, 6 * D), D),
        'ada_b': _bias(next(ks), (DEPTH, 6 * D)),
        'norm_mix_g': _gain(next(ks), (DEPTH, D)),
        'norm_ffn_g': _gain(next(ks), (DEPTH, D)),
        'lru_w_in': _normal(next(ks), (N_A_LAYERS, D, 2 * D_RNN), D),
        'lru_conv_w': _normal(next(ks), (N_A_LAYERS, CONV_WIDTH, D_RNN), CONV_WIDTH),
        'lru_conv_b': _bias(next(ks), (N_A_LAYERS, D_RNN)),
        'lru_wa': _normal(next(ks), (N_A_LAYERS, LRU_HEADS, LRU_BLOCK, LRU_BLOCK), LRU_BLOCK),
        'lru_ba': _bias(next(ks), (N_A_LAYERS, D_RNN)),
        'lru_wx': _normal(next(ks), (N_A_LAYERS, LRU_HEADS, LRU_BLOCK, LRU_BLOCK), LRU_BLOCK),
        'lru_bx': _bias(next(ks), (N_A_LAYERS, D_RNN)),
        'lru_lambda': lru_lambda,
        'lru_w_out': _normal(next(ks), (N_A_LAYERS, D_RNN, D), D_RNN),
        'kv_ada_w': _normal(next(ks), (D, 2 * D), D),
        'kv_ada_b': _bias(next(ks), (2 * D,)),
        'kv_norm_g': _gain(next(ks), (D,)),
        'mla_w_dkv': _normal(next(ks), (D, KV_LORA), D),
        'mla_w_kr': _normal(next(ks), (D, QK_ROPE), D),
        'mla_kv_latent_g': _gain(next(ks), (KV_LORA,)),
        'mla_w_uk': _normal(next(ks), (KV_LORA, MLA_HEADS * QK_NOPE), KV_LORA),
        'mla_w_uv': _normal(next(ks), (KV_LORA, MLA_HEADS * V_HEAD), KV_LORA),
        'mla_w_dq': _normal(next(ks), (N_B_LAYERS, D, Q_LORA), D),
        'mla_q_latent_g': _gain(next(ks), (N_B_LAYERS, Q_LORA)),
        'mla_w_uq': _normal(next(ks), (N_B_LAYERS, Q_LORA, MLA_HEADS * (QK_NOPE + QK_ROPE)), Q_LORA),
        'mla_w_o': _normal(next(ks), (N_B_LAYERS, MLA_HEADS * V_HEAD, D), MLA_HEADS * V_HEAD),
        'peer_w_q': _normal(next(ks), (DEPTH, D, PEER_HEADS * PEER_QDIM), D),
        'peer_sub_keys': _normal(next(ks), (DEPTH, PEER_HEADS, 2, N_KEYS, PEER_HALF), PEER_HALF),
        'peer_u': _normal(next(ks), (DEPTH, N_EXPERTS, D), D),
        'peer_v': _normal(next(ks), (DEPTH, N_EXPERTS, D), PEER_HEADS),
        'final_g': _gain(next(ks), (D,)),
    }


def reference(x, c, positions, ada_w, ada_b, norm_mix_g, norm_ffn_g,
              lru_w_in, lru_conv_w, lru_conv_b, lru_wa, lru_ba, lru_wx, lru_bx, lru_lambda, lru_w_out,
              kv_ada_w, kv_ada_b, kv_norm_g, mla_w_dkv, mla_w_kr, mla_kv_latent_g, mla_w_uk, mla_w_uv,
              mla_w_dq, mla_q_latent_g, mla_w_uq, mla_w_o,
              peer_w_q, peer_sub_keys, peer_u, peer_v, final_g):
    k_nope = k_rope = v = None
    for l in range(DEPTH):
        if l == N_A_LAYERS:
            k_nope, k_rope, v = shared_kv(x, c, kv_ada_w, kv_ada_b, kv_norm_g, mla_w_dkv, mla_w_kr,
                                          mla_kv_latent_g, mla_w_uk, mla_w_uv, positions)
        mod = (jax.nn.silu(c) @ ada_w[l] + ada_b[l])[:, None, :]
        sh1, sc1, g1, sh2, sc2, g2 = jnp.split(mod, 6, axis=-1)
        h = rmsnorm(x, norm_mix_g[l]) * (1.0 + sc1) + sh1
        if l < N_A_LAYERS:
            y = rglru_block(h, lru_w_in[l], lru_conv_w[l], lru_conv_b[l], lru_wa[l], lru_ba[l],
                            lru_wx[l], lru_bx[l], lru_lambda[l], lru_w_out[l])
        else:
            j = l - N_A_LAYERS
            y = mla_block(h, mla_w_dq[j], mla_q_latent_g[j], mla_w_uq[j], mla_w_o[j],
                          k_nope, k_rope, v, positions)
        x = x + g1 * y
        h = rmsnorm(x, norm_ffn_g[l]) * (1.0 + sc2) + sh2
        x = x + g2 * peer(h, peer_w_q[l], peer_sub_keys[l], peer_u[l], peer_v[l])
    return rmsnorm(x, final_g)
```

```python
import functools
import math

import jax
import jax.numpy as jnp
from jax import lax
from jax.experimental import pallas as pl
from jax.experimental.pallas import tpu as pltpu

F32 = jnp.float32
BF16 = jnp.bfloat16

D_MODEL = 1024
DEPTH = 4
N_A_LAYERS = DEPTH // 2
LRU_HEADS = 4
LRU_BLOCK = D_MODEL // LRU_HEADS
CONV_WIDTH = 4
LRU_C = 8.0
MLA_HEADS = 16
QK_NOPE = 64
QK_ROPE = 32
V_HEAD = 64
Q_LORA = 384
KV_LORA = 256
ROPE_THETA = 10000.0
ATTN_SCALE = 1.0 / math.sqrt(QK_NOPE + QK_ROPE)
PEER_HEADS = 8
N_KEYS = 128
PEER_TOPK = 16
PEER_HALF = 128
RMS_EPS = 1e-6
NEG_INF = -1e30

LANES = 128
SUBLANES = 8
HEAD_PAD = LANES
VMEM_LIMIT = 56 * 1024 * 1024

TS_LRU = 256
TM_TOK = 512
TQ_ATT = 512
TK_ATT = 512
TM_PEER = 512
EC_PEER = 1024


def _cparams(sem):
    return pltpu.CompilerParams(dimension_semantics=sem, vmem_limit_bytes=VMEM_LIMIT)


def _rms(x, g):
    return x * lax.rsqrt(jnp.mean(x * x, axis=-1, keepdims=True) + RMS_EPS) * g


def _gelu(x):
    return 0.5 * x * (1.0 + lax.erf(x * (1.0 / math.sqrt(2.0))))


def _expm1(y):
    series = y * (1.0 + 0.5 * y * (1.0 + (1.0 / 3.0) * y * (1.0 + 0.25 * y * (1.0 + 0.2 * y))))
    return jnp.where(jnp.abs(y) < 0.05, series, jnp.exp(y) - 1.0)


def _const_spec(shape):
    nd = len(shape)
    return pl.BlockSpec(shape, lambda *_: (0,) * nd)


def _mod_kernel(c_ref, w_ref, b_ref, o_ref):
    c = c_ref[...]
    sc = c * jax.nn.sigmoid(c)
    o_ref[...] = jnp.dot(sc.astype(BF16), w_ref[...].astype(BF16),
                         preferred_element_type=F32) + b_ref[...]


def _mod_call(c_pad, w, b, tn):
    L, D, N = w.shape
    return pl.pallas_call(
        _mod_kernel,
        out_shape=jax.ShapeDtypeStruct((L, SUBLANES, N), F32),
        grid=(L, N // tn),
        in_specs=[pl.BlockSpec((SUBLANES, D), lambda l, n: (0, 0)),
                  pl.BlockSpec((None, D, tn), lambda l, n: (l, 0, n)),
                  pl.BlockSpec((None, 1, tn), lambda l, n: (l, 0, n))],
        out_specs=pl.BlockSpec((None, SUBLANES, tn), lambda l, n: (l, 0, n)),
        compiler_params=_cparams(("arbitrary", "arbitrary")),
        name="adaln_mod",
    )(c_pad, w, b)


def _rope_kernel(pos_ref, freq_ref, ca_ref, sb_ref):
    ang = pos_ref[...].astype(F32) * freq_ref[...]
    lane = lax.broadcasted_iota(jnp.int32, ang.shape, 1)
    cos, sin = jnp.cos(ang), jnp.sin(ang)
    half = QK_ROPE // 2
    ca_ref[...] = jnp.where(lane < QK_NOPE, 1.0, jnp.where(lane < QK_NOPE + QK_ROPE, cos, 0.0))
    sb_ref[...] = jnp.where(lane < QK_NOPE, 0.0,
                            jnp.where(lane < QK_NOPE + half, -sin,
                                      jnp.where(lane < QK_NOPE + QK_ROPE, sin, 0.0)))


def _rope_call(pos_col, freq_lane):
    T = pos_col.shape[0]
    tm = TM_TOK
    spec = pl.BlockSpec((tm, LANES), lambda i: (i, 0))
    return pl.pallas_call(
        _rope_kernel,
        out_shape=(jax.ShapeDtypeStruct((T, LANES), F32),) * 2,
        grid=(T // tm,),
        in_specs=[pl.BlockSpec((tm, 1), lambda i: (i, 0)), _const_spec((1, LANES))],
        out_specs=(spec, spec),
        compiler_params=_cparams(("arbitrary",)),
        name="rope_tables",
    )(pos_col, freq_lane)


def _shift_rows(x, d, fill):
    n = x.shape[0]
    if d % SUBLANES == 0:
        return jnp.concatenate([jnp.full((d, x.shape[1]), fill, x.dtype), x[:n - d]], axis=0)
    rolled = pltpu.roll(x, d, 0)
    row = lax.broadcasted_iota(jnp.int32, x.shape, 0)
    return jnp.where(row < d, fill, rolled)


def _lru_kernel(x_ref, mod_ref, ng_ref, win_ref, cw_ref, cb_ref, wa_ref, ba_ref, wx_ref, bx_ref,
                lam_ref, wout_ref, o_ref, rec_s, h_s, *, ts):
    D = D_MODEL

    @pl.when(pl.program_id(1) == 0)
    def _():
        rec_s[0:SUBLANES, :] = jnp.zeros((SUBLANES, D), F32)
        h_s[...] = jnp.zeros((SUBLANES, D), F32)

    x = x_ref[...]
    mod = mod_ref[...]
    sh1, sc1, g1 = mod[0:1], mod[1:2], mod[2:3]
    h = _rms(x, ng_ref[...]) * (1.0 + sc1) + sh1
    u = jnp.dot(h.astype(BF16), win_ref[...], preferred_element_type=F32)
    gate = _gelu(u[:, :D])
    rec = u[:, D:]

    rec_s[SUBLANES:ts + SUBLANES, :] = rec
    cw = cw_ref[...]
    xc = (cb_ref[...] + rec * cw[3:4]
          + rec_s[SUBLANES - 1:ts + SUBLANES - 1, :] * cw[2:3]
          + rec_s[SUBLANES - 2:ts + SUBLANES - 2, :] * cw[1:2]
          + rec_s[SUBLANES - 3:ts + SUBLANES - 3, :] * cw[0:1])
    rec_s[0:SUBLANES, :] = rec[ts - SUBLANES:ts, :]

    xcb = xc.astype(BF16)
    r_pre = jnp.concatenate(
        [jnp.dot(xcb[:, k * LRU_BLOCK:(k + 1) * LRU_BLOCK], wa_ref[k], preferred_element_type=F32)
         for k in range(LRU_HEADS)], axis=-1)
    i_pre = jnp.concatenate(
        [jnp.dot(xcb[:, k * LRU_BLOCK:(k + 1) * LRU_BLOCK], wx_ref[k], preferred_element_type=F32)
         for k in range(LRU_HEADS)], axis=-1)
    r = jax.nn.sigmoid(r_pre + ba_ref[...])
    gi = jax.nn.sigmoid(i_pre + bx_ref[...])
    z = -lam_ref[...]
    softplus = jnp.maximum(z, 0.0) + jnp.log1p(jnp.exp(-jnp.abs(z)))
    log_a = (-LRU_C) * r * softplus
    a = jnp.exp(log_a)
    b = jnp.sqrt(-_expm1(2.0 * log_a)) * (gi * xc)

    d = 1
    while d < ts:
        a_sh = _shift_rows(a, d, 1.0)
        b_sh = _shift_rows(b, d, 0.0)
        b = a * b_sh + b
        a = a * a_sh
        d *= 2
    h0 = h_s[SUBLANES - 1:SUBLANES, :]
    hs = a * h0 + b
    h_s[...] = hs[ts - SUBLANES:ts, :]

    y = jnp.dot((gate * hs).astype(BF16), wout_ref[...], preferred_element_type=F32)
    o_ref[...] = x + g1 * y


def _lru_call(x, mod, ng, w_in, cw, cb, wa, ba, wx, bx, lam, w_out):
    B, S, D = x.shape
    ts = TS_LRU
    kern = functools.partial(_lru_kernel, ts=ts)
    return pl.pallas_call(
        kern,
        out_shape=jax.ShapeDtypeStruct((B, S, D), F32),
        grid=(B, S // ts),
        in_specs=[pl.BlockSpec((None, ts, D), lambda b, s: (b, s, 0)),
                  pl.BlockSpec((None, 6, D), lambda b, s: (b, 0, 0)),
                  _const_spec((1, D)), _const_spec((D, 2 * D)), _const_spec((CONV_WIDTH, D)),
                  _const_spec((1, D)), _const_spec((LRU_HEADS, LRU_BLOCK, LRU_BLOCK)), _const_spec((1, D)),
                  _const_spec((LRU_HEADS, LRU_BLOCK, LRU_BLOCK)), _const_spec((1, D)), _const_spec((1, D)),
                  _const_spec((D, D))],
        out_specs=pl.BlockSpec((None, ts, D), lambda b, s: (b, s, 0)),
        scratch_shapes=[pltpu.VMEM((ts + SUBLANES, D), F32), pltpu.VMEM((SUBLANES, D), F32)],
        compiler_params=_cparams(("arbitrary", "arbitrary")),
        name="rglru_layer",
    )(x, mod, ng, w_in, cw, cb, wa, ba, wx, bx, lam, w_out)


def _kv_kernel(x_ref, mod_ref, ng_ref, w1_ref, lg_ref, wuk_ref, wuv_ref, ca_ref, sb_ref, k_ref, v_ref):
    x = x_ref[...]
    mod = mod_ref[...]
    shift, scale = mod[0:1], mod[1:2]
    h = _rms(x, ng_ref[...]) * (1.0 + scale) + shift
    t = jnp.dot(h.astype(BF16), w1_ref[...], preferred_element_type=F32)
    c_kv = _rms(t[:, :KV_LORA], lg_ref[...])
    kr = t[:, KV_LORA:KV_LORA + LANES] * ca_ref[...] + t[:, KV_LORA + LANES:] * sb_ref[...]
    ckb = c_kv.astype(BF16)
    kn = jnp.dot(ckb, wuk_ref[...], preferred_element_type=F32)
    k_ref[...] = (kn + jnp.concatenate([kr] * MLA_HEADS, axis=-1)).astype(BF16)
    v_ref[...] = jnp.dot(ckb, wuv_ref[...], preferred_element_type=F32).astype(BF16)


def _kv_call(x2, mod, ng, w1, lg, wuk, wuv, ca, sb, S):
    T, D = x2.shape
    tm = TM_TOK
    nb = S // tm
    HP = MLA_HEADS * HEAD_PAD
    HV = MLA_HEADS * V_HEAD
    return pl.pallas_call(
        _kv_kernel,
        out_shape=(jax.ShapeDtypeStruct((T, HP), BF16), jax.ShapeDtypeStruct((T, HV), BF16)),
        grid=(T // tm,),
        in_specs=[pl.BlockSpec((tm, D), lambda i: (i, 0)),
                  pl.BlockSpec((None, 2, D), lambda i: (i // nb, 0, 0)),
                  _const_spec((1, D)), _const_spec(w1.shape), _const_spec((1, KV_LORA)),
                  _const_spec(wuk.shape), _const_spec(wuv.shape),
                  pl.BlockSpec((tm, LANES), lambda i: (i, 0)), pl.BlockSpec((tm, LANES), lambda i: (i, 0))],
        out_specs=(pl.BlockSpec((tm, HP), lambda i: (i, 0)), pl.BlockSpec((tm, HV), lambda i: (i, 0))),
        compiler_params=_cparams(("arbitrary",)),
        name="shared_kv",
    )(x2, mod, ng, w1, lg, wuk, wuv, ca, sb)


def _q_kernel(x_ref, mod_ref, ng_ref, wdq_ref, lg_ref, wa_ref, wb_ref, ca_ref, sb_ref, q_ref):
    x = x_ref[...]
    mod = mod_ref[...]
    sh1, sc1 = mod[0:1], mod[1:2]
    h = _rms(x, ng_ref[...]) * (1.0 + sc1) + sh1
    ql = _rms(jnp.dot(h.astype(BF16), wdq_ref[...], preferred_element_type=F32), lg_ref[...])
    qlb = ql.astype(BF16)
    qa = jnp.dot(qlb, wa_ref[...], preferred_element_type=F32)
    qb = jnp.dot(qlb, wb_ref[...], preferred_element_type=F32)
    ca = jnp.concatenate([ca_ref[...]] * MLA_HEADS, axis=-1)
    sb = jnp.concatenate([sb_ref[...]] * MLA_HEADS, axis=-1)
    q_ref[...] = ((qa * ca + qb * sb) * ATTN_SCALE).astype(BF16)


def _q_call(x2, mod, ng, wdq, lg, wa, wb, ca, sb, S):
    T, D = x2.shape
    tm = TM_TOK
    nb = S // tm
    HP = MLA_HEADS * HEAD_PAD
    return pl.pallas_call(
        _q_kernel,
        out_shape=jax.ShapeDtypeStruct((T, HP), BF16),
        grid=(T // tm,),
        in_specs=[pl.BlockSpec((tm, D), lambda i: (i, 0)),
                  pl.BlockSpec((None, 6, D), lambda i: (i // nb, 0, 0)),
                  _const_spec((1, D)), _const_spec(wdq.shape), _const_spec((1, Q_LORA)),
                  _const_spec(wa.shape), _const_spec(wb.shape),
                  pl.BlockSpec((tm, LANES), lambda i: (i, 0)), pl.BlockSpec((tm, LANES), lambda i: (i, 0))],
        out_specs=pl.BlockSpec((tm, HP), lambda i: (i, 0)),
        compiler_params=_cparams(("arbitrary",)),
        name="mla_queries",
    )(x2, mod, ng, wdq, lg, wa, wb, ca, sb)


def _attn_kernel(nfull_ref, nhi_ref, q_ref, k_ref, v_ref, pq_ref, pk_ref, o_ref, *, tq, tk, nq):
    b = pl.program_id(0)
    qi = pl.program_id(2)
    nfull = nfull_ref[b * nq + qi]
    nhi = nhi_ref[b * nq + qi]
    pos_q = pq_ref[...]
    outs = []
    for hh in range(2):
        q = q_ref[:, hh * HEAD_PAD:(hh + 1) * HEAD_PAD]

        def step(kc, carry, masked):
            m, l, acc = carry
            off = pl.multiple_of(kc * tk, tk)
            k = k_ref[pl.ds(off, tk), hh * HEAD_PAD:(hh + 1) * HEAD_PAD]
            v = v_ref[pl.ds(off, tk), :]
            s = lax.dot_general(q, k, (((1,), (1,)), ((), ())), preferred_element_type=F32)
            if masked:
                pos_k = pk_ref[:, pl.ds(off, tk)]
                s = jnp.where(pos_k <= pos_q, s, NEG_INF)
            m_new = jnp.maximum(m, jnp.max(s, axis=-1, keepdims=True))
            p = jnp.exp(s - m_new)
            alpha = jnp.exp(m - m_new)
            l_new = alpha * l + jnp.sum(p, axis=-1, keepdims=True)
            acc_new = alpha * acc + jnp.dot(p.astype(BF16), v, preferred_element_type=F32)
            return m_new, l_new, acc_new

        init = (jnp.full((tq, 1), NEG_INF, F32), jnp.zeros((tq, 1), F32), jnp.zeros((tq, 2 * V_HEAD), F32))
        carry = lax.fori_loop(0, nfull, functools.partial(step, masked=False), init)
        m, l, acc = lax.fori_loop(nfull, nhi, functools.partial(step, masked=True), carry)
        outs.append(acc / l)
    lane = lax.broadcasted_iota(jnp.int32, (tq, 2 * V_HEAD), 1)
    o_ref[...] = jnp.where(lane < V_HEAD, outs[0], outs[1]).astype(BF16)


def _attn_call(nfull, nhi, q, k, v, pos_col, pos_row, B, S):
    tq, tk = TQ_ATT, TK_ATT
    nq = S // tq
    T = B * S
    kern = functools.partial(_attn_kernel, tq=tq, tk=tk, nq=nq)
    grid_spec = pltpu.PrefetchScalarGridSpec(
        num_scalar_prefetch=2,
        grid=(B, MLA_HEADS // 2, nq),
        in_specs=[pl.BlockSpec((tq, 2 * HEAD_PAD), lambda b, h, i, *_: (b * nq + i, h)),
                  pl.BlockSpec((S, 2 * HEAD_PAD), lambda b, h, i, *_: (b, h)),
                  pl.BlockSpec((S, 2 * V_HEAD), lambda b, h, i, *_: (b, h)),
                  pl.BlockSpec((tq, 1), lambda b, h, i, *_: (b * nq + i, 0)),
                  pl.BlockSpec((None, 1, S), lambda b, h, i, *_: (b, 0, 0))],
        out_specs=pl.BlockSpec((tq, 2 * V_HEAD), lambda b, h, i, *_: (b * nq + i, h)),
    )
    return pl.pallas_call(
        kern,
        out_shape=jax.ShapeDtypeStruct((T, MLA_HEADS * V_HEAD), BF16),
        grid_spec=grid_spec,
        compiler_params=_cparams(("arbitrary", "arbitrary", "arbitrary")),
        name="mla_attention",
    )(nfull, nhi, q, k, v, pos_col, pos_row)


def _oproj_kernel(x_ref, o_ref, mod_ref, w_ref, out_ref):
    g1 = mod_ref[...][2:3]
    y = jnp.dot(o_ref[...], w_ref[...], preferred_element_type=F32)
    out_ref[...] = x_ref[...] + g1 * y


def _oproj_call(x2, o, mod, w_o, S):
    T, D = x2.shape
    tm = TM_TOK
    nb = S // tm
    return pl.pallas_call(
        _oproj_kernel,
        out_shape=jax.ShapeDtypeStruct((T, D), F32),
        grid=(T // tm,),
        in_specs=[pl.BlockSpec((tm, D), lambda i: (i, 0)),
                  pl.BlockSpec((tm, o.shape[1]), lambda i: (i, 0)),
                  pl.BlockSpec((None, 6, D), lambda i: (i // nb, 0, 0)),
                  _const_spec(w_o.shape)],
        out_specs=pl.BlockSpec((tm, D), lambda i: (i, 0)),
        compiler_params=_cparams(("arbitrary",)),
        name="mla_out_proj",
    )(x2, o, mod, w_o)


TOP_ROWS = 3 * SUBLANES


def _top17_rows(x):
    rows = lax.broadcasted_iota(jnp.int32, (TOP_ROWS, x.shape[1]), 0)
    out = jnp.full((TOP_ROWS, x.shape[1]), NEG_INF, F32)
    for j in range(PEER_TOPK + 1):
        m = jnp.max(x, axis=0, keepdims=True)
        out = jnp.where(rows == j, m, out)
        x = jnp.where(x == m, NEG_INF, x)
    return out


def _kth_and_next_largest_rows(x, k):
    shape = (1, x.shape[1])
    tk, tn = jnp.full(shape, NEG_INF, F32), jnp.full(shape, NEG_INF, F32)
    cnt = jnp.zeros(shape, F32)
    for _ in range(k + 1):
        m = jnp.max(x, axis=0, keepdims=True)
        eq = x == m
        tk = jnp.where(cnt < k, m, tk)
        tn = jnp.where(cnt < k + 1, m, tn)
        cnt = cnt + jnp.sum(jnp.where(eq, 1.0, 0.0), axis=0, keepdims=True)
        x = jnp.where(eq, NEG_INF, x)
    return tk, tn


def _peer_kernel(x_ref, mod_ref, ng_ref, wqt_ref, keys_ref, u_ref, vt_ref, fg_ref, o_ref,
                 h_s, st_s, thr_s, e1_s, e2_s, wa_s, acc_s, *, tm, ec, final_norm):
    D = D_MODEL
    ntc = tm // LANES
    e = pl.program_id(1)
    ne = pl.num_programs(1)

    @pl.when(e == 0)
    def _scores():
        x = x_ref[...]
        mod = mod_ref[...]
        sh2, sc2 = mod[3:4], mod[4:5]
        hb = (_rms(x, ng_ref[...]) * (1.0 + sc2) + sh2).astype(BF16)
        h_s[...] = hb
        qt = lax.dot_general(wqt_ref[...], hb, (((1,), (1,)), ((), ())), preferred_element_type=F32)
        qtb = qt.astype(BF16)
        for hp in range(2 * PEER_HEADS):
            st = jnp.dot(keys_ref[hp], qtb[hp * PEER_HALF:(hp + 1) * PEER_HALF, :],
                         preferred_element_type=F32)
            for tc in range(ntc):
                st_s[tc, hp] = st[:, tc * LANES:(tc + 1) * LANES]

        def per_block(i, _):
            tc = i // PEER_HEADS
            hd = i % PEER_HEADS
            s1 = st_s[tc, 2 * hd]
            s2 = st_s[tc, 2 * hd + 1]
            a = _top17_rows(s1)
            b = _top17_rows(s2)
            cands = [a[0:1] + b]
            for r in range(1, SUBLANES):
                cands.append(a[r:r + 1] + b[0:SUBLANES])
            cands.append(a[SUBLANES:TOP_ROWS] + b[0:1])
            cand = jnp.concatenate(cands, axis=0)
            c16, c17 = _kth_and_next_largest_rows(cand, PEER_TOPK)
            tau = 0.5 * (c16 + c17)
            m1, m2 = a[0:1], b[0:1]
            zsum = jnp.sum(jnp.where(cand >= tau, jnp.exp(cand - (m1 + m2)), 0.0), axis=0, keepdims=True)
            thr_s[tc, hd] = jnp.exp(tau - s1 - m2)
            e1_s[tc, hd] = jnp.exp(s1 - m1) / zsum
            e2_s[tc, hd] = jnp.exp(s2 - m2)
            return 0

        lax.fori_loop(0, ntc * PEER_HEADS, per_block, 0)
        acc_s[...] = jnp.zeros(acc_s.shape, F32)

    at = lax.dot_general(u_ref[...], h_s[...], (((1,), (1,)), ((), ())), preferred_element_type=F32)
    gact = _gelu(at)
    ngrp = ec // LANES
    for g8 in range(ngrp // SUBLANES):
        base = pl.multiple_of(e * ngrp + g8 * SUBLANES, SUBLANES)
        for tc in range(ntc):
            thr8 = [thr_s[tc, hd, pl.ds(base, SUBLANES), :] for hd in range(PEER_HEADS)]
            e18 = [e1_s[tc, hd, pl.ds(base, SUBLANES), :] for hd in range(PEER_HEADS)]
            for j in range(SUBLANES):
                w = jnp.zeros((LANES, LANES), F32)
                for hd in range(PEER_HEADS):
                    e2 = e2_s[tc, hd]
                    w = w + jnp.where(e2 >= thr8[hd][j:j + 1, :], e2, 0.0) * e18[hd][j:j + 1, :]
                r0 = (g8 * SUBLANES + j) * LANES
                wa_s[r0:r0 + LANES, tc * LANES:(tc + 1) * LANES] = (
                    w * gact[r0:r0 + LANES, tc * LANES:(tc + 1) * LANES]).astype(BF16)
    acc_s[...] += jnp.dot(vt_ref[...], wa_s[...], preferred_element_type=F32)

    @pl.when(e == ne - 1)
    def _finish():
        g2 = mod_ref[...][5:6]
        xn = x_ref[...] + g2 * acc_s[...].T
        if final_norm:
            xn = _rms(xn, fg_ref[...])
        o_ref[...] = xn


def _peer_call(x2, mod, ng, wqt, keys, u, vt, fg, S, final_norm):
    T, D = x2.shape
    E = u.shape[0]
    tm, ec = TM_PEER, EC_PEER
    nb = S // tm
    ntc = tm // LANES
    kern = functools.partial(_peer_kernel, tm=tm, ec=ec, final_norm=final_norm)
    return pl.pallas_call(
        kern,
        out_shape=jax.ShapeDtypeStruct((T, D), F32),
        grid=(T // tm, E // ec),
        in_specs=[pl.BlockSpec((tm, D), lambda i, e: (i, 0)),
                  pl.BlockSpec((None, 6, D), lambda i, e: (i // nb, 0, 0)),
                  _const_spec((1, D)), _const_spec(wqt.shape), _const_spec(keys.shape),
                  pl.BlockSpec((ec, D), lambda i, e: (e, 0)),
                  pl.BlockSpec((D, ec), lambda i, e: (0, e)),
                  _const_spec((1, D))],
        out_specs=pl.BlockSpec((tm, D), lambda i, e: (i, 0)),
        scratch_shapes=[pltpu.VMEM((tm, D), BF16),
                        pltpu.VMEM((ntc, 2 * PEER_HEADS, N_KEYS, LANES), F32),
                        pltpu.VMEM((ntc, PEER_HEADS, N_KEYS, LANES), F32),
                        pltpu.VMEM((ntc, PEER_HEADS, N_KEYS, LANES), F32),
                        pltpu.VMEM((ntc, PEER_HEADS, N_KEYS, LANES), F32),
                        pltpu.VMEM((ec, tm), BF16),
                        pltpu.VMEM((D, tm), F32)],
        compiler_params=_cparams(("arbitrary", "arbitrary")),
        name="peer_layer",
    )(x2, mod, ng, wqt, keys, u, vt, fg)


def _head_pad_cols(w_nope, w_rope, n_in):
    zeros = jnp.zeros((n_in, MLA_HEADS, HEAD_PAD - QK_NOPE - QK_ROPE), w_nope.dtype)
    return jnp.concatenate([w_nope, w_rope, zeros], axis=-1).reshape(n_in, MLA_HEADS * HEAD_PAD)


def _swap_halves(w):
    half = QK_ROPE // 2
    return jnp.concatenate([w[..., half:], w[..., :half]], axis=-1)


def kernel(x, c, positions, ada_w, ada_b, norm_mix_g, norm_ffn_g, lru_w_in, lru_conv_w, lru_conv_b, lru_wa, lru_ba, lru_wx, lru_bx, lru_lambda, lru_w_out, kv_ada_w, kv_ada_b, kv_norm_g, mla_w_dkv, mla_w_kr, mla_kv_latent_g, mla_w_uk, mla_w_uv, mla_w_dq, mla_q_latent_g, mla_w_uq, mla_w_o, peer_w_q, peer_sub_keys, peer_u, peer_v, final_g):
    B, S, D = x.shape
    T = B * S
    assert D == D_MODEL and S % TM_TOK == 0 and S % TS_LRU == 0 and S % TQ_ATT == 0 and S % TM_PEER == 0

    c_pad = jnp.zeros((SUBLANES, D), F32).at[:B].set(c)
    mod_all = _mod_call(c_pad, ada_w, ada_b[:, None, :], 1536)
    mod_all = mod_all[:, :B].reshape(DEPTH, B, 6, D)
    mod_kv = _mod_call(c_pad, kv_ada_w[None], kv_ada_b[None, None, :], 1024)[0, :B].reshape(B, 2, D)

    half = QK_ROPE // 2
    freqs = ROPE_THETA ** (-jnp.arange(half, dtype=F32) / half)
    freq_lane = jnp.zeros((1, LANES), F32).at[0, QK_NOPE:QK_NOPE + QK_ROPE].set(jnp.concatenate([freqs, freqs]))
    pos_col = positions.reshape(T, 1)
    ca, sb = _rope_call(pos_col, freq_lane)

    nq, nk = S // TQ_ATT, S // TK_ATT
    pq = positions.reshape(B, nq, TQ_ATT)
    pk = positions.reshape(B, nk, TK_ATT)
    qmin, qmax = pq.min(-1), pq.max(-1)
    kmin, kmax = pk.min(-1), pk.max(-1)
    needed = kmin[:, None, :] <= qmax[:, :, None]
    nhi = jnp.max(jnp.where(needed, jnp.arange(1, nk + 1, dtype=jnp.int32), 0), axis=-1)
    full = kmax[:, None, :] <= qmin[:, :, None]
    nfull = jnp.sum(jnp.cumprod(full.astype(jnp.int32), axis=-1), axis=-1)
    nfull = jnp.minimum(nfull, nhi).astype(jnp.int32).reshape(-1)
    nhi = nhi.astype(jnp.int32).reshape(-1)
    pos_row = positions.reshape(B, 1, S)

    def peer_layer(x2, l, final_norm):
        wqt = peer_w_q[l].T.astype(BF16)
        keys = peer_sub_keys[l].reshape(2 * PEER_HEADS, N_KEYS, PEER_HALF).astype(BF16)
        u = peer_u[l].astype(BF16)
        vt = peer_v[l].T.astype(BF16)
        return _peer_call(x2, mod_all[l], norm_ffn_g[l][None], wqt, keys, u, vt, final_g[None], S, final_norm)

    for l in range(N_A_LAYERS):
        x = _lru_call(x, mod_all[l], norm_mix_g[l][None], lru_w_in[l].astype(BF16), lru_conv_w[l],
                      lru_conv_b[l][None], lru_wa[l].astype(BF16), lru_ba[l][None], lru_wx[l].astype(BF16),
                      lru_bx[l][None], lru_lambda[l][None], lru_w_out[l].astype(BF16))
        x = peer_layer(x.reshape(T, D), l, False).reshape(B, S, D)

    x2 = x.reshape(T, D)

    zc = jnp.zeros((D, QK_NOPE), F32)
    zt = jnp.zeros((D, HEAD_PAD - QK_NOPE - QK_ROPE), F32)
    w1 = jnp.concatenate([mla_w_dkv, zc, mla_w_kr, zt, zc, _swap_halves(mla_w_kr), zt], axis=-1).astype(BF16)
    wuk = jnp.concatenate([mla_w_uk.reshape(KV_LORA, MLA_HEADS, QK_NOPE),
                           jnp.zeros((KV_LORA, MLA_HEADS, HEAD_PAD - QK_NOPE), F32)], axis=-1)
    wuk = wuk.reshape(KV_LORA, MLA_HEADS * HEAD_PAD).astype(BF16)
    k_all, v_all = _kv_call(x2, mod_kv, kv_norm_g[None], w1, mla_kv_latent_g[None], wuk,
                            mla_w_uv.astype(BF16), ca, sb, S)

    for j in range(DEPTH - N_A_LAYERS):
        l = N_A_LAYERS + j
        wq = mla_w_uq[j].reshape(Q_LORA, MLA_HEADS, QK_NOPE + QK_ROPE)
        w_nope, w_rope = wq[..., :QK_NOPE], wq[..., QK_NOPE:]
        wa = _head_pad_cols(w_nope, w_rope, Q_LORA).astype(BF16)
        wb = _head_pad_cols(jnp.zeros_like(w_nope), _swap_halves(w_rope), Q_LORA).astype(BF16)
        q = _q_call(x2, mod_all[l], norm_mix_g[l][None], mla_w_dq[j].astype(BF16), mla_q_latent_g[j][None],
                    wa, wb, ca, sb, S)
        o = _attn_call(nfull, nhi, q, k_all, v_all, pos_col, pos_row, B, S)
        x2 = _oproj_call(x2, o, mod_all[l], mla_w_o[j].astype(BF16), S)
        x2 = peer_layer(x2, l, l == DEPTH - 1)

    return x2.reshape(B, S, D)
```

```python
import functools
import math

import jax
import jax.numpy as jnp
from jax import lax
from jax.experimental import pallas as pl
from jax.experimental.pallas import tpu as pltpu

F32 = jnp.float32
BF16 = jnp.bfloat16

D_MODEL = 1024
DEPTH = 4
N_A_LAYERS = DEPTH // 2
LRU_HEADS = 4
LRU_BLOCK = D_MODEL // LRU_HEADS
CONV_WIDTH = 4
LRU_C = 8.0
MLA_HEADS = 16
QK_NOPE = 64
QK_ROPE = 32
V_HEAD = 64
Q_LORA = 384
KV_LORA = 256
ROPE_THETA = 10000.0
ATTN_SCALE = 1.0 / math.sqrt(QK_NOPE + QK_ROPE)
PEER_HEADS = 8
N_KEYS = 128
PEER_TOPK = 16
PEER_HALF = 128
RMS_EPS = 1e-6
NEG_INF = -1e30
LOG2E = math.log2(math.e)

LANES = 128
SUBLANES = 8
HEAD_PAD = LANES
VMEM_LIMIT = 56 * 1024 * 1024

TS_LRU = 256
TM_TOK = 512
TQ_ATT = 512
TK_ATT = 512
QSUB_ATT = 256
ONES_ROWS = 16
TM_PEER = 512
EC_PEER = 1024


def _cparams(sem):
    return pltpu.CompilerParams(dimension_semantics=sem, vmem_limit_bytes=VMEM_LIMIT)


def _rms(x, g):
    return x * lax.rsqrt(jnp.mean(x * x, axis=-1, keepdims=True) + RMS_EPS) * g


def _gelu(x):
    return 0.5 * x * (1.0 + lax.erf(x * (1.0 / math.sqrt(2.0))))


def _expm1(y):
    series = y * (1.0 + 0.5 * y * (1.0 + (1.0 / 3.0) * y * (1.0 + 0.25 * y * (1.0 + 0.2 * y))))
    return jnp.where(jnp.abs(y) < 0.05, series, jnp.exp(y) - 1.0)


def _const_spec(shape):
    nd = len(shape)
    return pl.BlockSpec(shape, lambda *_: (0,) * nd)


def _mod_kernel(c_ref, w_ref, b_ref, o_ref):
    c = c_ref[...]
    sc = c * jax.nn.sigmoid(c)
    o_ref[...] = jnp.dot(sc.astype(BF16), w_ref[...].astype(BF16),
                         preferred_element_type=F32) + b_ref[...]


def _mod_call(c_pad, w, b, tn):
    L, D, N = w.shape
    return pl.pallas_call(
        _mod_kernel,
        out_shape=jax.ShapeDtypeStruct((L, SUBLANES, N), F32),
        grid=(L, N // tn),
        in_specs=[pl.BlockSpec((SUBLANES, D), lambda l, n: (0, 0)),
                  pl.BlockSpec((None, D, tn), lambda l, n: (l, 0, n)),
                  pl.BlockSpec((None, 1, tn), lambda l, n: (l, 0, n))],
        out_specs=pl.BlockSpec((None, SUBLANES, tn), lambda l, n: (l, 0, n)),
        compiler_params=_cparams(("arbitrary", "arbitrary")),
        name="adaln_mod",
    )(c_pad, w, b)


def _rope_kernel(pos_ref, freq_ref, ca_ref, sb_ref):
    ang = pos_ref[...].astype(F32) * freq_ref[...]
    lane = lax.broadcasted_iota(jnp.int32, ang.shape, 1)
    cos, sin = jnp.cos(ang), jnp.sin(ang)
    half = QK_ROPE // 2
    ca_ref[...] = jnp.where(lane < QK_NOPE, 1.0, jnp.where(lane < QK_NOPE + QK_ROPE, cos, 0.0))
    sb_ref[...] = jnp.where(lane < QK_NOPE, 0.0,
                            jnp.where(lane < QK_NOPE + half, -sin,
                                      jnp.where(lane < QK_NOPE + QK_ROPE, sin, 0.0)))


def _rope_call(pos_col, freq_lane):
    T = pos_col.shape[0]
    tm = TM_TOK
    spec = pl.BlockSpec((tm, LANES), lambda i: (i, 0))
    return pl.pallas_call(
        _rope_kernel,
        out_shape=(jax.ShapeDtypeStruct((T, LANES), F32),) * 2,
        grid=(T // tm,),
        in_specs=[pl.BlockSpec((tm, 1), lambda i: (i, 0)), _const_spec((1, LANES))],
        out_specs=(spec, spec),
        compiler_params=_cparams(("arbitrary",)),
        name="rope_tables",
    )(pos_col, freq_lane)


def _shift_rows(x, d, fill):
    n = x.shape[0]
    if d % SUBLANES == 0:
        return jnp.concatenate([jnp.full((d, x.shape[1]), fill, x.dtype), x[:n - d]], axis=0)
    rolled = pltpu.roll(x, d, 0)
    row = lax.broadcasted_iota(jnp.int32, x.shape, 0)
    return jnp.where(row < d, fill, rolled)


def _lru_kernel(x_ref, mod_ref, ng_ref, win_ref, cw_ref, cb_ref, wa_ref, ba_ref, wx_ref, bx_ref,
                lam_ref, wout_ref, o_ref, rec_s, h_s, *, ts):
    D = D_MODEL

    @pl.when(pl.program_id(1) == 0)
    def _():
        rec_s[0:SUBLANES, :] = jnp.zeros((SUBLANES, D), F32)
        h_s[...] = jnp.zeros((SUBLANES, D), F32)

    x = x_ref[...]
    mod = mod_ref[...]
    sh1, sc1, g1 = mod[0:1], mod[1:2], mod[2:3]
    h = _rms(x, ng_ref[...]) * (1.0 + sc1) + sh1
    u = jnp.dot(h.astype(BF16), win_ref[...], preferred_element_type=F32)
    gate = _gelu(u[:, :D])
    rec = u[:, D:]

    rec_s[SUBLANES:ts + SUBLANES, :] = rec
    cw = cw_ref[...]
    xc = (cb_ref[...] + rec * cw[3:4]
          + rec_s[SUBLANES - 1:ts + SUBLANES - 1, :] * cw[2:3]
          + rec_s[SUBLANES - 2:ts + SUBLANES - 2, :] * cw[1:2]
          + rec_s[SUBLANES - 3:ts + SUBLANES - 3, :] * cw[0:1])
    rec_s[0:SUBLANES, :] = rec[ts - SUBLANES:ts, :]

    xcb = xc.astype(BF16)
    r_pre = jnp.concatenate(
        [jnp.dot(xcb[:, k * LRU_BLOCK:(k + 1) * LRU_BLOCK], wa_ref[k], preferred_element_type=F32)
         for k in range(LRU_HEADS)], axis=-1)
    i_pre = jnp.concatenate(
        [jnp.dot(xcb[:, k * LRU_BLOCK:(k + 1) * LRU_BLOCK], wx_ref[k], preferred_element_type=F32)
         for k in range(LRU_HEADS)], axis=-1)
    r = jax.nn.sigmoid(r_pre + ba_ref[...])
    gi = jax.nn.sigmoid(i_pre + bx_ref[...])
    z = -lam_ref[...]
    softplus = jnp.maximum(z, 0.0) + jnp.log1p(jnp.exp(-jnp.abs(z)))
    log_a = (-LRU_C) * r * softplus
    a = jnp.exp(log_a)
    b = jnp.sqrt(-_expm1(2.0 * log_a)) * (gi * xc)

    d = 1
    while d < ts:
        a_sh = _shift_rows(a, d, 1.0)
        b_sh = _shift_rows(b, d, 0.0)
        b = a * b_sh + b
        a = a * a_sh
        d *= 2
    h0 = h_s[SUBLANES - 1:SUBLANES, :]
    hs = a * h0 + b
    h_s[...] = hs[ts - SUBLANES:ts, :]

    y = jnp.dot((gate * hs).astype(BF16), wout_ref[...], preferred_element_type=F32)
    o_ref[...] = x + g1 * y


def _lru_call(x, mod, ng, w_in, cw, cb, wa, ba, wx, bx, lam, w_out):
    B, S, D = x.shape
    ts = TS_LRU
    kern = functools.partial(_lru_kernel, ts=ts)
    return pl.pallas_call(
        kern,
        out_shape=jax.ShapeDtypeStruct((B, S, D), F32),
        grid=(B, S // ts),
        in_specs=[pl.BlockSpec((None, ts, D), lambda b, s: (b, s, 0)),
                  pl.BlockSpec((None, 6, D), lambda b, s: (b, 0, 0)),
                  _const_spec((1, D)), _const_spec((D, 2 * D)), _const_spec((CONV_WIDTH, D)),
                  _const_spec((1, D)), _const_spec((LRU_HEADS, LRU_BLOCK, LRU_BLOCK)), _const_spec((1, D)),
                  _const_spec((LRU_HEADS, LRU_BLOCK, LRU_BLOCK)), _const_spec((1, D)), _const_spec((1, D)),
                  _const_spec((D, D))],
        out_specs=pl.BlockSpec((None, ts, D), lambda b, s: (b, s, 0)),
        scratch_shapes=[pltpu.VMEM((ts + SUBLANES, D), F32), pltpu.VMEM((SUBLANES, D), F32)],
        compiler_params=_cparams(("arbitrary", "arbitrary")),
        name="rglru_layer",
    )(x, mod, ng, w_in, cw, cb, wa, ba, wx, bx, lam, w_out)


def _kv_kernel(x_ref, mod_ref, ng_ref, w1_ref, lg_ref, wuk_ref, wuv_ref, ca_ref, sb_ref, k_ref, v_ref):
    x = x_ref[...]
    mod = mod_ref[...]
    shift, scale = mod[0:1], mod[1:2]
    h = _rms(x, ng_ref[...]) * (1.0 + scale) + shift
    t = jnp.dot(h.astype(BF16), w1_ref[...], preferred_element_type=F32)
    c_kv = _rms(t[:, :KV_LORA], lg_ref[...])
    kr = t[:, KV_LORA:KV_LORA + LANES] * ca_ref[...] + t[:, KV_LORA + LANES:] * sb_ref[...]
    ckb = c_kv.astype(BF16)
    kn = jnp.dot(ckb, wuk_ref[...], preferred_element_type=F32)
    k_ref[...] = (kn + jnp.concatenate([kr] * MLA_HEADS, axis=-1)).astype(BF16)
    v_ref[...] = jnp.dot(ckb, wuv_ref[...], preferred_element_type=F32).T.astype(BF16)


def _kv_call(x2, mod, ng, w1, lg, wuk, wuv, ca, sb, S):
    T, D = x2.shape
    tm = TM_TOK
    nb = S // tm
    HP = MLA_HEADS * HEAD_PAD
    HV = MLA_HEADS * V_HEAD
    return pl.pallas_call(
        _kv_kernel,
        out_shape=(jax.ShapeDtypeStruct((T, HP), BF16), jax.ShapeDtypeStruct((HV, T), BF16)),
        grid=(T // tm,),
        in_specs=[pl.BlockSpec((tm, D), lambda i: (i, 0)),
                  pl.BlockSpec((None, 2, D), lambda i: (i // nb, 0, 0)),
                  _const_spec((1, D)), _const_spec(w1.shape), _const_spec((1, KV_LORA)),
                  _const_spec(wuk.shape), _const_spec(wuv.shape),
                  pl.BlockSpec((tm, LANES), lambda i: (i, 0)), pl.BlockSpec((tm, LANES), lambda i: (i, 0))],
        out_specs=(pl.BlockSpec((tm, HP), lambda i: (i, 0)), pl.BlockSpec((HV, tm), lambda i: (0, i))),
        compiler_params=_cparams(("arbitrary",)),
        name="shared_kv",
    )(x2, mod, ng, w1, lg, wuk, wuv, ca, sb)


def _q_kernel(x_ref, mod_ref, ng_ref, wdq_ref, lg_ref, wa_ref, wb_ref, ca_ref, sb_ref, q_ref):
    x = x_ref[...]
    mod = mod_ref[...]
    sh1, sc1 = mod[0:1], mod[1:2]
    h = _rms(x, ng_ref[...]) * (1.0 + sc1) + sh1
    ql = _rms(jnp.dot(h.astype(BF16), wdq_ref[...], preferred_element_type=F32), lg_ref[...])
    qlb = ql.astype(BF16)
    qa = jnp.dot(qlb, wa_ref[...], preferred_element_type=F32)
    qb = jnp.dot(qlb, wb_ref[...], preferred_element_type=F32)
    ca = jnp.concatenate([ca_ref[...]] * MLA_HEADS, axis=-1)
    sb = jnp.concatenate([sb_ref[...]] * MLA_HEADS, axis=-1)
    q_ref[...] = ((qa * ca + qb * sb) * (ATTN_SCALE * LOG2E)).T.astype(BF16)


def _q_call(x2, mod, ng, wdq, lg, wa, wb, ca, sb, S):
    T, D = x2.shape
    tm = TM_TOK
    nb = S // tm
    HP = MLA_HEADS * HEAD_PAD
    return pl.pallas_call(
        _q_kernel,
        out_shape=jax.ShapeDtypeStruct((HP, T), BF16),
        grid=(T // tm,),
        in_specs=[pl.BlockSpec((tm, D), lambda i: (i, 0)),
                  pl.BlockSpec((None, 6, D), lambda i: (i // nb, 0, 0)),
                  _const_spec((1, D)), _const_spec(wdq.shape), _const_spec((1, Q_LORA)),
                  _const_spec(wa.shape), _const_spec(wb.shape),
                  pl.BlockSpec((tm, LANES), lambda i: (i, 0)), pl.BlockSpec((tm, LANES), lambda i: (i, 0))],
        out_specs=pl.BlockSpec((HP, tm), lambda i: (0, i)),
        compiler_params=_cparams(("arbitrary",)),
        name="mla_queries",
    )(x2, mod, ng, wdq, lg, wa, wb, ca, sb)


def _attn_kernel(nfull_ref, nhi_ref, qt_ref, k_ref, vt_ref, pq_ref, pk_ref, o_ref, *, tq, tk, nq):
    b = pl.program_id(0)
    qi = pl.program_id(2)
    nfull = nfull_ref[b * nq + qi]
    nhi = nhi_ref[b * nq + qi]
    pos_q = pq_ref[...]

    def step(kc, carry, masked):
        off = pl.multiple_of(kc * tk, tk)
        nstream = 2 * (tq // QSUB_ATT)

        def scores(sidx):
            hh, qs = sidx // (tq // QSUB_ATT), sidx % (tq // QSUB_ATT)
            k = k_ref[pl.ds(off, tk), hh * HEAD_PAD:(hh + 1) * HEAD_PAD]
            qt = qt_ref[hh * HEAD_PAD:(hh + 1) * HEAD_PAD, qs * QSUB_ATT:(qs + 1) * QSUB_ATT]
            return jnp.dot(k, qt, preferred_element_type=F32)

        def update(sidx, st):
            hh, qs = sidx // (tq // QSUB_ATT), sidx % (tq // QSUB_ATT)
            m, acc = carry[sidx]
            if masked:
                keep = pk_ref[pl.ds(off, tk), :] <= pos_q[:, qs * QSUB_ATT:(qs + 1) * QSUB_ATT]
                st = jnp.where(keep, st, NEG_INF)
            m_new = jnp.maximum(m, jnp.max(st, axis=0, keepdims=True))
            p = jnp.exp2((st - m_new).astype(BF16))
            alpha = jnp.exp2(m - m_new)
            vt = jnp.concatenate([vt_ref[hh * V_HEAD:(hh + 1) * V_HEAD, pl.ds(off, tk)],
                                  jnp.ones((ONES_ROWS, tk), BF16)], axis=0)
            acc_new = alpha * acc + jnp.dot(vt, p, preferred_element_type=F32)
            return m_new, acc_new

        new = []
        st_next = scores(0)
        for sidx in range(nstream):
            st_cur = st_next
            if sidx + 1 < nstream:
                st_next = scores(sidx + 1)
            new.append(update(sidx, st_cur))
        return tuple(new)

    nqs = tq // QSUB_ATT
    one = (jnp.full((1, QSUB_ATT), NEG_INF, F32), jnp.zeros((V_HEAD + ONES_ROWS, QSUB_ATT), F32))
    carry = lax.fori_loop(0, nfull, functools.partial(step, masked=False), (one,) * (2 * nqs))
    fin = lax.fori_loop(nfull, nhi, functools.partial(step, masked=True), carry)
    heads = [jnp.concatenate([fin[hh * nqs + qs][1][:V_HEAD] / fin[hh * nqs + qs][1][V_HEAD:V_HEAD + 1]
                              for qs in range(nqs)], axis=1) for hh in range(2)]
    o_ref[...] = jnp.concatenate(heads, axis=0).T.astype(BF16)


def _attn_call(nfull, nhi, qt, k, vt, pos_col, pos_row, B, S):
    tq, tk = TQ_ATT, TK_ATT
    nq = S // tq
    T = B * S
    kern = functools.partial(_attn_kernel, tq=tq, tk=tk, nq=nq)
    grid_spec = pltpu.PrefetchScalarGridSpec(
        num_scalar_prefetch=2,
        grid=(B, MLA_HEADS // 2, nq),
        in_specs=[pl.BlockSpec((2 * HEAD_PAD, tq), lambda b, h, i, *_: (h, b * nq + i)),
                  pl.BlockSpec((S, 2 * HEAD_PAD), lambda b, h, i, *_: (b, h)),
                  pl.BlockSpec((2 * V_HEAD, S), lambda b, h, i, *_: (h, b)),
                  pl.BlockSpec((None, 1, tq), lambda b, h, i, *_: (b, 0, i)),
                  pl.BlockSpec((S, 1), lambda b, h, i, *_: (b, 0))],
        out_specs=pl.BlockSpec((tq, 2 * V_HEAD), lambda b, h, i, *_: (b * nq + i, h)),
    )
    return pl.pallas_call(
        kern,
        out_shape=jax.ShapeDtypeStruct((T, MLA_HEADS * V_HEAD), BF16),
        grid_spec=grid_spec,
        compiler_params=_cparams(("arbitrary", "arbitrary", "arbitrary")),
        name="mla_attention",
    )(nfull, nhi, qt, k, vt, pos_row, pos_col)


def _oproj_kernel(x_ref, o_ref, mod_ref, w_ref, out_ref):
    g1 = mod_ref[...][2:3]
    y = jnp.dot(o_ref[...], w_ref[...], preferred_element_type=F32)
    out_ref[...] = x_ref[...] + g1 * y


def _oproj_call(x2, o, mod, w_o, S):
    T, D = x2.shape
    tm = TM_TOK
    nb = S // tm
    return pl.pallas_call(
        _oproj_kernel,
        out_shape=jax.ShapeDtypeStruct((T, D), F32),
        grid=(T // tm,),
        in_specs=[pl.BlockSpec((tm, D), lambda i: (i, 0)),
                  pl.BlockSpec((tm, o.shape[1]), lambda i: (i, 0)),
                  pl.BlockSpec((None, 6, D), lambda i: (i // nb, 0, 0)),
                  _const_spec(w_o.shape)],
        out_specs=pl.BlockSpec((tm, D), lambda i: (i, 0)),
        compiler_params=_cparams(("arbitrary",)),
        name="mla_out_proj",
    )(x2, o, mod, w_o)


TOP_ROWS = 3 * SUBLANES


def _top17_rows(x):
    rows = lax.broadcasted_iota(jnp.int32, (TOP_ROWS, x.shape[1]), 0)
    out = jnp.full((TOP_ROWS, x.shape[1]), NEG_INF, F32)
    for j in range(PEER_TOPK + 1):
        m = jnp.max(x, axis=0, keepdims=True)
        out = jnp.where(rows == j, m, out)
        x = jnp.where(x == m, NEG_INF, x)
    return out


def _kth_and_next_largest_rows(x, k):
    shape = (1, x.shape[1])
    tk, tn = jnp.full(shape, NEG_INF, F32), jnp.full(shape, NEG_INF, F32)
    cnt = jnp.zeros(shape, F32)
    for _ in range(k + 1):
        m = jnp.max(x, axis=0, keepdims=True)
        eq = x == m
        tk = jnp.where(cnt < k, m, tk)
        tn = jnp.where(cnt < k + 1, m, tn)
        cnt = cnt + jnp.sum(jnp.where(eq, 1.0, 0.0), axis=0, keepdims=True)
        x = jnp.where(eq, NEG_INF, x)
    return tk, tn


def _peer_kernel(x_ref, mod_ref, ng_ref, wqt_ref, keys_ref, u_ref, vt_ref, fg_ref, o_ref,
                 h_s, st_s, thr_s, e1_s, e2_s, wa_s, acc_s, *, tm, ec, final_norm):
    D = D_MODEL
    ntc = tm // LANES
    e = pl.program_id(1)
    ne = pl.num_programs(1)

    @pl.when(e == 0)
    def _scores():
        x = x_ref[...]
        mod = mod_ref[...]
        sh2, sc2 = mod[3:4], mod[4:5]
        hb = (_rms(x, ng_ref[...]) * (1.0 + sc2) + sh2).astype(BF16)
        h_s[...] = hb
        qt = lax.dot_general(wqt_ref[...], hb, (((1,), (1,)), ((), ())), preferred_element_type=F32)
        qtb = qt.astype(BF16)
        for hp in range(2 * PEER_HEADS):
            st = jnp.dot(keys_ref[hp], qtb[hp * PEER_HALF:(hp + 1) * PEER_HALF, :],
                         preferred_element_type=F32)
            for tc in range(ntc):
                st_s[tc, hp] = st[:, tc * LANES:(tc + 1) * LANES]

        def per_block(i, _):
            tc = i // PEER_HEADS
            hd = i % PEER_HEADS
            s1 = st_s[tc, 2 * hd]
            s2 = st_s[tc, 2 * hd + 1]
            a = _top17_rows(s1)
            b = _top17_rows(s2)
            cands = [a[0:1] + b]
            for r in range(1, SUBLANES):
                cands.append(a[r:r + 1] + b[0:SUBLANES])
            cands.append(a[SUBLANES:TOP_ROWS] + b[0:1])
            cand = jnp.concatenate(cands, axis=0)
            c16, c17 = _kth_and_next_largest_rows(cand, PEER_TOPK)
            tau = 0.5 * (c16 + c17)
            m1, m2 = a[0:1], b[0:1]
            zsum = jnp.sum(jnp.where(cand >= tau, jnp.exp(cand - (m1 + m2)), 0.0), axis=0, keepdims=True)
            thr_s[tc, hd] = jnp.exp(tau - s1 - m2)
            e1_s[tc, hd] = jnp.exp(s1 - m1) / zsum
            e2_s[tc, hd] = jnp.exp(s2 - m2)
            return 0

        lax.fori_loop(0, ntc * PEER_HEADS, per_block, 0)
        acc_s[...] = jnp.zeros(acc_s.shape, F32)

    at = lax.dot_general(u_ref[...], h_s[...], (((1,), (1,)), ((), ())), preferred_element_type=F32)
    gact = _gelu(at)
    ngrp = ec // LANES
    for g8 in range(ngrp // SUBLANES):
        base = pl.multiple_of(e * ngrp + g8 * SUBLANES, SUBLANES)
        for tc in range(ntc):
            thr8 = [thr_s[tc, hd, pl.ds(base, SUBLANES), :] for hd in range(PEER_HEADS)]
            e18 = [e1_s[tc, hd, pl.ds(base, SUBLANES), :] for hd in range(PEER_HEADS)]
            for j in range(SUBLANES):
                w = jnp.zeros((LANES, LANES), F32)
                for hd in range(PEER_HEADS):
                    e2 = e2_s[tc, hd]
                    w = w + jnp.where(e2 >= thr8[hd][j:j + 1, :], e2, 0.0) * e18[hd][j:j + 1, :]
                r0 = (g8 * SUBLANES + j) * LANES
                wa_s[r0:r0 + LANES, tc * LANES:(tc + 1) * LANES] = (
                    w * gact[r0:r0 + LANES, tc * LANES:(tc + 1) * LANES]).astype(BF16)
    acc_s[...] += jnp.dot(vt_ref[...], wa_s[...], preferred_element_type=F32)

    @pl.when(e == ne - 1)
    def _finish():
        g2 = mod_ref[...][5:6]
        xn = x_ref[...] + g2 * acc_s[...].T
        if final_norm:
            xn = _rms(xn, fg_ref[...])
        o_ref[...] = xn


def _peer_call(x2, mod, ng, wqt, keys, u, vt, fg, S, final_norm):
    T, D = x2.shape
    E = u.shape[0]
    tm, ec = TM_PEER, EC_PEER
    nb = S // tm
    ntc = tm // LANES
    kern = functools.partial(_peer_kernel, tm=tm, ec=ec, final_norm=final_norm)
    return pl.pallas_call(
        kern,
        out_shape=jax.ShapeDtypeStruct((T, D), F32),
        grid=(T // tm, E // ec),
        in_specs=[pl.BlockSpec((tm, D), lambda i, e: (i, 0)),
                  pl.BlockSpec((None, 6, D), lambda i, e: (i // nb, 0, 0)),
                  _const_spec((1, D)), _const_spec(wqt.shape), _const_spec(keys.shape),
                  pl.BlockSpec((ec, D), lambda i, e: (e, 0)),
                  pl.BlockSpec((D, ec), lambda i, e: (0, e)),
                  _const_spec((1, D))],
        out_specs=pl.BlockSpec((tm, D), lambda i, e: (i, 0)),
        scratch_shapes=[pltpu.VMEM((tm, D), BF16),
                        pltpu.VMEM((ntc, 2 * PEER_HEADS, N_KEYS, LANES), F32),
                        pltpu.VMEM((ntc, PEER_HEADS, N_KEYS, LANES), F32),
                        pltpu.VMEM((ntc, PEER_HEADS, N_KEYS, LANES), F32),
                        pltpu.VMEM((ntc, PEER_HEADS, N_KEYS, LANES), F32),
                        pltpu.VMEM((ec, tm), BF16),
                        pltpu.VMEM((D, tm), F32)],
        compiler_params=_cparams(("arbitrary", "arbitrary")),
        name="peer_layer",
    )(x2, mod, ng, wqt, keys, u, vt, fg)


def _head_pad_cols(w_nope, w_rope, n_in):
    zeros = jnp.zeros((n_in, MLA_HEADS, HEAD_PAD - QK_NOPE - QK_ROPE), w_nope.dtype)
    return jnp.concatenate([w_nope, w_rope, zeros], axis=-1).reshape(n_in, MLA_HEADS * HEAD_PAD)


def _swap_halves(w):
    half = QK_ROPE // 2
    return jnp.concatenate([w[..., half:], w[..., :half]], axis=-1)


def kernel(x, c, positions, ada_w, ada_b, norm_mix_g, norm_ffn_g, lru_w_in, lru_conv_w, lru_conv_b, lru_wa, lru_ba, lru_wx, lru_bx, lru_lambda, lru_w_out, kv_ada_w, kv_ada_b, kv_norm_g, mla_w_dkv, mla_w_kr, mla_kv_latent_g, mla_w_uk, mla_w_uv, mla_w_dq, mla_q_latent_g, mla_w_uq, mla_w_o, peer_w_q, peer_sub_keys, peer_u, peer_v, final_g):
    B, S, D = x.shape
    T = B * S
    assert D == D_MODEL and S % TM_TOK == 0 and S % TS_LRU == 0 and S % TQ_ATT == 0 and S % TM_PEER == 0

    c_pad = jnp.zeros((SUBLANES, D), F32).at[:B].set(c)
    mod_all = _mod_call(c_pad, ada_w, ada_b[:, None, :], 1536)
    mod_all = mod_all[:, :B].reshape(DEPTH, B, 6, D)
    mod_kv = _mod_call(c_pad, kv_ada_w[None], kv_ada_b[None, None, :], 1024)[0, :B].reshape(B, 2, D)

    half = QK_ROPE // 2
    freqs = ROPE_THETA ** (-jnp.arange(half, dtype=F32) / half)
    freq_lane = jnp.zeros((1, LANES), F32).at[0, QK_NOPE:QK_NOPE + QK_ROPE].set(jnp.concatenate([freqs, freqs]))
    pos_col = positions.reshape(T, 1)
    ca, sb = _rope_call(pos_col, freq_lane)

    nq, nk = S // TQ_ATT, S // TK_ATT
    pq = positions.reshape(B, nq, TQ_ATT)
    pk = positions.reshape(B, nk, TK_ATT)
    qmin, qmax = pq.min(-1), pq.max(-1)
    kmin, kmax = pk.min(-1), pk.max(-1)
    needed = kmin[:, None, :] <= qmax[:, :, None]
    nhi = jnp.max(jnp.where(needed, jnp.arange(1, nk + 1, dtype=jnp.int32), 0), axis=-1)
    full = kmax[:, None, :] <= qmin[:, :, None]
    nfull = jnp.min(jnp.where(full, nk, jnp.arange(nk, dtype=jnp.int32)), axis=-1)
    nfull = jnp.minimum(nfull, nhi).astype(jnp.int32).reshape(-1)
    nhi = nhi.astype(jnp.int32).reshape(-1)
    pos_row = positions.reshape(B, 1, S)

    def peer_layer(x2, l, final_norm):
        wqt = peer_w_q[l].T.astype(BF16)
        keys = peer_sub_keys[l].reshape(2 * PEER_HEADS, N_KEYS, PEER_HALF).astype(BF16)
        u = peer_u[l].astype(BF16)
        vt = peer_v[l].T.astype(BF16)
        return _peer_call(x2, mod_all[l], norm_ffn_g[l][None], wqt, keys, u, vt, final_g[None], S, final_norm)

    for l in range(N_A_LAYERS):
        x = _lru_call(x, mod_all[l], norm_mix_g[l][None], lru_w_in[l].astype(BF16), lru_conv_w[l],
                      lru_conv_b[l][None], lru_wa[l].astype(BF16), lru_ba[l][None], lru_wx[l].astype(BF16),
                      lru_bx[l][None], lru_lambda[l][None], lru_w_out[l].astype(BF16))
        x = peer_layer(x.reshape(T, D), l, False).reshape(B, S, D)

    x2 = x.reshape(T, D)

    zc = jnp.zeros((D, QK_NOPE), F32)
    zt = jnp.zeros((D, HEAD_PAD - QK_NOPE - QK_ROPE), F32)
    w1 = jnp.concatenate([mla_w_dkv, zc, mla_w_kr, zt, zc, _swap_halves(mla_w_kr), zt], axis=-1).astype(BF16)
    wuk = jnp.concatenate([mla_w_uk.reshape(KV_LORA, MLA_HEADS, QK_NOPE),
                           jnp.zeros((KV_LORA, MLA_HEADS, HEAD_PAD - QK_NOPE), F32)], axis=-1)
    wuk = wuk.reshape(KV_LORA, MLA_HEADS * HEAD_PAD).astype(BF16)
    k_all, v_all = _kv_call(x2, mod_kv, kv_norm_g[None], w1, mla_kv_latent_g[None], wuk,
                            mla_w_uv.astype(BF16), ca, sb, S)

    for j in range(DEPTH - N_A_LAYERS):
        l = N_A_LAYERS + j
        wq = mla_w_uq[j].reshape(Q_LORA, MLA_HEADS, QK_NOPE + QK_ROPE)
        w_nope, w_rope = wq[..., :QK_NOPE], wq[..., QK_NOPE:]
        wa = _head_pad_cols(w_nope, w_rope, Q_LORA).astype(BF16)
        wb = _head_pad_cols(jnp.zeros_like(w_nope), _swap_halves(w_rope), Q_LORA).astype(BF16)
        q = _q_call(x2, mod_all[l], norm_mix_g[l][None], mla_w_dq[j].astype(BF16), mla_q_latent_g[j][None],
                    wa, wb, ca, sb, S)
        o = _attn_call(nfull, nhi, q, k_all, v_all, pos_col, pos_row, B, S)
        x2 = _oproj_call(x2, o, mod_all[l], mla_w_o[j].astype(BF16), S)
        x2 = peer_layer(x2, l, l == DEPTH - 1)

    return x2.reshape(B, S, D)
```

```python
import functools
import math

import jax
import jax.numpy as jnp
from jax import lax
from jax.experimental import pallas as pl
from jax.experimental.pallas import tpu as pltpu

F32 = jnp.float32
BF16 = jnp.bfloat16

D_MODEL = 1024
DEPTH = 4
N_A_LAYERS = DEPTH // 2
LRU_HEADS = 4
LRU_BLOCK = D_MODEL // LRU_HEADS
CONV_WIDTH = 4
LRU_C = 8.0
MLA_HEADS = 16
QK_NOPE = 64
QK_ROPE = 32
V_HEAD = 64
Q_LORA = 384
KV_LORA = 256
ROPE_THETA = 10000.0
ATTN_SCALE = 1.0 / math.sqrt(QK_NOPE + QK_ROPE)
PEER_HEADS = 8
N_KEYS = 128
PEER_TOPK = 16
PEER_HALF = 128
RMS_EPS = 1e-6
NEG_INF = -1e30
LOG2E = math.log2(math.e)

LANES = 128
SUBLANES = 8
HEAD_PAD = LANES
VMEM_LIMIT = 56 * 1024 * 1024

TS_LRU = 256
TM_TOK = 512
TQ_ATT = 512
TK_ATT = 512
QSUB_ATT = 256
ONES_ROWS = 16
TM_PEER = 512
EC_PEER = 1024
ESUB_PEER = 256


def _cparams(sem):
    return pltpu.CompilerParams(dimension_semantics=sem, vmem_limit_bytes=VMEM_LIMIT)


def _rms(x, g):
    return x * lax.rsqrt(jnp.mean(x * x, axis=-1, keepdims=True) + RMS_EPS) * g


def _gelu(x):
    return 0.5 * x * (1.0 + lax.erf(x * (1.0 / math.sqrt(2.0))))


def _expm1(y):
    series = y * (1.0 + 0.5 * y * (1.0 + (1.0 / 3.0) * y * (1.0 + 0.25 * y * (1.0 + 0.2 * y))))
    return jnp.where(jnp.abs(y) < 0.05, series, jnp.exp(y) - 1.0)


def _const_spec(shape):
    nd = len(shape)
    return pl.BlockSpec(shape, lambda *_: (0,) * nd)


def _mod_kernel(c_ref, w_ref, b_ref, o_ref):
    c = c_ref[...]
    sc = c * jax.nn.sigmoid(c)
    o_ref[...] = jnp.dot(sc.astype(BF16), w_ref[...].astype(BF16),
                         preferred_element_type=F32) + b_ref[...]


def _mod_call(c_pad, w, b, tn):
    L, D, N = w.shape
    return pl.pallas_call(
        _mod_kernel,
        out_shape=jax.ShapeDtypeStruct((L, SUBLANES, N), F32),
        grid=(L, N // tn),
        in_specs=[pl.BlockSpec((SUBLANES, D), lambda l, n: (0, 0)),
                  pl.BlockSpec((None, D, tn), lambda l, n: (l, 0, n)),
                  pl.BlockSpec((None, 1, tn), lambda l, n: (l, 0, n))],
        out_specs=pl.BlockSpec((None, SUBLANES, tn), lambda l, n: (l, 0, n)),
        compiler_params=_cparams(("arbitrary", "arbitrary")),
        name="adaln_mod",
    )(c_pad, w, b)


def _rope_kernel(pos_ref, freq_ref, ca_ref, sb_ref):
    ang = pos_ref[...].astype(F32) * freq_ref[...]
    lane = lax.broadcasted_iota(jnp.int32, ang.shape, 1)
    cos, sin = jnp.cos(ang), jnp.sin(ang)
    half = QK_ROPE // 2
    ca_ref[...] = jnp.where(lane < QK_NOPE, 1.0, jnp.where(lane < QK_NOPE + QK_ROPE, cos, 0.0))
    sb_ref[...] = jnp.where(lane < QK_NOPE, 0.0,
                            jnp.where(lane < QK_NOPE + half, -sin,
                                      jnp.where(lane < QK_NOPE + QK_ROPE, sin, 0.0)))


def _rope_call(pos_col, freq_lane):
    T = pos_col.shape[0]
    tm = TM_TOK
    spec = pl.BlockSpec((tm, LANES), lambda i: (i, 0))
    return pl.pallas_call(
        _rope_kernel,
        out_shape=(jax.ShapeDtypeStruct((T, LANES), F32),) * 2,
        grid=(T // tm,),
        in_specs=[pl.BlockSpec((tm, 1), lambda i: (i, 0)), _const_spec((1, LANES))],
        out_specs=(spec, spec),
        compiler_params=_cparams(("arbitrary",)),
        name="rope_tables",
    )(pos_col, freq_lane)


def _shift_rows(x, d, fill):
    n = x.shape[0]
    if d % SUBLANES == 0:
        return jnp.concatenate([jnp.full((d, x.shape[1]), fill, x.dtype), x[:n - d]], axis=0)
    rolled = pltpu.roll(x, d, 0)
    row = lax.broadcasted_iota(jnp.int32, x.shape, 0)
    return jnp.where(row < d, fill, rolled)


def _lru_kernel(x_ref, mod_ref, ng_ref, win_ref, cw_ref, cb_ref, wa_ref, ba_ref, wx_ref, bx_ref,
                lam_ref, wout_ref, o_ref, rec_s, h_s, *, ts):
    D = D_MODEL

    @pl.when(pl.program_id(1) == 0)
    def _():
        rec_s[0:SUBLANES, :] = jnp.zeros((SUBLANES, D), F32)
        h_s[...] = jnp.zeros((SUBLANES, D), F32)

    x = x_ref[...]
    mod = mod_ref[...]
    sh1, sc1, g1 = mod[0:1], mod[1:2], mod[2:3]
    h = _rms(x, ng_ref[...]) * (1.0 + sc1) + sh1
    u = jnp.dot(h.astype(BF16), win_ref[...], preferred_element_type=F32)
    gate = _gelu(u[:, :D])
    rec = u[:, D:]

    rec_s[SUBLANES:ts + SUBLANES, :] = rec
    cw = cw_ref[...]
    xc = (cb_ref[...] + rec * cw[3:4]
          + rec_s[SUBLANES - 1:ts + SUBLANES - 1, :] * cw[2:3]
          + rec_s[SUBLANES - 2:ts + SUBLANES - 2, :] * cw[1:2]
          + rec_s[SUBLANES - 3:ts + SUBLANES - 3, :] * cw[0:1])
    rec_s[0:SUBLANES, :] = rec[ts - SUBLANES:ts, :]

    xcb = xc.astype(BF16)
    r_pre = jnp.concatenate(
        [jnp.dot(xcb[:, k * LRU_BLOCK:(k + 1) * LRU_BLOCK], wa_ref[k], preferred_element_type=F32)
         for k in range(LRU_HEADS)], axis=-1)
    i_pre = jnp.concatenate(
        [jnp.dot(xcb[:, k * LRU_BLOCK:(k + 1) * LRU_BLOCK], wx_ref[k], preferred_element_type=F32)
         for k in range(LRU_HEADS)], axis=-1)
    r = jax.nn.sigmoid(r_pre + ba_ref[...])
    gi = jax.nn.sigmoid(i_pre + bx_ref[...])
    z = -lam_ref[...]
    softplus = jnp.maximum(z, 0.0) + jnp.log1p(jnp.exp(-jnp.abs(z)))
    log_a = (-LRU_C) * r * softplus
    a = jnp.exp(log_a)
    b = jnp.sqrt(-_expm1(2.0 * log_a)) * (gi * xc)

    d = 1
    while d < ts:
        a_sh = _shift_rows(a, d, 1.0)
        b_sh = _shift_rows(b, d, 0.0)
        b = a * b_sh + b
        a = a * a_sh
        d *= 2
    h0 = h_s[SUBLANES - 1:SUBLANES, :]
    hs = a * h0 + b
    h_s[...] = hs[ts - SUBLANES:ts, :]

    y = jnp.dot((gate * hs).astype(BF16), wout_ref[...], preferred_element_type=F32)
    o_ref[...] = x + g1 * y


def _lru_call(x, mod, ng, w_in, cw, cb, wa, ba, wx, bx, lam, w_out):
    B, S, D = x.shape
    ts = TS_LRU
    kern = functools.partial(_lru_kernel, ts=ts)
    return pl.pallas_call(
        kern,
        out_shape=jax.ShapeDtypeStruct((B, S, D), F32),
        grid=(B, S // ts),
        in_specs=[pl.BlockSpec((None, ts, D), lambda b, s: (b, s, 0)),
                  pl.BlockSpec((None, 6, D), lambda b, s: (b, 0, 0)),
                  _const_spec((1, D)), _const_spec((D, 2 * D)), _const_spec((CONV_WIDTH, D)),
                  _const_spec((1, D)), _const_spec((LRU_HEADS, LRU_BLOCK, LRU_BLOCK)), _const_spec((1, D)),
                  _const_spec((LRU_HEADS, LRU_BLOCK, LRU_BLOCK)), _const_spec((1, D)), _const_spec((1, D)),
                  _const_spec((D, D))],
        out_specs=pl.BlockSpec((None, ts, D), lambda b, s: (b, s, 0)),
        scratch_shapes=[pltpu.VMEM((ts + SUBLANES, D), F32), pltpu.VMEM((SUBLANES, D), F32)],
        compiler_params=_cparams(("arbitrary", "arbitrary")),
        name="rglru_layer",
    )(x, mod, ng, w_in, cw, cb, wa, ba, wx, bx, lam, w_out)


def _kv_kernel(x_ref, mod_ref, ng_ref, w1_ref, lg_ref, wuk_ref, wuv_ref, ca_ref, sb_ref, k_ref, v_ref):
    x = x_ref[...]
    mod = mod_ref[...]
    shift, scale = mod[0:1], mod[1:2]
    h = _rms(x, ng_ref[...]) * (1.0 + scale) + shift
    t = jnp.dot(h.astype(BF16), w1_ref[...], preferred_element_type=F32)
    c_kv = _rms(t[:, :KV_LORA], lg_ref[...])
    kr = t[:, KV_LORA:KV_LORA + LANES] * ca_ref[...] + t[:, KV_LORA + LANES:] * sb_ref[...]
    ckb = c_kv.astype(BF16)
    kn = jnp.dot(ckb, wuk_ref[...], preferred_element_type=F32)
    k_ref[...] = (kn + jnp.concatenate([kr] * MLA_HEADS, axis=-1)).astype(BF16)
    v_ref[...] = jnp.dot(ckb, wuv_ref[...], preferred_element_type=F32).T.astype(BF16)


def _kv_call(x2, mod, ng, w1, lg, wuk, wuv, ca, sb, S):
    T, D = x2.shape
    tm = TM_TOK
    nb = S // tm
    HP = MLA_HEADS * HEAD_PAD
    HV = MLA_HEADS * V_HEAD
    return pl.pallas_call(
        _kv_kernel,
        out_shape=(jax.ShapeDtypeStruct((T, HP), BF16), jax.ShapeDtypeStruct((HV, T), BF16)),
        grid=(T // tm,),
        in_specs=[pl.BlockSpec((tm, D), lambda i: (i, 0)),
                  pl.BlockSpec((None, 2, D), lambda i: (i // nb, 0, 0)),
                  _const_spec((1, D)), _const_spec(w1.shape), _const_spec((1, KV_LORA)),
                  _const_spec(wuk.shape), _const_spec(wuv.shape),
                  pl.BlockSpec((tm, LANES), lambda i: (i, 0)), pl.BlockSpec((tm, LANES), lambda i: (i, 0))],
        out_specs=(pl.BlockSpec((tm, HP), lambda i: (i, 0)), pl.BlockSpec((HV, tm), lambda i: (0, i))),
        compiler_params=_cparams(("arbitrary",)),
        name="shared_kv",
    )(x2, mod, ng, w1, lg, wuk, wuv, ca, sb)


def _q_kernel(x_ref, mod_ref, ng_ref, wdq_ref, lg_ref, wa_ref, wb_ref, ca_ref, sb_ref, q_ref):
    x = x_ref[...]
    mod = mod_ref[...]
    sh1, sc1 = mod[0:1], mod[1:2]
    h = _rms(x, ng_ref[...]) * (1.0 + sc1) + sh1
    ql = _rms(jnp.dot(h.astype(BF16), wdq_ref[...], preferred_element_type=F32), lg_ref[...])
    qlb = ql.astype(BF16)
    qa = jnp.dot(qlb, wa_ref[...], preferred_element_type=F32)
    qb = jnp.dot(qlb, wb_ref[...], preferred_element_type=F32)
    ca = jnp.concatenate([ca_ref[...]] * MLA_HEADS, axis=-1)
    sb = jnp.concatenate([sb_ref[...]] * MLA_HEADS, axis=-1)
    q_ref[...] = ((qa * ca + qb * sb) * (ATTN_SCALE * LOG2E)).T.astype(BF16)


def _q_call(x2, mod, ng, wdq, lg, wa, wb, ca, sb, S):
    T, D = x2.shape
    tm = TM_TOK
    nb = S // tm
    HP = MLA_HEADS * HEAD_PAD
    return pl.pallas_call(
        _q_kernel,
        out_shape=jax.ShapeDtypeStruct((HP, T), BF16),
        grid=(T // tm,),
        in_specs=[pl.BlockSpec((tm, D), lambda i: (i, 0)),
                  pl.BlockSpec((None, 6, D), lambda i: (i // nb, 0, 0)),
                  _const_spec((1, D)), _const_spec(wdq.shape), _const_spec((1, Q_LORA)),
                  _const_spec(wa.shape), _const_spec(wb.shape),
                  pl.BlockSpec((tm, LANES), lambda i: (i, 0)), pl.BlockSpec((tm, LANES), lambda i: (i, 0))],
        out_specs=pl.BlockSpec((HP, tm), lambda i: (0, i)),
        compiler_params=_cparams(("arbitrary",)),
        name="mla_queries",
    )(x2, mod, ng, wdq, lg, wa, wb, ca, sb)


def _attn_kernel(nfull_ref, nhi_ref, qt_ref, k_ref, vt_ref, pq_ref, pk_ref, o_ref, *, tq, tk, nq):
    b = pl.program_id(0)
    qi = pl.program_id(2)
    nfull = nfull_ref[b * nq + qi]
    nhi = nhi_ref[b * nq + qi]
    pos_q = pq_ref[...]

    def step(kc, carry, masked):
        off = pl.multiple_of(kc * tk, tk)
        nstream = 2 * (tq // QSUB_ATT)

        def scores(sidx):
            hh, qs = sidx // (tq // QSUB_ATT), sidx % (tq // QSUB_ATT)
            k = k_ref[pl.ds(off, tk), hh * HEAD_PAD:(hh + 1) * HEAD_PAD]
            qt = qt_ref[hh * HEAD_PAD:(hh + 1) * HEAD_PAD, qs * QSUB_ATT:(qs + 1) * QSUB_ATT]
            return jnp.dot(k, qt, preferred_element_type=F32)

        def update(sidx, st):
            hh, qs = sidx // (tq // QSUB_ATT), sidx % (tq // QSUB_ATT)
            m, acc = carry[sidx]
            if masked:
                keep = pk_ref[pl.ds(off, tk), :] <= pos_q[:, qs * QSUB_ATT:(qs + 1) * QSUB_ATT]
                st = jnp.where(keep, st, NEG_INF)
            m_new = jnp.maximum(m, jnp.max(st, axis=0, keepdims=True))
            p = jnp.exp2((st - m_new).astype(BF16))
            alpha = jnp.exp2(m - m_new)
            vt = jnp.concatenate([vt_ref[hh * V_HEAD:(hh + 1) * V_HEAD, pl.ds(off, tk)],
                                  jnp.ones((ONES_ROWS, tk), BF16)], axis=0)
            acc_new = alpha * acc + jnp.dot(vt, p, preferred_element_type=F32)
            return m_new, acc_new

        new = []
        st_next = scores(0)
        for sidx in range(nstream):
            st_cur = st_next
            if sidx + 1 < nstream:
                st_next = scores(sidx + 1)
            new.append(update(sidx, st_cur))
        return tuple(new)

    nqs = tq // QSUB_ATT
    one = (jnp.full((1, QSUB_ATT), NEG_INF, F32), jnp.zeros((V_HEAD + ONES_ROWS, QSUB_ATT), F32))
    carry = lax.fori_loop(0, nfull, functools.partial(step, masked=False), (one,) * (2 * nqs))
    fin = lax.fori_loop(nfull, nhi, functools.partial(step, masked=True), carry)
    heads = [jnp.concatenate([fin[hh * nqs + qs][1][:V_HEAD] / fin[hh * nqs + qs][1][V_HEAD:V_HEAD + 1]
                              for qs in range(nqs)], axis=1) for hh in range(2)]
    o_ref[...] = jnp.concatenate(heads, axis=0).T.astype(BF16)


def _attn_call(nfull, nhi, qt, k, vt, pos_col, pos_row, B, S):
    tq, tk = TQ_ATT, TK_ATT
    nq = S // tq
    T = B * S
    kern = functools.partial(_attn_kernel, tq=tq, tk=tk, nq=nq)
    grid_spec = pltpu.PrefetchScalarGridSpec(
        num_scalar_prefetch=2,
        grid=(B, MLA_HEADS // 2, nq),
        in_specs=[pl.BlockSpec((2 * HEAD_PAD, tq), lambda b, h, i, *_: (h, b * nq + i)),
                  pl.BlockSpec((S, 2 * HEAD_PAD), lambda b, h, i, *_: (b, h)),
                  pl.BlockSpec((2 * V_HEAD, S), lambda b, h, i, *_: (h, b)),
                  pl.BlockSpec((None, 1, tq), lambda b, h, i, *_: (b, 0, i)),
                  pl.BlockSpec((S, 1), lambda b, h, i, *_: (b, 0))],
        out_specs=pl.BlockSpec((tq, 2 * V_HEAD), lambda b, h, i, *_: (b * nq + i, h)),
    )
    return pl.pallas_call(
        kern,
        out_shape=jax.ShapeDtypeStruct((T, MLA_HEADS * V_HEAD), BF16),
        grid_spec=grid_spec,
        compiler_params=_cparams(("arbitrary", "arbitrary", "arbitrary")),
        name="mla_attention",
    )(nfull, nhi, qt, k, vt, pos_row, pos_col)


def _oproj_kernel(x_ref, o_ref, mod_ref, w_ref, out_ref):
    g1 = mod_ref[...][2:3]
    y = jnp.dot(o_ref[...], w_ref[...], preferred_element_type=F32)
    out_ref[...] = x_ref[...] + g1 * y


def _oproj_call(x2, o, mod, w_o, S):
    T, D = x2.shape
    tm = TM_TOK
    nb = S // tm
    return pl.pallas_call(
        _oproj_kernel,
        out_shape=jax.ShapeDtypeStruct((T, D), F32),
        grid=(T // tm,),
        in_specs=[pl.BlockSpec((tm, D), lambda i: (i, 0)),
                  pl.BlockSpec((tm, o.shape[1]), lambda i: (i, 0)),
                  pl.BlockSpec((None, 6, D), lambda i: (i // nb, 0, 0)),
                  _const_spec(w_o.shape)],
        out_specs=pl.BlockSpec((tm, D), lambda i: (i, 0)),
        compiler_params=_cparams(("arbitrary",)),
        name="mla_out_proj",
    )(x2, o, mod, w_o)


TOP_ROWS = 3 * SUBLANES
LEVEL_STEP = 256.0


def _top17_rows(x):
    rows = lax.broadcasted_iota(jnp.int32, (TOP_ROWS, x.shape[1]), 0)
    out = jnp.full((TOP_ROWS, x.shape[1]), NEG_INF, F32)
    for j in range(PEER_TOPK + 1):
        m = jnp.max(x, axis=0, keepdims=True)
        out = jnp.where(rows == j, m, out)
        x = jnp.where(x == m, NEG_INF, x)
    return out


def _count_ge(x, y_rows):
    below = jnp.zeros(x.shape, jnp.int32)
    for j in range(PEER_TOPK + 1):
        diff = lax.bitcast_convert_type(x - y_rows[j:j + 1], jnp.int32)
        below = below + lax.shift_right_logical(diff, 31)
    return float(PEER_TOPK + 1) - below.astype(F32)


def _kth_and_next_largest_rows(x, k):
    shape = (1, x.shape[1])
    tk, tn = jnp.full(shape, NEG_INF, F32), jnp.full(shape, NEG_INF, F32)
    cnt = jnp.zeros(shape, F32)
    for _ in range(k + 1):
        m = jnp.max(x, axis=0, keepdims=True)
        eq = x == m
        tk = jnp.where(cnt < k, m, tk)
        tn = jnp.where(cnt < k + 1, m, tn)
        cnt = cnt + jnp.sum(jnp.where(eq, 1.0, 0.0), axis=0, keepdims=True)
        x = jnp.where(eq, NEG_INF, x)
    return tk, tn


def _peer_kernel(x_ref, mod_ref, ng_ref, wqt_ref, keys_ref, u_ref, vt_ref, fg_ref, o_ref,
                 h_s, st_s, thr_s, e1_s, lv_s, e2_s, wa_s, acc_s, *, tm, ec, final_norm):
    D = D_MODEL
    ntc = tm // LANES
    e = pl.program_id(1)
    ne = pl.num_programs(1)

    @pl.when(e == 0)
    def _scores():
        x = x_ref[...]
        mod = mod_ref[...]
        sh2, sc2 = mod[3:4], mod[4:5]
        ht = (_rms(x, ng_ref[...]) * (1.0 + sc2) + sh2).T.astype(BF16)
        h_s[...] = ht
        qt = jnp.dot(wqt_ref[...], ht, preferred_element_type=F32)
        qtb = qt.astype(BF16)
        for hp in range(2 * PEER_HEADS):
            st = jnp.dot(keys_ref[hp], qtb[hp * PEER_HALF:(hp + 1) * PEER_HALF, :],
                         preferred_element_type=F32)
            for tc in range(ntc):
                st_s[tc, hp] = st[:, tc * LANES:(tc + 1) * LANES]

        def per_block(i, _):
            tc = i // PEER_HEADS
            hd = i % PEER_HEADS
            s1 = st_s[tc, 2 * hd]
            s2 = st_s[tc, 2 * hd + 1]
            a = _top17_rows(s1)
            b = _top17_rows(s2)
            cands = [a[0:1] + b]
            for r in range(1, SUBLANES):
                cands.append(a[r:r + 1] + b[0:SUBLANES])
            cands.append(a[SUBLANES:TOP_ROWS] + b[0:1])
            cand = jnp.concatenate(cands, axis=0)
            c16, c17 = _kth_and_next_largest_rows(cand, PEER_TOPK)
            tau = 0.5 * (c16 + c17)
            m1, m2 = a[0:1], b[0:1]
            zsum = jnp.sum(jnp.where(cand >= tau, jnp.exp(cand - (m1 + m2)), 0.0), axis=0, keepdims=True)
            level2 = _count_ge(s2, b)
            cnt = _count_ge(s1 - tau, -b)
            thr_s[tc, hd] = (float(PEER_TOPK + 1) - cnt) * LEVEL_STEP
            e1_s[tc, hd] = jnp.exp(s1 - m1) / zsum
            lv_s[tc, hd] = (level2 * LEVEL_STEP).astype(BF16)
            e2_s[tc, hd] = jnp.exp(s2 - m2).astype(BF16)
            return 0

        lax.fori_loop(0, ntc * PEER_HEADS, per_block, 0)
        acc_s[...] = jnp.zeros(acc_s.shape, F32)

    ht = h_s[...]
    ngrp = ec // LANES
    nsub = ec // ESUB_PEER
    rows_per_sub = ESUB_PEER // LANES
    key_rows = {}

    def key_row(ref, g8, tc, hd, j):
        if (id(ref), g8, tc, hd) not in key_rows:
            base = pl.multiple_of(e * ngrp + g8 * SUBLANES, SUBLANES)
            key_rows[(id(ref), g8, tc, hd)] = ref[tc, hd, pl.ds(base, SUBLANES), :].astype(BF16)
        return key_rows[(id(ref), g8, tc, hd)][j:j + 1, :]

    def activations(k):
        return jnp.dot(u_ref[k * ESUB_PEER:(k + 1) * ESUB_PEER, :], ht, preferred_element_type=F32)

    def softmax_weights(k):
        for tc in range(ntc):
            w = [None] * rows_per_sub
            for hd in range(PEER_HEADS):
                lv, e2 = lv_s[tc, hd], e2_s[tc, hd]
                for r in range(rows_per_sub):
                    g8, j = divmod(k * rows_per_sub + r, SUBLANES)
                    sel = jnp.minimum(e2, jnp.maximum(lv - key_row(thr_s, g8, tc, hd, j), 0.0))
                    term = sel * key_row(e1_s, g8, tc, hd, j)
                    w[r] = term if w[r] is None else w[r] + term
            for r in range(rows_per_sub):
                r0 = (k * rows_per_sub + r) * LANES
                wa_s[r0:r0 + LANES, tc * LANES:(tc + 1) * LANES] = w[r]

    def apply_activations(k, at):
        rows = slice(k * ESUB_PEER, (k + 1) * ESUB_PEER)
        wa_s[rows, :] = wa_s[rows, :] * _gelu(at).astype(BF16)

    def values(k0, k1):
        cols = slice(k0 * ESUB_PEER, k1 * ESUB_PEER)
        acc_s[...] += jnp.dot(vt_ref[:, cols], wa_s[cols, :], preferred_element_type=F32)

    half = nsub // 2
    at_next = activations(0)
    softmax_weights(0)
    for k in range(nsub):
        at_cur = at_next
        if k + 1 < nsub:
            at_next = activations(k + 1)
        if k == half:
            values(0, half)
        apply_activations(k, at_cur)
        if k + 1 < nsub:
            softmax_weights(k + 1)
    values(half, nsub)

    @pl.when(e == ne - 1)
    def _finish():
        g2 = mod_ref[...][5:6]
        xn = x_ref[...] + g2 * acc_s[...].T
        if final_norm:
            xn = _rms(xn, fg_ref[...])
        o_ref[...] = xn


def _peer_call(x2, mod, ng, wqt, keys, u, vt, fg, S, final_norm):
    T, D = x2.shape
    E = u.shape[0]
    tm, ec = TM_PEER, EC_PEER
    nb = S // tm
    ntc = tm // LANES
    kern = functools.partial(_peer_kernel, tm=tm, ec=ec, final_norm=final_norm)
    return pl.pallas_call(
        kern,
        out_shape=jax.ShapeDtypeStruct((T, D), F32),
        grid=(T // tm, E // ec),
        in_specs=[pl.BlockSpec((tm, D), lambda i, e: (i, 0)),
                  pl.BlockSpec((None, 6, D), lambda i, e: (i // nb, 0, 0)),
                  _const_spec((1, D)), _const_spec(wqt.shape), _const_spec(keys.shape),
                  pl.BlockSpec((ec, D), lambda i, e: (e, 0)),
                  pl.BlockSpec((D, ec), lambda i, e: (0, e)),
                  _const_spec((1, D))],
        out_specs=pl.BlockSpec((tm, D), lambda i, e: (i, 0)),
        scratch_shapes=[pltpu.VMEM((D, tm), BF16),
                        pltpu.VMEM((ntc, 2 * PEER_HEADS, N_KEYS, LANES), F32),
                        pltpu.VMEM((ntc, PEER_HEADS, N_KEYS, LANES), F32),
                        pltpu.VMEM((ntc, PEER_HEADS, N_KEYS, LANES), F32),
                        pltpu.VMEM((ntc, PEER_HEADS, N_KEYS, LANES), BF16),
                        pltpu.VMEM((ntc, PEER_HEADS, N_KEYS, LANES), BF16),
                        pltpu.VMEM((ec, tm), BF16),
                        pltpu.VMEM((D, tm), F32)],
        compiler_params=_cparams(("arbitrary", "arbitrary")),
        name="peer_layer",
    )(x2, mod, ng, wqt, keys, u, vt, fg)


def _head_pad_cols(w_nope, w_rope, n_in):
    zeros = jnp.zeros((n_in, MLA_HEADS, HEAD_PAD - QK_NOPE - QK_ROPE), w_nope.dtype)
    return jnp.concatenate([w_nope, w_rope, zeros], axis=-1).reshape(n_in, MLA_HEADS * HEAD_PAD)


def _swap_halves(w):
    half = QK_ROPE // 2
    return jnp.concatenate([w[..., half:], w[..., :half]], axis=-1)


def kernel(x, c, positions, ada_w, ada_b, norm_mix_g, norm_ffn_g, lru_w_in, lru_conv_w, lru_conv_b, lru_wa, lru_ba, lru_wx, lru_bx, lru_lambda, lru_w_out, kv_ada_w, kv_ada_b, kv_norm_g, mla_w_dkv, mla_w_kr, mla_kv_latent_g, mla_w_uk, mla_w_uv, mla_w_dq, mla_q_latent_g, mla_w_uq, mla_w_o, peer_w_q, peer_sub_keys, peer_u, peer_v, final_g):
    B, S, D = x.shape
    T = B * S
    assert D == D_MODEL and S % TM_TOK == 0 and S % TS_LRU == 0 and S % TQ_ATT == 0 and S % TM_PEER == 0

    c_pad = jnp.zeros((SUBLANES, D), F32).at[:B].set(c)
    mod_all = _mod_call(c_pad, ada_w, ada_b[:, None, :], 1536)
    mod_all = mod_all[:, :B].reshape(DEPTH, B, 6, D)
    mod_kv = _mod_call(c_pad, kv_ada_w[None], kv_ada_b[None, None, :], 1024)[0, :B].reshape(B, 2, D)

    half = QK_ROPE // 2
    freqs = ROPE_THETA ** (-jnp.arange(half, dtype=F32) / half)
    freq_lane = jnp.zeros((1, LANES), F32).at[0, QK_NOPE:QK_NOPE + QK_ROPE].set(jnp.concatenate([freqs, freqs]))
    pos_col = positions.reshape(T, 1)
    ca, sb = _rope_call(pos_col, freq_lane)

    nq, nk = S // TQ_ATT, S // TK_ATT
    pq = positions.reshape(B, nq, TQ_ATT)
    pk = positions.reshape(B, nk, TK_ATT)
    qmin, qmax = pq.min(-1), pq.max(-1)
    kmin, kmax = pk.min(-1), pk.max(-1)
    needed = kmin[:, None, :] <= qmax[:, :, None]
    nhi = jnp.max(jnp.where(needed, jnp.arange(1, nk + 1, dtype=jnp.int32), 0), axis=-1)
    full = kmax[:, None, :] <= qmin[:, :, None]
    nfull = jnp.min(jnp.where(full, nk, jnp.arange(nk, dtype=jnp.int32)), axis=-1)
    nfull = jnp.minimum(nfull, nhi).astype(jnp.int32).reshape(-1)
    nhi = nhi.astype(jnp.int32).reshape(-1)
    pos_row = positions.reshape(B, 1, S)

    def peer_layer(x2, l, final_norm):
        wqt = peer_w_q[l].T.astype(BF16)
        keys = peer_sub_keys[l].reshape(2 * PEER_HEADS, N_KEYS, PEER_HALF).astype(BF16)
        u = peer_u[l].astype(BF16)
        vt = peer_v[l].T.astype(BF16)
        return _peer_call(x2, mod_all[l], norm_ffn_g[l][None], wqt, keys, u, vt, final_g[None], S, final_norm)

    for l in range(N_A_LAYERS):
        x = _lru_call(x, mod_all[l], norm_mix_g[l][None], lru_w_in[l].astype(BF16), lru_conv_w[l],
                      lru_conv_b[l][None], lru_wa[l].astype(BF16), lru_ba[l][None], lru_wx[l].astype(BF16),
                      lru_bx[l][None], lru_lambda[l][None], lru_w_out[l].astype(BF16))
        x = peer_layer(x.reshape(T, D), l, False).reshape(B, S, D)

    x2 = x.reshape(T, D)

    zc = jnp.zeros((D, QK_NOPE), F32)
    zt = jnp.zeros((D, HEAD_PAD - QK_NOPE - QK_ROPE), F32)
    w1 = jnp.concatenate([mla_w_dkv, zc, mla_w_kr, zt, zc, _swap_halves(mla_w_kr), zt], axis=-1).astype(BF16)
    wuk = jnp.concatenate([mla_w_uk.reshape(KV_LORA, MLA_HEADS, QK_NOPE),
                           jnp.zeros((KV_LORA, MLA_HEADS, HEAD_PAD - QK_NOPE), F32)], axis=-1)
    wuk = wuk.reshape(KV_LORA, MLA_HEADS * HEAD_PAD).astype(BF16)
    k_all, v_all = _kv_call(x2, mod_kv, kv_norm_g[None], w1, mla_kv_latent_g[None], wuk,
                            mla_w_uv.astype(BF16), ca, sb, S)

    for j in range(DEPTH - N_A_LAYERS):
        l = N_A_LAYERS + j
        wq = mla_w_uq[j].reshape(Q_LORA, MLA_HEADS, QK_NOPE + QK_ROPE)
        w_nope, w_rope = wq[..., :QK_NOPE], wq[..., QK_NOPE:]
        wa = _head_pad_cols(w_nope, w_rope, Q_LORA).astype(BF16)
        wb = _head_pad_cols(jnp.zeros_like(w_nope), _swap_halves(w_rope), Q_LORA).astype(BF16)
        q = _q_call(x2, mod_all[l], norm_mix_g[l][None], mla_w_dq[j].astype(BF16), mla_q_latent_g[j][None],
                    wa, wb, ca, sb, S)
        o = _attn_call(nfull, nhi, q, k_all, v_all, pos_col, pos_row, B, S)
        x2 = _oproj_call(x2, o, mod_all[l], mla_w_o[j].astype(BF16), S)
        x2 = peer_layer(x2, l, l == DEPTH - 1)

    return x2.reshape(B, S, D)
```

```python
import functools
import math

import jax
import jax.numpy as jnp
from jax import lax
from jax.experimental import pallas as pl
from jax.experimental.pallas import tpu as pltpu

F32 = jnp.float32
BF16 = jnp.bfloat16

D_MODEL = 1024
DEPTH = 4
N_A_LAYERS = DEPTH // 2
LRU_HEADS = 4
LRU_BLOCK = D_MODEL // LRU_HEADS
CONV_WIDTH = 4
LRU_C = 8.0
MLA_HEADS = 16
QK_NOPE = 64
QK_ROPE = 32
V_HEAD = 64
Q_LORA = 384
KV_LORA = 256
ROPE_THETA = 10000.0
ATTN_SCALE = 1.0 / math.sqrt(QK_NOPE + QK_ROPE)
PEER_HEADS = 8
N_KEYS = 128
PEER_TOPK = 16
PEER_HALF = 128
RMS_EPS = 1e-6
NEG_INF = -1e30
LOG2E = math.log2(math.e)

LANES = 128
SUBLANES = 8
HEAD_PAD = LANES
VMEM_LIMIT = 56 * 1024 * 1024

TS_LRU = 256
TM_TOK = 512
TQ_ATT = 512
TK_ATT = 1024
KSUB_ATT = 512
QSUB_ATT = 256
ONES_ROWS = 16
TM_PEER = 512
EC_PEER = 1024
ESUB_PEER = 256


def _cparams(sem):
    return pltpu.CompilerParams(dimension_semantics=sem, vmem_limit_bytes=VMEM_LIMIT)


def _rms(x, g):
    return x * lax.rsqrt(jnp.mean(x * x, axis=-1, keepdims=True) + RMS_EPS) * g


def _gelu(x):
    return 0.5 * x * (1.0 + lax.erf(x * (1.0 / math.sqrt(2.0))))


def _expm1(y):
    series = y * (1.0 + 0.5 * y * (1.0 + (1.0 / 3.0) * y * (1.0 + 0.25 * y * (1.0 + 0.2 * y))))
    return jnp.where(jnp.abs(y) < 0.05, series, jnp.exp(y) - 1.0)


def _const_spec(shape):
    nd = len(shape)
    return pl.BlockSpec(shape, lambda *_: (0,) * nd)


def _mod_kernel(c_ref, w_ref, b_ref, o_ref):
    c = c_ref[...]
    sc = c * jax.nn.sigmoid(c)
    o_ref[...] = jnp.dot(sc.astype(BF16), w_ref[...].astype(BF16),
                         preferred_element_type=F32) + b_ref[...]


def _mod_call(c_pad, w, b, tn):
    L, D, N = w.shape
    return pl.pallas_call(
        _mod_kernel,
        out_shape=jax.ShapeDtypeStruct((L, SUBLANES, N), F32),
        grid=(L, N // tn),
        in_specs=[pl.BlockSpec((SUBLANES, D), lambda l, n: (0, 0)),
                  pl.BlockSpec((None, D, tn), lambda l, n: (l, 0, n)),
                  pl.BlockSpec((None, 1, tn), lambda l, n: (l, 0, n))],
        out_specs=pl.BlockSpec((None, SUBLANES, tn), lambda l, n: (l, 0, n)),
        compiler_params=_cparams(("arbitrary", "arbitrary")),
        name="adaln_mod",
    )(c_pad, w, b)


def _rope_kernel(pos_ref, freq_ref, ca_ref, sb_ref):
    ang = pos_ref[...].astype(F32) * freq_ref[...]
    lane = lax.broadcasted_iota(jnp.int32, ang.shape, 1)
    cos, sin = jnp.cos(ang), jnp.sin(ang)
    half = QK_ROPE // 2
    ca_ref[...] = jnp.where(lane < QK_NOPE, 1.0, jnp.where(lane < QK_NOPE + QK_ROPE, cos, 0.0))
    sb_ref[...] = jnp.where(lane < QK_NOPE, 0.0,
                            jnp.where(lane < QK_NOPE + half, -sin,
                                      jnp.where(lane < QK_NOPE + QK_ROPE, sin, 0.0)))


def _rope_call(pos_col, freq_lane):
    T = pos_col.shape[0]
    tm = TM_TOK
    spec = pl.BlockSpec((tm, LANES), lambda i: (i, 0))
    return pl.pallas_call(
        _rope_kernel,
        out_shape=(jax.ShapeDtypeStruct((T, LANES), F32),) * 2,
        grid=(T // tm,),
        in_specs=[pl.BlockSpec((tm, 1), lambda i: (i, 0)), _const_spec((1, LANES))],
        out_specs=(spec, spec),
        compiler_params=_cparams(("arbitrary",)),
        name="rope_tables",
    )(pos_col, freq_lane)


def _shift_rows(x, d, fill):
    n = x.shape[0]
    if d % SUBLANES == 0:
        return jnp.concatenate([jnp.full((d, x.shape[1]), fill, x.dtype), x[:n - d]], axis=0)
    rolled = pltpu.roll(x, d, 0)
    row = lax.broadcasted_iota(jnp.int32, x.shape, 0)
    return jnp.where(row < d, fill, rolled)


def _lru_kernel(x_ref, mod_ref, ng_ref, win_ref, cw_ref, cb_ref, wa_ref, ba_ref, wx_ref, bx_ref,
                lam_ref, wout_ref, o_ref, rec_s, h_s, *, ts):
    D = D_MODEL

    @pl.when(pl.program_id(1) == 0)
    def _():
        rec_s[0:SUBLANES, :] = jnp.zeros((SUBLANES, D), F32)
        h_s[...] = jnp.zeros((SUBLANES, D), F32)

    x = x_ref[...]
    mod = mod_ref[...]
    sh1, sc1, g1 = mod[0:1], mod[1:2], mod[2:3]
    h = _rms(x, ng_ref[...]) * (1.0 + sc1) + sh1
    u = jnp.dot(h.astype(BF16), win_ref[...], preferred_element_type=F32)
    gate = _gelu(u[:, :D])
    rec = u[:, D:]

    rec_s[SUBLANES:ts + SUBLANES, :] = rec
    cw = cw_ref[...]
    xc = (cb_ref[...] + rec * cw[3:4]
          + rec_s[SUBLANES - 1:ts + SUBLANES - 1, :] * cw[2:3]
          + rec_s[SUBLANES - 2:ts + SUBLANES - 2, :] * cw[1:2]
          + rec_s[SUBLANES - 3:ts + SUBLANES - 3, :] * cw[0:1])
    rec_s[0:SUBLANES, :] = rec[ts - SUBLANES:ts, :]

    xcb = xc.astype(BF16)
    r_pre = jnp.concatenate(
        [jnp.dot(xcb[:, k * LRU_BLOCK:(k + 1) * LRU_BLOCK], wa_ref[k], preferred_element_type=F32)
         for k in range(LRU_HEADS)], axis=-1)
    i_pre = jnp.concatenate(
        [jnp.dot(xcb[:, k * LRU_BLOCK:(k + 1) * LRU_BLOCK], wx_ref[k], preferred_element_type=F32)
         for k in range(LRU_HEADS)], axis=-1)
    r = jax.nn.sigmoid(r_pre + ba_ref[...])
    gi = jax.nn.sigmoid(i_pre + bx_ref[...])
    z = -lam_ref[...]
    softplus = jnp.maximum(z, 0.0) + jnp.log1p(jnp.exp(-jnp.abs(z)))
    log_a = (-LRU_C) * r * softplus
    a = jnp.exp(log_a)
    b = jnp.sqrt(-_expm1(2.0 * log_a)) * (gi * xc)

    d = 1
    while d < ts:
        a_sh = _shift_rows(a, d, 1.0)
        b_sh = _shift_rows(b, d, 0.0)
        b = a * b_sh + b
        a = a * a_sh
        d *= 2
    h0 = h_s[SUBLANES - 1:SUBLANES, :]
    hs = a * h0 + b
    h_s[...] = hs[ts - SUBLANES:ts, :]

    y = jnp.dot((gate * hs).astype(BF16), wout_ref[...], preferred_element_type=F32)
    o_ref[...] = x + g1 * y


def _lru_call(x, mod, ng, w_in, cw, cb, wa, ba, wx, bx, lam, w_out):
    B, S, D = x.shape
    ts = TS_LRU
    kern = functools.partial(_lru_kernel, ts=ts)
    return pl.pallas_call(
        kern,
        out_shape=jax.ShapeDtypeStruct((B, S, D), F32),
        grid=(B, S // ts),
        in_specs=[pl.BlockSpec((None, ts, D), lambda b, s: (b, s, 0)),
                  pl.BlockSpec((None, 6, D), lambda b, s: (b, 0, 0)),
                  _const_spec((1, D)), _const_spec((D, 2 * D)), _const_spec((CONV_WIDTH, D)),
                  _const_spec((1, D)), _const_spec((LRU_HEADS, LRU_BLOCK, LRU_BLOCK)), _const_spec((1, D)),
                  _const_spec((LRU_HEADS, LRU_BLOCK, LRU_BLOCK)), _const_spec((1, D)), _const_spec((1, D)),
                  _const_spec((D, D))],
        out_specs=pl.BlockSpec((None, ts, D), lambda b, s: (b, s, 0)),
        scratch_shapes=[pltpu.VMEM((ts + SUBLANES, D), F32), pltpu.VMEM((SUBLANES, D), F32)],
        compiler_params=_cparams(("arbitrary", "arbitrary")),
        name="rglru_layer",
    )(x, mod, ng, w_in, cw, cb, wa, ba, wx, bx, lam, w_out)


def _kv_kernel(x_ref, mod_ref, ng_ref, w1_ref, lg_ref, wuk_ref, wuv_ref, ca_ref, sb_ref, k_ref, v_ref):
    x = x_ref[...]
    mod = mod_ref[...]
    shift, scale = mod[0:1], mod[1:2]
    h = _rms(x, ng_ref[...]) * (1.0 + scale) + shift
    t = jnp.dot(h.astype(BF16), w1_ref[...], preferred_element_type=F32)
    c_kv = _rms(t[:, :KV_LORA], lg_ref[...])
    kr = t[:, KV_LORA:KV_LORA + LANES] * ca_ref[...] + t[:, KV_LORA + LANES:] * sb_ref[...]
    ckb = c_kv.astype(BF16)
    kn = jnp.dot(ckb, wuk_ref[...], preferred_element_type=F32)
    k_ref[...] = (kn + jnp.concatenate([kr] * MLA_HEADS, axis=-1)).astype(BF16)
    v_ref[...] = jnp.dot(ckb, wuv_ref[...], preferred_element_type=F32).T.astype(BF16)


def _kv_call(x2, mod, ng, w1, lg, wuk, wuv, ca, sb, S):
    T, D = x2.shape
    tm = TM_TOK
    nb = S // tm
    HP = MLA_HEADS * HEAD_PAD
    HV = MLA_HEADS * V_HEAD
    return pl.pallas_call(
        _kv_kernel,
        out_shape=(jax.ShapeDtypeStruct((T, HP), BF16), jax.ShapeDtypeStruct((HV, T), BF16)),
        grid=(T // tm,),
        in_specs=[pl.BlockSpec((tm, D), lambda i: (i, 0)),
                  pl.BlockSpec((None, 2, D), lambda i: (i // nb, 0, 0)),
                  _const_spec((1, D)), _const_spec(w1.shape), _const_spec((1, KV_LORA)),
                  _const_spec(wuk.shape), _const_spec(wuv.shape),
                  pl.BlockSpec((tm, LANES), lambda i: (i, 0)), pl.BlockSpec((tm, LANES), lambda i: (i, 0))],
        out_specs=(pl.BlockSpec((tm, HP), lambda i: (i, 0)), pl.BlockSpec((HV, tm), lambda i: (0, i))),
        compiler_params=_cparams(("arbitrary",)),
        name="shared_kv",
    )(x2, mod, ng, w1, lg, wuk, wuv, ca, sb)


def _q_kernel(x_ref, mod_ref, ng_ref, wdq_ref, lg_ref, wa_ref, wb_ref, ca_ref, sb_ref, q_ref):
    x = x_ref[...]
    mod = mod_ref[...]
    sh1, sc1 = mod[0:1], mod[1:2]
    h = _rms(x, ng_ref[...]) * (1.0 + sc1) + sh1
    ql = _rms(jnp.dot(h.astype(BF16), wdq_ref[...], preferred_element_type=F32), lg_ref[...])
    qlb = ql.astype(BF16)
    qa = jnp.dot(qlb, wa_ref[...], preferred_element_type=F32)
    qb = jnp.dot(qlb, wb_ref[...], preferred_element_type=F32)
    ca = jnp.concatenate([ca_ref[...]] * MLA_HEADS, axis=-1)
    sb = jnp.concatenate([sb_ref[...]] * MLA_HEADS, axis=-1)
    q_ref[...] = ((qa * ca + qb * sb) * (ATTN_SCALE * LOG2E)).T.astype(BF16)


def _q_call(x2, mod, ng, wdq, lg, wa, wb, ca, sb, S):
    T, D = x2.shape
    tm = TM_TOK
    nb = S // tm
    HP = MLA_HEADS * HEAD_PAD
    return pl.pallas_call(
        _q_kernel,
        out_shape=jax.ShapeDtypeStruct((HP, T), BF16),
        grid=(T // tm,),
        in_specs=[pl.BlockSpec((tm, D), lambda i: (i, 0)),
                  pl.BlockSpec((None, 6, D), lambda i: (i // nb, 0, 0)),
                  _const_spec((1, D)), _const_spec(wdq.shape), _const_spec((1, Q_LORA)),
                  _const_spec(wa.shape), _const_spec(wb.shape),
                  pl.BlockSpec((tm, LANES), lambda i: (i, 0)), pl.BlockSpec((tm, LANES), lambda i: (i, 0))],
        out_specs=pl.BlockSpec((HP, tm), lambda i: (0, i)),
        compiler_params=_cparams(("arbitrary",)),
        name="mla_queries",
    )(x2, mod, ng, wdq, lg, wa, wb, ca, sb)


def _attn_kernel(nfull_ref, nhi_ref, qt_ref, k_ref, vt_ref, pq_ref, pk_ref, o_ref, *, tq, tk, nq):
    b = pl.program_id(0)
    qi = pl.program_id(2)
    nfull = nfull_ref[b * nq + qi]
    nhi = nhi_ref[b * nq + qi]
    pos_q = pq_ref[...]

    def step(kc, carry, masked):
        nstate = 2 * (tq // QSUB_ATT)
        nstream = nstate * (tk // KSUB_ATT)
        state = list(carry)

        def split(sidx):
            kb, sid = divmod(sidx, nstate)
            hh, qs = divmod(sid, tq // QSUB_ATT)
            return pl.multiple_of(kc * tk + kb * KSUB_ATT, KSUB_ATT), sid, hh, qs

        def scores(sidx):
            off, _, hh, qs = split(sidx)
            k = k_ref[pl.ds(off, KSUB_ATT), hh * HEAD_PAD:(hh + 1) * HEAD_PAD]
            qt = qt_ref[hh * HEAD_PAD:(hh + 1) * HEAD_PAD, qs * QSUB_ATT:(qs + 1) * QSUB_ATT]
            return jnp.dot(k, qt, preferred_element_type=F32)

        def update(sidx, st):
            off, sid, hh, qs = split(sidx)
            m, acc = state[sid]
            if masked:
                keep = pk_ref[pl.ds(off, KSUB_ATT), :] <= pos_q[:, qs * QSUB_ATT:(qs + 1) * QSUB_ATT]
                st = jnp.where(keep, st, NEG_INF)
            m_new = jnp.maximum(m, jnp.max(st, axis=0, keepdims=True))
            p = jnp.exp2((st - m_new).astype(BF16))
            alpha = jnp.exp2(m - m_new)
            vt = jnp.concatenate([vt_ref[hh * V_HEAD:(hh + 1) * V_HEAD, pl.ds(off, KSUB_ATT)],
                                  jnp.ones((ONES_ROWS, KSUB_ATT), BF16)], axis=0)
            state[sid] = (m_new, alpha * acc + jnp.dot(vt, p, preferred_element_type=F32))

        st_next = scores(0)
        for sidx in range(nstream):
            st_cur = st_next
            if sidx + 1 < nstream:
                st_next = scores(sidx + 1)
            update(sidx, st_cur)
        return tuple(state)

    nqs = tq // QSUB_ATT
    one = (jnp.full((1, QSUB_ATT), NEG_INF, F32), jnp.zeros((V_HEAD + ONES_ROWS, QSUB_ATT), F32))
    carry = lax.fori_loop(0, nfull, functools.partial(step, masked=False), (one,) * (2 * nqs))
    fin = lax.fori_loop(nfull, nhi, functools.partial(step, masked=True), carry)
    heads = [jnp.concatenate([fin[hh * nqs + qs][1][:V_HEAD] / fin[hh * nqs + qs][1][V_HEAD:V_HEAD + 1]
                              for qs in range(nqs)], axis=1) for hh in range(2)]
    o_ref[...] = jnp.concatenate(heads, axis=0).T.astype(BF16)


def _attn_call(nfull, nhi, qt, k, vt, pos_col, pos_row, B, S):
    tq, tk = TQ_ATT, TK_ATT
    nq = S // tq
    T = B * S
    kern = functools.partial(_attn_kernel, tq=tq, tk=tk, nq=nq)
    grid_spec = pltpu.PrefetchScalarGridSpec(
        num_scalar_prefetch=2,
        grid=(B, MLA_HEADS // 2, nq),
        in_specs=[pl.BlockSpec((2 * HEAD_PAD, tq), lambda b, h, i, *_: (h, b * nq + i)),
                  pl.BlockSpec((S, 2 * HEAD_PAD), lambda b, h, i, *_: (b, h)),
                  pl.BlockSpec((2 * V_HEAD, S), lambda b, h, i, *_: (h, b)),
                  pl.BlockSpec((None, 1, tq), lambda b, h, i, *_: (b, 0, i)),
                  pl.BlockSpec((S, 1), lambda b, h, i, *_: (b, 0))],
        out_specs=pl.BlockSpec((tq, 2 * V_HEAD), lambda b, h, i, *_: (b * nq + i, h)),
    )
    return pl.pallas_call(
        kern,
        out_shape=jax.ShapeDtypeStruct((T, MLA_HEADS * V_HEAD), BF16),
        grid_spec=grid_spec,
        compiler_params=_cparams(("arbitrary", "arbitrary", "arbitrary")),
        name="mla_attention",
    )(nfull, nhi, qt, k, vt, pos_row, pos_col)


def _oproj_kernel(x_ref, o_ref, mod_ref, w_ref, out_ref):
    g1 = mod_ref[...][2:3]
    y = jnp.dot(o_ref[...], w_ref[...], preferred_element_type=F32)
    out_ref[...] = x_ref[...] + g1 * y


def _oproj_call(x2, o, mod, w_o, S):
    T, D = x2.shape
    tm = TM_TOK
    nb = S // tm
    return pl.pallas_call(
        _oproj_kernel,
        out_shape=jax.ShapeDtypeStruct((T, D), F32),
        grid=(T // tm,),
        in_specs=[pl.BlockSpec((tm, D), lambda i: (i, 0)),
                  pl.BlockSpec((tm, o.shape[1]), lambda i: (i, 0)),
                  pl.BlockSpec((None, 6, D), lambda i: (i // nb, 0, 0)),
                  _const_spec(w_o.shape)],
        out_specs=pl.BlockSpec((tm, D), lambda i: (i, 0)),
        compiler_params=_cparams(("arbitrary",)),
        name="mla_out_proj",
    )(x2, o, mod, w_o)


TOP_ROWS = 3 * SUBLANES
LEVEL_STEP = 256.0


def _top17_rows(x):
    rows = lax.broadcasted_iota(jnp.int32, (TOP_ROWS, x.shape[1]), 0)
    out = jnp.full((TOP_ROWS, x.shape[1]), NEG_INF, F32)
    for j in range(PEER_TOPK + 1):
        m = jnp.max(x, axis=0, keepdims=True)
        out = jnp.where(rows == j, m, out)
        x = jnp.where(x == m, NEG_INF, x)
    return out


def _count_ge(x, y_rows):
    below = jnp.zeros(x.shape, jnp.int32)
    for j in range(PEER_TOPK + 1):
        diff = lax.bitcast_convert_type(x - y_rows[j:j + 1], jnp.int32)
        below = below + lax.shift_right_logical(diff, 31)
    return float(PEER_TOPK + 1) - below.astype(F32)


def _kth_and_next_largest_rows(x, k):
    shape = (1, x.shape[1])
    tk, tn = jnp.full(shape, NEG_INF, F32), jnp.full(shape, NEG_INF, F32)
    cnt = jnp.zeros(shape, F32)
    for _ in range(k + 1):
        m = jnp.max(x, axis=0, keepdims=True)
        eq = x == m
        tk = jnp.where(cnt < k, m, tk)
        tn = jnp.where(cnt < k + 1, m, tn)
        cnt = cnt + jnp.sum(jnp.where(eq, 1.0, 0.0), axis=0, keepdims=True)
        x = jnp.where(eq, NEG_INF, x)
    return tk, tn


def _peer_kernel(x_ref, mod_ref, ng_ref, wqt_ref, keys_ref, u_ref, vt_ref, fg_ref, o_ref,
                 h_s, st_s, thr_s, e1_s, lv_s, e2_s, wa_s, acc_s, *, tm, ec, final_norm):
    D = D_MODEL
    ntc = tm // LANES
    e = pl.program_id(1)
    ne = pl.num_programs(1)

    @pl.when(e == 0)
    def _scores():
        x = x_ref[...]
        mod = mod_ref[...]
        sh2, sc2 = mod[3:4], mod[4:5]
        ht = (_rms(x, ng_ref[...]) * (1.0 + sc2) + sh2).T.astype(BF16)
        h_s[...] = ht
        qt = jnp.dot(wqt_ref[...], ht, preferred_element_type=F32)
        qtb = qt.astype(BF16)
        for hp in range(2 * PEER_HEADS):
            st = jnp.dot(keys_ref[hp], qtb[hp * PEER_HALF:(hp + 1) * PEER_HALF, :],
                         preferred_element_type=F32)
            for tc in range(ntc):
                st_s[tc, hp] = st[:, tc * LANES:(tc + 1) * LANES]

        def per_block(i, _):
            tc = i // PEER_HEADS
            hd = i % PEER_HEADS
            s1 = st_s[tc, 2 * hd]
            s2 = st_s[tc, 2 * hd + 1]
            a = _top17_rows(s1)
            b = _top17_rows(s2)
            cands = [a[0:1] + b]
            for r in range(1, SUBLANES):
                cands.append(a[r:r + 1] + b[0:SUBLANES])
            cands.append(a[SUBLANES:TOP_ROWS] + b[0:1])
            cand = jnp.concatenate(cands, axis=0)
            c16, c17 = _kth_and_next_largest_rows(cand, PEER_TOPK)
            tau = 0.5 * (c16 + c17)
            m1, m2 = a[0:1], b[0:1]
            zsum = jnp.sum(jnp.where(cand >= tau, jnp.exp(cand - (m1 + m2)), 0.0), axis=0, keepdims=True)
            level2 = _count_ge(s2, b)
            cnt = _count_ge(s1 - tau, -b)
            thr_s[tc, hd] = (float(PEER_TOPK + 1) - cnt) * LEVEL_STEP
            e1_s[tc, hd] = jnp.exp(s1 - m1) / zsum
            lv_s[tc, hd] = (level2 * LEVEL_STEP).astype(BF16)
            e2_s[tc, hd] = jnp.exp(s2 - m2).astype(BF16)
            return 0

        lax.fori_loop(0, ntc * PEER_HEADS, per_block, 0)
        acc_s[...] = jnp.zeros(acc_s.shape, F32)

    ht = h_s[...]
    ngrp = ec // LANES
    nsub = ec // ESUB_PEER
    rows_per_sub = ESUB_PEER // LANES
    key_rows = {}

    def key_row(ref, g8, tc, hd, j):
        if (id(ref), g8, tc, hd) not in key_rows:
            base = pl.multiple_of(e * ngrp + g8 * SUBLANES, SUBLANES)
            key_rows[(id(ref), g8, tc, hd)] = ref[tc, hd, pl.ds(base, SUBLANES), :].astype(BF16)
        return key_rows[(id(ref), g8, tc, hd)][j:j + 1, :]

    def activations(k):
        return jnp.dot(u_ref[k * ESUB_PEER:(k + 1) * ESUB_PEER, :], ht, preferred_element_type=F32)

    def softmax_weights(k):
        for tc in range(ntc):
            w = [None] * rows_per_sub
            for hd in range(PEER_HEADS):
                lv, e2 = lv_s[tc, hd], e2_s[tc, hd]
                for r in range(rows_per_sub):
                    g8, j = divmod(k * rows_per_sub + r, SUBLANES)
                    sel = jnp.minimum(e2, jnp.maximum(lv - key_row(thr_s, g8, tc, hd, j), 0.0))
                    term = sel * key_row(e1_s, g8, tc, hd, j)
                    w[r] = term if w[r] is None else w[r] + term
            for r in range(rows_per_sub):
                r0 = (k * rows_per_sub + r) * LANES
                wa_s[r0:r0 + LANES, tc * LANES:(tc + 1) * LANES] = w[r]

    def apply_activations(k, at):
        rows = slice(k * ESUB_PEER, (k + 1) * ESUB_PEER)
        wa_s[rows, :] = wa_s[rows, :] * _gelu(at).astype(BF16)

    def values(k0, k1):
        cols = slice(k0 * ESUB_PEER, k1 * ESUB_PEER)
        acc_s[...] += jnp.dot(vt_ref[:, cols], wa_s[cols, :], preferred_element_type=F32)

    half = nsub // 2
    at_next = activations(0)
    softmax_weights(0)
    for k in range(nsub):
        at_cur = at_next
        if k + 1 < nsub:
            at_next = activations(k + 1)
        if k == half:
            values(0, half)
        apply_activations(k, at_cur)
        if k + 1 < nsub:
            softmax_weights(k + 1)
    values(half, nsub)

    @pl.when(e == ne - 1)
    def _finish():
        g2 = mod_ref[...][5:6]
        xn = x_ref[...] + g2 * acc_s[...].T
        if final_norm:
            xn = _rms(xn, fg_ref[...])
        o_ref[...] = xn


def _peer_call(x2, mod, ng, wqt, keys, u, vt, fg, S, final_norm):
    T, D = x2.shape
    E = u.shape[0]
    tm, ec = TM_PEER, EC_PEER
    nb = S // tm
    ntc = tm // LANES
    kern = functools.partial(_peer_kernel, tm=tm, ec=ec, final_norm=final_norm)
    return pl.pallas_call(
        kern,
        out_shape=jax.ShapeDtypeStruct((T, D), F32),
        grid=(T // tm, E // ec),
        in_specs=[pl.BlockSpec((tm, D), lambda i, e: (i, 0)),
                  pl.BlockSpec((None, 6, D), lambda i, e: (i // nb, 0, 0)),
                  _const_spec((1, D)), _const_spec(wqt.shape), _const_spec(keys.shape),
                  pl.BlockSpec((ec, D), lambda i, e: (e, 0)),
                  pl.BlockSpec((D, ec), lambda i, e: (0, e)),
                  _const_spec((1, D))],
        out_specs=pl.BlockSpec((tm, D), lambda i, e: (i, 0)),
        scratch_shapes=[pltpu.VMEM((D, tm), BF16),
                        pltpu.VMEM((ntc, 2 * PEER_HEADS, N_KEYS, LANES), F32),
                        pltpu.VMEM((ntc, PEER_HEADS, N_KEYS, LANES), F32),
                        pltpu.VMEM((ntc, PEER_HEADS, N_KEYS, LANES), F32),
                        pltpu.VMEM((ntc, PEER_HEADS, N_KEYS, LANES), BF16),
                        pltpu.VMEM((ntc, PEER_HEADS, N_KEYS, LANES), BF16),
                        pltpu.VMEM((ec, tm), BF16),
                        pltpu.VMEM((D, tm), F32)],
        compiler_params=_cparams(("arbitrary", "arbitrary")),
        name="peer_layer",
    )(x2, mod, ng, wqt, keys, u, vt, fg)


PAD_ROWS = 16


def _peer_kernel_x(x_ref, mod_ref, ng_ref, wqt_ref, keys_ref, u_ref, vt_ref, fg_ref, o_ref,
                   h_s, st_s, thr_s, e1_s, e2f_s, le_s, wa_s, acc_s, *, tm, ec, final_norm, var):
    D = D_MODEL
    ntc = tm // LANES
    e = pl.program_id(1)
    ne = pl.num_programs(1)
    levels, wbuild, nsub, use_ht, pad = var["levels"], var["wbuild"], var["nsub"], var["ht"], var["pad"]
    e2_off = N_KEYS + (PAD_ROWS if pad else 0)

    @pl.when(e == 0)
    def _scores():
        x = x_ref[...]
        mod = mod_ref[...]
        sh2, sc2 = mod[3:4], mod[4:5]
        hn = _rms(x, ng_ref[...]) * (1.0 + sc2) + sh2
        if use_ht:
            ht = hn.T.astype(BF16)
            h_s[...] = ht
            qt = jnp.dot(wqt_ref[...], ht, preferred_element_type=F32)
        else:
            hb = hn.astype(BF16)
            h_s[...] = hb
            qt = lax.dot_general(wqt_ref[...], hb, (((1,), (1,)), ((), ())), preferred_element_type=F32)
        qtb = qt.astype(BF16)
        for hp in range(2 * PEER_HEADS):
            st = jnp.dot(keys_ref[hp], qtb[hp * PEER_HALF:(hp + 1) * PEER_HALF, :],
                         preferred_element_type=F32)
            for tc in range(ntc):
                st_s[tc, hp] = st[:, tc * LANES:(tc + 1) * LANES]

        def per_block(i, _):
            tc = i // PEER_HEADS
            hd = i % PEER_HEADS
            s1 = st_s[tc, 2 * hd]
            s2 = st_s[tc, 2 * hd + 1]
            a = _top17_rows(s1)
            b = _top17_rows(s2)
            cands = [a[0:1] + b]
            for r in range(1, SUBLANES):
                cands.append(a[r:r + 1] + b[0:SUBLANES])
            cands.append(a[SUBLANES:TOP_ROWS] + b[0:1])
            cand = jnp.concatenate(cands, axis=0)
            c16, c17 = _kth_and_next_largest_rows(cand, PEER_TOPK)
            tau = 0.5 * (c16 + c17)
            m1, m2 = a[0:1], b[0:1]
            zsum = jnp.sum(jnp.where(cand >= tau, jnp.exp(cand - (m1 + m2)), 0.0), axis=0, keepdims=True)
            e1_s[tc, hd] = jnp.exp(s1 - m1) / zsum
            if levels:
                level2 = _count_ge(s2, b)
                cnt = _count_ge(s1 - tau, -b)
                thr_s[tc, hd] = (float(PEER_TOPK + 1) - cnt) * LEVEL_STEP
                le_s[tc, hd, 0:N_KEYS, :] = (level2 * LEVEL_STEP).astype(BF16)
                le_s[tc, hd, e2_off:e2_off + N_KEYS, :] = jnp.exp(s2 - m2).astype(BF16)
            else:
                thr_s[tc, hd] = jnp.exp(tau - s1 - m2)
                e2f_s[tc, hd] = jnp.exp(s2 - m2)
            return 0

        lax.fori_loop(0, ntc * PEER_HEADS, per_block, 0)
        acc_s[...] = jnp.zeros(acc_s.shape, F32)

    ngrp = ec // LANES
    esub = ec // nsub
    rows_per_sub = esub // LANES
    key_rows = {}
    wdt = F32 if wbuild == "f32sel" else BF16

    def key_row(ref, g8, tc, hd, j):
        if (id(ref), g8, tc, hd) not in key_rows:
            base = pl.multiple_of(e * ngrp + g8 * SUBLANES, SUBLANES)
            key_rows[(id(ref), g8, tc, hd)] = ref[tc, hd, pl.ds(base, SUBLANES), :].astype(wdt)
        return key_rows[(id(ref), g8, tc, hd)][j:j + 1, :]

    def activations(k):
        if use_ht:
            return jnp.dot(u_ref[k * esub:(k + 1) * esub, :], h_s[...], preferred_element_type=F32)
        return lax.dot_general(u_ref[k * esub:(k + 1) * esub, :], h_s[...], (((1,), (1,)), ((), ())),
                               preferred_element_type=F32)

    def masked_weights(tc, hd, g8, j):
        thr = key_row(thr_s, g8, tc, hd, j)
        if wbuild == "f32sel":
            e2 = e2f_s[tc, hd]
            return jnp.where(e2 >= thr, e2, 0.0)
        lv = le_s[tc, hd, 0:N_KEYS, :]
        e2 = le_s[tc, hd, e2_off:e2_off + N_KEYS, :]
        if wbuild == "bf16sel":
            return jnp.where(lv > thr, e2, 0.0)
        return jnp.minimum(e2, jnp.maximum(lv - thr, 0.0))

    if nsub == 1:
        gact = _gelu(activations(0))
        if wbuild != "f32sel":
            gact = gact.astype(BF16)
        for tc in range(ntc):
            for r in range(rows_per_sub):
                g8, j = divmod(r, SUBLANES)
                w = None
                for hd in range(PEER_HEADS):
                    term = masked_weights(tc, hd, g8, j) * key_row(e1_s, g8, tc, hd, j)
                    w = term if w is None else w + term
                rr = r * LANES
                wa_s[rr:rr + LANES, tc * LANES:(tc + 1) * LANES] = (
                    w * gact[rr:rr + LANES, tc * LANES:(tc + 1) * LANES]).astype(BF16)
        acc_s[...] += jnp.dot(vt_ref[...], wa_s[...], preferred_element_type=F32)
    else:
        def softmax_weights(k):
            for tc in range(ntc):
                w = [None] * rows_per_sub
                for hd in range(PEER_HEADS):
                    for r in range(rows_per_sub):
                        g8, j = divmod(k * rows_per_sub + r, SUBLANES)
                        term = masked_weights(tc, hd, g8, j) * key_row(e1_s, g8, tc, hd, j)
                        w[r] = term if w[r] is None else w[r] + term
                for r in range(rows_per_sub):
                    r0 = (k * rows_per_sub + r) * LANES
                    wa_s[r0:r0 + LANES, tc * LANES:(tc + 1) * LANES] = w[r].astype(BF16)

        def apply_activations(k, at):
            rows = slice(k * esub, (k + 1) * esub)
            wa_s[rows, :] = wa_s[rows, :] * _gelu(at).astype(BF16)

        def values(k0, k1):
            cols = slice(k0 * esub, k1 * esub)
            acc_s[...] += jnp.dot(vt_ref[:, cols], wa_s[cols, :], preferred_element_type=F32)

        half = nsub // 2
        at_next = activations(0)
        softmax_weights(0)
        for k in range(nsub):
            at_cur = at_next
            if k + 1 < nsub:
                at_next = activations(k + 1)
            if k == half:
                values(0, half)
            apply_activations(k, at_cur)
            if k + 1 < nsub:
                softmax_weights(k + 1)
        values(half, nsub)

    @pl.when(e == ne - 1)
    def _finish():
        g2 = mod_ref[...][5:6]
        xn = x_ref[...] + g2 * acc_s[...].T
        if final_norm:
            xn = _rms(xn, fg_ref[...])
        o_ref[...] = xn


def _peer_call_x(x2, mod, ng, wqt, keys, u, vt, fg, S, final_norm, var):
    T, D = x2.shape
    E = u.shape[0]
    tm, ec = TM_PEER, EC_PEER
    nb = S // tm
    ntc = tm // LANES
    kern = functools.partial(_peer_kernel_x, tm=tm, ec=ec, final_norm=final_norm, var=var)
    le_rows = 2 * N_KEYS + 2 * PAD_ROWS
    return pl.pallas_call(
        kern,
        out_shape=jax.ShapeDtypeStruct((T, D), F32),
        grid=(T // tm, E // ec),
        in_specs=[pl.BlockSpec((tm, D), lambda i, e: (i, 0)),
                  pl.BlockSpec((None, 6, D), lambda i, e: (i // nb, 0, 0)),
                  _const_spec((1, D)), _const_spec(wqt.shape), _const_spec(keys.shape),
                  pl.BlockSpec((ec, D), lambda i, e: (e, 0)),
                  pl.BlockSpec((D, ec), lambda i, e: (0, e)),
                  _const_spec((1, D))],
        out_specs=pl.BlockSpec((tm, D), lambda i, e: (i, 0)),
        scratch_shapes=[pltpu.VMEM((D, tm) if var["ht"] else (tm, D), BF16),
                        pltpu.VMEM((ntc, 2 * PEER_HEADS, N_KEYS, LANES), F32),
                        pltpu.VMEM((ntc, PEER_HEADS, N_KEYS, LANES), F32),
                        pltpu.VMEM((ntc, PEER_HEADS, N_KEYS, LANES), F32),
                        pltpu.VMEM((ntc, PEER_HEADS, N_KEYS, LANES), F32),
                        pltpu.VMEM((ntc, PEER_HEADS, le_rows, LANES), BF16),
                        pltpu.VMEM((ec, tm), BF16),
                        pltpu.VMEM((D, tm), F32)],
        compiler_params=_cparams(("arbitrary", "arbitrary")),
        name="peer_layer",
    )(x2, mod, ng, wqt, keys, u, vt, fg)


PEER_VARIANTS = (
    dict(levels=False, wbuild="f32sel", nsub=1, ht=False, pad=False),
    dict(levels=False, wbuild="f32sel", nsub=1, ht=True, pad=False),
    dict(levels=True, wbuild="bf16sel", nsub=1, ht=True, pad=True),
    dict(levels=True, wbuild="bf16arith", nsub=4, ht=True, pad=True),
)


def _head_pad_cols(w_nope, w_rope, n_in):
    zeros = jnp.zeros((n_in, MLA_HEADS, HEAD_PAD - QK_NOPE - QK_ROPE), w_nope.dtype)
    return jnp.concatenate([w_nope, w_rope, zeros], axis=-1).reshape(n_in, MLA_HEADS * HEAD_PAD)


def _swap_halves(w):
    half = QK_ROPE // 2
    return jnp.concatenate([w[..., half:], w[..., :half]], axis=-1)


def kernel(x, c, positions, ada_w, ada_b, norm_mix_g, norm_ffn_g, lru_w_in, lru_conv_w, lru_conv_b, lru_wa, lru_ba, lru_wx, lru_bx, lru_lambda, lru_w_out, kv_ada_w, kv_ada_b, kv_norm_g, mla_w_dkv, mla_w_kr, mla_kv_latent_g, mla_w_uk, mla_w_uv, mla_w_dq, mla_q_latent_g, mla_w_uq, mla_w_o, peer_w_q, peer_sub_keys, peer_u, peer_v, final_g):
    B, S, D = x.shape
    T = B * S
    assert D == D_MODEL and S % TM_TOK == 0 and S % TS_LRU == 0 and S % TQ_ATT == 0 and S % TM_PEER == 0

    c_pad = jnp.zeros((SUBLANES, D), F32).at[:B].set(c)
    mod_all = _mod_call(c_pad, ada_w, ada_b[:, None, :], 1536)
    mod_all = mod_all[:, :B].reshape(DEPTH, B, 6, D)
    mod_kv = _mod_call(c_pad, kv_ada_w[None], kv_ada_b[None, None, :], 1024)[0, :B].reshape(B, 2, D)

    half = QK_ROPE // 2
    freqs = ROPE_THETA ** (-jnp.arange(half, dtype=F32) / half)
    freq_lane = jnp.zeros((1, LANES), F32).at[0, QK_NOPE:QK_NOPE + QK_ROPE].set(jnp.concatenate([freqs, freqs]))
    pos_col = positions.reshape(T, 1)
    ca, sb = _rope_call(pos_col, freq_lane)

    nq, nk = S // TQ_ATT, S // TK_ATT
    pq = positions.reshape(B, nq, TQ_ATT)
    pk = positions.reshape(B, nk, TK_ATT)
    qmin, qmax = pq.min(-1), pq.max(-1)
    kmin, kmax = pk.min(-1), pk.max(-1)
    needed = kmin[:, None, :] <= qmax[:, :, None]
    nhi = jnp.max(jnp.where(needed, jnp.arange(1, nk + 1, dtype=jnp.int32), 0), axis=-1)
    full = kmax[:, None, :] <= qmin[:, :, None]
    nfull = jnp.min(jnp.where(full, nk, jnp.arange(nk, dtype=jnp.int32)), axis=-1)
    nfull = jnp.minimum(nfull, nhi).astype(jnp.int32).reshape(-1)
    nhi = nhi.astype(jnp.int32).reshape(-1)
    pos_row = positions.reshape(B, 1, S)

    def peer_layer(x2, l, final_norm):
        wqt = peer_w_q[l].T.astype(BF16)
        keys = peer_sub_keys[l].reshape(2 * PEER_HEADS, N_KEYS, PEER_HALF).astype(BF16)
        u = peer_u[l].astype(BF16)
        vt = peer_v[l].T.astype(BF16)
        return _peer_call_x(x2, mod_all[l], norm_ffn_g[l][None], wqt, keys, u, vt, final_g[None], S, final_norm,
                            PEER_VARIANTS[l])

    for l in range(N_A_LAYERS):
        x = _lru_call(x, mod_all[l], norm_mix_g[l][None], lru_w_in[l].astype(BF16), lru_conv_w[l],
                      lru_conv_b[l][None], lru_wa[l].astype(BF16), lru_ba[l][None], lru_wx[l].astype(BF16),
                      lru_bx[l][None], lru_lambda[l][None], lru_w_out[l].astype(BF16))
        x = peer_layer(x.reshape(T, D), l, False).reshape(B, S, D)

    x2 = x.reshape(T, D)

    zc = jnp.zeros((D, QK_NOPE), F32)
    zt = jnp.zeros((D, HEAD_PAD - QK_NOPE - QK_ROPE), F32)
    w1 = jnp.concatenate([mla_w_dkv, zc, mla_w_kr, zt, zc, _swap_halves(mla_w_kr), zt], axis=-1).astype(BF16)
    wuk = jnp.concatenate([mla_w_uk.reshape(KV_LORA, MLA_HEADS, QK_NOPE),
                           jnp.zeros((KV_LORA, MLA_HEADS, HEAD_PAD - QK_NOPE), F32)], axis=-1)
    wuk = wuk.reshape(KV_LORA, MLA_HEADS * HEAD_PAD).astype(BF16)
    k_all, v_all = _kv_call(x2, mod_kv, kv_norm_g[None], w1, mla_kv_latent_g[None], wuk,
                            mla_w_uv.astype(BF16), ca, sb, S)

    for j in range(DEPTH - N_A_LAYERS):
        l = N_A_LAYERS + j
        wq = mla_w_uq[j].reshape(Q_LORA, MLA_HEADS, QK_NOPE + QK_ROPE)
        w_nope, w_rope = wq[..., :QK_NOPE], wq[..., QK_NOPE:]
        wa = _head_pad_cols(w_nope, w_rope, Q_LORA).astype(BF16)
        wb = _head_pad_cols(jnp.zeros_like(w_nope), _swap_halves(w_rope), Q_LORA).astype(BF16)
        q = _q_call(x2, mod_all[l], norm_mix_g[l][None], mla_w_dq[j].astype(BF16), mla_q_latent_g[j][None],
                    wa, wb, ca, sb, S)
        o = _attn_call(nfull, nhi, q, k_all, v_all, pos_col, pos_row, B, S)
        x2 = _oproj_call(x2, o, mod_all[l], mla_w_o[j].astype(BF16), S)
        x2 = peer_layer(x2, l, l == DEPTH - 1)

    return x2.reshape(B, S, D)
```

```python
import functools
import math

import jax
import jax.numpy as jnp
from jax import lax
from jax.experimental import pallas as pl
from jax.experimental.pallas import tpu as pltpu

F32 = jnp.float32
BF16 = jnp.bfloat16

D_MODEL = 1024
DEPTH = 4
N_A_LAYERS = DEPTH // 2
LRU_HEADS = 4
LRU_BLOCK = D_MODEL // LRU_HEADS
CONV_WIDTH = 4
LRU_C = 8.0
MLA_HEADS = 16
QK_NOPE = 64
QK_ROPE = 32
V_HEAD = 64
Q_LORA = 384
KV_LORA = 256
ROPE_THETA = 10000.0
ATTN_SCALE = 1.0 / math.sqrt(QK_NOPE + QK_ROPE)
PEER_HEADS = 8
N_KEYS = 128
PEER_TOPK = 16
PEER_HALF = 128
RMS_EPS = 1e-6
NEG_INF = -1e30
LOG2E = math.log2(math.e)

LANES = 128
SUBLANES = 8
HEAD_PAD = LANES
VMEM_LIMIT = 56 * 1024 * 1024

TS_LRU = 256
TM_TOK = 512
TQ_ATT = 512
TK_ATT = 1024
KSUB_ATT = 512
QSUB_ATT = 256
ONES_ROWS = 16
TM_PEER = 512
EC_PEER = 1024
ESUB_PEER = 256


def _cparams(sem):
    return pltpu.CompilerParams(dimension_semantics=sem, vmem_limit_bytes=VMEM_LIMIT)


def _rms(x, g):
    return x * lax.rsqrt(jnp.mean(x * x, axis=-1, keepdims=True) + RMS_EPS) * g


def _gelu(x):
    return 0.5 * x * (1.0 + lax.erf(x * (1.0 / math.sqrt(2.0))))


def _expm1(y):
    series = y * (1.0 + 0.5 * y * (1.0 + (1.0 / 3.0) * y * (1.0 + 0.25 * y * (1.0 + 0.2 * y))))
    return jnp.where(jnp.abs(y) < 0.05, series, jnp.exp(y) - 1.0)


def _const_spec(shape):
    nd = len(shape)
    return pl.BlockSpec(shape, lambda *_: (0,) * nd)


def _mod_kernel(c_ref, w_ref, b_ref, o_ref):
    c = c_ref[...]
    sc = c * jax.nn.sigmoid(c)
    o_ref[...] = jnp.dot(sc.astype(BF16), w_ref[...].astype(BF16),
                         preferred_element_type=F32) + b_ref[...]


def _mod_call(c_pad, w, b, tn):
    L, D, N = w.shape
    return pl.pallas_call(
        _mod_kernel,
        out_shape=jax.ShapeDtypeStruct((L, SUBLANES, N), F32),
        grid=(L, N // tn),
        in_specs=[pl.BlockSpec((SUBLANES, D), lambda l, n: (0, 0)),
                  pl.BlockSpec((None, D, tn), lambda l, n: (l, 0, n)),
                  pl.BlockSpec((None, 1, tn), lambda l, n: (l, 0, n))],
        out_specs=pl.BlockSpec((None, SUBLANES, tn), lambda l, n: (l, 0, n)),
        compiler_params=_cparams(("arbitrary", "arbitrary")),
        name="adaln_mod",
    )(c_pad, w, b)


def _rope_kernel(pos_ref, freq_ref, ca_ref, sb_ref):
    ang = pos_ref[...].astype(F32) * freq_ref[...]
    lane = lax.broadcasted_iota(jnp.int32, ang.shape, 1)
    cos, sin = jnp.cos(ang), jnp.sin(ang)
    half = QK_ROPE // 2
    ca_ref[...] = jnp.where(lane < QK_NOPE, 1.0, jnp.where(lane < QK_NOPE + QK_ROPE, cos, 0.0))
    sb_ref[...] = jnp.where(lane < QK_NOPE, 0.0,
                            jnp.where(lane < QK_NOPE + half, -sin,
                                      jnp.where(lane < QK_NOPE + QK_ROPE, sin, 0.0)))


def _rope_call(pos_col, freq_lane):
    T = pos_col.shape[0]
    tm = TM_TOK
    spec = pl.BlockSpec((tm, LANES), lambda i: (i, 0))
    return pl.pallas_call(
        _rope_kernel,
        out_shape=(jax.ShapeDtypeStruct((T, LANES), F32),) * 2,
        grid=(T // tm,),
        in_specs=[pl.BlockSpec((tm, 1), lambda i: (i, 0)), _const_spec((1, LANES))],
        out_specs=(spec, spec),
        compiler_params=_cparams(("arbitrary",)),
        name="rope_tables",
    )(pos_col, freq_lane)


def _shift_rows(x, d, fill):
    n = x.shape[0]
    if d % SUBLANES == 0:
        return jnp.concatenate([jnp.full((d, x.shape[1]), fill, x.dtype), x[:n - d]], axis=0)
    rolled = pltpu.roll(x, d, 0)
    row = lax.broadcasted_iota(jnp.int32, x.shape, 0)
    return jnp.where(row < d, fill, rolled)


def _lru_kernel(x_ref, mod_ref, ng_ref, win_ref, cw_ref, cb_ref, wa_ref, ba_ref, wx_ref, bx_ref,
                lam_ref, wout_ref, o_ref, rec_s, h_s, *, ts):
    D = D_MODEL

    @pl.when(pl.program_id(1) == 0)
    def _():
        rec_s[0:SUBLANES, :] = jnp.zeros((SUBLANES, D), F32)
        h_s[...] = jnp.zeros((SUBLANES, D), F32)

    x = x_ref[...]
    mod = mod_ref[...]
    sh1, sc1, g1 = mod[0:1], mod[1:2], mod[2:3]
    h = _rms(x, ng_ref[...]) * (1.0 + sc1) + sh1
    u = jnp.dot(h.astype(BF16), win_ref[...], preferred_element_type=F32)
    gate = _gelu(u[:, :D])
    rec = u[:, D:]

    rec_s[SUBLANES:ts + SUBLANES, :] = rec
    cw = cw_ref[...]
    xc = (cb_ref[...] + rec * cw[3:4]
          + rec_s[SUBLANES - 1:ts + SUBLANES - 1, :] * cw[2:3]
          + rec_s[SUBLANES - 2:ts + SUBLANES - 2, :] * cw[1:2]
          + rec_s[SUBLANES - 3:ts + SUBLANES - 3, :] * cw[0:1])
    rec_s[0:SUBLANES, :] = rec[ts - SUBLANES:ts, :]

    xcb = xc.astype(BF16)
    r_pre = jnp.concatenate(
        [jnp.dot(xcb[:, k * LRU_BLOCK:(k + 1) * LRU_BLOCK], wa_ref[k], preferred_element_type=F32)
         for k in range(LRU_HEADS)], axis=-1)
    i_pre = jnp.concatenate(
        [jnp.dot(xcb[:, k * LRU_BLOCK:(k + 1) * LRU_BLOCK], wx_ref[k], preferred_element_type=F32)
         for k in range(LRU_HEADS)], axis=-1)
    r = jax.nn.sigmoid(r_pre + ba_ref[...])
    gi = jax.nn.sigmoid(i_pre + bx_ref[...])
    z = -lam_ref[...]
    softplus = jnp.maximum(z, 0.0) + jnp.log1p(jnp.exp(-jnp.abs(z)))
    log_a = (-LRU_C) * r * softplus
    a = jnp.exp(log_a)
    b = jnp.sqrt(-_expm1(2.0 * log_a)) * (gi * xc)

    d = 1
    while d < ts:
        a_sh = _shift_rows(a, d, 1.0)
        b_sh = _shift_rows(b, d, 0.0)
        b = a * b_sh + b
        a = a * a_sh
        d *= 2
    h0 = h_s[SUBLANES - 1:SUBLANES, :]
    hs = a * h0 + b
    h_s[...] = hs[ts - SUBLANES:ts, :]

    y = jnp.dot((gate * hs).astype(BF16), wout_ref[...], preferred_element_type=F32)
    o_ref[...] = x + g1 * y


def _lru_call(x, mod, ng, w_in, cw, cb, wa, ba, wx, bx, lam, w_out):
    B, S, D = x.shape
    ts = TS_LRU
    kern = functools.partial(_lru_kernel, ts=ts)
    return pl.pallas_call(
        kern,
        out_shape=jax.ShapeDtypeStruct((B, S, D), F32),
        grid=(B, S // ts),
        in_specs=[pl.BlockSpec((None, ts, D), lambda b, s: (b, s, 0)),
                  pl.BlockSpec((None, 6, D), lambda b, s: (b, 0, 0)),
                  _const_spec((1, D)), _const_spec((D, 2 * D)), _const_spec((CONV_WIDTH, D)),
                  _const_spec((1, D)), _const_spec((LRU_HEADS, LRU_BLOCK, LRU_BLOCK)), _const_spec((1, D)),
                  _const_spec((LRU_HEADS, LRU_BLOCK, LRU_BLOCK)), _const_spec((1, D)), _const_spec((1, D)),
                  _const_spec((D, D))],
        out_specs=pl.BlockSpec((None, ts, D), lambda b, s: (b, s, 0)),
        scratch_shapes=[pltpu.VMEM((ts + SUBLANES, D), F32), pltpu.VMEM((SUBLANES, D), F32)],
        compiler_params=_cparams(("arbitrary", "arbitrary")),
        name="rglru_layer",
    )(x, mod, ng, w_in, cw, cb, wa, ba, wx, bx, lam, w_out)


def _kv_kernel(x_ref, mod_ref, ng_ref, w1_ref, lg_ref, wuk_ref, wuv_ref, ca_ref, sb_ref, k_ref, v_ref):
    x = x_ref[...]
    mod = mod_ref[...]
    shift, scale = mod[0:1], mod[1:2]
    h = _rms(x, ng_ref[...]) * (1.0 + scale) + shift
    t = jnp.dot(h.astype(BF16), w1_ref[...], preferred_element_type=F32)
    c_kv = _rms(t[:, :KV_LORA], lg_ref[...])
    kr = t[:, KV_LORA:KV_LORA + LANES] * ca_ref[...] + t[:, KV_LORA + LANES:] * sb_ref[...]
    ckb = c_kv.astype(BF16)
    kn = jnp.dot(ckb, wuk_ref[...], preferred_element_type=F32)
    k_ref[...] = (kn + jnp.concatenate([kr] * MLA_HEADS, axis=-1)).astype(BF16)
    v_ref[...] = jnp.dot(ckb, wuv_ref[...], preferred_element_type=F32).T.astype(BF16)


def _kv_call(x2, mod, ng, w1, lg, wuk, wuv, ca, sb, S):
    T, D = x2.shape
    tm = TM_TOK
    nb = S // tm
    HP = MLA_HEADS * HEAD_PAD
    HV = MLA_HEADS * V_HEAD
    return pl.pallas_call(
        _kv_kernel,
        out_shape=(jax.ShapeDtypeStruct((T, HP), BF16), jax.ShapeDtypeStruct((HV, T), BF16)),
        grid=(T // tm,),
        in_specs=[pl.BlockSpec((tm, D), lambda i: (i, 0)),
                  pl.BlockSpec((None, 2, D), lambda i: (i // nb, 0, 0)),
                  _const_spec((1, D)), _const_spec(w1.shape), _const_spec((1, KV_LORA)),
                  _const_spec(wuk.shape), _const_spec(wuv.shape),
                  pl.BlockSpec((tm, LANES), lambda i: (i, 0)), pl.BlockSpec((tm, LANES), lambda i: (i, 0))],
        out_specs=(pl.BlockSpec((tm, HP), lambda i: (i, 0)), pl.BlockSpec((HV, tm), lambda i: (0, i))),
        compiler_params=_cparams(("arbitrary",)),
        name="shared_kv",
    )(x2, mod, ng, w1, lg, wuk, wuv, ca, sb)


def _q_kernel(x_ref, mod_ref, ng_ref, wdq_ref, lg_ref, wa_ref, wb_ref, ca_ref, sb_ref, q_ref):
    x = x_ref[...]
    mod = mod_ref[...]
    sh1, sc1 = mod[0:1], mod[1:2]
    h = _rms(x, ng_ref[...]) * (1.0 + sc1) + sh1
    ql = _rms(jnp.dot(h.astype(BF16), wdq_ref[...], preferred_element_type=F32), lg_ref[...])
    qlb = ql.astype(BF16)
    qa = jnp.dot(qlb, wa_ref[...], preferred_element_type=F32)
    qb = jnp.dot(qlb, wb_ref[...], preferred_element_type=F32)
    ca = jnp.concatenate([ca_ref[...]] * MLA_HEADS, axis=-1)
    sb = jnp.concatenate([sb_ref[...]] * MLA_HEADS, axis=-1)
    q_ref[...] = ((qa * ca + qb * sb) * (ATTN_SCALE * LOG2E)).T.astype(BF16)


def _q_call(x2, mod, ng, wdq, lg, wa, wb, ca, sb, S):
    T, D = x2.shape
    tm = TM_TOK
    nb = S // tm
    HP = MLA_HEADS * HEAD_PAD
    return pl.pallas_call(
        _q_kernel,
        out_shape=jax.ShapeDtypeStruct((HP, T), BF16),
        grid=(T // tm,),
        in_specs=[pl.BlockSpec((tm, D), lambda i: (i, 0)),
                  pl.BlockSpec((None, 6, D), lambda i: (i // nb, 0, 0)),
                  _const_spec((1, D)), _const_spec(wdq.shape), _const_spec((1, Q_LORA)),
                  _const_spec(wa.shape), _const_spec(wb.shape),
                  pl.BlockSpec((tm, LANES), lambda i: (i, 0)), pl.BlockSpec((tm, LANES), lambda i: (i, 0))],
        out_specs=pl.BlockSpec((HP, tm), lambda i: (0, i)),
        compiler_params=_cparams(("arbitrary",)),
        name="mla_queries",
    )(x2, mod, ng, wdq, lg, wa, wb, ca, sb)


def _attn_kernel(nfull_ref, nhi_ref, qt_ref, k_ref, vt_ref, pq_ref, pk_ref, o_ref, *, tq, tk, nq):
    b = pl.program_id(0)
    qi = pl.program_id(2)
    nfull = nfull_ref[b * nq + qi]
    nhi = nhi_ref[b * nq + qi]
    pos_q = pq_ref[...]

    def step(kc, carry, masked):
        nstate = 2 * (tq // QSUB_ATT)
        nstream = nstate * (tk // KSUB_ATT)
        state = list(carry)

        def split(sidx):
            kb, sid = divmod(sidx, nstate)
            hh, qs = divmod(sid, tq // QSUB_ATT)
            return pl.multiple_of(kc * tk + kb * KSUB_ATT, KSUB_ATT), sid, hh, qs

        def scores(sidx):
            off, _, hh, qs = split(sidx)
            k = k_ref[pl.ds(off, KSUB_ATT), hh * HEAD_PAD:(hh + 1) * HEAD_PAD]
            qt = qt_ref[hh * HEAD_PAD:(hh + 1) * HEAD_PAD, qs * QSUB_ATT:(qs + 1) * QSUB_ATT]
            return jnp.dot(k, qt, preferred_element_type=F32)

        def update(sidx, st):
            off, sid, hh, qs = split(sidx)
            m, acc = state[sid]
            if masked:
                keep = pk_ref[pl.ds(off, KSUB_ATT), :] <= pos_q[:, qs * QSUB_ATT:(qs + 1) * QSUB_ATT]
                st = jnp.where(keep, st, NEG_INF)
            m_new = jnp.maximum(m, jnp.max(st, axis=0, keepdims=True))
            p = jnp.exp2((st - m_new).astype(BF16))
            alpha = jnp.exp2(m - m_new)
            vt = jnp.concatenate([vt_ref[hh * V_HEAD:(hh + 1) * V_HEAD, pl.ds(off, KSUB_ATT)],
                                  jnp.ones((ONES_ROWS, KSUB_ATT), BF16)], axis=0)
            state[sid] = (m_new, alpha * acc + jnp.dot(vt, p, preferred_element_type=F32))

        st_next = scores(0)
        for sidx in range(nstream):
            st_cur = st_next
            if sidx + 1 < nstream:
                st_next = scores(sidx + 1)
            update(sidx, st_cur)
        return tuple(state)

    nqs = tq // QSUB_ATT
    one = (jnp.full((1, QSUB_ATT), NEG_INF, F32), jnp.zeros((V_HEAD + ONES_ROWS, QSUB_ATT), F32))
    carry = lax.fori_loop(0, nfull, functools.partial(step, masked=False), (one,) * (2 * nqs))
    fin = lax.fori_loop(nfull, nhi, functools.partial(step, masked=True), carry)
    heads = [jnp.concatenate([fin[hh * nqs + qs][1][:V_HEAD] / fin[hh * nqs + qs][1][V_HEAD:V_HEAD + 1]
                              for qs in range(nqs)], axis=1) for hh in range(2)]
    o_ref[...] = jnp.concatenate(heads, axis=0).T.astype(BF16)


def _attn_call(nfull, nhi, qt, k, vt, pos_col, pos_row, B, S):
    tq, tk = TQ_ATT, TK_ATT
    nq = S // tq
    T = B * S
    kern = functools.partial(_attn_kernel, tq=tq, tk=tk, nq=nq)
    grid_spec = pltpu.PrefetchScalarGridSpec(
        num_scalar_prefetch=2,
        grid=(B, MLA_HEADS // 2, nq),
        in_specs=[pl.BlockSpec((2 * HEAD_PAD, tq), lambda b, h, i, *_: (h, b * nq + i)),
                  pl.BlockSpec((S, 2 * HEAD_PAD), lambda b, h, i, *_: (b, h)),
                  pl.BlockSpec((2 * V_HEAD, S), lambda b, h, i, *_: (h, b)),
                  pl.BlockSpec((None, 1, tq), lambda b, h, i, *_: (b, 0, i)),
                  pl.BlockSpec((S, 1), lambda b, h, i, *_: (b, 0))],
        out_specs=pl.BlockSpec((tq, 2 * V_HEAD), lambda b, h, i, *_: (b * nq + i, h)),
    )
    return pl.pallas_call(
        kern,
        out_shape=jax.ShapeDtypeStruct((T, MLA_HEADS * V_HEAD), BF16),
        grid_spec=grid_spec,
        compiler_params=_cparams(("arbitrary", "arbitrary", "arbitrary")),
        name="mla_attention",
    )(nfull, nhi, qt, k, vt, pos_row, pos_col)


def _oproj_kernel(x_ref, o_ref, mod_ref, w_ref, out_ref):
    g1 = mod_ref[...][2:3]
    y = jnp.dot(o_ref[...], w_ref[...], preferred_element_type=F32)
    out_ref[...] = x_ref[...] + g1 * y


def _oproj_call(x2, o, mod, w_o, S):
    T, D = x2.shape
    tm = TM_TOK
    nb = S // tm
    return pl.pallas_call(
        _oproj_kernel,
        out_shape=jax.ShapeDtypeStruct((T, D), F32),
        grid=(T // tm,),
        in_specs=[pl.BlockSpec((tm, D), lambda i: (i, 0)),
                  pl.BlockSpec((tm, o.shape[1]), lambda i: (i, 0)),
                  pl.BlockSpec((None, 6, D), lambda i: (i // nb, 0, 0)),
                  _const_spec(w_o.shape)],
        out_specs=pl.BlockSpec((tm, D), lambda i: (i, 0)),
        compiler_params=_cparams(("arbitrary",)),
        name="mla_out_proj",
    )(x2, o, mod, w_o)


TOP_ROWS = 3 * SUBLANES
LEVEL_STEP = 256.0


def _top17_rows(x):
    rows = lax.broadcasted_iota(jnp.int32, (TOP_ROWS, x.shape[1]), 0)
    out = jnp.full((TOP_ROWS, x.shape[1]), NEG_INF, F32)
    for j in range(PEER_TOPK + 1):
        m = jnp.max(x, axis=0, keepdims=True)
        out = jnp.where(rows == j, m, out)
        x = jnp.where(x == m, NEG_INF, x)
    return out


def _count_ge(x, y_rows):
    below = jnp.zeros(x.shape, jnp.int32)
    for j in range(PEER_TOPK + 1):
        diff = lax.bitcast_convert_type(x - y_rows[j:j + 1], jnp.int32)
        below = below + lax.shift_right_logical(diff, 31)
    return float(PEER_TOPK + 1) - below.astype(F32)


def _kth_and_next_largest_rows(x, k):
    shape = (1, x.shape[1])
    tk, tn = jnp.full(shape, NEG_INF, F32), jnp.full(shape, NEG_INF, F32)
    cnt = jnp.zeros(shape, F32)
    for _ in range(k + 1):
        m = jnp.max(x, axis=0, keepdims=True)
        eq = x == m
        tk = jnp.where(cnt < k, m, tk)
        tn = jnp.where(cnt < k + 1, m, tn)
        cnt = cnt + jnp.sum(jnp.where(eq, 1.0, 0.0), axis=0, keepdims=True)
        x = jnp.where(eq, NEG_INF, x)
    return tk, tn


def _peer_kernel(x_ref, mod_ref, ng_ref, wqt_ref, keys_ref, u_ref, vt_ref, fg_ref, o_ref,
                 h_s, st_s, thr_s, e1_s, lv_s, e2_s, wa_s, acc_s, *, tm, ec, final_norm):
    D = D_MODEL
    ntc = tm // LANES
    e = pl.program_id(1)
    ne = pl.num_programs(1)

    @pl.when(e == 0)
    def _scores():
        x = x_ref[...]
        mod = mod_ref[...]
        sh2, sc2 = mod[3:4], mod[4:5]
        ht = (_rms(x, ng_ref[...]) * (1.0 + sc2) + sh2).T.astype(BF16)
        h_s[...] = ht
        qt = jnp.dot(wqt_ref[...], ht, preferred_element_type=F32)
        qtb = qt.astype(BF16)
        for hp in range(2 * PEER_HEADS):
            st = jnp.dot(keys_ref[hp], qtb[hp * PEER_HALF:(hp + 1) * PEER_HALF, :],
                         preferred_element_type=F32)
            for tc in range(ntc):
                st_s[tc, hp] = st[:, tc * LANES:(tc + 1) * LANES]

        def per_block(i, _):
            tc = i // PEER_HEADS
            hd = i % PEER_HEADS
            s1 = st_s[tc, 2 * hd]
            s2 = st_s[tc, 2 * hd + 1]
            a = _top17_rows(s1)
            b = _top17_rows(s2)
            cands = [a[0:1] + b]
            for r in range(1, SUBLANES):
                cands.append(a[r:r + 1] + b[0:SUBLANES])
            cands.append(a[SUBLANES:TOP_ROWS] + b[0:1])
            cand = jnp.concatenate(cands, axis=0)
            c16, c17 = _kth_and_next_largest_rows(cand, PEER_TOPK)
            tau = 0.5 * (c16 + c17)
            m1, m2 = a[0:1], b[0:1]
            zsum = jnp.sum(jnp.where(cand >= tau, jnp.exp(cand - (m1 + m2)), 0.0), axis=0, keepdims=True)
            level2 = _count_ge(s2, b)
            cnt = _count_ge(s1 - tau, -b)
            thr_s[tc, hd] = (float(PEER_TOPK + 1) - cnt) * LEVEL_STEP
            e1_s[tc, hd] = jnp.exp(s1 - m1) / zsum
            lv_s[tc, hd] = (level2 * LEVEL_STEP).astype(BF16)
            e2_s[tc, hd] = jnp.exp(s2 - m2).astype(BF16)
            return 0

        lax.fori_loop(0, ntc * PEER_HEADS, per_block, 0)
        acc_s[...] = jnp.zeros(acc_s.shape, F32)

    ht = h_s[...]
    ngrp = ec // LANES
    nsub = ec // ESUB_PEER
    rows_per_sub = ESUB_PEER // LANES
    key_rows = {}

    def key_row(ref, g8, tc, hd, j):
        if (id(ref), g8, tc, hd) not in key_rows:
            base = pl.multiple_of(e * ngrp + g8 * SUBLANES, SUBLANES)
            key_rows[(id(ref), g8, tc, hd)] = ref[tc, hd, pl.ds(base, SUBLANES), :].astype(BF16)
        return key_rows[(id(ref), g8, tc, hd)][j:j + 1, :]

    def activations(k):
        return jnp.dot(u_ref[k * ESUB_PEER:(k + 1) * ESUB_PEER, :], ht, preferred_element_type=F32)

    def softmax_weights(k):
        for tc in range(ntc):
            w = [None] * rows_per_sub
            for hd in range(PEER_HEADS):
                lv, e2 = lv_s[tc, hd], e2_s[tc, hd]
                for r in range(rows_per_sub):
                    g8, j = divmod(k * rows_per_sub + r, SUBLANES)
                    sel = jnp.minimum(e2, jnp.maximum(lv - key_row(thr_s, g8, tc, hd, j), 0.0))
                    term = sel * key_row(e1_s, g8, tc, hd, j)
                    w[r] = term if w[r] is None else w[r] + term
            for r in range(rows_per_sub):
                r0 = (k * rows_per_sub + r) * LANES
                wa_s[r0:r0 + LANES, tc * LANES:(tc + 1) * LANES] = w[r]

    def apply_activations(k, at):
        rows = slice(k * ESUB_PEER, (k + 1) * ESUB_PEER)
        wa_s[rows, :] = wa_s[rows, :] * _gelu(at).astype(BF16)

    def values(k0, k1):
        cols = slice(k0 * ESUB_PEER, k1 * ESUB_PEER)
        acc_s[...] += jnp.dot(vt_ref[:, cols], wa_s[cols, :], preferred_element_type=F32)

    half = nsub // 2
    at_next = activations(0)
    softmax_weights(0)
    for k in range(nsub):
        at_cur = at_next
        if k + 1 < nsub:
            at_next = activations(k + 1)
        if k == half:
            values(0, half)
        apply_activations(k, at_cur)
        if k + 1 < nsub:
            softmax_weights(k + 1)
    values(half, nsub)

    @pl.when(e == ne - 1)
    def _finish():
        g2 = mod_ref[...][5:6]
        xn = x_ref[...] + g2 * acc_s[...].T
        if final_norm:
            xn = _rms(xn, fg_ref[...])
        o_ref[...] = xn


def _peer_call(x2, mod, ng, wqt, keys, u, vt, fg, S, final_norm):
    T, D = x2.shape
    E = u.shape[0]
    tm, ec = TM_PEER, EC_PEER
    nb = S // tm
    ntc = tm // LANES
    kern = functools.partial(_peer_kernel, tm=tm, ec=ec, final_norm=final_norm)
    return pl.pallas_call(
        kern,
        out_shape=jax.ShapeDtypeStruct((T, D), F32),
        grid=(T // tm, E // ec),
        in_specs=[pl.BlockSpec((tm, D), lambda i, e: (i, 0)),
                  pl.BlockSpec((None, 6, D), lambda i, e: (i // nb, 0, 0)),
                  _const_spec((1, D)), _const_spec(wqt.shape), _const_spec(keys.shape),
                  pl.BlockSpec((ec, D), lambda i, e: (e, 0)),
                  pl.BlockSpec((D, ec), lambda i, e: (0, e)),
                  _const_spec((1, D))],
        out_specs=pl.BlockSpec((tm, D), lambda i, e: (i, 0)),
        scratch_shapes=[pltpu.VMEM((D, tm), BF16),
                        pltpu.VMEM((ntc, 2 * PEER_HEADS, N_KEYS, LANES), F32),
                        pltpu.VMEM((ntc, PEER_HEADS, N_KEYS, LANES), F32),
                        pltpu.VMEM((ntc, PEER_HEADS, N_KEYS, LANES), F32),
                        pltpu.VMEM((ntc, PEER_HEADS, N_KEYS, LANES), BF16),
                        pltpu.VMEM((ntc, PEER_HEADS, N_KEYS, LANES), BF16),
                        pltpu.VMEM((ec, tm), BF16),
                        pltpu.VMEM((D, tm), F32)],
        compiler_params=_cparams(("arbitrary", "arbitrary")),
        name="peer_layer",
    )(x2, mod, ng, wqt, keys, u, vt, fg)


PAD_ROWS = 16


def _peer_kernel_x(x_ref, mod_ref, ng_ref, wqt_ref, keys_ref, u_ref, vt_ref, fg_ref, o_ref,
                   h_s, st_s, thr_s, e1_s, e2f_s, le_s, wa_s, acc_s, *, tm, ec, final_norm, var):
    D = D_MODEL
    ntc = tm // LANES
    e = pl.program_id(1)
    ne = pl.num_programs(1)
    levels, wbuild, nsub, use_ht, pad = var["levels"], var["wbuild"], var["nsub"], var["ht"], var["pad"]
    e2_off = N_KEYS + (PAD_ROWS if pad else 0)

    @pl.when(e == 0)
    def _scores():
        x = x_ref[...]
        mod = mod_ref[...]
        sh2, sc2 = mod[3:4], mod[4:5]
        hn = _rms(x, ng_ref[...]) * (1.0 + sc2) + sh2
        if use_ht:
            ht = hn.T.astype(BF16)
            h_s[...] = ht
            qt = jnp.dot(wqt_ref[...], ht, preferred_element_type=F32)
        else:
            hb = hn.astype(BF16)
            h_s[...] = hb
            qt = lax.dot_general(wqt_ref[...], hb, (((1,), (1,)), ((), ())), preferred_element_type=F32)
        qtb = qt.astype(BF16)
        for hp in range(2 * PEER_HEADS):
            st = jnp.dot(keys_ref[hp], qtb[hp * PEER_HALF:(hp + 1) * PEER_HALF, :],
                         preferred_element_type=F32)
            for tc in range(ntc):
                st_s[tc, hp] = st[:, tc * LANES:(tc + 1) * LANES]

        def per_block(i, _):
            tc = i // PEER_HEADS
            hd = i % PEER_HEADS
            s1 = st_s[tc, 2 * hd]
            s2 = st_s[tc, 2 * hd + 1]
            a = _top17_rows(s1)
            b = _top17_rows(s2)
            cands = [a[0:1] + b]
            for r in range(1, SUBLANES):
                cands.append(a[r:r + 1] + b[0:SUBLANES])
            cands.append(a[SUBLANES:TOP_ROWS] + b[0:1])
            cand = jnp.concatenate(cands, axis=0)
            c16, c17 = _kth_and_next_largest_rows(cand, PEER_TOPK)
            tau = 0.5 * (c16 + c17)
            m1, m2 = a[0:1], b[0:1]
            zsum = jnp.sum(jnp.where(cand >= tau, jnp.exp(cand - (m1 + m2)), 0.0), axis=0, keepdims=True)
            e1_s[tc, hd] = jnp.exp(s1 - m1) / zsum
            if levels:
                level2 = _count_ge(s2, b)
                cnt = _count_ge(s1 - tau, -b)
                thr_s[tc, hd] = (float(PEER_TOPK + 1) - cnt) * LEVEL_STEP
                le_s[tc, hd, 0:N_KEYS, :] = (level2 * LEVEL_STEP).astype(BF16)
                le_s[tc, hd, e2_off:e2_off + N_KEYS, :] = jnp.exp(s2 - m2).astype(BF16)
            else:
                thr_s[tc, hd] = jnp.exp(tau - s1 - m2)
                e2f_s[tc, hd] = jnp.exp(s2 - m2)
            return 0

        lax.fori_loop(0, ntc * PEER_HEADS, per_block, 0)
        acc_s[...] = jnp.zeros(acc_s.shape, F32)

    ngrp = ec // LANES
    esub = ec // nsub
    rows_per_sub = esub // LANES
    key_rows = {}
    wdt = F32 if wbuild.startswith("f32") else BF16

    def key_row(ref, g8, tc, hd, j):
        if (id(ref), g8, tc, hd) not in key_rows:
            base = pl.multiple_of(e * ngrp + g8 * SUBLANES, SUBLANES)
            key_rows[(id(ref), g8, tc, hd)] = ref[tc, hd, pl.ds(base, SUBLANES), :].astype(wdt)
        return key_rows[(id(ref), g8, tc, hd)][j:j + 1, :]

    def activations(k):
        if use_ht:
            return jnp.dot(u_ref[k * esub:(k + 1) * esub, :], h_s[...], preferred_element_type=F32)
        return lax.dot_general(u_ref[k * esub:(k + 1) * esub, :], h_s[...], (((1,), (1,)), ((), ())),
                               preferred_element_type=F32)

    def masked_weights(tc, hd, g8, j):
        thr = key_row(thr_s, g8, tc, hd, j)
        if wbuild in ("f32sel", "f32mix", "f32bits"):
            e2 = e2f_s[tc, hd]
            if wbuild == "f32sel" or (wbuild == "f32mix" and hd < PEER_HEADS // 2):
                return jnp.where(e2 >= thr, e2, 0.0)
            keep = lax.shift_right_arithmetic(lax.bitcast_convert_type(thr - e2, jnp.int32), 31)
            return lax.bitcast_convert_type(lax.bitcast_convert_type(e2, jnp.int32) & keep, F32)
        lv = le_s[tc, hd, 0:N_KEYS, :]
        e2 = le_s[tc, hd, e2_off:e2_off + N_KEYS, :]
        if wbuild == "bf16sel":
            return jnp.where(lv > thr, e2, 0.0)
        return jnp.minimum(e2, jnp.maximum(lv - thr, 0.0))

    if nsub == 1:
        gact = _gelu(activations(0))
        if not wbuild.startswith("f32"):
            gact = gact.astype(BF16)
        for tc in range(ntc):
            for r in range(rows_per_sub):
                g8, j = divmod(r, SUBLANES)
                w = None
                for hd in range(PEER_HEADS):
                    term = masked_weights(tc, hd, g8, j) * key_row(e1_s, g8, tc, hd, j)
                    w = term if w is None else w + term
                rr = r * LANES
                wa_s[rr:rr + LANES, tc * LANES:(tc + 1) * LANES] = (
                    w * gact[rr:rr + LANES, tc * LANES:(tc + 1) * LANES]).astype(BF16)
        acc_s[...] += jnp.dot(vt_ref[...], wa_s[...], preferred_element_type=F32)
    else:
        def weigh(k, at):
            gact = _gelu(at)
            if not wbuild.startswith("f32"):
                gact = gact.astype(BF16)
            for tc in range(ntc):
                for r in range(rows_per_sub):
                    g8, j = divmod(k * rows_per_sub + r, SUBLANES)
                    w = None
                    for hd in range(PEER_HEADS):
                        term = masked_weights(tc, hd, g8, j) * key_row(e1_s, g8, tc, hd, j)
                        w = term if w is None else w + term
                    r0 = (k * rows_per_sub + r) * LANES
                    wa_s[r0:r0 + LANES, tc * LANES:(tc + 1) * LANES] = (
                        w * gact[r * LANES:(r + 1) * LANES, tc * LANES:(tc + 1) * LANES]).astype(BF16)

        def values(k0, k1):
            cols = slice(k0 * esub, k1 * esub)
            acc_s[...] += jnp.dot(vt_ref[:, cols], wa_s[cols, :], preferred_element_type=F32)

        half = nsub // 2
        at_next = activations(0)
        for k in range(nsub):
            at_cur = at_next
            if k + 1 < nsub:
                at_next = activations(k + 1)
            if k == half:
                values(0, half)
            weigh(k, at_cur)
        values(half, nsub)

    @pl.when(e == ne - 1)
    def _finish():
        g2 = mod_ref[...][5:6]
        xn = x_ref[...] + g2 * acc_s[...].T
        if final_norm:
            xn = _rms(xn, fg_ref[...])
        o_ref[...] = xn


def _peer_call_x(x2, mod, ng, wqt, keys, u, vt, fg, S, final_norm, var):
    T, D = x2.shape
    E = u.shape[0]
    tm, ec = TM_PEER, EC_PEER
    nb = S // tm
    ntc = tm // LANES
    kern = functools.partial(_peer_kernel_x, tm=tm, ec=ec, final_norm=final_norm, var=var)
    le_rows = 2 * N_KEYS + 2 * PAD_ROWS
    return pl.pallas_call(
        kern,
        out_shape=jax.ShapeDtypeStruct((T, D), F32),
        grid=(T // tm, E // ec),
        in_specs=[pl.BlockSpec((tm, D), lambda i, e: (i, 0)),
                  pl.BlockSpec((None, 6, D), lambda i, e: (i // nb, 0, 0)),
                  _const_spec((1, D)), _const_spec(wqt.shape), _const_spec(keys.shape),
                  pl.BlockSpec((ec, D), lambda i, e: (e, 0)),
                  pl.BlockSpec((D, ec), lambda i, e: (0, e)),
                  _const_spec((1, D))],
        out_specs=pl.BlockSpec((tm, D), lambda i, e: (i, 0)),
        scratch_shapes=[pltpu.VMEM((D, tm) if var["ht"] else (tm, D), BF16),
                        pltpu.VMEM((ntc, 2 * PEER_HEADS, N_KEYS, LANES), F32),
                        pltpu.VMEM((ntc, PEER_HEADS, N_KEYS, LANES), F32),
                        pltpu.VMEM((ntc, PEER_HEADS, N_KEYS, LANES), F32),
                        pltpu.VMEM((ntc, PEER_HEADS, N_KEYS, LANES), F32),
                        pltpu.VMEM((ntc, PEER_HEADS, le_rows, LANES), BF16),
                        pltpu.VMEM((ec, tm), BF16),
                        pltpu.VMEM((D, tm), F32)],
        compiler_params=_cparams(("arbitrary", "arbitrary")),
        name="peer_layer",
    )(x2, mod, ng, wqt, keys, u, vt, fg)


PEER_VARIANTS = (
    dict(levels=False, wbuild="f32sel", nsub=1, ht=True, pad=False),
    dict(levels=False, wbuild="f32sel", nsub=2, ht=True, pad=False),
    dict(levels=False, wbuild="f32sel", nsub=4, ht=True, pad=False),
    dict(levels=False, wbuild="f32sel", nsub=8, ht=True, pad=False),
)


def _head_pad_cols(w_nope, w_rope, n_in):
    zeros = jnp.zeros((n_in, MLA_HEADS, HEAD_PAD - QK_NOPE - QK_ROPE), w_nope.dtype)
    return jnp.concatenate([w_nope, w_rope, zeros], axis=-1).reshape(n_in, MLA_HEADS * HEAD_PAD)


def _swap_halves(w):
    half = QK_ROPE // 2
    return jnp.concatenate([w[..., half:], w[..., :half]], axis=-1)


def kernel(x, c, positions, ada_w, ada_b, norm_mix_g, norm_ffn_g, lru_w_in, lru_conv_w, lru_conv_b, lru_wa, lru_ba, lru_wx, lru_bx, lru_lambda, lru_w_out, kv_ada_w, kv_ada_b, kv_norm_g, mla_w_dkv, mla_w_kr, mla_kv_latent_g, mla_w_uk, mla_w_uv, mla_w_dq, mla_q_latent_g, mla_w_uq, mla_w_o, peer_w_q, peer_sub_keys, peer_u, peer_v, final_g):
    B, S, D = x.shape
    T = B * S
    assert D == D_MODEL and S % TM_TOK == 0 and S % TS_LRU == 0 and S % TQ_ATT == 0 and S % TM_PEER == 0

    c_pad = jnp.zeros((SUBLANES, D), F32).at[:B].set(c)
    mod_all = _mod_call(c_pad, ada_w, ada_b[:, None, :], 1536)
    mod_all = mod_all[:, :B].reshape(DEPTH, B, 6, D)
    mod_kv = _mod_call(c_pad, kv_ada_w[None], kv_ada_b[None, None, :], 1024)[0, :B].reshape(B, 2, D)

    half = QK_ROPE // 2
    freqs = ROPE_THETA ** (-jnp.arange(half, dtype=F32) / half)
    freq_lane = jnp.zeros((1, LANES), F32).at[0, QK_NOPE:QK_NOPE + QK_ROPE].set(jnp.concatenate([freqs, freqs]))
    pos_col = positions.reshape(T, 1)
    ca, sb = _rope_call(pos_col, freq_lane)

    nq, nk = S // TQ_ATT, S // TK_ATT
    pq = positions.reshape(B, nq, TQ_ATT)
    pk = positions.reshape(B, nk, TK_ATT)
    qmin, qmax = pq.min(-1), pq.max(-1)
    kmin, kmax = pk.min(-1), pk.max(-1)
    needed = kmin[:, None, :] <= qmax[:, :, None]
    nhi = jnp.max(jnp.where(needed, jnp.arange(1, nk + 1, dtype=jnp.int32), 0), axis=-1)
    full = kmax[:, None, :] <= qmin[:, :, None]
    nfull = jnp.min(jnp.where(full, nk, jnp.arange(nk, dtype=jnp.int32)), axis=-1)
    nfull = jnp.minimum(nfull, nhi).astype(jnp.int32).reshape(-1)
    nhi = nhi.astype(jnp.int32).reshape(-1)
    pos_row = positions.reshape(B, 1, S)

    def peer_layer(x2, l, final_norm):
        wqt = peer_w_q[l].T.astype(BF16)
        keys = peer_sub_keys[l].reshape(2 * PEER_HEADS, N_KEYS, PEER_HALF).astype(BF16)
        u = peer_u[l].astype(BF16)
        vt = peer_v[l].T.astype(BF16)
        return _peer_call_x(x2, mod_all[l], norm_ffn_g[l][None], wqt, keys, u, vt, final_g[None], S, final_norm,
                            PEER_VARIANTS[l])

    for l in range(N_A_LAYERS):
        x = _lru_call(x, mod_all[l], norm_mix_g[l][None], lru_w_in[l].astype(BF16), lru_conv_w[l],
                      lru_conv_b[l][None], lru_wa[l].astype(BF16), lru_ba[l][None], lru_wx[l].astype(BF16),
                      lru_bx[l][None], lru_lambda[l][None], lru_w_out[l].astype(BF16))
        x = peer_layer(x.reshape(T, D), l, False).reshape(B, S, D)

    x2 = x.reshape(T, D)

    zc = jnp.zeros((D, QK_NOPE), F32)
    zt = jnp.zeros((D, HEAD_PAD - QK_NOPE - QK_ROPE), F32)
    w1 = jnp.concatenate([mla_w_dkv, zc, mla_w_kr, zt, zc, _swap_halves(mla_w_kr), zt], axis=-1).astype(BF16)
    wuk = jnp.concatenate([mla_w_uk.reshape(KV_LORA, MLA_HEADS, QK_NOPE),
                           jnp.zeros((KV_LORA, MLA_HEADS, HEAD_PAD - QK_NOPE), F32)], axis=-1)
    wuk = wuk.reshape(KV_LORA, MLA_HEADS * HEAD_PAD).astype(BF16)
    k_all, v_all = _kv_call(x2, mod_kv, kv_norm_g[None], w1, mla_kv_latent_g[None], wuk,
                            mla_w_uv.astype(BF16), ca, sb, S)

    for j in range(DEPTH - N_A_LAYERS):
        l = N_A_LAYERS + j
        wq = mla_w_uq[j].reshape(Q_LORA, MLA_HEADS, QK_NOPE + QK_ROPE)
        w_nope, w_rope = wq[..., :QK_NOPE], wq[..., QK_NOPE:]
        wa = _head_pad_cols(w_nope, w_rope, Q_LORA).astype(BF16)
        wb = _head_pad_cols(jnp.zeros_like(w_nope), _swap_halves(w_rope), Q_LORA).astype(BF16)
        q = _q_call(x2, mod_all[l], norm_mix_g[l][None], mla_w_dq[j].astype(BF16), mla_q_latent_g[j][None],
                    wa, wb, ca, sb, S)
        o = _attn_call(nfull, nhi, q, k_all, v_all, pos_col, pos_row, B, S)
        x2 = _oproj_call(x2, o, mod_all[l], mla_w_o[j].astype(BF16), S)
        x2 = peer_layer(x2, l, l == DEPTH - 1)

    return x2.reshape(B, S, D)
```

```python
import functools
import math

import jax
import jax.numpy as jnp
from jax import lax
from jax.experimental import pallas as pl
from jax.experimental.pallas import tpu as pltpu

F32 = jnp.float32
BF16 = jnp.bfloat16

D_MODEL = 1024
DEPTH = 4
N_A_LAYERS = DEPTH // 2
LRU_HEADS = 4
LRU_BLOCK = D_MODEL // LRU_HEADS
CONV_WIDTH = 4
LRU_C = 8.0
MLA_HEADS = 16
QK_NOPE = 64
QK_ROPE = 32
V_HEAD = 64
Q_LORA = 384
KV_LORA = 256
ROPE_THETA = 10000.0
ATTN_SCALE = 1.0 / math.sqrt(QK_NOPE + QK_ROPE)
PEER_HEADS = 8
N_KEYS = 128
PEER_TOPK = 16
PEER_HALF = 128
RMS_EPS = 1e-6
NEG_INF = -1e30
LOG2E = math.log2(math.e)

LANES = 128
SUBLANES = 8
HEAD_PAD = LANES
VMEM_LIMIT = 56 * 1024 * 1024

TS_LRU = 256
TM_TOK = 512
TQ_ATT = 512
TK_ATT = 1024
KSUB_ATT = 512
QSUB_ATT = 256
ONES_ROWS = 16
TM_PEER = 512
EC_PEER = 1024
ESUB_PEER = 256


def _cparams(sem):
    return pltpu.CompilerParams(dimension_semantics=sem, vmem_limit_bytes=VMEM_LIMIT)


def _rms(x, g):
    return x * lax.rsqrt(jnp.mean(x * x, axis=-1, keepdims=True) + RMS_EPS) * g


def _gelu(x):
    return 0.5 * x * (1.0 + lax.erf(x * (1.0 / math.sqrt(2.0))))


def _expm1(y):
    series = y * (1.0 + 0.5 * y * (1.0 + (1.0 / 3.0) * y * (1.0 + 0.25 * y * (1.0 + 0.2 * y))))
    return jnp.where(jnp.abs(y) < 0.05, series, jnp.exp(y) - 1.0)


def _const_spec(shape):
    nd = len(shape)
    return pl.BlockSpec(shape, lambda *_: (0,) * nd)


def _mod_kernel(c_ref, w_ref, b_ref, o_ref):
    c = c_ref[...]
    sc = c * jax.nn.sigmoid(c)
    o_ref[...] = jnp.dot(sc.astype(BF16), w_ref[...].astype(BF16),
                         preferred_element_type=F32) + b_ref[...]


def _mod_call(c_pad, w, b, tn):
    L, D, N = w.shape
    return pl.pallas_call(
        _mod_kernel,
        out_shape=jax.ShapeDtypeStruct((L, SUBLANES, N), F32),
        grid=(L, N // tn),
        in_specs=[pl.BlockSpec((SUBLANES, D), lambda l, n: (0, 0)),
                  pl.BlockSpec((None, D, tn), lambda l, n: (l, 0, n)),
                  pl.BlockSpec((None, 1, tn), lambda l, n: (l, 0, n))],
        out_specs=pl.BlockSpec((None, SUBLANES, tn), lambda l, n: (l, 0, n)),
        compiler_params=_cparams(("arbitrary", "arbitrary")),
        name="adaln_mod",
    )(c_pad, w, b)


def _rope_kernel(pos_ref, freq_ref, ca_ref, sb_ref):
    ang = pos_ref[...].astype(F32) * freq_ref[...]
    lane = lax.broadcasted_iota(jnp.int32, ang.shape, 1)
    cos, sin = jnp.cos(ang), jnp.sin(ang)
    half = QK_ROPE // 2
    ca_ref[...] = jnp.where(lane < QK_NOPE, 1.0, jnp.where(lane < QK_NOPE + QK_ROPE, cos, 0.0))
    sb_ref[...] = jnp.where(lane < QK_NOPE, 0.0,
                            jnp.where(lane < QK_NOPE + half, -sin,
                                      jnp.where(lane < QK_NOPE + QK_ROPE, sin, 0.0)))


def _rope_call(pos_col, freq_lane):
    T = pos_col.shape[0]
    tm = TM_TOK
    spec = pl.BlockSpec((tm, LANES), lambda i: (i, 0))
    return pl.pallas_call(
        _rope_kernel,
        out_shape=(jax.ShapeDtypeStruct((T, LANES), F32),) * 2,
        grid=(T // tm,),
        in_specs=[pl.BlockSpec((tm, 1), lambda i: (i, 0)), _const_spec((1, LANES))],
        out_specs=(spec, spec),
        compiler_params=_cparams(("arbitrary",)),
        name="rope_tables",
    )(pos_col, freq_lane)


def _shift_rows(x, d, fill):
    n = x.shape[0]
    if d % SUBLANES == 0:
        return jnp.concatenate([jnp.full((d, x.shape[1]), fill, x.dtype), x[:n - d]], axis=0)
    rolled = pltpu.roll(x, d, 0)
    row = lax.broadcasted_iota(jnp.int32, x.shape, 0)
    return jnp.where(row < d, fill, rolled)


def _lru_kernel(x_ref, mod_ref, ng_ref, win_ref, cw_ref, cb_ref, wa_ref, ba_ref, wx_ref, bx_ref,
                lam_ref, wout_ref, o_ref, rec_s, h_s, *, ts):
    D = D_MODEL

    @pl.when(pl.program_id(1) == 0)
    def _():
        rec_s[0:SUBLANES, :] = jnp.zeros((SUBLANES, D), F32)
        h_s[...] = jnp.zeros((SUBLANES, D), F32)

    x = x_ref[...]
    mod = mod_ref[...]
    sh1, sc1, g1 = mod[0:1], mod[1:2], mod[2:3]
    h = _rms(x, ng_ref[...]) * (1.0 + sc1) + sh1
    u = jnp.dot(h.astype(BF16), win_ref[...], preferred_element_type=F32)
    gate = _gelu(u[:, :D])
    rec = u[:, D:]

    rec_s[SUBLANES:ts + SUBLANES, :] = rec
    cw = cw_ref[...]
    xc = (cb_ref[...] + rec * cw[3:4]
          + rec_s[SUBLANES - 1:ts + SUBLANES - 1, :] * cw[2:3]
          + rec_s[SUBLANES - 2:ts + SUBLANES - 2, :] * cw[1:2]
          + rec_s[SUBLANES - 3:ts + SUBLANES - 3, :] * cw[0:1])
    rec_s[0:SUBLANES, :] = rec[ts - SUBLANES:ts, :]

    xcb = xc.astype(BF16)
    r_pre = jnp.concatenate(
        [jnp.dot(xcb[:, k * LRU_BLOCK:(k + 1) * LRU_BLOCK], wa_ref[k], preferred_element_type=F32)
         for k in range(LRU_HEADS)], axis=-1)
    i_pre = jnp.concatenate(
        [jnp.dot(xcb[:, k * LRU_BLOCK:(k + 1) * LRU_BLOCK], wx_ref[k], preferred_element_type=F32)
         for k in range(LRU_HEADS)], axis=-1)
    r = jax.nn.sigmoid(r_pre + ba_ref[...])
    gi = jax.nn.sigmoid(i_pre + bx_ref[...])
    z = -lam_ref[...]
    softplus = jnp.maximum(z, 0.0) + jnp.log1p(jnp.exp(-jnp.abs(z)))
    log_a = (-LRU_C) * r * softplus
    a = jnp.exp(log_a)
    b = jnp.sqrt(-_expm1(2.0 * log_a)) * (gi * xc)

    d = 1
    while d < ts:
        a_sh = _shift_rows(a, d, 1.0)
        b_sh = _shift_rows(b, d, 0.0)
        b = a * b_sh + b
        a = a * a_sh
        d *= 2
    h0 = h_s[SUBLANES - 1:SUBLANES, :]
    hs = a * h0 + b
    h_s[...] = hs[ts - SUBLANES:ts, :]

    y = jnp.dot((gate * hs).astype(BF16), wout_ref[...], preferred_element_type=F32)
    o_ref[...] = x + g1 * y


def _lru_call(x, mod, ng, w_in, cw, cb, wa, ba, wx, bx, lam, w_out):
    B, S, D = x.shape
    ts = TS_LRU
    kern = functools.partial(_lru_kernel, ts=ts)
    return pl.pallas_call(
        kern,
        out_shape=jax.ShapeDtypeStruct((B, S, D), F32),
        grid=(B, S // ts),
        in_specs=[pl.BlockSpec((None, ts, D), lambda b, s: (b, s, 0)),
                  pl.BlockSpec((None, 6, D), lambda b, s: (b, 0, 0)),
                  _const_spec((1, D)), _const_spec((D, 2 * D)), _const_spec((CONV_WIDTH, D)),
                  _const_spec((1, D)), _const_spec((LRU_HEADS, LRU_BLOCK, LRU_BLOCK)), _const_spec((1, D)),
                  _const_spec((LRU_HEADS, LRU_BLOCK, LRU_BLOCK)), _const_spec((1, D)), _const_spec((1, D)),
                  _const_spec((D, D))],
        out_specs=pl.BlockSpec((None, ts, D), lambda b, s: (b, s, 0)),
        scratch_shapes=[pltpu.VMEM((ts + SUBLANES, D), F32), pltpu.VMEM((SUBLANES, D), F32)],
        compiler_params=_cparams(("arbitrary", "arbitrary")),
        name="rglru_layer",
    )(x, mod, ng, w_in, cw, cb, wa, ba, wx, bx, lam, w_out)


def _kv_kernel(x_ref, mod_ref, ng_ref, w1_ref, lg_ref, wuk_ref, wuv_ref, ca_ref, sb_ref, k_ref, v_ref):
    x = x_ref[...]
    mod = mod_ref[...]
    shift, scale = mod[0:1], mod[1:2]
    h = _rms(x, ng_ref[...]) * (1.0 + scale) + shift
    t = jnp.dot(h.astype(BF16), w1_ref[...], preferred_element_type=F32)
    c_kv = _rms(t[:, :KV_LORA], lg_ref[...])
    kr = t[:, KV_LORA:KV_LORA + LANES] * ca_ref[...] + t[:, KV_LORA + LANES:] * sb_ref[...]
    ckb = c_kv.astype(BF16)
    kn = jnp.dot(ckb, wuk_ref[...], preferred_element_type=F32)
    k_ref[...] = (kn + jnp.concatenate([kr] * MLA_HEADS, axis=-1)).astype(BF16)
    v_ref[...] = jnp.dot(ckb, wuv_ref[...], preferred_element_type=F32).T.astype(BF16)


def _kv_call(x2, mod, ng, w1, lg, wuk, wuv, ca, sb, S):
    T, D = x2.shape
    tm = TM_TOK
    nb = S // tm
    HP = MLA_HEADS * HEAD_PAD
    HV = MLA_HEADS * V_HEAD
    return pl.pallas_call(
        _kv_kernel,
        out_shape=(jax.ShapeDtypeStruct((T, HP), BF16), jax.ShapeDtypeStruct((HV, T), BF16)),
        grid=(T // tm,),
        in_specs=[pl.BlockSpec((tm, D), lambda i: (i, 0)),
                  pl.BlockSpec((None, 2, D), lambda i: (i // nb, 0, 0)),
                  _const_spec((1, D)), _const_spec(w1.shape), _const_spec((1, KV_LORA)),
                  _const_spec(wuk.shape), _const_spec(wuv.shape),
                  pl.BlockSpec((tm, LANES), lambda i: (i, 0)), pl.BlockSpec((tm, LANES), lambda i: (i, 0))],
        out_specs=(pl.BlockSpec((tm, HP), lambda i: (i, 0)), pl.BlockSpec((HV, tm), lambda i: (0, i))),
        compiler_params=_cparams(("arbitrary",)),
        name="shared_kv",
    )(x2, mod, ng, w1, lg, wuk, wuv, ca, sb)


def _q_kernel(x_ref, mod_ref, ng_ref, wdq_ref, lg_ref, wa_ref, wb_ref, ca_ref, sb_ref, q_ref):
    x = x_ref[...]
    mod = mod_ref[...]
    sh1, sc1 = mod[0:1], mod[1:2]
    h = _rms(x, ng_ref[...]) * (1.0 + sc1) + sh1
    ql = _rms(jnp.dot(h.astype(BF16), wdq_ref[...], preferred_element_type=F32), lg_ref[...])
    qlb = ql.astype(BF16)
    qa = jnp.dot(qlb, wa_ref[...], preferred_element_type=F32)
    qb = jnp.dot(qlb, wb_ref[...], preferred_element_type=F32)
    ca = jnp.concatenate([ca_ref[...]] * MLA_HEADS, axis=-1)
    sb = jnp.concatenate([sb_ref[...]] * MLA_HEADS, axis=-1)
    q_ref[...] = ((qa * ca + qb * sb) * (ATTN_SCALE * LOG2E)).T.astype(BF16)


def _q_call(x2, mod, ng, wdq, lg, wa, wb, ca, sb, S):
    T, D = x2.shape
    tm = TM_TOK
    nb = S // tm
    HP = MLA_HEADS * HEAD_PAD
    return pl.pallas_call(
        _q_kernel,
        out_shape=jax.ShapeDtypeStruct((HP, T), BF16),
        grid=(T // tm,),
        in_specs=[pl.BlockSpec((tm, D), lambda i: (i, 0)),
                  pl.BlockSpec((None, 6, D), lambda i: (i // nb, 0, 0)),
                  _const_spec((1, D)), _const_spec(wdq.shape), _const_spec((1, Q_LORA)),
                  _const_spec(wa.shape), _const_spec(wb.shape),
                  pl.BlockSpec((tm, LANES), lambda i: (i, 0)), pl.BlockSpec((tm, LANES), lambda i: (i, 0))],
        out_specs=pl.BlockSpec((HP, tm), lambda i: (0, i)),
        compiler_params=_cparams(("arbitrary",)),
        name="mla_queries",
    )(x2, mod, ng, wdq, lg, wa, wb, ca, sb)


def _attn_kernel(nfull_ref, nhi_ref, qt_ref, k_ref, vt_ref, pq_ref, pk_ref, o_ref, *, tq, tk, nq):
    b = pl.program_id(0)
    qi = pl.program_id(2)
    nfull = nfull_ref[b * nq + qi]
    nhi = nhi_ref[b * nq + qi]
    pos_q = pq_ref[...]

    def step(kc, carry, masked):
        nstate = 2 * (tq // QSUB_ATT)
        nstream = nstate * (tk // KSUB_ATT)
        state = list(carry)

        def split(sidx):
            kb, sid = divmod(sidx, nstate)
            hh, qs = divmod(sid, tq // QSUB_ATT)
            return pl.multiple_of(kc * tk + kb * KSUB_ATT, KSUB_ATT), sid, hh, qs

        def scores(sidx):
            off, _, hh, qs = split(sidx)
            k = k_ref[pl.ds(off, KSUB_ATT), hh * HEAD_PAD:(hh + 1) * HEAD_PAD]
            qt = qt_ref[hh * HEAD_PAD:(hh + 1) * HEAD_PAD, qs * QSUB_ATT:(qs + 1) * QSUB_ATT]
            return jnp.dot(k, qt, preferred_element_type=F32)

        def update(sidx, st):
            off, sid, hh, qs = split(sidx)
            m, acc = state[sid]
            if masked:
                keep = pk_ref[pl.ds(off, KSUB_ATT), :] <= pos_q[:, qs * QSUB_ATT:(qs + 1) * QSUB_ATT]
                st = jnp.where(keep, st, NEG_INF)
            m_new = jnp.maximum(m, jnp.max(st, axis=0, keepdims=True))
            p = jnp.exp2((st - m_new).astype(BF16))
            alpha = jnp.exp2(m - m_new)
            vt = jnp.concatenate([vt_ref[hh * V_HEAD:(hh + 1) * V_HEAD, pl.ds(off, KSUB_ATT)],
                                  jnp.ones((ONES_ROWS, KSUB_ATT), BF16)], axis=0)
            state[sid] = (m_new, alpha * acc + jnp.dot(vt, p, preferred_element_type=F32))

        st_next = scores(0)
        for sidx in range(nstream):
            st_cur = st_next
            if sidx + 1 < nstream:
                st_next = scores(sidx + 1)
            update(sidx, st_cur)
        return tuple(state)

    nqs = tq // QSUB_ATT
    one = (jnp.full((1, QSUB_ATT), NEG_INF, F32), jnp.zeros((V_HEAD + ONES_ROWS, QSUB_ATT), F32))
    carry = lax.fori_loop(0, nfull, functools.partial(step, masked=False), (one,) * (2 * nqs))
    fin = lax.fori_loop(nfull, nhi, functools.partial(step, masked=True), carry)
    heads = [jnp.concatenate([fin[hh * nqs + qs][1][:V_HEAD] / fin[hh * nqs + qs][1][V_HEAD:V_HEAD + 1]
                              for qs in range(nqs)], axis=1) for hh in range(2)]
    o_ref[...] = jnp.concatenate(heads, axis=0).T.astype(BF16)


def _attn_call(nfull, nhi, qt, k, vt, pos_col, pos_row, B, S):
    tq, tk = TQ_ATT, TK_ATT
    nq = S // tq
    T = B * S
    kern = functools.partial(_attn_kernel, tq=tq, tk=tk, nq=nq)
    grid_spec = pltpu.PrefetchScalarGridSpec(
        num_scalar_prefetch=2,
        grid=(B, MLA_HEADS // 2, nq),
        in_specs=[pl.BlockSpec((2 * HEAD_PAD, tq), lambda b, h, i, *_: (h, b * nq + i)),
                  pl.BlockSpec((S, 2 * HEAD_PAD), lambda b, h, i, *_: (b, h)),
                  pl.BlockSpec((2 * V_HEAD, S), lambda b, h, i, *_: (h, b)),
                  pl.BlockSpec((None, 1, tq), lambda b, h, i, *_: (b, 0, i)),
                  pl.BlockSpec((S, 1), lambda b, h, i, *_: (b, 0))],
        out_specs=pl.BlockSpec((tq, 2 * V_HEAD), lambda b, h, i, *_: (b * nq + i, h)),
    )
    return pl.pallas_call(
        kern,
        out_shape=jax.ShapeDtypeStruct((T, MLA_HEADS * V_HEAD), BF16),
        grid_spec=grid_spec,
        compiler_params=_cparams(("arbitrary", "arbitrary", "arbitrary")),
        name="mla_attention",
    )(nfull, nhi, qt, k, vt, pos_row, pos_col)


def _oproj_kernel(x_ref, o_ref, mod_ref, w_ref, out_ref):
    g1 = mod_ref[...][2:3]
    y = jnp.dot(o_ref[...], w_ref[...], preferred_element_type=F32)
    out_ref[...] = x_ref[...] + g1 * y


def _oproj_call(x2, o, mod, w_o, S):
    T, D = x2.shape
    tm = TM_TOK
    nb = S // tm
    return pl.pallas_call(
        _oproj_kernel,
        out_shape=jax.ShapeDtypeStruct((T, D), F32),
        grid=(T // tm,),
        in_specs=[pl.BlockSpec((tm, D), lambda i: (i, 0)),
                  pl.BlockSpec((tm, o.shape[1]), lambda i: (i, 0)),
                  pl.BlockSpec((None, 6, D), lambda i: (i // nb, 0, 0)),
                  _const_spec(w_o.shape)],
        out_specs=pl.BlockSpec((tm, D), lambda i: (i, 0)),
        compiler_params=_cparams(("arbitrary",)),
        name="mla_out_proj",
    )(x2, o, mod, w_o)


TOP_ROWS = 3 * SUBLANES
LEVEL_STEP = 256.0


def _top17_rows(x):
    rows = lax.broadcasted_iota(jnp.int32, (TOP_ROWS, x.shape[1]), 0)
    out = jnp.full((TOP_ROWS, x.shape[1]), NEG_INF, F32)
    for j in range(PEER_TOPK + 1):
        m = jnp.max(x, axis=0, keepdims=True)
        out = jnp.where(rows == j, m, out)
        x = jnp.where(x == m, NEG_INF, x)
    return out


def _count_ge(x, y_rows):
    below = jnp.zeros(x.shape, jnp.int32)
    for j in range(PEER_TOPK + 1):
        diff = lax.bitcast_convert_type(x - y_rows[j:j + 1], jnp.int32)
        below = below + lax.shift_right_logical(diff, 31)
    return float(PEER_TOPK + 1) - below.astype(F32)


def _kth_and_next_largest_rows(x, k):
    shape = (1, x.shape[1])
    tk, tn = jnp.full(shape, NEG_INF, F32), jnp.full(shape, NEG_INF, F32)
    cnt = jnp.zeros(shape, F32)
    for _ in range(k + 1):
        m = jnp.max(x, axis=0, keepdims=True)
        eq = x == m
        tk = jnp.where(cnt < k, m, tk)
        tn = jnp.where(cnt < k + 1, m, tn)
        cnt = cnt + jnp.sum(jnp.where(eq, 1.0, 0.0), axis=0, keepdims=True)
        x = jnp.where(eq, NEG_INF, x)
    return tk, tn


def _peer_kernel(x_ref, mod_ref, ng_ref, wqt_ref, keys_ref, u_ref, vt_ref, fg_ref, o_ref,
                 h_s, st_s, thr_s, e1_s, lv_s, e2_s, wa_s, acc_s, *, tm, ec, final_norm):
    D = D_MODEL
    ntc = tm // LANES
    e = pl.program_id(1)
    ne = pl.num_programs(1)

    @pl.when(e == 0)
    def _scores():
        x = x_ref[...]
        mod = mod_ref[...]
        sh2, sc2 = mod[3:4], mod[4:5]
        ht = (_rms(x, ng_ref[...]) * (1.0 + sc2) + sh2).T.astype(BF16)
        h_s[...] = ht
        qt = jnp.dot(wqt_ref[...], ht, preferred_element_type=F32)
        qtb = qt.astype(BF16)
        for hp in range(2 * PEER_HEADS):
            st = jnp.dot(keys_ref[hp], qtb[hp * PEER_HALF:(hp + 1) * PEER_HALF, :],
                         preferred_element_type=F32)
            for tc in range(ntc):
                st_s[tc, hp] = st[:, tc * LANES:(tc + 1) * LANES]

        def per_block(i, _):
            tc = i // PEER_HEADS
            hd = i % PEER_HEADS
            s1 = st_s[tc, 2 * hd]
            s2 = st_s[tc, 2 * hd + 1]
            a = _top17_rows(s1)
            b = _top17_rows(s2)
            cands = [a[0:1] + b]
            for r in range(1, SUBLANES):
                cands.append(a[r:r + 1] + b[0:SUBLANES])
            cands.append(a[SUBLANES:TOP_ROWS] + b[0:1])
            cand = jnp.concatenate(cands, axis=0)
            c16, c17 = _kth_and_next_largest_rows(cand, PEER_TOPK)
            tau = 0.5 * (c16 + c17)
            m1, m2 = a[0:1], b[0:1]
            zsum = jnp.sum(jnp.where(cand >= tau, jnp.exp(cand - (m1 + m2)), 0.0), axis=0, keepdims=True)
            level2 = _count_ge(s2, b)
            cnt = _count_ge(s1 - tau, -b)
            thr_s[tc, hd] = (float(PEER_TOPK + 1) - cnt) * LEVEL_STEP
            e1_s[tc, hd] = jnp.exp(s1 - m1) / zsum
            lv_s[tc, hd] = (level2 * LEVEL_STEP).astype(BF16)
            e2_s[tc, hd] = jnp.exp(s2 - m2).astype(BF16)
            return 0

        lax.fori_loop(0, ntc * PEER_HEADS, per_block, 0)
        acc_s[...] = jnp.zeros(acc_s.shape, F32)

    ht = h_s[...]
    ngrp = ec // LANES
    nsub = ec // ESUB_PEER
    rows_per_sub = ESUB_PEER // LANES
    key_rows = {}

    def key_row(ref, g8, tc, hd, j):
        if (id(ref), g8, tc, hd) not in key_rows:
            base = pl.multiple_of(e * ngrp + g8 * SUBLANES, SUBLANES)
            key_rows[(id(ref), g8, tc, hd)] = ref[tc, hd, pl.ds(base, SUBLANES), :].astype(BF16)
        return key_rows[(id(ref), g8, tc, hd)][j:j + 1, :]

    def activations(k):
        return jnp.dot(u_ref[k * ESUB_PEER:(k + 1) * ESUB_PEER, :], ht, preferred_element_type=F32)

    def softmax_weights(k):
        for tc in range(ntc):
            w = [None] * rows_per_sub
            for hd in range(PEER_HEADS):
                lv, e2 = lv_s[tc, hd], e2_s[tc, hd]
                for r in range(rows_per_sub):
                    g8, j = divmod(k * rows_per_sub + r, SUBLANES)
                    sel = jnp.minimum(e2, jnp.maximum(lv - key_row(thr_s, g8, tc, hd, j), 0.0))
                    term = sel * key_row(e1_s, g8, tc, hd, j)
                    w[r] = term if w[r] is None else w[r] + term
            for r in range(rows_per_sub):
                r0 = (k * rows_per_sub + r) * LANES
                wa_s[r0:r0 + LANES, tc * LANES:(tc + 1) * LANES] = w[r]

    def apply_activations(k, at):
        rows = slice(k * ESUB_PEER, (k + 1) * ESUB_PEER)
        wa_s[rows, :] = wa_s[rows, :] * _gelu(at).astype(BF16)

    def values(k0, k1):
        cols = slice(k0 * ESUB_PEER, k1 * ESUB_PEER)
        acc_s[...] += jnp.dot(vt_ref[:, cols], wa_s[cols, :], preferred_element_type=F32)

    half = nsub // 2
    at_next = activations(0)
    softmax_weights(0)
    for k in range(nsub):
        at_cur = at_next
        if k + 1 < nsub:
            at_next = activations(k + 1)
        if k == half:
            values(0, half)
        apply_activations(k, at_cur)
        if k + 1 < nsub:
            softmax_weights(k + 1)
    values(half, nsub)

    @pl.when(e == ne - 1)
    def _finish():
        g2 = mod_ref[...][5:6]
        xn = x_ref[...] + g2 * acc_s[...].T
        if final_norm:
            xn = _rms(xn, fg_ref[...])
        o_ref[...] = xn


def _peer_call(x2, mod, ng, wqt, keys, u, vt, fg, S, final_norm):
    T, D = x2.shape
    E = u.shape[0]
    tm, ec = TM_PEER, EC_PEER
    nb = S // tm
    ntc = tm // LANES
    kern = functools.partial(_peer_kernel, tm=tm, ec=ec, final_norm=final_norm)
    return pl.pallas_call(
        kern,
        out_shape=jax.ShapeDtypeStruct((T, D), F32),
        grid=(T // tm, E // ec),
        in_specs=[pl.BlockSpec((tm, D), lambda i, e: (i, 0)),
                  pl.BlockSpec((None, 6, D), lambda i, e: (i // nb, 0, 0)),
                  _const_spec((1, D)), _const_spec(wqt.shape), _const_spec(keys.shape),
                  pl.BlockSpec((ec, D), lambda i, e: (e, 0)),
                  pl.BlockSpec((D, ec), lambda i, e: (0, e)),
                  _const_spec((1, D))],
        out_specs=pl.BlockSpec((tm, D), lambda i, e: (i, 0)),
        scratch_shapes=[pltpu.VMEM((D, tm), BF16),
                        pltpu.VMEM((ntc, 2 * PEER_HEADS, N_KEYS, LANES), F32),
                        pltpu.VMEM((ntc, PEER_HEADS, N_KEYS, LANES), F32),
                        pltpu.VMEM((ntc, PEER_HEADS, N_KEYS, LANES), F32),
                        pltpu.VMEM((ntc, PEER_HEADS, N_KEYS, LANES), BF16),
                        pltpu.VMEM((ntc, PEER_HEADS, N_KEYS, LANES), BF16),
                        pltpu.VMEM((ec, tm), BF16),
                        pltpu.VMEM((D, tm), F32)],
        compiler_params=_cparams(("arbitrary", "arbitrary")),
        name="peer_layer",
    )(x2, mod, ng, wqt, keys, u, vt, fg)


PAD_ROWS = 16


def _peer_kernel_x(x_ref, mod_ref, ng_ref, wqt_ref, keys_ref, u_ref, vt_ref, fg_ref, o_ref,
                   h_s, st_s, thr_s, e1_s, e2f_s, le_s, wa_s, acc_s, *, tm, ec, final_norm, var):
    D = D_MODEL
    ntc = tm // LANES
    e = pl.program_id(1)
    ne = pl.num_programs(1)
    levels, wbuild, nsub, use_ht, pad = var["levels"], var["wbuild"], var["nsub"], var["ht"], var["pad"]
    e2_off = N_KEYS + (PAD_ROWS if pad else 0)

    @pl.when(e == 0)
    def _scores():
        x = x_ref[...]
        mod = mod_ref[...]
        sh2, sc2 = mod[3:4], mod[4:5]
        hn = _rms(x, ng_ref[...]) * (1.0 + sc2) + sh2
        if use_ht:
            ht = hn.T.astype(BF16)
            h_s[...] = ht
            qt = jnp.dot(wqt_ref[...], ht, preferred_element_type=F32)
        else:
            hb = hn.astype(BF16)
            h_s[...] = hb
            qt = lax.dot_general(wqt_ref[...], hb, (((1,), (1,)), ((), ())), preferred_element_type=F32)
        qtb = qt.astype(BF16)
        for hp in range(2 * PEER_HEADS):
            st = jnp.dot(keys_ref[hp], qtb[hp * PEER_HALF:(hp + 1) * PEER_HALF, :],
                         preferred_element_type=F32)
            for tc in range(ntc):
                st_s[tc, hp] = st[:, tc * LANES:(tc + 1) * LANES]

        nbatch = var.get("batch", 1)

        def per_group(g, _):
            for bi in range(nbatch):
                per_block(g * nbatch + bi)
            return 0

        def per_block(i):
            tc = i // PEER_HEADS
            hd = i % PEER_HEADS
            s1 = st_s[tc, 2 * hd]
            s2 = st_s[tc, 2 * hd + 1]
            a = _top17_rows(s1)
            b = _top17_rows(s2)
            cands = [a[0:1] + b]
            for r in range(1, SUBLANES):
                cands.append(a[r:r + 1] + b[0:SUBLANES])
            cands.append(a[SUBLANES:TOP_ROWS] + b[0:1])
            cand = jnp.concatenate(cands, axis=0)
            c16, c17 = _kth_and_next_largest_rows(cand, PEER_TOPK)
            tau = 0.5 * (c16 + c17)
            m1, m2 = a[0:1], b[0:1]
            zsum = jnp.sum(jnp.where(cand >= tau, jnp.exp(cand - (m1 + m2)), 0.0), axis=0, keepdims=True)
            e1_s[tc, hd] = jnp.exp(s1 - m1) / zsum
            if levels:
                level2 = _count_ge(s2, b)
                cnt = _count_ge(s1 - tau, -b)
                thr_s[tc, hd] = (float(PEER_TOPK + 1) - cnt) * LEVEL_STEP
                le_s[tc, hd, 0:N_KEYS, :] = (level2 * LEVEL_STEP).astype(BF16)
                le_s[tc, hd, e2_off:e2_off + N_KEYS, :] = jnp.exp(s2 - m2).astype(BF16)
            else:
                thr_s[tc, hd] = jnp.exp(tau - s1 - m2)
                e2f_s[tc, hd] = jnp.exp(s2 - m2)

        lax.fori_loop(0, ntc * PEER_HEADS // nbatch, per_group, 0)
        acc_s[...] = jnp.zeros(acc_s.shape, F32)

    ngrp = ec // LANES
    esub = ec // nsub
    rows_per_sub = esub // LANES
    key_rows = {}
    wdt = F32 if wbuild.startswith("f32") else BF16

    def key_row(ref, g8, tc, hd, j):
        if (id(ref), g8, tc, hd) not in key_rows:
            base = pl.multiple_of(e * ngrp + g8 * SUBLANES, SUBLANES)
            key_rows[(id(ref), g8, tc, hd)] = ref[tc, hd, pl.ds(base, SUBLANES), :].astype(wdt)
        return key_rows[(id(ref), g8, tc, hd)][j:j + 1, :]

    def activations(k):
        if use_ht:
            return jnp.dot(u_ref[k * esub:(k + 1) * esub, :], h_s[...], preferred_element_type=F32)
        return lax.dot_general(u_ref[k * esub:(k + 1) * esub, :], h_s[...], (((1,), (1,)), ((), ())),
                               preferred_element_type=F32)

    def masked_weights(tc, hd, g8, j):
        thr = key_row(thr_s, g8, tc, hd, j)
        if wbuild in ("f32sel", "f32mix", "f32bits"):
            e2 = e2f_s[tc, hd]
            if wbuild == "f32sel" or (wbuild == "f32mix" and hd < PEER_HEADS // 2):
                return jnp.where(e2 >= thr, e2, 0.0)
            keep = lax.shift_right_arithmetic(lax.bitcast_convert_type(thr - e2, jnp.int32), 31)
            return lax.bitcast_convert_type(lax.bitcast_convert_type(e2, jnp.int32) & keep, F32)
        lv = le_s[tc, hd, 0:N_KEYS, :]
        e2 = le_s[tc, hd, e2_off:e2_off + N_KEYS, :]
        if wbuild == "bf16sel":
            return jnp.where(lv > thr, e2, 0.0)
        return jnp.minimum(e2, jnp.maximum(lv - thr, 0.0))

    if nsub == 1:
        gact = _gelu(activations(0))
        if not wbuild.startswith("f32"):
            gact = gact.astype(BF16)
        for tc in range(ntc):
            for r in range(rows_per_sub):
                g8, j = divmod(r, SUBLANES)
                w = None
                for hd in range(PEER_HEADS):
                    term = masked_weights(tc, hd, g8, j) * key_row(e1_s, g8, tc, hd, j)
                    w = term if w is None else w + term
                rr = r * LANES
                wa_s[rr:rr + LANES, tc * LANES:(tc + 1) * LANES] = (
                    w * gact[rr:rr + LANES, tc * LANES:(tc + 1) * LANES]).astype(BF16)
        acc_s[...] += jnp.dot(vt_ref[...], wa_s[...], preferred_element_type=F32)
    else:
        def weigh(k, at):
            gact = _gelu(at)
            if not wbuild.startswith("f32"):
                gact = gact.astype(BF16)
            for tc in range(ntc):
                for r in range(rows_per_sub):
                    g8, j = divmod(k * rows_per_sub + r, SUBLANES)
                    w = None
                    for hd in range(PEER_HEADS):
                        term = masked_weights(tc, hd, g8, j) * key_row(e1_s, g8, tc, hd, j)
                        w = term if w is None else w + term
                    r0 = (k * rows_per_sub + r) * LANES
                    wa_s[r0:r0 + LANES, tc * LANES:(tc + 1) * LANES] = (
                        w * gact[r * LANES:(r + 1) * LANES, tc * LANES:(tc + 1) * LANES]).astype(BF16)

        def values(k0, k1):
            cols = slice(k0 * esub, k1 * esub)
            acc_s[...] += jnp.dot(vt_ref[:, cols], wa_s[cols, :], preferred_element_type=F32)

        half = nsub // 2
        at_next = activations(0)
        for k in range(nsub):
            at_cur = at_next
            if k + 1 < nsub:
                at_next = activations(k + 1)
            if k == half:
                values(0, half)
            weigh(k, at_cur)
        values(half, nsub)

    @pl.when(e == ne - 1)
    def _finish():
        g2 = mod_ref[...][5:6]
        xn = x_ref[...] + g2 * acc_s[...].T
        if final_norm:
            xn = _rms(xn, fg_ref[...])
        o_ref[...] = xn


def _peer_call_x(x2, mod, ng, wqt, keys, u, vt, fg, S, final_norm, var):
    T, D = x2.shape
    E = u.shape[0]
    tm, ec = TM_PEER, EC_PEER
    nb = S // tm
    ntc = tm // LANES
    kern = functools.partial(_peer_kernel_x, tm=tm, ec=ec, final_norm=final_norm, var=var)
    le_rows = 2 * N_KEYS + 2 * PAD_ROWS
    return pl.pallas_call(
        kern,
        out_shape=jax.ShapeDtypeStruct((T, D), F32),
        grid=(T // tm, E // ec),
        in_specs=[pl.BlockSpec((tm, D), lambda i, e: (i, 0)),
                  pl.BlockSpec((None, 6, D), lambda i, e: (i // nb, 0, 0)),
                  _const_spec((1, D)), _const_spec(wqt.shape), _const_spec(keys.shape),
                  pl.BlockSpec((ec, D), lambda i, e: (e, 0)),
                  pl.BlockSpec((D, ec), lambda i, e: (0, e)),
                  _const_spec((1, D))],
        out_specs=pl.BlockSpec((tm, D), lambda i, e: (i, 0)),
        scratch_shapes=[pltpu.VMEM((D, tm) if var["ht"] else (tm, D), BF16),
                        pltpu.VMEM((ntc, 2 * PEER_HEADS, N_KEYS, LANES), F32),
                        pltpu.VMEM((ntc, PEER_HEADS, N_KEYS, LANES), F32),
                        pltpu.VMEM((ntc, PEER_HEADS, N_KEYS, LANES), F32),
                        pltpu.VMEM((ntc, PEER_HEADS, N_KEYS, LANES), F32),
                        pltpu.VMEM((ntc, PEER_HEADS, le_rows, LANES), BF16),
                        pltpu.VMEM((ec, tm), BF16),
                        pltpu.VMEM((D, tm), F32)],
        compiler_params=_cparams(("arbitrary", "arbitrary")),
        name="peer_layer",
    )(x2, mod, ng, wqt, keys, u, vt, fg)


PEER_VARIANTS = (
    dict(levels=False, wbuild="f32sel", nsub=1, ht=True, pad=False, batch=1),
    dict(levels=False, wbuild="f32sel", nsub=1, ht=True, pad=False, batch=2),
    dict(levels=False, wbuild="f32sel", nsub=1, ht=True, pad=False, batch=4),
    dict(levels=False, wbuild="f32sel", nsub=1, ht=True, pad=False, batch=8),
)


def _head_pad_cols(w_nope, w_rope, n_in):
    zeros = jnp.zeros((n_in, MLA_HEADS, HEAD_PAD - QK_NOPE - QK_ROPE), w_nope.dtype)
    return jnp.concatenate([w_nope, w_rope, zeros], axis=-1).reshape(n_in, MLA_HEADS * HEAD_PAD)


def _swap_halves(w):
    half = QK_ROPE // 2
    return jnp.concatenate([w[..., half:], w[..., :half]], axis=-1)


def kernel(x, c, positions, ada_w, ada_b, norm_mix_g, norm_ffn_g, lru_w_in, lru_conv_w, lru_conv_b, lru_wa, lru_ba, lru_wx, lru_bx, lru_lambda, lru_w_out, kv_ada_w, kv_ada_b, kv_norm_g, mla_w_dkv, mla_w_kr, mla_kv_latent_g, mla_w_uk, mla_w_uv, mla_w_dq, mla_q_latent_g, mla_w_uq, mla_w_o, peer_w_q, peer_sub_keys, peer_u, peer_v, final_g):
    B, S, D = x.shape
    T = B * S
    assert D == D_MODEL and S % TM_TOK == 0 and S % TS_LRU == 0 and S % TQ_ATT == 0 and S % TM_PEER == 0

    c_pad = jnp.zeros((SUBLANES, D), F32).at[:B].set(c)
    mod_all = _mod_call(c_pad, ada_w, ada_b[:, None, :], 1536)
    mod_all = mod_all[:, :B].reshape(DEPTH, B, 6, D)
    mod_kv = _mod_call(c_pad, kv_ada_w[None], kv_ada_b[None, None, :], 1024)[0, :B].reshape(B, 2, D)

    half = QK_ROPE // 2
    freqs = ROPE_THETA ** (-jnp.arange(half, dtype=F32) / half)
    freq_lane = jnp.zeros((1, LANES), F32).at[0, QK_NOPE:QK_NOPE + QK_ROPE].set(jnp.concatenate([freqs, freqs]))
    pos_col = positions.reshape(T, 1)
    ca, sb = _rope_call(pos_col, freq_lane)

    nq, nk = S // TQ_ATT, S // TK_ATT
    pq = positions.reshape(B, nq, TQ_ATT)
    pk = positions.reshape(B, nk, TK_ATT)
    qmin, qmax = pq.min(-1), pq.max(-1)
    kmin, kmax = pk.min(-1), pk.max(-1)
    needed = kmin[:, None, :] <= qmax[:, :, None]
    nhi = jnp.max(jnp.where(needed, jnp.arange(1, nk + 1, dtype=jnp.int32), 0), axis=-1)
    full = kmax[:, None, :] <= qmin[:, :, None]
    nfull = jnp.min(jnp.where(full, nk, jnp.arange(nk, dtype=jnp.int32)), axis=-1)
    nfull = jnp.minimum(nfull, nhi).astype(jnp.int32).reshape(-1)
    nhi = nhi.astype(jnp.int32).reshape(-1)
    pos_row = positions.reshape(B, 1, S)

    def peer_layer(x2, l, final_norm):
        wqt = peer_w_q[l].T.astype(BF16)
        keys = peer_sub_keys[l].reshape(2 * PEER_HEADS, N_KEYS, PEER_HALF).astype(BF16)
        u = peer_u[l].astype(BF16)
        vt = peer_v[l].T.astype(BF16)
        return _peer_call_x(x2, mod_all[l], norm_ffn_g[l][None], wqt, keys, u, vt, final_g[None], S, final_norm,
                            PEER_VARIANTS[l])

    for l in range(N_A_LAYERS):
        x = _lru_call(x, mod_all[l], norm_mix_g[l][None], lru_w_in[l].astype(BF16), lru_conv_w[l],
                      lru_conv_b[l][None], lru_wa[l].astype(BF16), lru_ba[l][None], lru_wx[l].astype(BF16),
                      lru_bx[l][None], lru_lambda[l][None], lru_w_out[l].astype(BF16))
        x = peer_layer(x.reshape(T, D), l, False).reshape(B, S, D)

    x2 = x.reshape(T, D)

    zc = jnp.zeros((D, QK_NOPE), F32)
    zt = jnp.zeros((D, HEAD_PAD - QK_NOPE - QK_ROPE), F32)
    w1 = jnp.concatenate([mla_w_dkv, zc, mla_w_kr, zt, zc, _swap_halves(mla_w_kr), zt], axis=-1).astype(BF16)
    wuk = jnp.concatenate([mla_w_uk.reshape(KV_LORA, MLA_HEADS, QK_NOPE),
                           jnp.zeros((KV_LORA, MLA_HEADS, HEAD_PAD - QK_NOPE), F32)], axis=-1)
    wuk = wuk.reshape(KV_LORA, MLA_HEADS * HEAD_PAD).astype(BF16)
    k_all, v_all = _kv_call(x2, mod_kv, kv_norm_g[None], w1, mla_kv_latent_g[None], wuk,
                            mla_w_uv.astype(BF16), ca, sb, S)

    for j in range(DEPTH - N_A_LAYERS):
        l = N_A_LAYERS + j
        wq = mla_w_uq[j].reshape(Q_LORA, MLA_HEADS, QK_NOPE + QK_ROPE)
        w_nope, w_rope = wq[..., :QK_NOPE], wq[..., QK_NOPE:]
        wa = _head_pad_cols(w_nope, w_rope, Q_LORA).astype(BF16)
        wb = _head_pad_cols(jnp.zeros_like(w_nope), _swap_halves(w_rope), Q_LORA).astype(BF16)
        q = _q_call(x2, mod_all[l], norm_mix_g[l][None], mla_w_dq[j].astype(BF16), mla_q_latent_g[j][None],
                    wa, wb, ca, sb, S)
        o = _attn_call(nfull, nhi, q, k_all, v_all, pos_col, pos_row, B, S)
        x2 = _oproj_call(x2, o, mod_all[l], mla_w_o[j].astype(BF16), S)
        x2 = peer_layer(x2, l, l == DEPTH - 1)

    return x2.reshape(B, S, D)
```

```python
import functools
import math

import jax
import jax.numpy as jnp
from jax import lax
from jax.experimental import pallas as pl
from jax.experimental.pallas import tpu as pltpu

F32 = jnp.float32
BF16 = jnp.bfloat16

D_MODEL = 1024
DEPTH = 4
N_A_LAYERS = DEPTH // 2
LRU_HEADS = 4
LRU_BLOCK = D_MODEL // LRU_HEADS
CONV_WIDTH = 4
LRU_C = 8.0
MLA_HEADS = 16
QK_NOPE = 64
QK_ROPE = 32
V_HEAD = 64
Q_LORA = 384
KV_LORA = 256
ROPE_THETA = 10000.0
ATTN_SCALE = 1.0 / math.sqrt(QK_NOPE + QK_ROPE)
PEER_HEADS = 8
N_KEYS = 128
PEER_TOPK = 16
PEER_HALF = 128
RMS_EPS = 1e-6
NEG_INF = -1e30
LOG2E = math.log2(math.e)
GELU_OUT_SCALE = 0.5 * math.sqrt(2.0)

LANES = 128
SUBLANES = 8
HEAD_PAD = LANES
VMEM_LIMIT = 56 * 1024 * 1024

TS_LRU = 256
TM_TOK = 512
TQ_ATT = 512
TK_ATT = 1024
KSUB_ATT = 512
QSUB_ATT = 256
ONES_ROWS = 16
TM_PEER = 512
EC_PEER = 1024
ESUB_PEER = 256


def _cparams(sem):
    return pltpu.CompilerParams(dimension_semantics=sem, vmem_limit_bytes=VMEM_LIMIT)


def _rms(x, g):
    return x * lax.rsqrt(jnp.mean(x * x, axis=-1, keepdims=True) + RMS_EPS) * g


def _gelu(x):
    return 0.5 * x * (1.0 + lax.erf(x * (1.0 / math.sqrt(2.0))))


def _expm1(y):
    series = y * (1.0 + 0.5 * y * (1.0 + (1.0 / 3.0) * y * (1.0 + 0.25 * y * (1.0 + 0.2 * y))))
    return jnp.where(jnp.abs(y) < 0.05, series, jnp.exp(y) - 1.0)


def _const_spec(shape):
    nd = len(shape)
    return pl.BlockSpec(shape, lambda *_: (0,) * nd)


def _mod_kernel(c_ref, w_ref, b_ref, o_ref):
    c = c_ref[...]
    sc = c * jax.nn.sigmoid(c)
    o_ref[...] = jnp.dot(sc.astype(BF16), w_ref[...].astype(BF16),
                         preferred_element_type=F32) + b_ref[...]


def _mod_call(c_pad, w, b, tn):
    L, D, N = w.shape
    return pl.pallas_call(
        _mod_kernel,
        out_shape=jax.ShapeDtypeStruct((L, SUBLANES, N), F32),
        grid=(L, N // tn),
        in_specs=[pl.BlockSpec((SUBLANES, D), lambda l, n: (0, 0)),
                  pl.BlockSpec((None, D, tn), lambda l, n: (l, 0, n)),
                  pl.BlockSpec((None, 1, tn), lambda l, n: (l, 0, n))],
        out_specs=pl.BlockSpec((None, SUBLANES, tn), lambda l, n: (l, 0, n)),
        compiler_params=_cparams(("arbitrary", "arbitrary")),
        name="adaln_mod",
    )(c_pad, w, b)


def _rope_kernel(pos_ref, freq_ref, ca_ref, sb_ref):
    ang = pos_ref[...].astype(F32) * freq_ref[...]
    lane = lax.broadcasted_iota(jnp.int32, ang.shape, 1)
    cos, sin = jnp.cos(ang), jnp.sin(ang)
    half = QK_ROPE // 2
    ca_ref[...] = jnp.where(lane < QK_NOPE, 1.0, jnp.where(lane < QK_NOPE + QK_ROPE, cos, 0.0))
    sb_ref[...] = jnp.where(lane < QK_NOPE, 0.0,
                            jnp.where(lane < QK_NOPE + half, -sin,
                                      jnp.where(lane < QK_NOPE + QK_ROPE, sin, 0.0)))


def _rope_call(pos_col, freq_lane):
    T = pos_col.shape[0]
    tm = TM_TOK
    spec = pl.BlockSpec((tm, LANES), lambda i: (i, 0))
    return pl.pallas_call(
        _rope_kernel,
        out_shape=(jax.ShapeDtypeStruct((T, LANES), F32),) * 2,
        grid=(T // tm,),
        in_specs=[pl.BlockSpec((tm, 1), lambda i: (i, 0)), _const_spec((1, LANES))],
        out_specs=(spec, spec),
        compiler_params=_cparams(("arbitrary",)),
        name="rope_tables",
    )(pos_col, freq_lane)


def _shift_rows(x, d, fill):
    n = x.shape[0]
    if d % SUBLANES == 0:
        return jnp.concatenate([jnp.full((d, x.shape[1]), fill, x.dtype), x[:n - d]], axis=0)
    rolled = pltpu.roll(x, d, 0)
    row = lax.broadcasted_iota(jnp.int32, x.shape, 0)
    return jnp.where(row < d, fill, rolled)


def _lru_kernel(x_ref, mod_ref, ng_ref, win_ref, cw_ref, cb_ref, wa_ref, ba_ref, wx_ref, bx_ref,
                lam_ref, wout_ref, o_ref, rec_s, h_s, *, ts):
    D = D_MODEL

    @pl.when(pl.program_id(1) == 0)
    def _():
        rec_s[0:SUBLANES, :] = jnp.zeros((SUBLANES, D), F32)
        h_s[...] = jnp.zeros((SUBLANES, D), F32)

    x = x_ref[...]
    mod = mod_ref[...]
    sh1, sc1, g1 = mod[0:1], mod[1:2], mod[2:3]
    h = _rms(x, ng_ref[...]) * (1.0 + sc1) + sh1
    u = jnp.dot(h.astype(BF16), win_ref[...], preferred_element_type=F32)
    gate = _gelu(u[:, :D])
    rec = u[:, D:]

    rec_s[SUBLANES:ts + SUBLANES, :] = rec
    cw = cw_ref[...]
    xc = (cb_ref[...] + rec * cw[3:4]
          + rec_s[SUBLANES - 1:ts + SUBLANES - 1, :] * cw[2:3]
          + rec_s[SUBLANES - 2:ts + SUBLANES - 2, :] * cw[1:2]
          + rec_s[SUBLANES - 3:ts + SUBLANES - 3, :] * cw[0:1])
    rec_s[0:SUBLANES, :] = rec[ts - SUBLANES:ts, :]

    xcb = xc.astype(BF16)
    r_pre = jnp.concatenate(
        [jnp.dot(xcb[:, k * LRU_BLOCK:(k + 1) * LRU_BLOCK], wa_ref[k], preferred_element_type=F32)
         for k in range(LRU_HEADS)], axis=-1)
    i_pre = jnp.concatenate(
        [jnp.dot(xcb[:, k * LRU_BLOCK:(k + 1) * LRU_BLOCK], wx_ref[k], preferred_element_type=F32)
         for k in range(LRU_HEADS)], axis=-1)
    r = jax.nn.sigmoid(r_pre + ba_ref[...])
    gi = jax.nn.sigmoid(i_pre + bx_ref[...])
    z = -lam_ref[...]
    softplus = jnp.maximum(z, 0.0) + jnp.log1p(jnp.exp(-jnp.abs(z)))
    log_a = (-LRU_C) * r * softplus
    a = jnp.exp(log_a)
    b = jnp.sqrt(-_expm1(2.0 * log_a)) * (gi * xc)

    d = 1
    while d < ts:
        a_sh = _shift_rows(a, d, 1.0)
        b_sh = _shift_rows(b, d, 0.0)
        b = a * b_sh + b
        a = a * a_sh
        d *= 2
    h0 = h_s[SUBLANES - 1:SUBLANES, :]
    hs = a * h0 + b
    h_s[...] = hs[ts - SUBLANES:ts, :]

    y = jnp.dot((gate * hs).astype(BF16), wout_ref[...], preferred_element_type=F32)
    o_ref[...] = x + g1 * y


def _lru_call(x, mod, ng, w_in, cw, cb, wa, ba, wx, bx, lam, w_out):
    B, S, D = x.shape
    ts = TS_LRU
    kern = functools.partial(_lru_kernel, ts=ts)
    return pl.pallas_call(
        kern,
        out_shape=jax.ShapeDtypeStruct((B, S, D), F32),
        grid=(B, S // ts),
        in_specs=[pl.BlockSpec((None, ts, D), lambda b, s: (b, s, 0)),
                  pl.BlockSpec((None, 6, D), lambda b, s: (b, 0, 0)),
                  _const_spec((1, D)), _const_spec((D, 2 * D)), _const_spec((CONV_WIDTH, D)),
                  _const_spec((1, D)), _const_spec((LRU_HEADS, LRU_BLOCK, LRU_BLOCK)), _const_spec((1, D)),
                  _const_spec((LRU_HEADS, LRU_BLOCK, LRU_BLOCK)), _const_spec((1, D)), _const_spec((1, D)),
                  _const_spec((D, D))],
        out_specs=pl.BlockSpec((None, ts, D), lambda b, s: (b, s, 0)),
        scratch_shapes=[pltpu.VMEM((ts + SUBLANES, D), F32), pltpu.VMEM((SUBLANES, D), F32)],
        compiler_params=_cparams(("arbitrary", "arbitrary")),
        name="rglru_layer",
    )(x, mod, ng, w_in, cw, cb, wa, ba, wx, bx, lam, w_out)


def _kv_kernel(x_ref, mod_ref, ng_ref, w1_ref, lg_ref, wuk_ref, wuv_ref, ca_ref, sb_ref, k_ref, v_ref):
    x = x_ref[...]
    mod = mod_ref[...]
    shift, scale = mod[0:1], mod[1:2]
    h = _rms(x, ng_ref[...]) * (1.0 + scale) + shift
    t = jnp.dot(h.astype(BF16), w1_ref[...], preferred_element_type=F32)
    c_kv = _rms(t[:, :KV_LORA], lg_ref[...])
    kr = t[:, KV_LORA:KV_LORA + LANES] * ca_ref[...] + t[:, KV_LORA + LANES:] * sb_ref[...]
    ckb = c_kv.astype(BF16)
    kn = jnp.dot(ckb, wuk_ref[...], preferred_element_type=F32)
    k_ref[...] = (kn + jnp.concatenate([kr] * MLA_HEADS, axis=-1)).astype(BF16)
    v_ref[...] = jnp.dot(ckb, wuv_ref[...], preferred_element_type=F32).T.astype(BF16)


def _kv_call(x2, mod, ng, w1, lg, wuk, wuv, ca, sb, S):
    T, D = x2.shape
    tm = TM_TOK
    nb = S // tm
    HP = MLA_HEADS * HEAD_PAD
    HV = MLA_HEADS * V_HEAD
    return pl.pallas_call(
        _kv_kernel,
        out_shape=(jax.ShapeDtypeStruct((T, HP), BF16), jax.ShapeDtypeStruct((HV, T), BF16)),
        grid=(T // tm,),
        in_specs=[pl.BlockSpec((tm, D), lambda i: (i, 0)),
                  pl.BlockSpec((None, 2, D), lambda i: (i // nb, 0, 0)),
                  _const_spec((1, D)), _const_spec(w1.shape), _const_spec((1, KV_LORA)),
                  _const_spec(wuk.shape), _const_spec(wuv.shape),
                  pl.BlockSpec((tm, LANES), lambda i: (i, 0)), pl.BlockSpec((tm, LANES), lambda i: (i, 0))],
        out_specs=(pl.BlockSpec((tm, HP), lambda i: (i, 0)), pl.BlockSpec((HV, tm), lambda i: (0, i))),
        compiler_params=_cparams(("arbitrary",)),
        name="shared_kv",
    )(x2, mod, ng, w1, lg, wuk, wuv, ca, sb)


def _q_kernel(x_ref, mod_ref, ng_ref, wdq_ref, lg_ref, wa_ref, wb_ref, ca_ref, sb_ref, q_ref):
    x = x_ref[...]
    mod = mod_ref[...]
    sh1, sc1 = mod[0:1], mod[1:2]
    h = _rms(x, ng_ref[...]) * (1.0 + sc1) + sh1
    ql = _rms(jnp.dot(h.astype(BF16), wdq_ref[...], preferred_element_type=F32), lg_ref[...])
    qlb = ql.astype(BF16)
    qa = jnp.dot(qlb, wa_ref[...], preferred_element_type=F32)
    qb = jnp.dot(qlb, wb_ref[...], preferred_element_type=F32)
    ca = jnp.concatenate([ca_ref[...]] * MLA_HEADS, axis=-1)
    sb = jnp.concatenate([sb_ref[...]] * MLA_HEADS, axis=-1)
    q_ref[...] = ((qa * ca + qb * sb) * (ATTN_SCALE * LOG2E)).T.astype(BF16)


def _q_call(x2, mod, ng, wdq, lg, wa, wb, ca, sb, S):
    T, D = x2.shape
    tm = TM_TOK
    nb = S // tm
    HP = MLA_HEADS * HEAD_PAD
    return pl.pallas_call(
        _q_kernel,
        out_shape=jax.ShapeDtypeStruct((HP, T), BF16),
        grid=(T // tm,),
        in_specs=[pl.BlockSpec((tm, D), lambda i: (i, 0)),
                  pl.BlockSpec((None, 6, D), lambda i: (i // nb, 0, 0)),
                  _const_spec((1, D)), _const_spec(wdq.shape), _const_spec((1, Q_LORA)),
                  _const_spec(wa.shape), _const_spec(wb.shape),
                  pl.BlockSpec((tm, LANES), lambda i: (i, 0)), pl.BlockSpec((tm, LANES), lambda i: (i, 0))],
        out_specs=pl.BlockSpec((HP, tm), lambda i: (0, i)),
        compiler_params=_cparams(("arbitrary",)),
        name="mla_queries",
    )(x2, mod, ng, wdq, lg, wa, wb, ca, sb)


def _attn_kernel(nfull_ref, nhi_ref, qt_ref, k_ref, vt_ref, pq_ref, pk_ref, o_ref, *, tq, tk, nq):
    b = pl.program_id(0)
    qi = pl.program_id(2)
    nfull = nfull_ref[b * nq + qi]
    nhi = nhi_ref[b * nq + qi]
    pos_q = pq_ref[...]

    def step(kc, carry, masked):
        nstate = 2 * (tq // QSUB_ATT)
        nstream = nstate * (tk // KSUB_ATT)
        state = list(carry)

        def split(sidx):
            kb, sid = divmod(sidx, nstate)
            hh, qs = divmod(sid, tq // QSUB_ATT)
            return pl.multiple_of(kc * tk + kb * KSUB_ATT, KSUB_ATT), sid, hh, qs

        def scores(sidx):
            off, _, hh, qs = split(sidx)
            k = k_ref[pl.ds(off, KSUB_ATT), hh * HEAD_PAD:(hh + 1) * HEAD_PAD]
            qt = qt_ref[hh * HEAD_PAD:(hh + 1) * HEAD_PAD, qs * QSUB_ATT:(qs + 1) * QSUB_ATT]
            return jnp.dot(k, qt, preferred_element_type=F32)

        def update(sidx, st):
            off, sid, hh, qs = split(sidx)
            m, acc = state[sid]
            if masked:
                keep = pk_ref[pl.ds(off, KSUB_ATT), :] <= pos_q[:, qs * QSUB_ATT:(qs + 1) * QSUB_ATT]
                st = jnp.where(keep, st, NEG_INF)
            m_new = jnp.maximum(m, jnp.max(st, axis=0, keepdims=True))
            p = jnp.exp2((st - m_new).astype(BF16))
            alpha = jnp.exp2(m - m_new)
            vt = jnp.concatenate([vt_ref[hh * V_HEAD:(hh + 1) * V_HEAD, pl.ds(off, KSUB_ATT)],
                                  jnp.ones((ONES_ROWS, KSUB_ATT), BF16)], axis=0)
            state[sid] = (m_new, alpha * acc + jnp.dot(vt, p, preferred_element_type=F32))

        st_next = scores(0)
        for sidx in range(nstream):
            st_cur = st_next
            if sidx + 1 < nstream:
                st_next = scores(sidx + 1)
            update(sidx, st_cur)
        return tuple(state)

    nqs = tq // QSUB_ATT
    one = (jnp.full((1, QSUB_ATT), NEG_INF, F32), jnp.zeros((V_HEAD + ONES_ROWS, QSUB_ATT), F32))
    carry = lax.fori_loop(0, nfull, functools.partial(step, masked=False), (one,) * (2 * nqs))
    fin = lax.fori_loop(nfull, nhi, functools.partial(step, masked=True), carry)
    heads = [jnp.concatenate([fin[hh * nqs + qs][1][:V_HEAD] / fin[hh * nqs + qs][1][V_HEAD:V_HEAD + 1]
                              for qs in range(nqs)], axis=1) for hh in range(2)]
    o_ref[...] = jnp.concatenate(heads, axis=0).T.astype(BF16)


def _attn_call(nfull, nhi, qt, k, vt, pos_col, pos_row, B, S):
    tq, tk = TQ_ATT, TK_ATT
    nq = S // tq
    T = B * S
    kern = functools.partial(_attn_kernel, tq=tq, tk=tk, nq=nq)
    grid_spec = pltpu.PrefetchScalarGridSpec(
        num_scalar_prefetch=2,
        grid=(B, MLA_HEADS // 2, nq),
        in_specs=[pl.BlockSpec((2 * HEAD_PAD, tq), lambda b, h, i, *_: (h, b * nq + i)),
                  pl.BlockSpec((S, 2 * HEAD_PAD), lambda b, h, i, *_: (b, h)),
                  pl.BlockSpec((2 * V_HEAD, S), lambda b, h, i, *_: (h, b)),
                  pl.BlockSpec((None, 1, tq), lambda b, h, i, *_: (b, 0, i)),
                  pl.BlockSpec((S, 1), lambda b, h, i, *_: (b, 0))],
        out_specs=pl.BlockSpec((tq, 2 * V_HEAD), lambda b, h, i, *_: (b * nq + i, h)),
    )
    return pl.pallas_call(
        kern,
        out_shape=jax.ShapeDtypeStruct((T, MLA_HEADS * V_HEAD), BF16),
        grid_spec=grid_spec,
        compiler_params=_cparams(("arbitrary", "arbitrary", "arbitrary")),
        name="mla_attention",
    )(nfull, nhi, qt, k, vt, pos_row, pos_col)


def _oproj_kernel(x_ref, o_ref, mod_ref, w_ref, out_ref):
    g1 = mod_ref[...][2:3]
    y = jnp.dot(o_ref[...], w_ref[...], preferred_element_type=F32)
    out_ref[...] = x_ref[...] + g1 * y


def _oproj_call(x2, o, mod, w_o, S):
    T, D = x2.shape
    tm = TM_TOK
    nb = S // tm
    return pl.pallas_call(
        _oproj_kernel,
        out_shape=jax.ShapeDtypeStruct((T, D), F32),
        grid=(T // tm,),
        in_specs=[pl.BlockSpec((tm, D), lambda i: (i, 0)),
                  pl.BlockSpec((tm, o.shape[1]), lambda i: (i, 0)),
                  pl.BlockSpec((None, 6, D), lambda i: (i // nb, 0, 0)),
                  _const_spec(w_o.shape)],
        out_specs=pl.BlockSpec((tm, D), lambda i: (i, 0)),
        compiler_params=_cparams(("arbitrary",)),
        name="mla_out_proj",
    )(x2, o, mod, w_o)


TOP_ROWS = 3 * SUBLANES
LEVEL_STEP = 256.0


def _top17_rows(x):
    rows = lax.broadcasted_iota(jnp.int32, (TOP_ROWS, x.shape[1]), 0)
    out = jnp.full((TOP_ROWS, x.shape[1]), NEG_INF, F32)
    for j in range(PEER_TOPK + 1):
        m = jnp.max(x, axis=0, keepdims=True)
        out = jnp.where(rows == j, m, out)
        x = jnp.where(x == m, NEG_INF, x)
    return out


def _count_ge(x, y_rows):
    below = jnp.zeros(x.shape, jnp.int32)
    for j in range(PEER_TOPK + 1):
        diff = lax.bitcast_convert_type(x - y_rows[j:j + 1], jnp.int32)
        below = below + lax.shift_right_logical(diff, 31)
    return float(PEER_TOPK + 1) - below.astype(F32)


def _kth_and_next_largest_rows(x, k):
    shape = (1, x.shape[1])
    tk, tn = jnp.full(shape, NEG_INF, F32), jnp.full(shape, NEG_INF, F32)
    cnt = jnp.zeros(shape, F32)
    for _ in range(k + 1):
        m = jnp.max(x, axis=0, keepdims=True)
        eq = x == m
        tk = jnp.where(cnt < k, m, tk)
        tn = jnp.where(cnt < k + 1, m, tn)
        cnt = cnt + jnp.sum(jnp.where(eq, 1.0, 0.0), axis=0, keepdims=True)
        x = jnp.where(eq, NEG_INF, x)
    return tk, tn


def _peer_kernel(x_ref, mod_ref, ng_ref, wqt_ref, keys_ref, u_ref, vt_ref, fg_ref, o_ref,
                 h_s, st_s, thr_s, e1_s, lv_s, e2_s, wa_s, acc_s, *, tm, ec, final_norm):
    D = D_MODEL
    ntc = tm // LANES
    e = pl.program_id(1)
    ne = pl.num_programs(1)

    @pl.when(e == 0)
    def _scores():
        x = x_ref[...]
        mod = mod_ref[...]
        sh2, sc2 = mod[3:4], mod[4:5]
        ht = (_rms(x, ng_ref[...]) * (1.0 + sc2) + sh2).T.astype(BF16)
        h_s[...] = ht
        qt = jnp.dot(wqt_ref[...], ht, preferred_element_type=F32)
        qtb = qt.astype(BF16)
        for hp in range(2 * PEER_HEADS):
            st = jnp.dot(keys_ref[hp], qtb[hp * PEER_HALF:(hp + 1) * PEER_HALF, :],
                         preferred_element_type=F32)
            for tc in range(ntc):
                st_s[tc, hp] = st[:, tc * LANES:(tc + 1) * LANES]

        def per_block(i, _):
            tc = i // PEER_HEADS
            hd = i % PEER_HEADS
            s1 = st_s[tc, 2 * hd]
            s2 = st_s[tc, 2 * hd + 1]
            a = _top17_rows(s1)
            b = _top17_rows(s2)
            cands = [a[0:1] + b]
            for r in range(1, SUBLANES):
                cands.append(a[r:r + 1] + b[0:SUBLANES])
            cands.append(a[SUBLANES:TOP_ROWS] + b[0:1])
            cand = jnp.concatenate(cands, axis=0)
            c16, c17 = _kth_and_next_largest_rows(cand, PEER_TOPK)
            tau = 0.5 * (c16 + c17)
            m1, m2 = a[0:1], b[0:1]
            zsum = jnp.sum(jnp.where(cand >= tau, jnp.exp(cand - (m1 + m2)), 0.0), axis=0, keepdims=True)
            level2 = _count_ge(s2, b)
            cnt = _count_ge(s1 - tau, -b)
            thr_s[tc, hd] = (float(PEER_TOPK + 1) - cnt) * LEVEL_STEP
            e1_s[tc, hd] = jnp.exp(s1 - m1) / zsum
            lv_s[tc, hd] = (level2 * LEVEL_STEP).astype(BF16)
            e2_s[tc, hd] = jnp.exp(s2 - m2).astype(BF16)
            return 0

        lax.fori_loop(0, ntc * PEER_HEADS, per_block, 0)
        acc_s[...] = jnp.zeros(acc_s.shape, F32)

    ht = h_s[...]
    ngrp = ec // LANES
    nsub = ec // ESUB_PEER
    rows_per_sub = ESUB_PEER // LANES
    key_rows = {}

    def key_row(ref, g8, tc, hd, j):
        if (id(ref), g8, tc, hd) not in key_rows:
            base = pl.multiple_of(e * ngrp + g8 * SUBLANES, SUBLANES)
            key_rows[(id(ref), g8, tc, hd)] = ref[tc, hd, pl.ds(base, SUBLANES), :].astype(BF16)
        return key_rows[(id(ref), g8, tc, hd)][j:j + 1, :]

    def activations(k):
        return jnp.dot(u_ref[k * ESUB_PEER:(k + 1) * ESUB_PEER, :], ht, preferred_element_type=F32)

    def softmax_weights(k):
        for tc in range(ntc):
            w = [None] * rows_per_sub
            for hd in range(PEER_HEADS):
                lv, e2 = lv_s[tc, hd], e2_s[tc, hd]
                for r in range(rows_per_sub):
                    g8, j = divmod(k * rows_per_sub + r, SUBLANES)
                    sel = jnp.minimum(e2, jnp.maximum(lv - key_row(thr_s, g8, tc, hd, j), 0.0))
                    term = sel * key_row(e1_s, g8, tc, hd, j)
                    w[r] = term if w[r] is None else w[r] + term
            for r in range(rows_per_sub):
                r0 = (k * rows_per_sub + r) * LANES
                wa_s[r0:r0 + LANES, tc * LANES:(tc + 1) * LANES] = w[r]

    def apply_activations(k, at):
        rows = slice(k * ESUB_PEER, (k + 1) * ESUB_PEER)
        wa_s[rows, :] = wa_s[rows, :] * _gelu(at).astype(BF16)

    def values(k0, k1):
        cols = slice(k0 * ESUB_PEER, k1 * ESUB_PEER)
        acc_s[...] += jnp.dot(vt_ref[:, cols], wa_s[cols, :], preferred_element_type=F32)

    half = nsub // 2
    at_next = activations(0)
    softmax_weights(0)
    for k in range(nsub):
        at_cur = at_next
        if k + 1 < nsub:
            at_next = activations(k + 1)
        if k == half:
            values(0, half)
        apply_activations(k, at_cur)
        if k + 1 < nsub:
            softmax_weights(k + 1)
    values(half, nsub)

    @pl.when(e == ne - 1)
    def _finish():
        g2 = mod_ref[...][5:6]
        xn = x_ref[...] + g2 * acc_s[...].T
        if final_norm:
            xn = _rms(xn, fg_ref[...])
        o_ref[...] = xn


def _peer_call(x2, mod, ng, wqt, keys, u, vt, fg, S, final_norm):
    T, D = x2.shape
    E = u.shape[0]
    tm, ec = TM_PEER, EC_PEER
    nb = S // tm
    ntc = tm // LANES
    kern = functools.partial(_peer_kernel, tm=tm, ec=ec, final_norm=final_norm)
    return pl.pallas_call(
        kern,
        out_shape=jax.ShapeDtypeStruct((T, D), F32),
        grid=(T // tm, E // ec),
        in_specs=[pl.BlockSpec((tm, D), lambda i, e: (i, 0)),
                  pl.BlockSpec((None, 6, D), lambda i, e: (i // nb, 0, 0)),
                  _const_spec((1, D)), _const_spec(wqt.shape), _const_spec(keys.shape),
                  pl.BlockSpec((ec, D), lambda i, e: (e, 0)),
                  pl.BlockSpec((D, ec), lambda i, e: (0, e)),
                  _const_spec((1, D))],
        out_specs=pl.BlockSpec((tm, D), lambda i, e: (i, 0)),
        scratch_shapes=[pltpu.VMEM((D, tm), BF16),
                        pltpu.VMEM((ntc, 2 * PEER_HEADS, N_KEYS, LANES), F32),
                        pltpu.VMEM((ntc, PEER_HEADS, N_KEYS, LANES), F32),
                        pltpu.VMEM((ntc, PEER_HEADS, N_KEYS, LANES), F32),
                        pltpu.VMEM((ntc, PEER_HEADS, N_KEYS, LANES), BF16),
                        pltpu.VMEM((ntc, PEER_HEADS, N_KEYS, LANES), BF16),
                        pltpu.VMEM((ec, tm), BF16),
                        pltpu.VMEM((D, tm), F32)],
        compiler_params=_cparams(("arbitrary", "arbitrary")),
        name="peer_layer",
    )(x2, mod, ng, wqt, keys, u, vt, fg)


PAD_ROWS = 16


def _peer_kernel_x(x_ref, mod_ref, ng_ref, wqt_ref, keys_ref, u_ref, vt_ref, fg_ref, o_ref,
                   h_s, st_s, thr_s, e1_s, e2f_s, le_s, wa_s, acc_s, *, tm, ec, final_norm, var):
    D = D_MODEL
    ntc = tm // LANES
    e = pl.program_id(1)
    ne = pl.num_programs(1)
    levels, wbuild, nsub, use_ht, pad = var["levels"], var["wbuild"], var["nsub"], var["ht"], var["pad"]
    e2_off = N_KEYS + (PAD_ROWS if pad else 0)

    @pl.when(e == 0)
    def _scores():
        x = x_ref[...]
        mod = mod_ref[...]
        sh2, sc2 = mod[3:4], mod[4:5]
        hn = _rms(x, ng_ref[...]) * (1.0 + sc2) + sh2
        if use_ht:
            ht = hn.T.astype(BF16)
            h_s[...] = ht
            qt = jnp.dot(wqt_ref[...], ht, preferred_element_type=F32)
        else:
            hb = hn.astype(BF16)
            h_s[...] = hb
            qt = lax.dot_general(wqt_ref[...], hb, (((1,), (1,)), ((), ())), preferred_element_type=F32)
        qtb = qt.astype(BF16)
        for hp in range(2 * PEER_HEADS):
            st = jnp.dot(keys_ref[hp], qtb[hp * PEER_HALF:(hp + 1) * PEER_HALF, :],
                         preferred_element_type=F32)
            for tc in range(ntc):
                st_s[tc, hp] = st[:, tc * LANES:(tc + 1) * LANES]

        nbatch = var.get("batch", 1)

        def per_group(g, _):
            for bi in range(nbatch):
                per_block(g * nbatch + bi)
            return 0

        def per_block(i):
            tc = i // PEER_HEADS
            hd = i % PEER_HEADS
            s1 = st_s[tc, 2 * hd]
            s2 = st_s[tc, 2 * hd + 1]
            a = _top17_rows(s1)
            b = _top17_rows(s2)
            cands = [a[0:1] + b]
            for r in range(1, SUBLANES):
                cands.append(a[r:r + 1] + b[0:SUBLANES])
            cands.append(a[SUBLANES:TOP_ROWS] + b[0:1])
            cand = jnp.concatenate(cands, axis=0)
            c16, c17 = _kth_and_next_largest_rows(cand, PEER_TOPK)
            tau = 0.5 * (c16 + c17)
            m1, m2 = a[0:1], b[0:1]
            zsum = jnp.sum(jnp.where(cand >= tau, jnp.exp(cand - (m1 + m2)), 0.0), axis=0, keepdims=True)
            e1_s[tc, hd] = jnp.exp(s1 - m1) * ((GELU_OUT_SCALE if var.get("gfold", False) else 1.0) / zsum)
            if levels:
                level2 = _count_ge(s2, b)
                cnt = _count_ge(s1 - tau, -b)
                thr_s[tc, hd] = (float(PEER_TOPK + 1) - cnt) * LEVEL_STEP
                le_s[tc, hd, 0:N_KEYS, :] = (level2 * LEVEL_STEP).astype(BF16)
                le_s[tc, hd, e2_off:e2_off + N_KEYS, :] = jnp.exp(s2 - m2).astype(BF16)
            else:
                thr_s[tc, hd] = jnp.exp(tau - s1 - m2)
                e2f_s[tc, hd] = jnp.exp(s2 - m2)

        lax.fori_loop(0, ntc * PEER_HEADS // nbatch, per_group, 0)
        acc_s[...] = jnp.zeros(acc_s.shape, F32)

    ngrp = ec // LANES
    esub = ec // nsub
    rows_per_sub = esub // LANES
    key_rows = {}
    wdt = F32 if wbuild.startswith("f32") else BF16

    def key_row(ref, g8, tc, hd, j):
        if (id(ref), g8, tc, hd) not in key_rows:
            base = pl.multiple_of(e * ngrp + g8 * SUBLANES, SUBLANES)
            key_rows[(id(ref), g8, tc, hd)] = ref[tc, hd, pl.ds(base, SUBLANES), :].astype(wdt)
        return key_rows[(id(ref), g8, tc, hd)][j:j + 1, :]

    def activations(k):
        if use_ht:
            return jnp.dot(u_ref[k * esub:(k + 1) * esub, :], h_s[...], preferred_element_type=F32)
        return lax.dot_general(u_ref[k * esub:(k + 1) * esub, :], h_s[...], (((1,), (1,)), ((), ())),
                               preferred_element_type=F32)

    def masked_weights(tc, hd, g8, j):
        thr = key_row(thr_s, g8, tc, hd, j)
        if wbuild in ("f32sel", "f32mix", "f32bits"):
            e2 = e2f_s[tc, hd]
            if wbuild == "f32sel" or (wbuild == "f32mix" and hd < PEER_HEADS // 2):
                return jnp.where(e2 >= thr, e2, 0.0)
            keep = lax.shift_right_arithmetic(lax.bitcast_convert_type(thr - e2, jnp.int32), 31)
            return lax.bitcast_convert_type(lax.bitcast_convert_type(e2, jnp.int32) & keep, F32)
        lv = le_s[tc, hd, 0:N_KEYS, :]
        e2 = le_s[tc, hd, e2_off:e2_off + N_KEYS, :]
        if wbuild == "bf16sel":
            return jnp.where(lv > thr, e2, 0.0)
        return jnp.minimum(e2, jnp.maximum(lv - thr, 0.0))

    gfold = var.get("gfold", False)

    def act_fn(at):
        return at * (1.0 + lax.erf(at)) if gfold else _gelu(at)

    ncol = var.get("colsub", 1)
    if ncol > 1:
        cw = tm // ncol
        tc_per = cw // LANES
        nrows = ec // LANES

        def act_cols(c):
            return jnp.dot(u_ref[...], h_s[:, c * cw:(c + 1) * cw], preferred_element_type=F32)

        def weigh_cols(c, at):
            gact = act_fn(at)
            for tl in range(tc_per):
                tc = c * tc_per + tl
                for r in range(nrows):
                    g8, j = divmod(r, SUBLANES)
                    w = None
                    for hd in range(PEER_HEADS):
                        term = masked_weights(tc, hd, g8, j) * key_row(e1_s, g8, tc, hd, j)
                        w = term if w is None else w + term
                    wa_s[r * LANES:(r + 1) * LANES, tc * LANES:(tc + 1) * LANES] = (
                        w * gact[r * LANES:(r + 1) * LANES, tl * LANES:(tl + 1) * LANES]).astype(BF16)

        def values_cols(c):
            cols = slice(c * cw, (c + 1) * cw)
            acc_s[:, cols] += jnp.dot(vt_ref[...], wa_s[:, cols], preferred_element_type=F32)

        at_next = act_cols(0)
        for c in range(ncol):
            at_cur = at_next
            if c + 1 < ncol:
                at_next = act_cols(c + 1)
            weigh_cols(c, at_cur)
            values_cols(c)
    elif nsub == 1:
        gact = act_fn(activations(0))
        if not wbuild.startswith("f32"):
            gact = gact.astype(BF16)
        for tc in range(ntc):
            for r in range(rows_per_sub):
                g8, j = divmod(r, SUBLANES)
                w = None
                for hd in range(PEER_HEADS):
                    term = masked_weights(tc, hd, g8, j) * key_row(e1_s, g8, tc, hd, j)
                    w = term if w is None else w + term
                rr = r * LANES
                wa_s[rr:rr + LANES, tc * LANES:(tc + 1) * LANES] = (
                    w * gact[rr:rr + LANES, tc * LANES:(tc + 1) * LANES]).astype(BF16)
        acc_s[...] += jnp.dot(vt_ref[...], wa_s[...], preferred_element_type=F32)
    else:
        def weigh(k, at):
            gact = _gelu(at)
            if not wbuild.startswith("f32"):
                gact = gact.astype(BF16)
            for tc in range(ntc):
                for r in range(rows_per_sub):
                    g8, j = divmod(k * rows_per_sub + r, SUBLANES)
                    w = None
                    for hd in range(PEER_HEADS):
                        term = masked_weights(tc, hd, g8, j) * key_row(e1_s, g8, tc, hd, j)
                        w = term if w is None else w + term
                    r0 = (k * rows_per_sub + r) * LANES
                    wa_s[r0:r0 + LANES, tc * LANES:(tc + 1) * LANES] = (
                        w * gact[r * LANES:(r + 1) * LANES, tc * LANES:(tc + 1) * LANES]).astype(BF16)

        def values(k0, k1):
            cols = slice(k0 * esub, k1 * esub)
            acc_s[...] += jnp.dot(vt_ref[:, cols], wa_s[cols, :], preferred_element_type=F32)

        half = nsub // 2
        at_next = activations(0)
        for k in range(nsub):
            at_cur = at_next
            if k + 1 < nsub:
                at_next = activations(k + 1)
            if k == half:
                values(0, half)
            weigh(k, at_cur)
        values(half, nsub)

    @pl.when(e == ne - 1)
    def _finish():
        g2 = mod_ref[...][5:6]
        xn = x_ref[...] + g2 * acc_s[...].T
        if final_norm:
            xn = _rms(xn, fg_ref[...])
        o_ref[...] = xn


def _peer_call_x(x2, mod, ng, wqt, keys, u, vt, fg, S, final_norm, var):
    T, D = x2.shape
    E = u.shape[0]
    tm, ec = TM_PEER, var.get("ec", EC_PEER)
    nb = S // tm
    ntc = tm // LANES
    kern = functools.partial(_peer_kernel_x, tm=tm, ec=ec, final_norm=final_norm, var=var)
    le_rows = 2 * N_KEYS + 2 * PAD_ROWS
    return pl.pallas_call(
        kern,
        out_shape=jax.ShapeDtypeStruct((T, D), F32),
        grid=(T // tm, E // ec),
        in_specs=[pl.BlockSpec((tm, D), lambda i, e: (i, 0)),
                  pl.BlockSpec((None, 6, D), lambda i, e: (i // nb, 0, 0)),
                  _const_spec((1, D)), _const_spec(wqt.shape), _const_spec(keys.shape),
                  pl.BlockSpec((ec, D), lambda i, e: (e, 0)),
                  pl.BlockSpec((D, ec), lambda i, e: (0, e)),
                  _const_spec((1, D))],
        out_specs=pl.BlockSpec((tm, D), lambda i, e: (i, 0)),
        scratch_shapes=[pltpu.VMEM((D, tm) if var["ht"] else (tm, D), BF16),
                        pltpu.VMEM((ntc, 2 * PEER_HEADS, N_KEYS, LANES), F32),
                        pltpu.VMEM((ntc, PEER_HEADS, N_KEYS, LANES), F32),
                        pltpu.VMEM((ntc, PEER_HEADS, N_KEYS, LANES), F32),
                        pltpu.VMEM((ntc, PEER_HEADS, N_KEYS, LANES), F32),
                        pltpu.VMEM((ntc, PEER_HEADS, le_rows, LANES), BF16),
                        pltpu.VMEM((ec, tm), BF16),
                        pltpu.VMEM((D, tm), F32)],
        compiler_params=_cparams(("arbitrary", "arbitrary")),
        name="peer_layer",
    )(x2, mod, ng, wqt, keys, u, vt, fg)


PEER_VARIANTS = (
    dict(levels=False, wbuild="f32sel", nsub=1, ht=True, pad=False, batch=4, gfold=True),
    dict(levels=False, wbuild="f32sel", nsub=1, ht=True, pad=False, batch=4, gfold=True, colsub=2),
    dict(levels=False, wbuild="f32sel", nsub=1, ht=True, pad=False, batch=4, gfold=True, ec=2048),
    dict(levels=False, wbuild="f32sel", nsub=1, ht=True, pad=False, batch=4, gfold=True, ec=2048, colsub=2),
)


def _head_pad_cols(w_nope, w_rope, n_in):
    zeros = jnp.zeros((n_in, MLA_HEADS, HEAD_PAD - QK_NOPE - QK_ROPE), w_nope.dtype)
    return jnp.concatenate([w_nope, w_rope, zeros], axis=-1).reshape(n_in, MLA_HEADS * HEAD_PAD)


def _swap_halves(w):
    half = QK_ROPE // 2
    return jnp.concatenate([w[..., half:], w[..., :half]], axis=-1)


def kernel(x, c, positions, ada_w, ada_b, norm_mix_g, norm_ffn_g, lru_w_in, lru_conv_w, lru_conv_b, lru_wa, lru_ba, lru_wx, lru_bx, lru_lambda, lru_w_out, kv_ada_w, kv_ada_b, kv_norm_g, mla_w_dkv, mla_w_kr, mla_kv_latent_g, mla_w_uk, mla_w_uv, mla_w_dq, mla_q_latent_g, mla_w_uq, mla_w_o, peer_w_q, peer_sub_keys, peer_u, peer_v, final_g):
    B, S, D = x.shape
    T = B * S
    assert D == D_MODEL and S % TM_TOK == 0 and S % TS_LRU == 0 and S % TQ_ATT == 0 and S % TM_PEER == 0

    c_pad = jnp.zeros((SUBLANES, D), F32).at[:B].set(c)
    mod_all = _mod_call(c_pad, ada_w, ada_b[:, None, :], 1536)
    mod_all = mod_all[:, :B].reshape(DEPTH, B, 6, D)
    mod_kv = _mod_call(c_pad, kv_ada_w[None], kv_ada_b[None, None, :], 1024)[0, :B].reshape(B, 2, D)

    half = QK_ROPE // 2
    freqs = ROPE_THETA ** (-jnp.arange(half, dtype=F32) / half)
    freq_lane = jnp.zeros((1, LANES), F32).at[0, QK_NOPE:QK_NOPE + QK_ROPE].set(jnp.concatenate([freqs, freqs]))
    pos_col = positions.reshape(T, 1)
    ca, sb = _rope_call(pos_col, freq_lane)

    nq, nk = S // TQ_ATT, S // TK_ATT
    pq = positions.reshape(B, nq, TQ_ATT)
    pk = positions.reshape(B, nk, TK_ATT)
    qmin, qmax = pq.min(-1), pq.max(-1)
    kmin, kmax = pk.min(-1), pk.max(-1)
    needed = kmin[:, None, :] <= qmax[:, :, None]
    nhi = jnp.max(jnp.where(needed, jnp.arange(1, nk + 1, dtype=jnp.int32), 0), axis=-1)
    full = kmax[:, None, :] <= qmin[:, :, None]
    nfull = jnp.min(jnp.where(full, nk, jnp.arange(nk, dtype=jnp.int32)), axis=-1)
    nfull = jnp.minimum(nfull, nhi).astype(jnp.int32).reshape(-1)
    nhi = nhi.astype(jnp.int32).reshape(-1)
    pos_row = positions.reshape(B, 1, S)

    def peer_layer(x2, l, final_norm):
        wqt = peer_w_q[l].T.astype(BF16)
        keys = peer_sub_keys[l].reshape(2 * PEER_HEADS, N_KEYS, PEER_HALF).astype(BF16)
        u = peer_u[l]
        if PEER_VARIANTS[l].get("gfold", False):
            u = u * (1.0 / math.sqrt(2.0))
        u = u.astype(BF16)
        vt = peer_v[l].T.astype(BF16)
        return _peer_call_x(x2, mod_all[l], norm_ffn_g[l][None], wqt, keys, u, vt, final_g[None], S, final_norm,
                            PEER_VARIANTS[l])

    for l in range(N_A_LAYERS):
        x = _lru_call(x, mod_all[l], norm_mix_g[l][None], lru_w_in[l].astype(BF16), lru_conv_w[l],
                      lru_conv_b[l][None], lru_wa[l].astype(BF16), lru_ba[l][None], lru_wx[l].astype(BF16),
                      lru_bx[l][None], lru_lambda[l][None], lru_w_out[l].astype(BF16))
        x = peer_layer(x.reshape(T, D), l, False).reshape(B, S, D)

    x2 = x.reshape(T, D)

    zc = jnp.zeros((D, QK_NOPE), F32)
    zt = jnp.zeros((D, HEAD_PAD - QK_NOPE - QK_ROPE), F32)
    w1 = jnp.concatenate([mla_w_dkv, zc, mla_w_kr, zt, zc, _swap_halves(mla_w_kr), zt], axis=-1).astype(BF16)
    wuk = jnp.concatenate([mla_w_uk.reshape(KV_LORA, MLA_HEADS, QK_NOPE),
                           jnp.zeros((KV_LORA, MLA_HEADS, HEAD_PAD - QK_NOPE), F32)], axis=-1)
    wuk = wuk.reshape(KV_LORA, MLA_HEADS * HEAD_PAD).astype(BF16)
    k_all, v_all = _kv_call(x2, mod_kv, kv_norm_g[None], w1, mla_kv_latent_g[None], wuk,
                            mla_w_uv.astype(BF16), ca, sb, S)

    for j in range(DEPTH - N_A_LAYERS):
        l = N_A_LAYERS + j
        wq = mla_w_uq[j].reshape(Q_LORA, MLA_HEADS, QK_NOPE + QK_ROPE)
        w_nope, w_rope = wq[..., :QK_NOPE], wq[..., QK_NOPE:]
        wa = _head_pad_cols(w_nope, w_rope, Q_LORA).astype(BF16)
        wb = _head_pad_cols(jnp.zeros_like(w_nope), _swap_halves(w_rope), Q_LORA).astype(BF16)
        q = _q_call(x2, mod_all[l], norm_mix_g[l][None], mla_w_dq[j].astype(BF16), mla_q_latent_g[j][None],
                    wa, wb, ca, sb, S)
        o = _attn_call(nfull, nhi, q, k_all, v_all, pos_col, pos_row, B, S)
        x2 = _oproj_call(x2, o, mod_all[l], mla_w_o[j].astype(BF16), S)
        x2 = peer_layer(x2, l, l == DEPTH - 1)

    return x2.reshape(B, S, D)
```

```python
import functools
import math

import jax
import jax.numpy as jnp
from jax import lax
from jax.experimental import pallas as pl
from jax.experimental.pallas import tpu as pltpu

F32 = jnp.float32
BF16 = jnp.bfloat16

D_MODEL = 1024
DEPTH = 4
N_A_LAYERS = DEPTH // 2
LRU_HEADS = 4
LRU_BLOCK = D_MODEL // LRU_HEADS
CONV_WIDTH = 4
LRU_C = 8.0
MLA_HEADS = 16
QK_NOPE = 64
QK_ROPE = 32
V_HEAD = 64
Q_LORA = 384
KV_LORA = 256
ROPE_THETA = 10000.0
ATTN_SCALE = 1.0 / math.sqrt(QK_NOPE + QK_ROPE)
PEER_HEADS = 8
N_KEYS = 128
PEER_TOPK = 16
PEER_HALF = 128
RMS_EPS = 1e-6
NEG_INF = -1e30
LOG2E = math.log2(math.e)
GELU_OUT_SCALE = 0.5 * math.sqrt(2.0)

LANES = 128
SUBLANES = 8
HEAD_PAD = LANES
VMEM_LIMIT = 56 * 1024 * 1024

TS_LRU = 256
TM_TOK = 512
TQ_ATT = 512
TK_ATT = 1024
KSUB_ATT = 512
QSUB_ATT = 256
ONES_ROWS = 16
TM_PEER = 512
EC_PEER = 1024
ESUB_PEER = 256


def _cparams(sem):
    return pltpu.CompilerParams(dimension_semantics=sem, vmem_limit_bytes=VMEM_LIMIT)


def _rms(x, g):
    return x * lax.rsqrt(jnp.mean(x * x, axis=-1, keepdims=True) + RMS_EPS) * g


def _gelu(x):
    return 0.5 * x * (1.0 + lax.erf(x * (1.0 / math.sqrt(2.0))))


def _expm1(y):
    series = y * (1.0 + 0.5 * y * (1.0 + (1.0 / 3.0) * y * (1.0 + 0.25 * y * (1.0 + 0.2 * y))))
    return jnp.where(jnp.abs(y) < 0.05, series, jnp.exp(y) - 1.0)


def _const_spec(shape):
    nd = len(shape)
    return pl.BlockSpec(shape, lambda *_: (0,) * nd)


def _mod_kernel(c_ref, w_ref, b_ref, o_ref):
    c = c_ref[...]
    sc = c * jax.nn.sigmoid(c)
    o_ref[...] = jnp.dot(sc.astype(BF16), w_ref[...].astype(BF16),
                         preferred_element_type=F32) + b_ref[...]


def _mod_call(c_pad, w, b, tn):
    L, D, N = w.shape
    return pl.pallas_call(
        _mod_kernel,
        out_shape=jax.ShapeDtypeStruct((L, SUBLANES, N), F32),
        grid=(L, N // tn),
        in_specs=[pl.BlockSpec((SUBLANES, D), lambda l, n: (0, 0)),
                  pl.BlockSpec((None, D, tn), lambda l, n: (l, 0, n)),
                  pl.BlockSpec((None, 1, tn), lambda l, n: (l, 0, n))],
        out_specs=pl.BlockSpec((None, SUBLANES, tn), lambda l, n: (l, 0, n)),
        compiler_params=_cparams(("arbitrary", "arbitrary")),
        name="adaln_mod",
    )(c_pad, w, b)


def _rope_kernel(pos_ref, freq_ref, ca_ref, sb_ref):
    ang = pos_ref[...].astype(F32) * freq_ref[...]
    lane = lax.broadcasted_iota(jnp.int32, ang.shape, 1)
    cos, sin = jnp.cos(ang), jnp.sin(ang)
    half = QK_ROPE // 2
    ca_ref[...] = jnp.where(lane < QK_NOPE, 1.0, jnp.where(lane < QK_NOPE + QK_ROPE, cos, 0.0))
    sb_ref[...] = jnp.where(lane < QK_NOPE, 0.0,
                            jnp.where(lane < QK_NOPE + half, -sin,
                                      jnp.where(lane < QK_NOPE + QK_ROPE, sin, 0.0)))


def _rope_call(pos_col, freq_lane):
    T = pos_col.shape[0]
    tm = TM_TOK
    spec = pl.BlockSpec((tm, LANES), lambda i: (i, 0))
    return pl.pallas_call(
        _rope_kernel,
        out_shape=(jax.ShapeDtypeStruct((T, LANES), F32),) * 2,
        grid=(T // tm,),
        in_specs=[pl.BlockSpec((tm, 1), lambda i: (i, 0)), _const_spec((1, LANES))],
        out_specs=(spec, spec),
        compiler_params=_cparams(("arbitrary",)),
        name="rope_tables",
    )(pos_col, freq_lane)


def _shift_rows(x, d, fill):
    n = x.shape[0]
    if d % SUBLANES == 0:
        return jnp.concatenate([jnp.full((d, x.shape[1]), fill, x.dtype), x[:n - d]], axis=0)
    rolled = pltpu.roll(x, d, 0)
    row = lax.broadcasted_iota(jnp.int32, x.shape, 0)
    return jnp.where(row < d, fill, rolled)


def _lru_kernel(x_ref, mod_ref, ng_ref, win_ref, cw_ref, cb_ref, wa_ref, ba_ref, wx_ref, bx_ref,
                lam_ref, wout_ref, o_ref, rec_s, h_s, *, ts):
    D = D_MODEL

    @pl.when(pl.program_id(1) == 0)
    def _():
        rec_s[0:SUBLANES, :] = jnp.zeros((SUBLANES, D), F32)
        h_s[...] = jnp.zeros((SUBLANES, D), F32)

    x = x_ref[...]
    mod = mod_ref[...]
    sh1, sc1, g1 = mod[0:1], mod[1:2], mod[2:3]
    h = _rms(x, ng_ref[...]) * (1.0 + sc1) + sh1
    u = jnp.dot(h.astype(BF16), win_ref[...], preferred_element_type=F32)
    gate = _gelu(u[:, :D])
    rec = u[:, D:]

    rec_s[SUBLANES:ts + SUBLANES, :] = rec
    cw = cw_ref[...]
    xc = (cb_ref[...] + rec * cw[3:4]
          + rec_s[SUBLANES - 1:ts + SUBLANES - 1, :] * cw[2:3]
          + rec_s[SUBLANES - 2:ts + SUBLANES - 2, :] * cw[1:2]
          + rec_s[SUBLANES - 3:ts + SUBLANES - 3, :] * cw[0:1])
    rec_s[0:SUBLANES, :] = rec[ts - SUBLANES:ts, :]

    xcb = xc.astype(BF16)
    r_pre = jnp.concatenate(
        [jnp.dot(xcb[:, k * LRU_BLOCK:(k + 1) * LRU_BLOCK], wa_ref[k], preferred_element_type=F32)
         for k in range(LRU_HEADS)], axis=-1)
    i_pre = jnp.concatenate(
        [jnp.dot(xcb[:, k * LRU_BLOCK:(k + 1) * LRU_BLOCK], wx_ref[k], preferred_element_type=F32)
         for k in range(LRU_HEADS)], axis=-1)
    r = jax.nn.sigmoid(r_pre + ba_ref[...])
    gi = jax.nn.sigmoid(i_pre + bx_ref[...])
    z = -lam_ref[...]
    softplus = jnp.maximum(z, 0.0) + jnp.log1p(jnp.exp(-jnp.abs(z)))
    log_a = (-LRU_C) * r * softplus
    a = jnp.exp(log_a)
    b = jnp.sqrt(-_expm1(2.0 * log_a)) * (gi * xc)

    d = 1
    while d < ts:
        a_sh = _shift_rows(a, d, 1.0)
        b_sh = _shift_rows(b, d, 0.0)
        b = a * b_sh + b
        a = a * a_sh
        d *= 2
    h0 = h_s[SUBLANES - 1:SUBLANES, :]
    hs = a * h0 + b
    h_s[...] = hs[ts - SUBLANES:ts, :]

    y = jnp.dot((gate * hs).astype(BF16), wout_ref[...], preferred_element_type=F32)
    o_ref[...] = x + g1 * y


def _lru_call(x, mod, ng, w_in, cw, cb, wa, ba, wx, bx, lam, w_out):
    B, S, D = x.shape
    ts = TS_LRU
    kern = functools.partial(_lru_kernel, ts=ts)
    return pl.pallas_call(
        kern,
        out_shape=jax.ShapeDtypeStruct((B, S, D), F32),
        grid=(B, S // ts),
        in_specs=[pl.BlockSpec((None, ts, D), lambda b, s: (b, s, 0)),
                  pl.BlockSpec((None, 6, D), lambda b, s: (b, 0, 0)),
                  _const_spec((1, D)), _const_spec((D, 2 * D)), _const_spec((CONV_WIDTH, D)),
                  _const_spec((1, D)), _const_spec((LRU_HEADS, LRU_BLOCK, LRU_BLOCK)), _const_spec((1, D)),
                  _const_spec((LRU_HEADS, LRU_BLOCK, LRU_BLOCK)), _const_spec((1, D)), _const_spec((1, D)),
                  _const_spec((D, D))],
        out_specs=pl.BlockSpec((None, ts, D), lambda b, s: (b, s, 0)),
        scratch_shapes=[pltpu.VMEM((ts + SUBLANES, D), F32), pltpu.VMEM((SUBLANES, D), F32)],
        compiler_params=_cparams(("arbitrary", "arbitrary")),
        name="rglru_layer",
    )(x, mod, ng, w_in, cw, cb, wa, ba, wx, bx, lam, w_out)


def _kv_kernel(x_ref, mod_ref, ng_ref, w1_ref, lg_ref, wuk_ref, wuv_ref, ca_ref, sb_ref, k_ref, v_ref):
    x = x_ref[...]
    mod = mod_ref[...]
    shift, scale = mod[0:1], mod[1:2]
    h = _rms(x, ng_ref[...]) * (1.0 + scale) + shift
    t = jnp.dot(h.astype(BF16), w1_ref[...], preferred_element_type=F32)
    c_kv = _rms(t[:, :KV_LORA], lg_ref[...])
    kr = t[:, KV_LORA:KV_LORA + LANES] * ca_ref[...] + t[:, KV_LORA + LANES:] * sb_ref[...]
    ckb = c_kv.astype(BF16)
    kn = jnp.dot(ckb, wuk_ref[...], preferred_element_type=F32)
    k_ref[...] = (kn + jnp.concatenate([kr] * MLA_HEADS, axis=-1)).astype(BF16)
    v_ref[...] = jnp.dot(ckb, wuv_ref[...], preferred_element_type=F32).T.astype(BF16)


def _kv_call(x2, mod, ng, w1, lg, wuk, wuv, ca, sb, S):
    T, D = x2.shape
    tm = TM_TOK
    nb = S // tm
    HP = MLA_HEADS * HEAD_PAD
    HV = MLA_HEADS * V_HEAD
    return pl.pallas_call(
        _kv_kernel,
        out_shape=(jax.ShapeDtypeStruct((T, HP), BF16), jax.ShapeDtypeStruct((HV, T), BF16)),
        grid=(T // tm,),
        in_specs=[pl.BlockSpec((tm, D), lambda i: (i, 0)),
                  pl.BlockSpec((None, 2, D), lambda i: (i // nb, 0, 0)),
                  _const_spec((1, D)), _const_spec(w1.shape), _const_spec((1, KV_LORA)),
                  _const_spec(wuk.shape), _const_spec(wuv.shape),
                  pl.BlockSpec((tm, LANES), lambda i: (i, 0)), pl.BlockSpec((tm, LANES), lambda i: (i, 0))],
        out_specs=(pl.BlockSpec((tm, HP), lambda i: (i, 0)), pl.BlockSpec((HV, tm), lambda i: (0, i))),
        compiler_params=_cparams(("arbitrary",)),
        name="shared_kv",
    )(x2, mod, ng, w1, lg, wuk, wuv, ca, sb)


def _q_kernel(x_ref, mod_ref, ng_ref, wdq_ref, lg_ref, wa_ref, wb_ref, ca_ref, sb_ref, q_ref):
    x = x_ref[...]
    mod = mod_ref[...]
    sh1, sc1 = mod[0:1], mod[1:2]
    h = _rms(x, ng_ref[...]) * (1.0 + sc1) + sh1
    ql = _rms(jnp.dot(h.astype(BF16), wdq_ref[...], preferred_element_type=F32), lg_ref[...])
    qlb = ql.astype(BF16)
    qa = jnp.dot(qlb, wa_ref[...], preferred_element_type=F32)
    qb = jnp.dot(qlb, wb_ref[...], preferred_element_type=F32)
    ca = jnp.concatenate([ca_ref[...]] * MLA_HEADS, axis=-1)
    sb = jnp.concatenate([sb_ref[...]] * MLA_HEADS, axis=-1)
    q_ref[...] = ((qa * ca + qb * sb) * (ATTN_SCALE * LOG2E)).T.astype(BF16)


def _q_call(x2, mod, ng, wdq, lg, wa, wb, ca, sb, S):
    T, D = x2.shape
    tm = TM_TOK
    nb = S // tm
    HP = MLA_HEADS * HEAD_PAD
    return pl.pallas_call(
        _q_kernel,
        out_shape=jax.ShapeDtypeStruct((HP, T), BF16),
        grid=(T // tm,),
        in_specs=[pl.BlockSpec((tm, D), lambda i: (i, 0)),
                  pl.BlockSpec((None, 6, D), lambda i: (i // nb, 0, 0)),
                  _const_spec((1, D)), _const_spec(wdq.shape), _const_spec((1, Q_LORA)),
                  _const_spec(wa.shape), _const_spec(wb.shape),
                  pl.BlockSpec((tm, LANES), lambda i: (i, 0)), pl.BlockSpec((tm, LANES), lambda i: (i, 0))],
        out_specs=pl.BlockSpec((HP, tm), lambda i: (0, i)),
        compiler_params=_cparams(("arbitrary",)),
        name="mla_queries",
    )(x2, mod, ng, wdq, lg, wa, wb, ca, sb)


def _attn_kernel(nfull_ref, nhi_ref, qt_ref, k_ref, vt_ref, pq_ref, pk_ref, o_ref, *, tq, tk, nq):
    b = pl.program_id(0)
    qi = pl.program_id(2)
    nfull = nfull_ref[b * nq + qi]
    nhi = nhi_ref[b * nq + qi]
    pos_q = pq_ref[...]

    def step(kc, carry, masked):
        nstate = 2 * (tq // QSUB_ATT)
        nstream = nstate * (tk // KSUB_ATT)
        state = list(carry)

        def split(sidx):
            kb, sid = divmod(sidx, nstate)
            hh, qs = divmod(sid, tq // QSUB_ATT)
            return pl.multiple_of(kc * tk + kb * KSUB_ATT, KSUB_ATT), sid, hh, qs

        def scores(sidx):
            off, _, hh, qs = split(sidx)
            k = k_ref[pl.ds(off, KSUB_ATT), hh * HEAD_PAD:(hh + 1) * HEAD_PAD]
            qt = qt_ref[hh * HEAD_PAD:(hh + 1) * HEAD_PAD, qs * QSUB_ATT:(qs + 1) * QSUB_ATT]
            return jnp.dot(k, qt, preferred_element_type=F32)

        def update(sidx, st):
            off, sid, hh, qs = split(sidx)
            m, acc = state[sid]
            if masked:
                keep = pk_ref[pl.ds(off, KSUB_ATT), :] <= pos_q[:, qs * QSUB_ATT:(qs + 1) * QSUB_ATT]
                st = jnp.where(keep, st, NEG_INF)
            m_new = jnp.maximum(m, jnp.max(st, axis=0, keepdims=True))
            p = jnp.exp2((st - m_new).astype(BF16))
            alpha = jnp.exp2(m - m_new)
            vt = jnp.concatenate([vt_ref[hh * V_HEAD:(hh + 1) * V_HEAD, pl.ds(off, KSUB_ATT)],
                                  jnp.ones((ONES_ROWS, KSUB_ATT), BF16)], axis=0)
            state[sid] = (m_new, alpha * acc + jnp.dot(vt, p, preferred_element_type=F32))

        st_next = scores(0)
        for sidx in range(nstream):
            st_cur = st_next
            if sidx + 1 < nstream:
                st_next = scores(sidx + 1)
            update(sidx, st_cur)
        return tuple(state)

    nqs = tq // QSUB_ATT
    one = (jnp.full((1, QSUB_ATT), NEG_INF, F32), jnp.zeros((V_HEAD + ONES_ROWS, QSUB_ATT), F32))
    carry = lax.fori_loop(0, nfull, functools.partial(step, masked=False), (one,) * (2 * nqs))
    fin = lax.fori_loop(nfull, nhi, functools.partial(step, masked=True), carry)
    heads = [jnp.concatenate([fin[hh * nqs + qs][1][:V_HEAD] / fin[hh * nqs + qs][1][V_HEAD:V_HEAD + 1]
                              for qs in range(nqs)], axis=1) for hh in range(2)]
    o_ref[...] = jnp.concatenate(heads, axis=0).T.astype(BF16)


def _attn_call(nfull, nhi, qt, k, vt, pos_col, pos_row, B, S):
    tq, tk = TQ_ATT, TK_ATT
    nq = S // tq
    T = B * S
    kern = functools.partial(_attn_kernel, tq=tq, tk=tk, nq=nq)
    grid_spec = pltpu.PrefetchScalarGridSpec(
        num_scalar_prefetch=2,
        grid=(B, MLA_HEADS // 2, nq),
        in_specs=[pl.BlockSpec((2 * HEAD_PAD, tq), lambda b, h, i, *_: (h, b * nq + i)),
                  pl.BlockSpec((S, 2 * HEAD_PAD), lambda b, h, i, *_: (b, h)),
                  pl.BlockSpec((2 * V_HEAD, S), lambda b, h, i, *_: (h, b)),
                  pl.BlockSpec((None, 1, tq), lambda b, h, i, *_: (b, 0, i)),
                  pl.BlockSpec((S, 1), lambda b, h, i, *_: (b, 0))],
        out_specs=pl.BlockSpec((tq, 2 * V_HEAD), lambda b, h, i, *_: (b * nq + i, h)),
    )
    return pl.pallas_call(
        kern,
        out_shape=jax.ShapeDtypeStruct((T, MLA_HEADS * V_HEAD), BF16),
        grid_spec=grid_spec,
        compiler_params=_cparams(("arbitrary", "arbitrary", "arbitrary")),
        name="mla_attention",
    )(nfull, nhi, qt, k, vt, pos_row, pos_col)


def _oproj_kernel(x_ref, o_ref, mod_ref, w_ref, out_ref):
    g1 = mod_ref[...][2:3]
    y = jnp.dot(o_ref[...], w_ref[...], preferred_element_type=F32)
    out_ref[...] = x_ref[...] + g1 * y


def _oproj_call(x2, o, mod, w_o, S):
    T, D = x2.shape
    tm = TM_TOK
    nb = S // tm
    return pl.pallas_call(
        _oproj_kernel,
        out_shape=jax.ShapeDtypeStruct((T, D), F32),
        grid=(T // tm,),
        in_specs=[pl.BlockSpec((tm, D), lambda i: (i, 0)),
                  pl.BlockSpec((tm, o.shape[1]), lambda i: (i, 0)),
                  pl.BlockSpec((None, 6, D), lambda i: (i // nb, 0, 0)),
                  _const_spec(w_o.shape)],
        out_specs=pl.BlockSpec((tm, D), lambda i: (i, 0)),
        compiler_params=_cparams(("arbitrary",)),
        name="mla_out_proj",
    )(x2, o, mod, w_o)


TOP_ROWS = 3 * SUBLANES
LEVEL_STEP = 256.0


def _top17_rows(x, concat=False):
    if concat:
        ms = []
        for j in range(PEER_TOPK + 1):
            m = jnp.max(x, axis=0, keepdims=True)
            ms.append(m)
            if j < PEER_TOPK:
                x = jnp.where(x == m, NEG_INF, x)
        pad = jnp.full((TOP_ROWS - PEER_TOPK - 1, x.shape[1]), NEG_INF, F32)
        return jnp.concatenate(ms + [pad], axis=0)
    rows = lax.broadcasted_iota(jnp.int32, (TOP_ROWS, x.shape[1]), 0)
    out = jnp.full((TOP_ROWS, x.shape[1]), NEG_INF, F32)
    for j in range(PEER_TOPK + 1):
        m = jnp.max(x, axis=0, keepdims=True)
        out = jnp.where(rows == j, m, out)
        x = jnp.where(x == m, NEG_INF, x)
    return out


def _count_ge(x, y_rows):
    below = jnp.zeros(x.shape, jnp.int32)
    for j in range(PEER_TOPK + 1):
        diff = lax.bitcast_convert_type(x - y_rows[j:j + 1], jnp.int32)
        below = below + lax.shift_right_logical(diff, 31)
    return float(PEER_TOPK + 1) - below.astype(F32)


def _kth_and_next_largest_rows(x, k):
    shape = (1, x.shape[1])
    tk, tn = jnp.full(shape, NEG_INF, F32), jnp.full(shape, NEG_INF, F32)
    cnt = jnp.zeros(shape, F32)
    for _ in range(k + 1):
        m = jnp.max(x, axis=0, keepdims=True)
        eq = x == m
        tk = jnp.where(cnt < k, m, tk)
        tn = jnp.where(cnt < k + 1, m, tn)
        cnt = cnt + jnp.sum(jnp.where(eq, 1.0, 0.0), axis=0, keepdims=True)
        x = jnp.where(eq, NEG_INF, x)
    return tk, tn


def _peer_kernel(x_ref, mod_ref, ng_ref, wqt_ref, keys_ref, u_ref, vt_ref, fg_ref, o_ref,
                 h_s, st_s, thr_s, e1_s, lv_s, e2_s, wa_s, acc_s, *, tm, ec, final_norm):
    D = D_MODEL
    ntc = tm // LANES
    e = pl.program_id(1)
    ne = pl.num_programs(1)

    @pl.when(e == 0)
    def _scores():
        x = x_ref[...]
        mod = mod_ref[...]
        sh2, sc2 = mod[3:4], mod[4:5]
        ht = (_rms(x, ng_ref[...]) * (1.0 + sc2) + sh2).T.astype(BF16)
        h_s[...] = ht
        qt = jnp.dot(wqt_ref[...], ht, preferred_element_type=F32)
        qtb = qt.astype(BF16)
        for hp in range(2 * PEER_HEADS):
            st = jnp.dot(keys_ref[hp], qtb[hp * PEER_HALF:(hp + 1) * PEER_HALF, :],
                         preferred_element_type=F32)
            for tc in range(ntc):
                st_s[tc, hp] = st[:, tc * LANES:(tc + 1) * LANES]

        def per_block(i, _):
            tc = i // PEER_HEADS
            hd = i % PEER_HEADS
            s1 = st_s[tc, 2 * hd]
            s2 = st_s[tc, 2 * hd + 1]
            a = _top17_rows(s1)
            b = _top17_rows(s2)
            cands = [a[0:1] + b]
            for r in range(1, SUBLANES):
                cands.append(a[r:r + 1] + b[0:SUBLANES])
            cands.append(a[SUBLANES:TOP_ROWS] + b[0:1])
            cand = jnp.concatenate(cands, axis=0)
            c16, c17 = _kth_and_next_largest_rows(cand, PEER_TOPK)
            tau = 0.5 * (c16 + c17)
            m1, m2 = a[0:1], b[0:1]
            zsum = jnp.sum(jnp.where(cand >= tau, jnp.exp(cand - (m1 + m2)), 0.0), axis=0, keepdims=True)
            level2 = _count_ge(s2, b)
            cnt = _count_ge(s1 - tau, -b)
            thr_s[tc, hd] = (float(PEER_TOPK + 1) - cnt) * LEVEL_STEP
            e1_s[tc, hd] = jnp.exp(s1 - m1) / zsum
            lv_s[tc, hd] = (level2 * LEVEL_STEP).astype(BF16)
            e2_s[tc, hd] = jnp.exp(s2 - m2).astype(BF16)
            return 0

        lax.fori_loop(0, ntc * PEER_HEADS, per_block, 0)
        acc_s[...] = jnp.zeros(acc_s.shape, F32)

    ht = h_s[...]
    ngrp = ec // LANES
    nsub = ec // ESUB_PEER
    rows_per_sub = ESUB_PEER // LANES
    key_rows = {}

    def key_row(ref, g8, tc, hd, j):
        if (id(ref), g8, tc, hd) not in key_rows:
            base = pl.multiple_of(e * ngrp + g8 * SUBLANES, SUBLANES)
            key_rows[(id(ref), g8, tc, hd)] = ref[tc, hd, pl.ds(base, SUBLANES), :].astype(BF16)
        return key_rows[(id(ref), g8, tc, hd)][j:j + 1, :]

    def activations(k):
        return jnp.dot(u_ref[k * ESUB_PEER:(k + 1) * ESUB_PEER, :], ht, preferred_element_type=F32)

    def softmax_weights(k):
        for tc in range(ntc):
            w = [None] * rows_per_sub
            for hd in range(PEER_HEADS):
                lv, e2 = lv_s[tc, hd], e2_s[tc, hd]
                for r in range(rows_per_sub):
                    g8, j = divmod(k * rows_per_sub + r, SUBLANES)
                    sel = jnp.minimum(e2, jnp.maximum(lv - key_row(thr_s, g8, tc, hd, j), 0.0))
                    term = sel * key_row(e1_s, g8, tc, hd, j)
                    w[r] = term if w[r] is None else w[r] + term
            for r in range(rows_per_sub):
                r0 = (k * rows_per_sub + r) * LANES
                wa_s[r0:r0 + LANES, tc * LANES:(tc + 1) * LANES] = w[r]

    def apply_activations(k, at):
        rows = slice(k * ESUB_PEER, (k + 1) * ESUB_PEER)
        wa_s[rows, :] = wa_s[rows, :] * _gelu(at).astype(BF16)

    def values(k0, k1):
        cols = slice(k0 * ESUB_PEER, k1 * ESUB_PEER)
        acc_s[...] += jnp.dot(vt_ref[:, cols], wa_s[cols, :], preferred_element_type=F32)

    half = nsub // 2
    at_next = activations(0)
    softmax_weights(0)
    for k in range(nsub):
        at_cur = at_next
        if k + 1 < nsub:
            at_next = activations(k + 1)
        if k == half:
            values(0, half)
        apply_activations(k, at_cur)
        if k + 1 < nsub:
            softmax_weights(k + 1)
    values(half, nsub)

    @pl.when(e == ne - 1)
    def _finish():
        g2 = mod_ref[...][5:6]
        xn = x_ref[...] + g2 * acc_s[...].T
        if final_norm:
            xn = _rms(xn, fg_ref[...])
        o_ref[...] = xn


def _peer_call(x2, mod, ng, wqt, keys, u, vt, fg, S, final_norm):
    T, D = x2.shape
    E = u.shape[0]
    tm, ec = TM_PEER, EC_PEER
    nb = S // tm
    ntc = tm // LANES
    kern = functools.partial(_peer_kernel, tm=tm, ec=ec, final_norm=final_norm)
    return pl.pallas_call(
        kern,
        out_shape=jax.ShapeDtypeStruct((T, D), F32),
        grid=(T // tm, E // ec),
        in_specs=[pl.BlockSpec((tm, D), lambda i, e: (i, 0)),
                  pl.BlockSpec((None, 6, D), lambda i, e: (i // nb, 0, 0)),
                  _const_spec((1, D)), _const_spec(wqt.shape), _const_spec(keys.shape),
                  pl.BlockSpec((ec, D), lambda i, e: (e, 0)),
                  pl.BlockSpec((D, ec), lambda i, e: (0, e)),
                  _const_spec((1, D))],
        out_specs=pl.BlockSpec((tm, D), lambda i, e: (i, 0)),
        scratch_shapes=[pltpu.VMEM((D, tm), BF16),
                        pltpu.VMEM((ntc, 2 * PEER_HEADS, N_KEYS, LANES), F32),
                        pltpu.VMEM((ntc, PEER_HEADS, N_KEYS, LANES), F32),
                        pltpu.VMEM((ntc, PEER_HEADS, N_KEYS, LANES), F32),
                        pltpu.VMEM((ntc, PEER_HEADS, N_KEYS, LANES), BF16),
                        pltpu.VMEM((ntc, PEER_HEADS, N_KEYS, LANES), BF16),
                        pltpu.VMEM((ec, tm), BF16),
                        pltpu.VMEM((D, tm), F32)],
        compiler_params=_cparams(("arbitrary", "arbitrary")),
        name="peer_layer",
    )(x2, mod, ng, wqt, keys, u, vt, fg)


PAD_ROWS = 16


def _peer_kernel_x(x_ref, mod_ref, ng_ref, wqt_ref, keys_ref, u_ref, vt_ref, fg_ref, o_ref,
                   h_s, st_s, thr_s, e1_s, e2f_s, le_s, wa_s, acc_s, *, tm, ec, final_norm, var):
    D = D_MODEL
    ntc = tm // LANES
    e = pl.program_id(1)
    ne = pl.num_programs(1)
    levels, wbuild, nsub, use_ht, pad = var["levels"], var["wbuild"], var["nsub"], var["ht"], var["pad"]
    e2_off = N_KEYS + (PAD_ROWS if pad else 0)

    @pl.when(e == 0)
    def _scores():
        x = x_ref[...]
        mod = mod_ref[...]
        sh2, sc2 = mod[3:4], mod[4:5]
        hn = _rms(x, ng_ref[...]) * (1.0 + sc2) + sh2
        if use_ht:
            ht = hn.T.astype(BF16)
            h_s[...] = ht
            qt = jnp.dot(wqt_ref[...], ht, preferred_element_type=F32)
        else:
            hb = hn.astype(BF16)
            h_s[...] = hb
            qt = lax.dot_general(wqt_ref[...], hb, (((1,), (1,)), ((), ())), preferred_element_type=F32)
        qtb = qt.astype(BF16)
        for hp in range(2 * PEER_HEADS):
            st = jnp.dot(keys_ref[hp], qtb[hp * PEER_HALF:(hp + 1) * PEER_HALF, :],
                         preferred_element_type=F32)
            for tc in range(ntc):
                st_s[tc, hp] = st[:, tc * LANES:(tc + 1) * LANES]

        nbatch = var.get("batch", 1)

        def per_group(g, _):
            for bi in range(nbatch):
                per_block(g * nbatch + bi)
            return 0

        def per_block(i):
            tc = i // PEER_HEADS
            hd = i % PEER_HEADS
            s1 = st_s[tc, 2 * hd]
            s2 = st_s[tc, 2 * hd + 1]
            a = _top17_rows(s1, var.get("concat_rows", False))
            b = _top17_rows(s2, var.get("concat_rows", False))
            cands = [a[0:1] + b]
            for r in range(1, SUBLANES):
                cands.append(a[r:r + 1] + b[0:SUBLANES])
            cands.append(a[SUBLANES:TOP_ROWS] + b[0:1])
            cand = jnp.concatenate(cands, axis=0)
            if var.get("kth_simple", False):
                xk = cand
                for it in range(PEER_TOPK + 1):
                    mk = jnp.max(xk, axis=0, keepdims=True)
                    if it == PEER_TOPK - 1:
                        c16 = mk
                    if it < PEER_TOPK:
                        xk = jnp.where(xk == mk, NEG_INF, xk)
                c17 = mk
            else:
                c16, c17 = _kth_and_next_largest_rows(cand, PEER_TOPK)
            tau = 0.5 * (c16 + c17)
            m1, m2 = a[0:1], b[0:1]
            zsum = jnp.sum(jnp.where(cand >= tau, jnp.exp(cand - (m1 + m2)), 0.0), axis=0, keepdims=True)
            e1_s[tc, hd] = jnp.exp(s1 - m1) * ((GELU_OUT_SCALE if var.get("gfold", False) else 1.0) / zsum)
            if levels:
                level2 = _count_ge(s2, b)
                cnt = _count_ge(s1 - tau, -b)
                thr_s[tc, hd] = (float(PEER_TOPK + 1) - cnt) * LEVEL_STEP
                le_s[tc, hd, 0:N_KEYS, :] = (level2 * LEVEL_STEP).astype(BF16)
                le_s[tc, hd, e2_off:e2_off + N_KEYS, :] = jnp.exp(s2 - m2).astype(BF16)
            else:
                thr_s[tc, hd] = jnp.exp(tau - s1 - m2)
                e2f_s[tc, hd] = jnp.exp(s2 - m2)

        lax.fori_loop(0, ntc * PEER_HEADS // nbatch, per_group, 0)
        acc_s[...] = jnp.zeros(acc_s.shape, F32)

    ngrp = ec // LANES
    esub = ec // nsub
    rows_per_sub = esub // LANES
    key_rows = {}
    wdt = F32 if wbuild.startswith("f32") else BF16

    def key_row(ref, g8, tc, hd, j):
        if (id(ref), g8, tc, hd) not in key_rows:
            base = pl.multiple_of(e * ngrp + g8 * SUBLANES, SUBLANES)
            key_rows[(id(ref), g8, tc, hd)] = ref[tc, hd, pl.ds(base, SUBLANES), :].astype(wdt)
        return key_rows[(id(ref), g8, tc, hd)][j:j + 1, :]

    def activations(k):
        if use_ht:
            return jnp.dot(u_ref[k * esub:(k + 1) * esub, :], h_s[...], preferred_element_type=F32)
        return lax.dot_general(u_ref[k * esub:(k + 1) * esub, :], h_s[...], (((1,), (1,)), ((), ())),
                               preferred_element_type=F32)

    def masked_weights(tc, hd, g8, j):
        thr = key_row(thr_s, g8, tc, hd, j)
        if wbuild in ("f32sel", "f32mix", "f32bits"):
            e2 = e2f_s[tc, hd]
            if wbuild == "f32sel" or (wbuild == "f32mix" and hd < PEER_HEADS // 2):
                return jnp.where(e2 >= thr, e2, 0.0)
            keep = lax.shift_right_arithmetic(lax.bitcast_convert_type(thr - e2, jnp.int32), 31)
            return lax.bitcast_convert_type(lax.bitcast_convert_type(e2, jnp.int32) & keep, F32)
        lv = le_s[tc, hd, 0:N_KEYS, :]
        e2 = le_s[tc, hd, e2_off:e2_off + N_KEYS, :]
        if wbuild == "bf16sel":
            return jnp.where(lv > thr, e2, 0.0)
        return jnp.minimum(e2, jnp.maximum(lv - thr, 0.0))

    gfold = var.get("gfold", False)

    def act_fn(at):
        return at * (1.0 + lax.erf(at)) if gfold else _gelu(at)

    ncol = var.get("colsub", 1)
    if ncol > 1:
        cw = tm // ncol
        tc_per = cw // LANES
        nrows = ec // LANES

        def act_cols(c):
            return jnp.dot(u_ref[...], h_s[:, c * cw:(c + 1) * cw], preferred_element_type=F32)

        def weigh_cols(c, at):
            gact = act_fn(at)
            for tl in range(tc_per):
                tc = c * tc_per + tl
                for r in range(nrows):
                    g8, j = divmod(r, SUBLANES)
                    w = None
                    for hd in range(PEER_HEADS):
                        term = masked_weights(tc, hd, g8, j) * key_row(e1_s, g8, tc, hd, j)
                        w = term if w is None else w + term
                    wa_s[r * LANES:(r + 1) * LANES, tc * LANES:(tc + 1) * LANES] = (
                        w * gact[r * LANES:(r + 1) * LANES, tl * LANES:(tl + 1) * LANES]).astype(BF16)

        def values_cols(c):
            cols = slice(c * cw, (c + 1) * cw)
            acc_s[:, cols] += jnp.dot(vt_ref[...], wa_s[:, cols], preferred_element_type=F32)

        at_next = act_cols(0)
        for c in range(ncol):
            at_cur = at_next
            if c + 1 < ncol:
                at_next = act_cols(c + 1)
            weigh_cols(c, at_cur)
            values_cols(c)
    elif nsub == 1:
        gact = act_fn(activations(0))
        if not wbuild.startswith("f32"):
            gact = gact.astype(BF16)
        for tc in range(ntc):
            for r in range(rows_per_sub):
                g8, j = divmod(r, SUBLANES)
                w = None
                for hd in range(PEER_HEADS):
                    term = masked_weights(tc, hd, g8, j) * key_row(e1_s, g8, tc, hd, j)
                    w = term if w is None else w + term
                rr = r * LANES
                wa_s[rr:rr + LANES, tc * LANES:(tc + 1) * LANES] = (
                    w * gact[rr:rr + LANES, tc * LANES:(tc + 1) * LANES]).astype(BF16)
        acc_s[...] += jnp.dot(vt_ref[...], wa_s[...], preferred_element_type=F32)
    else:
        def weigh(k, at):
            gact = _gelu(at)
            if not wbuild.startswith("f32"):
                gact = gact.astype(BF16)
            for tc in range(ntc):
                for r in range(rows_per_sub):
                    g8, j = divmod(k * rows_per_sub + r, SUBLANES)
                    w = None
                    for hd in range(PEER_HEADS):
                        term = masked_weights(tc, hd, g8, j) * key_row(e1_s, g8, tc, hd, j)
                        w = term if w is None else w + term
                    r0 = (k * rows_per_sub + r) * LANES
                    wa_s[r0:r0 + LANES, tc * LANES:(tc + 1) * LANES] = (
                        w * gact[r * LANES:(r + 1) * LANES, tc * LANES:(tc + 1) * LANES]).astype(BF16)

        def values(k0, k1):
            cols = slice(k0 * esub, k1 * esub)
            acc_s[...] += jnp.dot(vt_ref[:, cols], wa_s[cols, :], preferred_element_type=F32)

        half = nsub // 2
        at_next = activations(0)
        for k in range(nsub):
            at_cur = at_next
            if k + 1 < nsub:
                at_next = activations(k + 1)
            if k == half:
                values(0, half)
            weigh(k, at_cur)
        values(half, nsub)

    @pl.when(e == ne - 1)
    def _finish():
        g2 = mod_ref[...][5:6]
        xn = x_ref[...] + g2 * acc_s[...].T
        if final_norm:
            xn = _rms(xn, fg_ref[...])
        o_ref[...] = xn


def _peer_call_x(x2, mod, ng, wqt, keys, u, vt, fg, S, final_norm, var):
    T, D = x2.shape
    E = u.shape[0]
    tm, ec = TM_PEER, var.get("ec", EC_PEER)
    nb = S // tm
    ntc = tm // LANES
    kern = functools.partial(_peer_kernel_x, tm=tm, ec=ec, final_norm=final_norm, var=var)
    le_rows = 2 * N_KEYS + 2 * PAD_ROWS
    return pl.pallas_call(
        kern,
        out_shape=jax.ShapeDtypeStruct((T, D), F32),
        grid=(T // tm, E // ec),
        in_specs=[pl.BlockSpec((tm, D), lambda i, e: (i, 0)),
                  pl.BlockSpec((None, 6, D), lambda i, e: (i // nb, 0, 0)),
                  _const_spec((1, D)), _const_spec(wqt.shape), _const_spec(keys.shape),
                  pl.BlockSpec((ec, D), lambda i, e: (e, 0)),
                  pl.BlockSpec((D, ec), lambda i, e: (0, e)),
                  _const_spec((1, D))],
        out_specs=pl.BlockSpec((tm, D), lambda i, e: (i, 0)),
        scratch_shapes=[pltpu.VMEM((D, tm) if var["ht"] else (tm, D), BF16),
                        pltpu.VMEM((ntc, 2 * PEER_HEADS, N_KEYS, LANES), F32),
                        pltpu.VMEM((ntc, PEER_HEADS, N_KEYS, LANES), F32),
                        pltpu.VMEM((ntc, PEER_HEADS, N_KEYS, LANES), F32),
                        pltpu.VMEM((ntc, PEER_HEADS, N_KEYS, LANES), F32),
                        pltpu.VMEM((ntc, PEER_HEADS, le_rows, LANES), BF16),
                        pltpu.VMEM((ec, tm), BF16),
                        pltpu.VMEM((D, tm), F32)],
        compiler_params=_cparams(("arbitrary", "arbitrary")),
        name="peer_layer",
    )(x2, mod, ng, wqt, keys, u, vt, fg)


PEER_VARIANTS = (
    dict(levels=False, wbuild="f32sel", nsub=1, ht=True, pad=False, batch=4, gfold=True, ec=2048),
    dict(levels=False, wbuild="f32sel", nsub=1, ht=True, pad=False, batch=4, gfold=True, ec=2048, kth_simple=True),
    dict(levels=False, wbuild="f32sel", nsub=1, ht=True, pad=False, batch=4, gfold=True, ec=2048, kth_simple=True,
         concat_rows=True),
    dict(levels=False, wbuild="f32sel", nsub=1, ht=True, pad=False, batch=8, gfold=True, ec=2048, kth_simple=True,
         concat_rows=True),
)


def _head_pad_cols(w_nope, w_rope, n_in):
    zeros = jnp.zeros((n_in, MLA_HEADS, HEAD_PAD - QK_NOPE - QK_ROPE), w_nope.dtype)
    return jnp.concatenate([w_nope, w_rope, zeros], axis=-1).reshape(n_in, MLA_HEADS * HEAD_PAD)


def _swap_halves(w):
    half = QK_ROPE // 2
    return jnp.concatenate([w[..., half:], w[..., :half]], axis=-1)


def kernel(x, c, positions, ada_w, ada_b, norm_mix_g, norm_ffn_g, lru_w_in, lru_conv_w, lru_conv_b, lru_wa, lru_ba, lru_wx, lru_bx, lru_lambda, lru_w_out, kv_ada_w, kv_ada_b, kv_norm_g, mla_w_dkv, mla_w_kr, mla_kv_latent_g, mla_w_uk, mla_w_uv, mla_w_dq, mla_q_latent_g, mla_w_uq, mla_w_o, peer_w_q, peer_sub_keys, peer_u, peer_v, final_g):
    B, S, D = x.shape
    T = B * S
    assert D == D_MODEL and S % TM_TOK == 0 and S % TS_LRU == 0 and S % TQ_ATT == 0 and S % TM_PEER == 0

    c_pad = jnp.zeros((SUBLANES, D), F32).at[:B].set(c)
    mod_all = _mod_call(c_pad, ada_w, ada_b[:, None, :], 1536)
    mod_all = mod_all[:, :B].reshape(DEPTH, B, 6, D)
    mod_kv = _mod_call(c_pad, kv_ada_w[None], kv_ada_b[None, None, :], 1024)[0, :B].reshape(B, 2, D)

    half = QK_ROPE // 2
    freqs = ROPE_THETA ** (-jnp.arange(half, dtype=F32) / half)
    freq_lane = jnp.zeros((1, LANES), F32).at[0, QK_NOPE:QK_NOPE + QK_ROPE].set(jnp.concatenate([freqs, freqs]))
    pos_col = positions.reshape(T, 1)
    ca, sb = _rope_call(pos_col, freq_lane)

    nq, nk = S // TQ_ATT, S // TK_ATT
    pq = positions.reshape(B, nq, TQ_ATT)
    pk = positions.reshape(B, nk, TK_ATT)
    qmin, qmax = pq.min(-1), pq.max(-1)
    kmin, kmax = pk.min(-1), pk.max(-1)
    needed = kmin[:, None, :] <= qmax[:, :, None]
    nhi = jnp.max(jnp.where(needed, jnp.arange(1, nk + 1, dtype=jnp.int32), 0), axis=-1)
    full = kmax[:, None, :] <= qmin[:, :, None]
    nfull = jnp.min(jnp.where(full, nk, jnp.arange(nk, dtype=jnp.int32)), axis=-1)
    nfull = jnp.minimum(nfull, nhi).astype(jnp.int32).reshape(-1)
    nhi = nhi.astype(jnp.int32).reshape(-1)
    pos_row = positions.reshape(B, 1, S)

    def peer_layer(x2, l, final_norm):
        wqt = peer_w_q[l].T.astype(BF16)
        keys = peer_sub_keys[l].reshape(2 * PEER_HEADS, N_KEYS, PEER_HALF).astype(BF16)
        u = peer_u[l]
        if PEER_VARIANTS[l].get("gfold", False):
            u = u * (1.0 / math.sqrt(2.0))
        u = u.astype(BF16)
        vt = peer_v[l].T.astype(BF16)
        return _peer_call_x(x2, mod_all[l], norm_ffn_g[l][None], wqt, keys, u, vt, final_g[None], S, final_norm,
                            PEER_VARIANTS[l])

    for l in range(N_A_LAYERS):
        x = _lru_call(x, mod_all[l], norm_mix_g[l][None], lru_w_in[l].astype(BF16), lru_conv_w[l],
                      lru_conv_b[l][None], lru_wa[l].astype(BF16), lru_ba[l][None], lru_wx[l].astype(BF16),
                      lru_bx[l][None], lru_lambda[l][None], lru_w_out[l].astype(BF16))
        x = peer_layer(x.reshape(T, D), l, False).reshape(B, S, D)

    x2 = x.reshape(T, D)

    zc = jnp.zeros((D, QK_NOPE), F32)
    zt = jnp.zeros((D, HEAD_PAD - QK_NOPE - QK_ROPE), F32)
    w1 = jnp.concatenate([mla_w_dkv, zc, mla_w_kr, zt, zc, _swap_halves(mla_w_kr), zt], axis=-1).astype(BF16)
    wuk = jnp.concatenate([mla_w_uk.reshape(KV_LORA, MLA_HEADS, QK_NOPE),
                           jnp.zeros((KV_LORA, MLA_HEADS, HEAD_PAD - QK_NOPE), F32)], axis=-1)
    wuk = wuk.reshape(KV_LORA, MLA_HEADS * HEAD_PAD).astype(BF16)
    k_all, v_all = _kv_call(x2, mod_kv, kv_norm_g[None], w1, mla_kv_latent_g[None], wuk,
                            mla_w_uv.astype(BF16), ca, sb, S)

    for j in range(DEPTH - N_A_LAYERS):
        l = N_A_LAYERS + j
        wq = mla_w_uq[j].reshape(Q_LORA, MLA_HEADS, QK_NOPE + QK_ROPE)
        w_nope, w_rope = wq[..., :QK_NOPE], wq[..., QK_NOPE:]
        wa = _head_pad_cols(w_nope, w_rope, Q_LORA).astype(BF16)
        wb = _head_pad_cols(jnp.zeros_like(w_nope), _swap_halves(w_rope), Q_LORA).astype(BF16)
        q = _q_call(x2, mod_all[l], norm_mix_g[l][None], mla_w_dq[j].astype(BF16), mla_q_latent_g[j][None],
                    wa, wb, ca, sb, S)
        o = _attn_call(nfull, nhi, q, k_all, v_all, pos_col, pos_row, B, S)
        x2 = _oproj_call(x2, o, mod_all[l], mla_w_o[j].astype(BF16), S)
        x2 = peer_layer(x2, l, l == DEPTH - 1)

    return x2.reshape(B, S, D)
```

```python
import functools
import math

import jax
import jax.numpy as jnp
from jax import lax
from jax.experimental import pallas as pl
from jax.experimental.pallas import tpu as pltpu

F32 = jnp.float32
BF16 = jnp.bfloat16

D_MODEL = 1024
DEPTH = 4
N_A_LAYERS = DEPTH // 2
LRU_HEADS = 4
LRU_BLOCK = D_MODEL // LRU_HEADS
CONV_WIDTH = 4
LRU_C = 8.0
MLA_HEADS = 16
QK_NOPE = 64
QK_ROPE = 32
V_HEAD = 64
Q_LORA = 384
KV_LORA = 256
ROPE_THETA = 10000.0
ATTN_SCALE = 1.0 / math.sqrt(QK_NOPE + QK_ROPE)
PEER_HEADS = 8
N_KEYS = 128
PEER_TOPK = 16
PEER_HALF = 128
RMS_EPS = 1e-6
NEG_INF = -1e30
LOG2E = math.log2(math.e)
GELU_OUT_SCALE = 0.5 * math.sqrt(2.0)

LANES = 128
SUBLANES = 8
HEAD_PAD = LANES
VMEM_LIMIT = 56 * 1024 * 1024

TS_LRU = 256
TM_TOK = 512
TQ_ATT = 512
TK_ATT = 1024
KSUB_ATT = 512
QSUB_ATT = 256
ONES_ROWS = 16
TM_PEER = 512
EC_PEER = 2048


def _cparams(sem):
    return pltpu.CompilerParams(dimension_semantics=sem, vmem_limit_bytes=VMEM_LIMIT)


def _rms(x, g):
    return x * lax.rsqrt(jnp.mean(x * x, axis=-1, keepdims=True) + RMS_EPS) * g


def _gelu(x):
    return 0.5 * x * (1.0 + lax.erf(x * (1.0 / math.sqrt(2.0))))


def _expm1(y):
    series = y * (1.0 + 0.5 * y * (1.0 + (1.0 / 3.0) * y * (1.0 + 0.25 * y * (1.0 + 0.2 * y))))
    return jnp.where(jnp.abs(y) < 0.05, series, jnp.exp(y) - 1.0)


def _const_spec(shape):
    nd = len(shape)
    return pl.BlockSpec(shape, lambda *_: (0,) * nd)


def _mod_kernel(c_ref, w_ref, b_ref, o_ref):
    c = c_ref[...]
    sc = c * jax.nn.sigmoid(c)
    o_ref[...] = jnp.dot(sc.astype(BF16), w_ref[...].astype(BF16),
                         preferred_element_type=F32) + b_ref[...]


def _mod_call(c_pad, w, b, tn):
    L, D, N = w.shape
    return pl.pallas_call(
        _mod_kernel,
        out_shape=jax.ShapeDtypeStruct((L, SUBLANES, N), F32),
        grid=(L, N // tn),
        in_specs=[pl.BlockSpec((SUBLANES, D), lambda l, n: (0, 0)),
                  pl.BlockSpec((None, D, tn), lambda l, n: (l, 0, n)),
                  pl.BlockSpec((None, 1, tn), lambda l, n: (l, 0, n))],
        out_specs=pl.BlockSpec((None, SUBLANES, tn), lambda l, n: (l, 0, n)),
        compiler_params=_cparams(("arbitrary", "arbitrary")),
        name="adaln_mod",
    )(c_pad, w, b)


def _rope_kernel(pos_ref, freq_ref, ca_ref, sb_ref):
    ang = pos_ref[...].astype(F32) * freq_ref[...]
    lane = lax.broadcasted_iota(jnp.int32, ang.shape, 1)
    cos, sin = jnp.cos(ang), jnp.sin(ang)
    half = QK_ROPE // 2
    ca_ref[...] = jnp.where(lane < QK_NOPE, 1.0, jnp.where(lane < QK_NOPE + QK_ROPE, cos, 0.0))
    sb_ref[...] = jnp.where(lane < QK_NOPE, 0.0,
                            jnp.where(lane < QK_NOPE + half, -sin,
                                      jnp.where(lane < QK_NOPE + QK_ROPE, sin, 0.0)))


def _rope_call(pos_col, freq_lane):
    T = pos_col.shape[0]
    tm = TM_TOK
    spec = pl.BlockSpec((tm, LANES), lambda i: (i, 0))
    return pl.pallas_call(
        _rope_kernel,
        out_shape=(jax.ShapeDtypeStruct((T, LANES), F32),) * 2,
        grid=(T // tm,),
        in_specs=[pl.BlockSpec((tm, 1), lambda i: (i, 0)), _const_spec((1, LANES))],
        out_specs=(spec, spec),
        compiler_params=_cparams(("arbitrary",)),
        name="rope_tables",
    )(pos_col, freq_lane)


def _shift_rows(x, d, fill):
    n = x.shape[0]
    if d % SUBLANES == 0:
        return jnp.concatenate([jnp.full((d, x.shape[1]), fill, x.dtype), x[:n - d]], axis=0)
    rolled = pltpu.roll(x, d, 0)
    row = lax.broadcasted_iota(jnp.int32, x.shape, 0)
    return jnp.where(row < d, fill, rolled)


def _lru_kernel(x_ref, mod_ref, ng_ref, win_ref, cw_ref, cb_ref, wa_ref, ba_ref, wx_ref, bx_ref,
                lam_ref, wout_ref, o_ref, rec_s, h_s, *, ts):
    D = D_MODEL

    @pl.when(pl.program_id(1) == 0)
    def _():
        rec_s[0:SUBLANES, :] = jnp.zeros((SUBLANES, D), F32)
        h_s[...] = jnp.zeros((SUBLANES, D), F32)

    x = x_ref[...]
    mod = mod_ref[...]
    sh1, sc1, g1 = mod[0:1], mod[1:2], mod[2:3]
    h = _rms(x, ng_ref[...]) * (1.0 + sc1) + sh1
    u = jnp.dot(h.astype(BF16), win_ref[...], preferred_element_type=F32)
    gate = _gelu(u[:, :D])
    rec = u[:, D:]

    rec_s[SUBLANES:ts + SUBLANES, :] = rec
    cw = cw_ref[...]
    xc = (cb_ref[...] + rec * cw[3:4]
          + rec_s[SUBLANES - 1:ts + SUBLANES - 1, :] * cw[2:3]
          + rec_s[SUBLANES - 2:ts + SUBLANES - 2, :] * cw[1:2]
          + rec_s[SUBLANES - 3:ts + SUBLANES - 3, :] * cw[0:1])
    rec_s[0:SUBLANES, :] = rec[ts - SUBLANES:ts, :]

    xcb = xc.astype(BF16)
    r_pre = jnp.concatenate(
        [jnp.dot(xcb[:, k * LRU_BLOCK:(k + 1) * LRU_BLOCK], wa_ref[k], preferred_element_type=F32)
         for k in range(LRU_HEADS)], axis=-1)
    i_pre = jnp.concatenate(
        [jnp.dot(xcb[:, k * LRU_BLOCK:(k + 1) * LRU_BLOCK], wx_ref[k], preferred_element_type=F32)
         for k in range(LRU_HEADS)], axis=-1)
    r = jax.nn.sigmoid(r_pre + ba_ref[...])
    gi = jax.nn.sigmoid(i_pre + bx_ref[...])
    z = -lam_ref[...]
    softplus = jnp.maximum(z, 0.0) + jnp.log1p(jnp.exp(-jnp.abs(z)))
    log_a = (-LRU_C) * r * softplus
    a = jnp.exp(log_a)
    b = jnp.sqrt(-_expm1(2.0 * log_a)) * (gi * xc)

    d = 1
    while d < ts:
        a_sh = _shift_rows(a, d, 1.0)
        b_sh = _shift_rows(b, d, 0.0)
        b = a * b_sh + b
        a = a * a_sh
        d *= 2
    h0 = h_s[SUBLANES - 1:SUBLANES, :]
    hs = a * h0 + b
    h_s[...] = hs[ts - SUBLANES:ts, :]

    y = jnp.dot((gate * hs).astype(BF16), wout_ref[...], preferred_element_type=F32)
    o_ref[...] = x + g1 * y


def _lru_call(x, mod, ng, w_in, cw, cb, wa, ba, wx, bx, lam, w_out):
    B, S, D = x.shape
    ts = TS_LRU
    kern = functools.partial(_lru_kernel, ts=ts)
    return pl.pallas_call(
        kern,
        out_shape=jax.ShapeDtypeStruct((B, S, D), F32),
        grid=(B, S // ts),
        in_specs=[pl.BlockSpec((None, ts, D), lambda b, s: (b, s, 0)),
                  pl.BlockSpec((None, 6, D), lambda b, s: (b, 0, 0)),
                  _const_spec((1, D)), _const_spec((D, 2 * D)), _const_spec((CONV_WIDTH, D)),
                  _const_spec((1, D)), _const_spec((LRU_HEADS, LRU_BLOCK, LRU_BLOCK)), _const_spec((1, D)),
                  _const_spec((LRU_HEADS, LRU_BLOCK, LRU_BLOCK)), _const_spec((1, D)), _const_spec((1, D)),
                  _const_spec((D, D))],
        out_specs=pl.BlockSpec((None, ts, D), lambda b, s: (b, s, 0)),
        scratch_shapes=[pltpu.VMEM((ts + SUBLANES, D), F32), pltpu.VMEM((SUBLANES, D), F32)],
        compiler_params=_cparams(("arbitrary", "arbitrary")),
        name="rglru_layer",
    )(x, mod, ng, w_in, cw, cb, wa, ba, wx, bx, lam, w_out)


def _kv_kernel(x_ref, mod_ref, ng_ref, w1_ref, lg_ref, wuk_ref, wuv_ref, ca_ref, sb_ref, k_ref, v_ref):
    x = x_ref[...]
    mod = mod_ref[...]
    shift, scale = mod[0:1], mod[1:2]
    h = _rms(x, ng_ref[...]) * (1.0 + scale) + shift
    t = jnp.dot(h.astype(BF16), w1_ref[...], preferred_element_type=F32)
    c_kv = _rms(t[:, :KV_LORA], lg_ref[...])
    kr = t[:, KV_LORA:KV_LORA + LANES] * ca_ref[...] + t[:, KV_LORA + LANES:] * sb_ref[...]
    ckb = c_kv.astype(BF16)
    kn = jnp.dot(ckb, wuk_ref[...], preferred_element_type=F32)
    k_ref[...] = (kn + jnp.concatenate([kr] * MLA_HEADS, axis=-1)).astype(BF16)
    v_ref[...] = jnp.dot(ckb, wuv_ref[...], preferred_element_type=F32).T.astype(BF16)


def _kv_call(x2, mod, ng, w1, lg, wuk, wuv, ca, sb, S):
    T, D = x2.shape
    tm = TM_TOK
    nb = S // tm
    HP = MLA_HEADS * HEAD_PAD
    HV = MLA_HEADS * V_HEAD
    return pl.pallas_call(
        _kv_kernel,
        out_shape=(jax.ShapeDtypeStruct((T, HP), BF16), jax.ShapeDtypeStruct((HV, T), BF16)),
        grid=(T // tm,),
        in_specs=[pl.BlockSpec((tm, D), lambda i: (i, 0)),
                  pl.BlockSpec((None, 2, D), lambda i: (i // nb, 0, 0)),
                  _const_spec((1, D)), _const_spec(w1.shape), _const_spec((1, KV_LORA)),
                  _const_spec(wuk.shape), _const_spec(wuv.shape),
                  pl.BlockSpec((tm, LANES), lambda i: (i, 0)), pl.BlockSpec((tm, LANES), lambda i: (i, 0))],
        out_specs=(pl.BlockSpec((tm, HP), lambda i: (i, 0)), pl.BlockSpec((HV, tm), lambda i: (0, i))),
        compiler_params=_cparams(("arbitrary",)),
        name="shared_kv",
    )(x2, mod, ng, w1, lg, wuk, wuv, ca, sb)


def _q_kernel(x_ref, mod_ref, ng_ref, wdq_ref, lg_ref, wa_ref, wb_ref, ca_ref, sb_ref, q_ref):
    x = x_ref[...]
    mod = mod_ref[...]
    sh1, sc1 = mod[0:1], mod[1:2]
    h = _rms(x, ng_ref[...]) * (1.0 + sc1) + sh1
    ql = _rms(jnp.dot(h.astype(BF16), wdq_ref[...], preferred_element_type=F32), lg_ref[...])
    qlb = ql.astype(BF16)
    qa = jnp.dot(qlb, wa_ref[...], preferred_element_type=F32)
    qb = jnp.dot(qlb, wb_ref[...], preferred_element_type=F32)
    ca = jnp.concatenate([ca_ref[...]] * MLA_HEADS, axis=-1)
    sb = jnp.concatenate([sb_ref[...]] * MLA_HEADS, axis=-1)
    q_ref[...] = ((qa * ca + qb * sb) * (ATTN_SCALE * LOG2E)).T.astype(BF16)


def _q_call(x2, mod, ng, wdq, lg, wa, wb, ca, sb, S):
    T, D = x2.shape
    tm = TM_TOK
    nb = S // tm
    HP = MLA_HEADS * HEAD_PAD
    return pl.pallas_call(
        _q_kernel,
        out_shape=jax.ShapeDtypeStruct((HP, T), BF16),
        grid=(T // tm,),
        in_specs=[pl.BlockSpec((tm, D), lambda i: (i, 0)),
                  pl.BlockSpec((None, 6, D), lambda i: (i // nb, 0, 0)),
                  _const_spec((1, D)), _const_spec(wdq.shape), _const_spec((1, Q_LORA)),
                  _const_spec(wa.shape), _const_spec(wb.shape),
                  pl.BlockSpec((tm, LANES), lambda i: (i, 0)), pl.BlockSpec((tm, LANES), lambda i: (i, 0))],
        out_specs=pl.BlockSpec((HP, tm), lambda i: (0, i)),
        compiler_params=_cparams(("arbitrary",)),
        name="mla_queries",
    )(x2, mod, ng, wdq, lg, wa, wb, ca, sb)


def _attn_kernel(nfull_ref, nhi_ref, qt_ref, k_ref, vt_ref, pq_ref, pk_ref, o_ref, *, tq, tk, nq):
    b = pl.program_id(0)
    qi = pl.program_id(2)
    nfull = nfull_ref[b * nq + qi]
    nhi = nhi_ref[b * nq + qi]
    pos_q = pq_ref[...]

    def step(kc, carry, masked):
        nstate = 2 * (tq // QSUB_ATT)
        nstream = nstate * (tk // KSUB_ATT)
        state = list(carry)

        def split(sidx):
            kb, sid = divmod(sidx, nstate)
            hh, qs = divmod(sid, tq // QSUB_ATT)
            return pl.multiple_of(kc * tk + kb * KSUB_ATT, KSUB_ATT), sid, hh, qs

        def scores(sidx):
            off, _, hh, qs = split(sidx)
            k = k_ref[pl.ds(off, KSUB_ATT), hh * HEAD_PAD:(hh + 1) * HEAD_PAD]
            qt = qt_ref[hh * HEAD_PAD:(hh + 1) * HEAD_PAD, qs * QSUB_ATT:(qs + 1) * QSUB_ATT]
            return jnp.dot(k, qt, preferred_element_type=F32)

        def update(sidx, st):
            off, sid, hh, qs = split(sidx)
            m, acc = state[sid]
            if masked:
                keep = pk_ref[pl.ds(off, KSUB_ATT), :] <= pos_q[:, qs * QSUB_ATT:(qs + 1) * QSUB_ATT]
                st = jnp.where(keep, st, NEG_INF)
            m_new = jnp.maximum(m, jnp.max(st, axis=0, keepdims=True))
            p = jnp.exp2((st - m_new).astype(BF16))
            alpha = jnp.exp2(m - m_new)
            vt = jnp.concatenate([vt_ref[hh * V_HEAD:(hh + 1) * V_HEAD, pl.ds(off, KSUB_ATT)],
                                  jnp.ones((ONES_ROWS, KSUB_ATT), BF16)], axis=0)
            state[sid] = (m_new, alpha * acc + jnp.dot(vt, p, preferred_element_type=F32))

        st_next = scores(0)
        for sidx in range(nstream):
            st_cur = st_next
            if sidx + 1 < nstream:
                st_next = scores(sidx + 1)
            update(sidx, st_cur)
        return tuple(state)

    nqs = tq // QSUB_ATT
    one = (jnp.full((1, QSUB_ATT), NEG_INF, F32), jnp.zeros((V_HEAD + ONES_ROWS, QSUB_ATT), F32))
    carry = lax.fori_loop(0, nfull, functools.partial(step, masked=False), (one,) * (2 * nqs))
    fin = lax.fori_loop(nfull, nhi, functools.partial(step, masked=True), carry)
    heads = [jnp.concatenate([fin[hh * nqs + qs][1][:V_HEAD] / fin[hh * nqs + qs][1][V_HEAD:V_HEAD + 1]
                              for qs in range(nqs)], axis=1) for hh in range(2)]
    o_ref[...] = jnp.concatenate(heads, axis=0).T.astype(BF16)


def _attn_call(nfull, nhi, qt, k, vt, pos_col, pos_row, B, S):
    tq, tk = TQ_ATT, TK_ATT
    nq = S // tq
    T = B * S
    kern = functools.partial(_attn_kernel, tq=tq, tk=tk, nq=nq)
    grid_spec = pltpu.PrefetchScalarGridSpec(
        num_scalar_prefetch=2,
        grid=(B, MLA_HEADS // 2, nq),
        in_specs=[pl.BlockSpec((2 * HEAD_PAD, tq), lambda b, h, i, *_: (h, b * nq + i)),
                  pl.BlockSpec((S, 2 * HEAD_PAD), lambda b, h, i, *_: (b, h)),
                  pl.BlockSpec((2 * V_HEAD, S), lambda b, h, i, *_: (h, b)),
                  pl.BlockSpec((None, 1, tq), lambda b, h, i, *_: (b, 0, i)),
                  pl.BlockSpec((S, 1), lambda b, h, i, *_: (b, 0))],
        out_specs=pl.BlockSpec((tq, 2 * V_HEAD), lambda b, h, i, *_: (b * nq + i, h)),
    )
    return pl.pallas_call(
        kern,
        out_shape=jax.ShapeDtypeStruct((T, MLA_HEADS * V_HEAD), BF16),
        grid_spec=grid_spec,
        compiler_params=_cparams(("arbitrary", "arbitrary", "arbitrary")),
        name="mla_attention",
    )(nfull, nhi, qt, k, vt, pos_row, pos_col)


def _oproj_kernel(x_ref, o_ref, mod_ref, w_ref, out_ref):
    g1 = mod_ref[...][2:3]
    y = jnp.dot(o_ref[...], w_ref[...], preferred_element_type=F32)
    out_ref[...] = x_ref[...] + g1 * y


def _oproj_call(x2, o, mod, w_o, S):
    T, D = x2.shape
    tm = TM_TOK
    nb = S // tm
    return pl.pallas_call(
        _oproj_kernel,
        out_shape=jax.ShapeDtypeStruct((T, D), F32),
        grid=(T // tm,),
        in_specs=[pl.BlockSpec((tm, D), lambda i: (i, 0)),
                  pl.BlockSpec((tm, o.shape[1]), lambda i: (i, 0)),
                  pl.BlockSpec((None, 6, D), lambda i: (i // nb, 0, 0)),
                  _const_spec(w_o.shape)],
        out_specs=pl.BlockSpec((tm, D), lambda i: (i, 0)),
        compiler_params=_cparams(("arbitrary",)),
        name="mla_out_proj",
    )(x2, o, mod, w_o)


TOP_ROWS = 3 * SUBLANES
SCORE_BLOCKS_PER_ITER = 4


def _top17_rows(x):
    tops = []
    for j in range(PEER_TOPK + 1):
        m = jnp.max(x, axis=0, keepdims=True)
        tops.append(m)
        if j < PEER_TOPK:
            x = jnp.where(x == m, NEG_INF, x)
    pad = jnp.full((TOP_ROWS - PEER_TOPK - 1, x.shape[1]), NEG_INF, F32)
    return jnp.concatenate(tops + [pad], axis=0)


def _kth_and_next_largest_rows(x, k):
    for it in range(k + 1):
        m = jnp.max(x, axis=0, keepdims=True)
        if it == k - 1:
            kth = m
        if it < k:
            x = jnp.where(x == m, NEG_INF, x)
    return kth, m


def _peer_kernel(x_ref, mod_ref, ng_ref, wqt_ref, keys_ref, u_ref, vt_ref, fg_ref, o_ref,
                 h_s, st_s, thr_s, e1_s, e2_s, wa_s, acc_s, *, tm, ec, final_norm):
    ntc = tm // LANES
    e = pl.program_id(1)
    ne = pl.num_programs(1)

    @pl.when(e == 0)
    def _scores():
        x = x_ref[...]
        mod = mod_ref[...]
        sh2, sc2 = mod[3:4], mod[4:5]
        ht = (_rms(x, ng_ref[...]) * (1.0 + sc2) + sh2).T.astype(BF16)
        h_s[...] = ht
        qtb = jnp.dot(wqt_ref[...], ht, preferred_element_type=F32).astype(BF16)
        for hp in range(2 * PEER_HEADS):
            st = jnp.dot(keys_ref[hp], qtb[hp * PEER_HALF:(hp + 1) * PEER_HALF, :],
                         preferred_element_type=F32)
            for tc in range(ntc):
                st_s[tc, hp] = st[:, tc * LANES:(tc + 1) * LANES]

        def score_block(i):
            tc = i // PEER_HEADS
            hd = i % PEER_HEADS
            s1 = st_s[tc, 2 * hd]
            s2 = st_s[tc, 2 * hd + 1]
            a = _top17_rows(s1)
            b = _top17_rows(s2)
            cands = [a[0:1] + b]
            for r in range(1, SUBLANES):
                cands.append(a[r:r + 1] + b[0:SUBLANES])
            cands.append(a[SUBLANES:TOP_ROWS] + b[0:1])
            cand = jnp.concatenate(cands, axis=0)
            c16, c17 = _kth_and_next_largest_rows(cand, PEER_TOPK)
            tau = 0.5 * (c16 + c17)
            m1, m2 = a[0:1], b[0:1]
            zsum = jnp.sum(jnp.where(cand >= tau, jnp.exp(cand - (m1 + m2)), 0.0), axis=0, keepdims=True)
            thr_s[tc, hd] = jnp.exp(tau - s1 - m2)
            e1_s[tc, hd] = jnp.exp(s1 - m1) * (GELU_OUT_SCALE / zsum)
            e2_s[tc, hd] = jnp.exp(s2 - m2)

        def score_blocks(g, _):
            for bi in range(SCORE_BLOCKS_PER_ITER):
                score_block(g * SCORE_BLOCKS_PER_ITER + bi)
            return 0

        lax.fori_loop(0, ntc * PEER_HEADS // SCORE_BLOCKS_PER_ITER, score_blocks, 0)
        acc_s[...] = jnp.zeros(acc_s.shape, F32)

    at = jnp.dot(u_ref[...], h_s[...], preferred_element_type=F32)
    gact = at * (1.0 + lax.erf(at))
    ngrp = ec // LANES
    for g8 in range(ngrp // SUBLANES):
        base = pl.multiple_of(e * ngrp + g8 * SUBLANES, SUBLANES)
        for tc in range(ntc):
            thr8 = [thr_s[tc, hd, pl.ds(base, SUBLANES), :] for hd in range(PEER_HEADS)]
            e18 = [e1_s[tc, hd, pl.ds(base, SUBLANES), :] for hd in range(PEER_HEADS)]
            for j in range(SUBLANES):
                w = None
                for hd in range(PEER_HEADS):
                    e2 = e2_s[tc, hd]
                    term = jnp.where(e2 >= thr8[hd][j:j + 1, :], e2, 0.0) * e18[hd][j:j + 1, :]
                    w = term if w is None else w + term
                r0 = (g8 * SUBLANES + j) * LANES
                wa_s[r0:r0 + LANES, tc * LANES:(tc + 1) * LANES] = (
                    w * gact[r0:r0 + LANES, tc * LANES:(tc + 1) * LANES]).astype(BF16)
    acc_s[...] += jnp.dot(vt_ref[...], wa_s[...], preferred_element_type=F32)

    @pl.when(e == ne - 1)
    def _finish():
        g2 = mod_ref[...][5:6]
        xn = x_ref[...] + g2 * acc_s[...].T
        if final_norm:
            xn = _rms(xn, fg_ref[...])
        o_ref[...] = xn


def _peer_call(x2, mod, ng, wqt, keys, u, vt, fg, S, final_norm):
    T, D = x2.shape
    E = u.shape[0]
    tm, ec = TM_PEER, EC_PEER
    nb = S // tm
    ntc = tm // LANES
    kern = functools.partial(_peer_kernel, tm=tm, ec=ec, final_norm=final_norm)
    return pl.pallas_call(
        kern,
        out_shape=jax.ShapeDtypeStruct((T, D), F32),
        grid=(T // tm, E // ec),
        in_specs=[pl.BlockSpec((tm, D), lambda i, e: (i, 0)),
                  pl.BlockSpec((None, 6, D), lambda i, e: (i // nb, 0, 0)),
                  _const_spec((1, D)), _const_spec(wqt.shape), _const_spec(keys.shape),
                  pl.BlockSpec((ec, D), lambda i, e: (e, 0)),
                  pl.BlockSpec((D, ec), lambda i, e: (0, e)),
                  _const_spec((1, D))],
        out_specs=pl.BlockSpec((tm, D), lambda i, e: (i, 0)),
        scratch_shapes=[pltpu.VMEM((D, tm), BF16),
                        pltpu.VMEM((ntc, 2 * PEER_HEADS, N_KEYS, LANES), F32),
                        pltpu.VMEM((ntc, PEER_HEADS, N_KEYS, LANES), F32),
                        pltpu.VMEM((ntc, PEER_HEADS, N_KEYS, LANES), F32),
                        pltpu.VMEM((ntc, PEER_HEADS, N_KEYS, LANES), F32),
                        pltpu.VMEM((ec, tm), BF16),
                        pltpu.VMEM((D, tm), F32)],
        compiler_params=_cparams(("arbitrary", "arbitrary")),
        name="peer_layer",
    )(x2, mod, ng, wqt, keys, u, vt, fg)


def _head_pad_cols(w_nope, w_rope, n_in):
    zeros = jnp.zeros((n_in, MLA_HEADS, HEAD_PAD - QK_NOPE - QK_ROPE), w_nope.dtype)
    return jnp.concatenate([w_nope, w_rope, zeros], axis=-1).reshape(n_in, MLA_HEADS * HEAD_PAD)


def _swap_halves(w):
    half = QK_ROPE // 2
    return jnp.concatenate([w[..., half:], w[..., :half]], axis=-1)


def kernel(x, c, positions, ada_w, ada_b, norm_mix_g, norm_ffn_g, lru_w_in, lru_conv_w, lru_conv_b, lru_wa, lru_ba, lru_wx, lru_bx, lru_lambda, lru_w_out, kv_ada_w, kv_ada_b, kv_norm_g, mla_w_dkv, mla_w_kr, mla_kv_latent_g, mla_w_uk, mla_w_uv, mla_w_dq, mla_q_latent_g, mla_w_uq, mla_w_o, peer_w_q, peer_sub_keys, peer_u, peer_v, final_g):
    B, S, D = x.shape
    T = B * S
    assert D == D_MODEL and S % TM_TOK == 0 and S % TS_LRU == 0 and S % TM_PEER == 0
    assert S % TQ_ATT == 0 and S % TK_ATT == 0

    c_pad = jnp.zeros((SUBLANES, D), F32).at[:B].set(c)
    mod_all = _mod_call(c_pad, ada_w, ada_b[:, None, :], 1536)
    mod_all = mod_all[:, :B].reshape(DEPTH, B, 6, D)
    mod_kv = _mod_call(c_pad, kv_ada_w[None], kv_ada_b[None, None, :], 1024)[0, :B].reshape(B, 2, D)

    half = QK_ROPE // 2
    freqs = ROPE_THETA ** (-jnp.arange(half, dtype=F32) / half)
    freq_lane = jnp.zeros((1, LANES), F32).at[0, QK_NOPE:QK_NOPE + QK_ROPE].set(jnp.concatenate([freqs, freqs]))
    pos_col = positions.reshape(T, 1)
    ca, sb = _rope_call(pos_col, freq_lane)

    nq, nk = S // TQ_ATT, S // TK_ATT
    pq = positions.reshape(B, nq, TQ_ATT)
    pk = positions.reshape(B, nk, TK_ATT)
    qmin, qmax = pq.min(-1), pq.max(-1)
    kmin, kmax = pk.min(-1), pk.max(-1)
    needed = kmin[:, None, :] <= qmax[:, :, None]
    nhi = jnp.max(jnp.where(needed, jnp.arange(1, nk + 1, dtype=jnp.int32), 0), axis=-1)
    full = kmax[:, None, :] <= qmin[:, :, None]
    nfull = jnp.min(jnp.where(full, nk, jnp.arange(nk, dtype=jnp.int32)), axis=-1)
    nfull = jnp.minimum(nfull, nhi).astype(jnp.int32).reshape(-1)
    nhi = nhi.astype(jnp.int32).reshape(-1)
    pos_row = positions.reshape(B, 1, S)

    def peer_layer(x2, l, final_norm):
        wqt = peer_w_q[l].T.astype(BF16)
        keys = peer_sub_keys[l].reshape(2 * PEER_HEADS, N_KEYS, PEER_HALF).astype(BF16)
        u = (peer_u[l] * (1.0 / math.sqrt(2.0))).astype(BF16)
        vt = peer_v[l].T.astype(BF16)
        return _peer_call(x2, mod_all[l], norm_ffn_g[l][None], wqt, keys, u, vt, final_g[None], S, final_norm)

    for l in range(N_A_LAYERS):
        x = _lru_call(x, mod_all[l], norm_mix_g[l][None], lru_w_in[l].astype(BF16), lru_conv_w[l],
                      lru_conv_b[l][None], lru_wa[l].astype(BF16), lru_ba[l][None], lru_wx[l].astype(BF16),
                      lru_bx[l][None], lru_lambda[l][None], lru_w_out[l].astype(BF16))
        x = peer_layer(x.reshape(T, D), l, False).reshape(B, S, D)

    x2 = x.reshape(T, D)

    zc = jnp.zeros((D, QK_NOPE), F32)
    zt = jnp.zeros((D, HEAD_PAD - QK_NOPE - QK_ROPE), F32)
    w1 = jnp.concatenate([mla_w_dkv, zc, mla_w_kr, zt, zc, _swap_halves(mla_w_kr), zt], axis=-1).astype(BF16)
    wuk = jnp.concatenate([mla_w_uk.reshape(KV_LORA, MLA_HEADS, QK_NOPE),
                           jnp.zeros((KV_LORA, MLA_HEADS, HEAD_PAD - QK_NOPE), F32)], axis=-1)
    wuk = wuk.reshape(KV_LORA, MLA_HEADS * HEAD_PAD).astype(BF16)
    k_all, v_all = _kv_call(x2, mod_kv, kv_norm_g[None], w1, mla_kv_latent_g[None], wuk,
                            mla_w_uv.astype(BF16), ca, sb, S)

    for j in range(DEPTH - N_A_LAYERS):
        l = N_A_LAYERS + j
        wq = mla_w_uq[j].reshape(Q_LORA, MLA_HEADS, QK_NOPE + QK_ROPE)
        w_nope, w_rope = wq[..., :QK_NOPE], wq[..., QK_NOPE:]
        wa = _head_pad_cols(w_nope, w_rope, Q_LORA).astype(BF16)
        wb = _head_pad_cols(jnp.zeros_like(w_nope), _swap_halves(w_rope), Q_LORA).astype(BF16)
        q = _q_call(x2, mod_all[l], norm_mix_g[l][None], mla_w_dq[j].astype(BF16), mla_q_latent_g[j][None],
                    wa, wb, ca, sb, S)
        o = _attn_call(nfull, nhi, q, k_all, v_all, pos_col, pos_row, B, S)
        x2 = _oproj_call(x2, o, mod_all[l], mla_w_o[j].astype(BF16), S)
        x2 = peer_layer(x2, l, l == DEPTH - 1)

    return x2.reshape(B, S, D)
```

```python
import functools
import math

import jax
import jax.numpy as jnp
from jax import lax
from jax.experimental import pallas as pl
from jax.experimental.pallas import tpu as pltpu

F32 = jnp.float32
BF16 = jnp.bfloat16

D_MODEL = 1024
DEPTH = 4
N_A_LAYERS = DEPTH // 2
LRU_HEADS = 4
LRU_BLOCK = D_MODEL // LRU_HEADS
CONV_WIDTH = 4
LRU_C = 8.0
MLA_HEADS = 16
QK_NOPE = 64
QK_ROPE = 32
V_HEAD = 64
Q_LORA = 384
KV_LORA = 256
ROPE_THETA = 10000.0
ATTN_SCALE = 1.0 / math.sqrt(QK_NOPE + QK_ROPE)
PEER_HEADS = 8
N_KEYS = 128
PEER_TOPK = 16
PEER_HALF = 128
RMS_EPS = 1e-6
NEG_INF = -1e30
LOG2E = math.log2(math.e)
GELU_OUT_SCALE = 0.5 * math.sqrt(2.0)

LANES = 128
SUBLANES = 8
HEAD_PAD = LANES
VMEM_LIMIT = 56 * 1024 * 1024

TS_LRU = 256
TM_TOK = 512
TQ_ATT = 512
TK_ATT = 1024
KSUB_ATT = 512
QSUB_ATT = 256
ONES_ROWS = 16
TM_PEER = 512
EC_PEER = 2048


def _cparams(sem):
    return pltpu.CompilerParams(dimension_semantics=sem, vmem_limit_bytes=VMEM_LIMIT)


def _rms(x, g):
    return x * lax.rsqrt(jnp.mean(x * x, axis=-1, keepdims=True) + RMS_EPS) * g


def _gelu(x):
    return 0.5 * x * (1.0 + lax.erf(x * (1.0 / math.sqrt(2.0))))


def _expm1(y):
    series = y * (1.0 + 0.5 * y * (1.0 + (1.0 / 3.0) * y * (1.0 + 0.25 * y * (1.0 + 0.2 * y))))
    return jnp.where(jnp.abs(y) < 0.05, series, jnp.exp(y) - 1.0)


def _const_spec(shape):
    nd = len(shape)
    return pl.BlockSpec(shape, lambda *_: (0,) * nd)


def _mod_kernel(c_ref, w_ref, b_ref, o_ref):
    c = c_ref[...]
    sc = c * jax.nn.sigmoid(c)
    o_ref[...] = jnp.dot(sc.astype(BF16), w_ref[...].astype(BF16),
                         preferred_element_type=F32) + b_ref[...]


def _mod_call(c_pad, w, b, tn):
    L, D, N = w.shape
    return pl.pallas_call(
        _mod_kernel,
        out_shape=jax.ShapeDtypeStruct((L, SUBLANES, N), F32),
        grid=(L, N // tn),
        in_specs=[pl.BlockSpec((SUBLANES, D), lambda l, n: (0, 0)),
                  pl.BlockSpec((None, D, tn), lambda l, n: (l, 0, n)),
                  pl.BlockSpec((None, 1, tn), lambda l, n: (l, 0, n))],
        out_specs=pl.BlockSpec((None, SUBLANES, tn), lambda l, n: (l, 0, n)),
        compiler_params=_cparams(("arbitrary", "arbitrary")),
        name="adaln_mod",
    )(c_pad, w, b)


def _rope_kernel(pos_ref, freq_ref, ca_ref, sb_ref):
    ang = pos_ref[...].astype(F32) * freq_ref[...]
    lane = lax.broadcasted_iota(jnp.int32, ang.shape, 1)
    cos, sin = jnp.cos(ang), jnp.sin(ang)
    half = QK_ROPE // 2
    ca_ref[...] = jnp.where(lane < QK_NOPE, 1.0, jnp.where(lane < QK_NOPE + QK_ROPE, cos, 0.0))
    sb_ref[...] = jnp.where(lane < QK_NOPE, 0.0,
                            jnp.where(lane < QK_NOPE + half, -sin,
                                      jnp.where(lane < QK_NOPE + QK_ROPE, sin, 0.0)))


def _rope_call(pos_col, freq_lane):
    T = pos_col.shape[0]
    tm = TM_TOK
    spec = pl.BlockSpec((tm, LANES), lambda i: (i, 0))
    return pl.pallas_call(
        _rope_kernel,
        out_shape=(jax.ShapeDtypeStruct((T, LANES), F32),) * 2,
        grid=(T // tm,),
        in_specs=[pl.BlockSpec((tm, 1), lambda i: (i, 0)), _const_spec((1, LANES))],
        out_specs=(spec, spec),
        compiler_params=_cparams(("arbitrary",)),
        name="rope_tables",
    )(pos_col, freq_lane)


def _shift_rows(x, d, fill):
    n = x.shape[0]
    if d % SUBLANES == 0:
        return jnp.concatenate([jnp.full((d, x.shape[1]), fill, x.dtype), x[:n - d]], axis=0)
    rolled = pltpu.roll(x, d, 0)
    row = lax.broadcasted_iota(jnp.int32, x.shape, 0)
    return jnp.where(row < d, fill, rolled)


def _lru_kernel(x_ref, mod_ref, ng_ref, win_ref, cw_ref, cb_ref, wa_ref, ba_ref, wx_ref, bx_ref,
                lam_ref, wout_ref, o_ref, rec_s, h_s, *, ts):
    D = D_MODEL

    @pl.when(pl.program_id(1) == 0)
    def _():
        rec_s[0:SUBLANES, :] = jnp.zeros((SUBLANES, D), F32)
        h_s[...] = jnp.zeros((SUBLANES, D), F32)

    x = x_ref[...]
    mod = mod_ref[...]
    sh1, sc1, g1 = mod[0:1], mod[1:2], mod[2:3]
    h = _rms(x, ng_ref[...]) * (1.0 + sc1) + sh1
    u = jnp.dot(h.astype(BF16), win_ref[...], preferred_element_type=F32)
    gate = _gelu(u[:, :D])
    rec = u[:, D:]

    rec_s[SUBLANES:ts + SUBLANES, :] = rec
    cw = cw_ref[...]
    xc = (cb_ref[...] + rec * cw[3:4]
          + rec_s[SUBLANES - 1:ts + SUBLANES - 1, :] * cw[2:3]
          + rec_s[SUBLANES - 2:ts + SUBLANES - 2, :] * cw[1:2]
          + rec_s[SUBLANES - 3:ts + SUBLANES - 3, :] * cw[0:1])
    rec_s[0:SUBLANES, :] = rec[ts - SUBLANES:ts, :]

    xcb = xc.astype(BF16)
    r_pre = jnp.concatenate(
        [jnp.dot(xcb[:, k * LRU_BLOCK:(k + 1) * LRU_BLOCK], wa_ref[k], preferred_element_type=F32)
         for k in range(LRU_HEADS)], axis=-1)
    i_pre = jnp.concatenate(
        [jnp.dot(xcb[:, k * LRU_BLOCK:(k + 1) * LRU_BLOCK], wx_ref[k], preferred_element_type=F32)
         for k in range(LRU_HEADS)], axis=-1)
    r = jax.nn.sigmoid(r_pre + ba_ref[...])
    gi = jax.nn.sigmoid(i_pre + bx_ref[...])
    z = -lam_ref[...]
    softplus = jnp.maximum(z, 0.0) + jnp.log1p(jnp.exp(-jnp.abs(z)))
    log_a = (-LRU_C) * r * softplus
    a = jnp.exp(log_a)
    b = jnp.sqrt(-_expm1(2.0 * log_a)) * (gi * xc)

    d = 1
    while d < ts:
        a_sh = _shift_rows(a, d, 1.0)
        b_sh = _shift_rows(b, d, 0.0)
        b = a * b_sh + b
        a = a * a_sh
        d *= 2
    h0 = h_s[SUBLANES - 1:SUBLANES, :]
    hs = a * h0 + b
    h_s[...] = hs[ts - SUBLANES:ts, :]

    y = jnp.dot((gate * hs).astype(BF16), wout_ref[...], preferred_element_type=F32)
    o_ref[...] = x + g1 * y


def _lru_call(x, mod, ng, w_in, cw, cb, wa, ba, wx, bx, lam, w_out):
    B, S, D = x.shape
    ts = TS_LRU
    kern = functools.partial(_lru_kernel, ts=ts)
    return pl.pallas_call(
        kern,
        out_shape=jax.ShapeDtypeStruct((B, S, D), F32),
        grid=(B, S // ts),
        in_specs=[pl.BlockSpec((None, ts, D), lambda b, s: (b, s, 0)),
                  pl.BlockSpec((None, 6, D), lambda b, s: (b, 0, 0)),
                  _const_spec((1, D)), _const_spec((D, 2 * D)), _const_spec((CONV_WIDTH, D)),
                  _const_spec((1, D)), _const_spec((LRU_HEADS, LRU_BLOCK, LRU_BLOCK)), _const_spec((1, D)),
                  _const_spec((LRU_HEADS, LRU_BLOCK, LRU_BLOCK)), _const_spec((1, D)), _const_spec((1, D)),
                  _const_spec((D, D))],
        out_specs=pl.BlockSpec((None, ts, D), lambda b, s: (b, s, 0)),
        scratch_shapes=[pltpu.VMEM((ts + SUBLANES, D), F32), pltpu.VMEM((SUBLANES, D), F32)],
        compiler_params=_cparams(("arbitrary", "arbitrary")),
        name="rglru_layer",
    )(x, mod, ng, w_in, cw, cb, wa, ba, wx, bx, lam, w_out)


def _kv_kernel(x_ref, mod_ref, ng_ref, w1_ref, lg_ref, wuk_ref, wuv_ref, ca_ref, sb_ref, k_ref, v_ref):
    x = x_ref[...]
    mod = mod_ref[...]
    shift, scale = mod[0:1], mod[1:2]
    h = _rms(x, ng_ref[...]) * (1.0 + scale) + shift
    t = jnp.dot(h.astype(BF16), w1_ref[...], preferred_element_type=F32)
    c_kv = _rms(t[:, :KV_LORA], lg_ref[...])
    kr = t[:, KV_LORA:KV_LORA + LANES] * ca_ref[...] + t[:, KV_LORA + LANES:] * sb_ref[...]
    ckb = c_kv.astype(BF16)
    kn = jnp.dot(ckb, wuk_ref[...], preferred_element_type=F32)
    k_ref[...] = (kn + jnp.concatenate([kr] * MLA_HEADS, axis=-1)).astype(BF16)
    v_ref[...] = jnp.dot(ckb, wuv_ref[...], preferred_element_type=F32).T.astype(BF16)


def _kv_call(x2, mod, ng, w1, lg, wuk, wuv, ca, sb, S):
    T, D = x2.shape
    tm = TM_TOK
    nb = S // tm
    HP = MLA_HEADS * HEAD_PAD
    HV = MLA_HEADS * V_HEAD
    return pl.pallas_call(
        _kv_kernel,
        out_shape=(jax.ShapeDtypeStruct((T, HP), BF16), jax.ShapeDtypeStruct((HV, T), BF16)),
        grid=(T // tm,),
        in_specs=[pl.BlockSpec((tm, D), lambda i: (i, 0)),
                  pl.BlockSpec((None, 2, D), lambda i: (i // nb, 0, 0)),
                  _const_spec((1, D)), _const_spec(w1.shape), _const_spec((1, KV_LORA)),
                  _const_spec(wuk.shape), _const_spec(wuv.shape),
                  pl.BlockSpec((tm, LANES), lambda i: (i, 0)), pl.BlockSpec((tm, LANES), lambda i: (i, 0))],
        out_specs=(pl.BlockSpec((tm, HP), lambda i: (i, 0)), pl.BlockSpec((HV, tm), lambda i: (0, i))),
        compiler_params=_cparams(("arbitrary",)),
        name="shared_kv",
    )(x2, mod, ng, w1, lg, wuk, wuv, ca, sb)


def _q_kernel(x_ref, mod_ref, ng_ref, wdq_ref, lg_ref, wa_ref, wb_ref, ca_ref, sb_ref, q_ref):
    x = x_ref[...]
    mod = mod_ref[...]
    sh1, sc1 = mod[0:1], mod[1:2]
    h = _rms(x, ng_ref[...]) * (1.0 + sc1) + sh1
    ql = _rms(jnp.dot(h.astype(BF16), wdq_ref[...], preferred_element_type=F32), lg_ref[...])
    qlb = ql.astype(BF16)
    qa = jnp.dot(qlb, wa_ref[...], preferred_element_type=F32)
    qb = jnp.dot(qlb, wb_ref[...], preferred_element_type=F32)
    ca = jnp.concatenate([ca_ref[...]] * MLA_HEADS, axis=-1)
    sb = jnp.concatenate([sb_ref[...]] * MLA_HEADS, axis=-1)
    q_ref[...] = ((qa * ca + qb * sb) * (ATTN_SCALE * LOG2E)).T.astype(BF16)


def _q_call(x2, mod, ng, wdq, lg, wa, wb, ca, sb, S):
    T, D = x2.shape
    tm = TM_TOK
    nb = S // tm
    HP = MLA_HEADS * HEAD_PAD
    return pl.pallas_call(
        _q_kernel,
        out_shape=jax.ShapeDtypeStruct((HP, T), BF16),
        grid=(T // tm,),
        in_specs=[pl.BlockSpec((tm, D), lambda i: (i, 0)),
                  pl.BlockSpec((None, 6, D), lambda i: (i // nb, 0, 0)),
                  _const_spec((1, D)), _const_spec(wdq.shape), _const_spec((1, Q_LORA)),
                  _const_spec(wa.shape), _const_spec(wb.shape),
                  pl.BlockSpec((tm, LANES), lambda i: (i, 0)), pl.BlockSpec((tm, LANES), lambda i: (i, 0))],
        out_specs=pl.BlockSpec((HP, tm), lambda i: (0, i)),
        compiler_params=_cparams(("arbitrary",)),
        name="mla_queries",
    )(x2, mod, ng, wdq, lg, wa, wb, ca, sb)


def _attn_kernel(nfull_ref, nhi_ref, qt_ref, k_ref, vt_ref, pq_ref, pk_ref, o_ref, *, tq, tk, nq):
    b = pl.program_id(0)
    qi = pl.program_id(2)
    nfull = nfull_ref[b * nq + qi]
    nhi = nhi_ref[b * nq + qi]
    pos_q = pq_ref[...]

    def step(kc, carry, masked):
        nstate = 2 * (tq // QSUB_ATT)
        nstream = nstate * (tk // KSUB_ATT)
        state = list(carry)

        def split(sidx):
            kb, sid = divmod(sidx, nstate)
            hh, qs = divmod(sid, tq // QSUB_ATT)
            return pl.multiple_of(kc * tk + kb * KSUB_ATT, KSUB_ATT), sid, hh, qs

        def scores(sidx):
            off, _, hh, qs = split(sidx)
            k = k_ref[pl.ds(off, KSUB_ATT), hh * HEAD_PAD:(hh + 1) * HEAD_PAD]
            qt = qt_ref[hh * HEAD_PAD:(hh + 1) * HEAD_PAD, qs * QSUB_ATT:(qs + 1) * QSUB_ATT]
            return jnp.dot(k, qt, preferred_element_type=F32)

        def update(sidx, st):
            off, sid, hh, qs = split(sidx)
            m, acc = state[sid]
            if masked:
                keep = pk_ref[pl.ds(off, KSUB_ATT), :] <= pos_q[:, qs * QSUB_ATT:(qs + 1) * QSUB_ATT]
                st = jnp.where(keep, st, NEG_INF)
            m_new = jnp.maximum(m, jnp.max(st, axis=0, keepdims=True))
            p = jnp.exp2((st - m_new).astype(BF16))
            alpha = jnp.exp2(m - m_new)
            vt = jnp.concatenate([vt_ref[hh * V_HEAD:(hh + 1) * V_HEAD, pl.ds(off, KSUB_ATT)],
                                  jnp.ones((ONES_ROWS, KSUB_ATT), BF16)], axis=0)
            state[sid] = (m_new, alpha * acc + jnp.dot(vt, p, preferred_element_type=F32))

        st_next = scores(0)
        for sidx in range(nstream):
            st_cur = st_next
            if sidx + 1 < nstream:
                st_next = scores(sidx + 1)
            update(sidx, st_cur)
        return tuple(state)

    nqs = tq // QSUB_ATT
    one = (jnp.full((1, QSUB_ATT), NEG_INF, F32), jnp.zeros((V_HEAD + ONES_ROWS, QSUB_ATT), F32))
    carry = lax.fori_loop(0, nfull, functools.partial(step, masked=False), (one,) * (2 * nqs))
    fin = lax.fori_loop(nfull, nhi, functools.partial(step, masked=True), carry)
    heads = [jnp.concatenate([fin[hh * nqs + qs][1][:V_HEAD] / fin[hh * nqs + qs][1][V_HEAD:V_HEAD + 1]
                              for qs in range(nqs)], axis=1) for hh in range(2)]
    o_ref[...] = jnp.concatenate(heads, axis=0).T.astype(BF16)


def _attn_call(nfull, nhi, qt, k, vt, pos_col, pos_row, B, S):
    tq, tk = TQ_ATT, TK_ATT
    nq = S // tq
    T = B * S
    kern = functools.partial(_attn_kernel, tq=tq, tk=tk, nq=nq)
    grid_spec = pltpu.PrefetchScalarGridSpec(
        num_scalar_prefetch=2,
        grid=(B, MLA_HEADS // 2, nq),
        in_specs=[pl.BlockSpec((2 * HEAD_PAD, tq), lambda b, h, i, *_: (h, b * nq + i)),
                  pl.BlockSpec((S, 2 * HEAD_PAD), lambda b, h, i, *_: (b, h)),
                  pl.BlockSpec((2 * V_HEAD, S), lambda b, h, i, *_: (h, b)),
                  pl.BlockSpec((None, 1, tq), lambda b, h, i, *_: (b, 0, i)),
                  pl.BlockSpec((S, 1), lambda b, h, i, *_: (b, 0))],
        out_specs=pl.BlockSpec((tq, 2 * V_HEAD), lambda b, h, i, *_: (b * nq + i, h)),
    )
    return pl.pallas_call(
        kern,
        out_shape=jax.ShapeDtypeStruct((T, MLA_HEADS * V_HEAD), BF16),
        grid_spec=grid_spec,
        compiler_params=_cparams(("arbitrary", "arbitrary", "arbitrary")),
        name="mla_attention",
    )(nfull, nhi, qt, k, vt, pos_row, pos_col)


def _oproj_kernel(x_ref, o_ref, mod_ref, w_ref, out_ref):
    g1 = mod_ref[...][2:3]
    y = jnp.dot(o_ref[...], w_ref[...], preferred_element_type=F32)
    out_ref[...] = x_ref[...] + g1 * y


def _oproj_call(x2, o, mod, w_o, S):
    T, D = x2.shape
    tm = TM_TOK
    nb = S // tm
    return pl.pallas_call(
        _oproj_kernel,
        out_shape=jax.ShapeDtypeStruct((T, D), F32),
        grid=(T // tm,),
        in_specs=[pl.BlockSpec((tm, D), lambda i: (i, 0)),
                  pl.BlockSpec((tm, o.shape[1]), lambda i: (i, 0)),
                  pl.BlockSpec((None, 6, D), lambda i: (i // nb, 0, 0)),
                  _const_spec(w_o.shape)],
        out_specs=pl.BlockSpec((tm, D), lambda i: (i, 0)),
        compiler_params=_cparams(("arbitrary",)),
        name="mla_out_proj",
    )(x2, o, mod, w_o)


TOP_ROWS = 3 * SUBLANES
SCORE_BLOCKS_PER_ITER = 4


def _merge_sort_network(n):
    pairs, p = [], 1
    while p < n:
        k = p
        while k >= 1:
            for j in range(k % p, n - k, 2 * k):
                for i in range(min(k, n - j - k)):
                    if (i + j) // (2 * p) == (i + j + k) // (2 * p):
                        pairs.append((i + j, i + j + k))
            k //= 2
        p *= 2
    return pairs


def _largest_rows(x, count):
    v = [x[r * SUBLANES:(r + 1) * SUBLANES, :] for r in range(x.shape[0] // SUBLANES)]
    size = 1 << (len(v) - 1).bit_length()
    v = v + [None] * (size - len(v))
    for i, j in _merge_sort_network(size):
        if v[j] is None:
            continue
        if v[i] is None:
            v[i], v[j] = v[j], None
        else:
            v[i], v[j] = jnp.maximum(v[i], v[j]), jnp.minimum(v[i], v[j])
    v = [t for t in v if t is not None]
    tops = []
    for j in range(count):
        m = jnp.max(v[0], axis=0, keepdims=True)
        tops.append(m)
        if j + 1 < count:
            won = v[0] == m
            depth = min(len(v), count - j)
            v = [jnp.where(won, v[r + 1] if r + 1 < len(v) else NEG_INF, v[r]) for r in range(depth)]
    return tops


def _top17_rows(x):
    pad = jnp.full((TOP_ROWS - PEER_TOPK - 1, x.shape[1]), NEG_INF, F32)
    return jnp.concatenate(_largest_rows(x, PEER_TOPK + 1) + [pad], axis=0)


def _peer_kernel(x_ref, mod_ref, ng_ref, wqt_ref, keys_ref, u_ref, vt_ref, fg_ref, o_ref,
                 h_s, st_s, thr_s, e1_s, e2_s, wa_s, acc_s, *, tm, ec, final_norm):
    ntc = tm // LANES
    e = pl.program_id(1)
    ne = pl.num_programs(1)

    @pl.when(e == 0)
    def _scores():
        x = x_ref[...]
        mod = mod_ref[...]
        sh2, sc2 = mod[3:4], mod[4:5]
        ht = (_rms(x, ng_ref[...]) * (1.0 + sc2) + sh2).T.astype(BF16)
        h_s[...] = ht
        qtb = jnp.dot(wqt_ref[...], ht, preferred_element_type=F32).astype(BF16)
        for hp in range(2 * PEER_HEADS):
            st = jnp.dot(keys_ref[hp], qtb[hp * PEER_HALF:(hp + 1) * PEER_HALF, :],
                         preferred_element_type=F32)
            for tc in range(ntc):
                st_s[tc, hp] = st[:, tc * LANES:(tc + 1) * LANES]

        def score_block(i):
            tc = i // PEER_HEADS
            hd = i % PEER_HEADS
            s1 = st_s[tc, 2 * hd]
            s2 = st_s[tc, 2 * hd + 1]
            a = _top17_rows(s1)
            b = _top17_rows(s2)
            cands = [a[0:1] + b]
            for r in range(1, SUBLANES):
                cands.append(a[r:r + 1] + b[0:SUBLANES])
            cands.append(a[SUBLANES:TOP_ROWS] + b[0:1])
            cand = jnp.concatenate(cands, axis=0)
            c16, c17 = _largest_rows(cand, PEER_TOPK + 1)[PEER_TOPK - 1:]
            tau = 0.5 * (c16 + c17)
            m1, m2 = a[0:1], b[0:1]
            zsum = jnp.sum(jnp.where(cand >= tau, jnp.exp(cand - (m1 + m2)), 0.0), axis=0, keepdims=True)
            thr_s[tc, hd] = jnp.exp(tau - s1 - m2)
            e1_s[tc, hd] = jnp.exp(s1 - m1) * (GELU_OUT_SCALE / zsum)
            e2_s[tc, hd] = jnp.exp(s2 - m2)

        def score_blocks(g, _):
            for bi in range(SCORE_BLOCKS_PER_ITER):
                score_block(g * SCORE_BLOCKS_PER_ITER + bi)
            return 0

        lax.fori_loop(0, ntc * PEER_HEADS // SCORE_BLOCKS_PER_ITER, score_blocks, 0)
        acc_s[...] = jnp.zeros(acc_s.shape, F32)

    at = jnp.dot(u_ref[...], h_s[...], preferred_element_type=F32)
    gact = at * (1.0 + lax.erf(at))
    ngrp = ec // LANES
    for g8 in range(ngrp // SUBLANES):
        base = pl.multiple_of(e * ngrp + g8 * SUBLANES, SUBLANES)
        for tc in range(ntc):
            thr8 = [thr_s[tc, hd, pl.ds(base, SUBLANES), :] for hd in range(PEER_HEADS)]
            e18 = [e1_s[tc, hd, pl.ds(base, SUBLANES), :] for hd in range(PEER_HEADS)]
            for j in range(SUBLANES):
                w = None
                for hd in range(PEER_HEADS):
                    e2 = e2_s[tc, hd]
                    term = jnp.where(e2 >= thr8[hd][j:j + 1, :], e2, 0.0) * e18[hd][j:j + 1, :]
                    w = term if w is None else w + term
                r0 = (g8 * SUBLANES + j) * LANES
                wa_s[r0:r0 + LANES, tc * LANES:(tc + 1) * LANES] = (
                    w * gact[r0:r0 + LANES, tc * LANES:(tc + 1) * LANES]).astype(BF16)
    acc_s[...] += jnp.dot(vt_ref[...], wa_s[...], preferred_element_type=F32)

    @pl.when(e == ne - 1)
    def _finish():
        g2 = mod_ref[...][5:6]
        xn = x_ref[...] + g2 * acc_s[...].T
        if final_norm:
            xn = _rms(xn, fg_ref[...])
        o_ref[...] = xn


def _peer_call(x2, mod, ng, wqt, keys, u, vt, fg, S, final_norm):
    T, D = x2.shape
    E = u.shape[0]
    tm, ec = TM_PEER, EC_PEER
    nb = S // tm
    ntc = tm // LANES
    kern = functools.partial(_peer_kernel, tm=tm, ec=ec, final_norm=final_norm)
    return pl.pallas_call(
        kern,
        out_shape=jax.ShapeDtypeStruct((T, D), F32),
        grid=(T // tm, E // ec),
        in_specs=[pl.BlockSpec((tm, D), lambda i, e: (i, 0)),
                  pl.BlockSpec((None, 6, D), lambda i, e: (i // nb, 0, 0)),
                  _const_spec((1, D)), _const_spec(wqt.shape), _const_spec(keys.shape),
                  pl.BlockSpec((ec, D), lambda i, e: (e, 0)),
                  pl.BlockSpec((D, ec), lambda i, e: (0, e)),
                  _const_spec((1, D))],
        out_specs=pl.BlockSpec((tm, D), lambda i, e: (i, 0)),
        scratch_shapes=[pltpu.VMEM((D, tm), BF16),
                        pltpu.VMEM((ntc, 2 * PEER_HEADS, N_KEYS, LANES), F32),
                        pltpu.VMEM((ntc, PEER_HEADS, N_KEYS, LANES), F32),
                        pltpu.VMEM((ntc, PEER_HEADS, N_KEYS, LANES), F32),
                        pltpu.VMEM((ntc, PEER_HEADS, N_KEYS, LANES), F32),
                        pltpu.VMEM((ec, tm), BF16),
                        pltpu.VMEM((D, tm), F32)],
        compiler_params=_cparams(("arbitrary", "arbitrary")),
        name="peer_layer",
    )(x2, mod, ng, wqt, keys, u, vt, fg)


def _head_pad_cols(w_nope, w_rope, n_in):
    zeros = jnp.zeros((n_in, MLA_HEADS, HEAD_PAD - QK_NOPE - QK_ROPE), w_nope.dtype)
    return jnp.concatenate([w_nope, w_rope, zeros], axis=-1).reshape(n_in, MLA_HEADS * HEAD_PAD)


def _swap_halves(w):
    half = QK_ROPE // 2
    return jnp.concatenate([w[..., half:], w[..., :half]], axis=-1)


def kernel(x, c, positions, ada_w, ada_b, norm_mix_g, norm_ffn_g, lru_w_in, lru_conv_w, lru_conv_b, lru_wa, lru_ba, lru_wx, lru_bx, lru_lambda, lru_w_out, kv_ada_w, kv_ada_b, kv_norm_g, mla_w_dkv, mla_w_kr, mla_kv_latent_g, mla_w_uk, mla_w_uv, mla_w_dq, mla_q_latent_g, mla_w_uq, mla_w_o, peer_w_q, peer_sub_keys, peer_u, peer_v, final_g):
    B, S, D = x.shape
    T = B * S
    assert D == D_MODEL and S % TM_TOK == 0 and S % TS_LRU == 0 and S % TM_PEER == 0
    assert S % TQ_ATT == 0 and S % TK_ATT == 0

    c_pad = jnp.zeros((SUBLANES, D), F32).at[:B].set(c)
    mod_all = _mod_call(c_pad, ada_w, ada_b[:, None, :], 1536)
    mod_all = mod_all[:, :B].reshape(DEPTH, B, 6, D)
    mod_kv = _mod_call(c_pad, kv_ada_w[None], kv_ada_b[None, None, :], 1024)[0, :B].reshape(B, 2, D)

    half = QK_ROPE // 2
    freqs = ROPE_THETA ** (-jnp.arange(half, dtype=F32) / half)
    freq_lane = jnp.zeros((1, LANES), F32).at[0, QK_NOPE:QK_NOPE + QK_ROPE].set(jnp.concatenate([freqs, freqs]))
    pos_col = positions.reshape(T, 1)
    ca, sb = _rope_call(pos_col, freq_lane)

    nq, nk = S // TQ_ATT, S // TK_ATT
    pq = positions.reshape(B, nq, TQ_ATT)
    pk = positions.reshape(B, nk, TK_ATT)
    qmin, qmax = pq.min(-1), pq.max(-1)
    kmin, kmax = pk.min(-1), pk.max(-1)
    needed = kmin[:, None, :] <= qmax[:, :, None]
    nhi = jnp.max(jnp.where(needed, jnp.arange(1, nk + 1, dtype=jnp.int32), 0), axis=-1)
    full = kmax[:, None, :] <= qmin[:, :, None]
    nfull = jnp.min(jnp.where(full, nk, jnp.arange(nk, dtype=jnp.int32)), axis=-1)
    nfull = jnp.minimum(nfull, nhi).astype(jnp.int32).reshape(-1)
    nhi = nhi.astype(jnp.int32).reshape(-1)
    pos_row = positions.reshape(B, 1, S)

    def peer_layer(x2, l, final_norm):
        wqt = peer_w_q[l].T.astype(BF16)
        keys = peer_sub_keys[l].reshape(2 * PEER_HEADS, N_KEYS, PEER_HALF).astype(BF16)
        u = (peer_u[l] * (1.0 / math.sqrt(2.0))).astype(BF16)
        vt = peer_v[l].T.astype(BF16)
        return _peer_call(x2, mod_all[l], norm_ffn_g[l][None], wqt, keys, u, vt, final_g[None], S, final_norm)

    for l in range(N_A_LAYERS):
        x = _lru_call(x, mod_all[l], norm_mix_g[l][None], lru_w_in[l].astype(BF16), lru_conv_w[l],
                      lru_conv_b[l][None], lru_wa[l].astype(BF16), lru_ba[l][None], lru_wx[l].astype(BF16),
                      lru_bx[l][None], lru_lambda[l][None], lru_w_out[l].astype(BF16))
        x = peer_layer(x.reshape(T, D), l, False).reshape(B, S, D)

    x2 = x.reshape(T, D)

    zc = jnp.zeros((D, QK_NOPE), F32)
    zt = jnp.zeros((D, HEAD_PAD - QK_NOPE - QK_ROPE), F32)
    w1 = jnp.concatenate([mla_w_dkv, zc, mla_w_kr, zt, zc, _swap_halves(mla_w_kr), zt], axis=-1).astype(BF16)
    wuk = jnp.concatenate([mla_w_uk.reshape(KV_LORA, MLA_HEADS, QK_NOPE),
                           jnp.zeros((KV_LORA, MLA_HEADS, HEAD_PAD - QK_NOPE), F32)], axis=-1)
    wuk = wuk.reshape(KV_LORA, MLA_HEADS * HEAD_PAD).astype(BF16)
    k_all, v_all = _kv_call(x2, mod_kv, kv_norm_g[None], w1, mla_kv_latent_g[None], wuk,
                            mla_w_uv.astype(BF16), ca, sb, S)

    for j in range(DEPTH - N_A_LAYERS):
        l = N_A_LAYERS + j
        wq = mla_w_uq[j].reshape(Q_LORA, MLA_HEADS, QK_NOPE + QK_ROPE)
        w_nope, w_rope = wq[..., :QK_NOPE], wq[..., QK_NOPE:]
        wa = _head_pad_cols(w_nope, w_rope, Q_LORA).astype(BF16)
        wb = _head_pad_cols(jnp.zeros_like(w_nope), _swap_halves(w_rope), Q_LORA).astype(BF16)
        q = _q_call(x2, mod_all[l], norm_mix_g[l][None], mla_w_dq[j].astype(BF16), mla_q_latent_g[j][None],
                    wa, wb, ca, sb, S)
        o = _attn_call(nfull, nhi, q, k_all, v_all, pos_col, pos_row, B, S)
        x2 = _oproj_call(x2, o, mod_all[l], mla_w_o[j].astype(BF16), S)
        x2 = peer_layer(x2, l, l == DEPTH - 1)

    return x2.reshape(B, S, D)
```

```python
import functools
import math

import jax
import jax.numpy as jnp
from jax import lax
from jax.experimental import pallas as pl
from jax.experimental.pallas import tpu as pltpu

F32 = jnp.float32
BF16 = jnp.bfloat16

D_MODEL = 1024
DEPTH = 4
N_A_LAYERS = DEPTH // 2
LRU_HEADS = 4
LRU_BLOCK = D_MODEL // LRU_HEADS
CONV_WIDTH = 4
LRU_C = 8.0
MLA_HEADS = 16
QK_NOPE = 64
QK_ROPE = 32
V_HEAD = 64
Q_LORA = 384
KV_LORA = 256
ROPE_THETA = 10000.0
ATTN_SCALE = 1.0 / math.sqrt(QK_NOPE + QK_ROPE)
PEER_HEADS = 8
N_KEYS = 128
PEER_TOPK = 16
PEER_HALF = 128
RMS_EPS = 1e-6
NEG_INF = -1e30
LOG2E = math.log2(math.e)
GELU_OUT_SCALE = 0.5 * math.sqrt(2.0)

LANES = 128
SUBLANES = 8
HEAD_PAD = LANES
VMEM_LIMIT = 56 * 1024 * 1024

TS_LRU = 256
TM_TOK = 512
TQ_ATT = 512
TK_ATT = 1024
KSUB_ATT = 512
QSUB_ATT = 256
ONES_ROWS = 16
TM_PEER = 512
EC_PEER = 2048


def _cparams(sem):
    return pltpu.CompilerParams(dimension_semantics=sem, vmem_limit_bytes=VMEM_LIMIT)


def _rms(x, g):
    return x * lax.rsqrt(jnp.mean(x * x, axis=-1, keepdims=True) + RMS_EPS) * g


def _gelu(x):
    return 0.5 * x * (1.0 + lax.erf(x * (1.0 / math.sqrt(2.0))))


def _expm1(y):
    series = y * (1.0 + 0.5 * y * (1.0 + (1.0 / 3.0) * y * (1.0 + 0.25 * y * (1.0 + 0.2 * y))))
    return jnp.where(jnp.abs(y) < 0.05, series, jnp.exp(y) - 1.0)


def _const_spec(shape):
    nd = len(shape)
    return pl.BlockSpec(shape, lambda *_: (0,) * nd)


def _mod_kernel(c_ref, w_ref, b_ref, o_ref):
    c = c_ref[...]
    sc = c * jax.nn.sigmoid(c)
    o_ref[...] = jnp.dot(sc.astype(BF16), w_ref[...].astype(BF16),
                         preferred_element_type=F32) + b_ref[...]


def _mod_call(c_pad, w, b, tn):
    L, D, N = w.shape
    return pl.pallas_call(
        _mod_kernel,
        out_shape=jax.ShapeDtypeStruct((L, SUBLANES, N), F32),
        grid=(L, N // tn),
        in_specs=[pl.BlockSpec((SUBLANES, D), lambda l, n: (0, 0)),
                  pl.BlockSpec((None, D, tn), lambda l, n: (l, 0, n)),
                  pl.BlockSpec((None, 1, tn), lambda l, n: (l, 0, n))],
        out_specs=pl.BlockSpec((None, SUBLANES, tn), lambda l, n: (l, 0, n)),
        compiler_params=_cparams(("arbitrary", "arbitrary")),
        name="adaln_mod",
    )(c_pad, w, b)


def _rope_kernel(pos_ref, freq_ref, ca_ref, sb_ref):
    ang = pos_ref[...].astype(F32) * freq_ref[...]
    lane = lax.broadcasted_iota(jnp.int32, ang.shape, 1)
    cos, sin = jnp.cos(ang), jnp.sin(ang)
    half = QK_ROPE // 2
    ca_ref[...] = jnp.where(lane < QK_NOPE, 1.0, jnp.where(lane < QK_NOPE + QK_ROPE, cos, 0.0))
    sb_ref[...] = jnp.where(lane < QK_NOPE, 0.0,
                            jnp.where(lane < QK_NOPE + half, -sin,
                                      jnp.where(lane < QK_NOPE + QK_ROPE, sin, 0.0)))


def _rope_call(pos_col, freq_lane):
    T = pos_col.shape[0]
    tm = TM_TOK
    spec = pl.BlockSpec((tm, LANES), lambda i: (i, 0))
    return pl.pallas_call(
        _rope_kernel,
        out_shape=(jax.ShapeDtypeStruct((T, LANES), F32),) * 2,
        grid=(T // tm,),
        in_specs=[pl.BlockSpec((tm, 1), lambda i: (i, 0)), _const_spec((1, LANES))],
        out_specs=(spec, spec),
        compiler_params=_cparams(("arbitrary",)),
        name="rope_tables",
    )(pos_col, freq_lane)


def _shift_rows(x, d, fill):
    n = x.shape[0]
    if d % SUBLANES == 0:
        return jnp.concatenate([jnp.full((d, x.shape[1]), fill, x.dtype), x[:n - d]], axis=0)
    rolled = pltpu.roll(x, d, 0)
    row = lax.broadcasted_iota(jnp.int32, x.shape, 0)
    return jnp.where(row < d, fill, rolled)


def _lru_kernel(x_ref, mod_ref, ng_ref, win_ref, cw_ref, cb_ref, wa_ref, ba_ref, wx_ref, bx_ref,
                lam_ref, wout_ref, o_ref, rec_s, h_s, *, ts):
    D = D_MODEL

    @pl.when(pl.program_id(1) == 0)
    def _():
        rec_s[0:SUBLANES, :] = jnp.zeros((SUBLANES, D), F32)
        h_s[...] = jnp.zeros((SUBLANES, D), F32)

    x = x_ref[...]
    mod = mod_ref[...]
    sh1, sc1, g1 = mod[0:1], mod[1:2], mod[2:3]
    h = _rms(x, ng_ref[...]) * (1.0 + sc1) + sh1
    u = jnp.dot(h.astype(BF16), win_ref[...], preferred_element_type=F32)
    gate = _gelu(u[:, :D])
    rec = u[:, D:]

    rec_s[SUBLANES:ts + SUBLANES, :] = rec
    cw = cw_ref[...]
    xc = (cb_ref[...] + rec * cw[3:4]
          + rec_s[SUBLANES - 1:ts + SUBLANES - 1, :] * cw[2:3]
          + rec_s[SUBLANES - 2:ts + SUBLANES - 2, :] * cw[1:2]
          + rec_s[SUBLANES - 3:ts + SUBLANES - 3, :] * cw[0:1])
    rec_s[0:SUBLANES, :] = rec[ts - SUBLANES:ts, :]

    xcb = xc.astype(BF16)
    r_pre = jnp.concatenate(
        [jnp.dot(xcb[:, k * LRU_BLOCK:(k + 1) * LRU_BLOCK], wa_ref[k], preferred_element_type=F32)
         for k in range(LRU_HEADS)], axis=-1)
    i_pre = jnp.concatenate(
        [jnp.dot(xcb[:, k * LRU_BLOCK:(k + 1) * LRU_BLOCK], wx_ref[k], preferred_element_type=F32)
         for k in range(LRU_HEADS)], axis=-1)
    r = jax.nn.sigmoid(r_pre + ba_ref[...])
    gi = jax.nn.sigmoid(i_pre + bx_ref[...])
    z = -lam_ref[...]
    softplus = jnp.maximum(z, 0.0) + jnp.log1p(jnp.exp(-jnp.abs(z)))
    log_a = (-LRU_C) * r * softplus
    a = jnp.exp(log_a)
    b = jnp.sqrt(-_expm1(2.0 * log_a)) * (gi * xc)

    d = 1
    while d < ts:
        a_sh = _shift_rows(a, d, 1.0)
        b_sh = _shift_rows(b, d, 0.0)
        b = a * b_sh + b
        a = a * a_sh
        d *= 2
    h0 = h_s[SUBLANES - 1:SUBLANES, :]
    hs = a * h0 + b
    h_s[...] = hs[ts - SUBLANES:ts, :]

    y = jnp.dot((gate * hs).astype(BF16), wout_ref[...], preferred_element_type=F32)
    o_ref[...] = x + g1 * y


def _lru_call(x, mod, ng, w_in, cw, cb, wa, ba, wx, bx, lam, w_out):
    B, S, D = x.shape
    ts = TS_LRU
    kern = functools.partial(_lru_kernel, ts=ts)
    return pl.pallas_call(
        kern,
        out_shape=jax.ShapeDtypeStruct((B, S, D), F32),
        grid=(B, S // ts),
        in_specs=[pl.BlockSpec((None, ts, D), lambda b, s: (b, s, 0)),
                  pl.BlockSpec((None, 6, D), lambda b, s: (b, 0, 0)),
                  _const_spec((1, D)), _const_spec((D, 2 * D)), _const_spec((CONV_WIDTH, D)),
                  _const_spec((1, D)), _const_spec((LRU_HEADS, LRU_BLOCK, LRU_BLOCK)), _const_spec((1, D)),
                  _const_spec((LRU_HEADS, LRU_BLOCK, LRU_BLOCK)), _const_spec((1, D)), _const_spec((1, D)),
                  _const_spec((D, D))],
        out_specs=pl.BlockSpec((None, ts, D), lambda b, s: (b, s, 0)),
        scratch_shapes=[pltpu.VMEM((ts + SUBLANES, D), F32), pltpu.VMEM((SUBLANES, D), F32)],
        compiler_params=_cparams(("arbitrary", "arbitrary")),
        name="rglru_layer",
    )(x, mod, ng, w_in, cw, cb, wa, ba, wx, bx, lam, w_out)


def _kv_kernel(x_ref, mod_ref, ng_ref, w1_ref, lg_ref, wuk_ref, wuv_ref, ca_ref, sb_ref, k_ref, v_ref):
    x = x_ref[...]
    mod = mod_ref[...]
    shift, scale = mod[0:1], mod[1:2]
    h = _rms(x, ng_ref[...]) * (1.0 + scale) + shift
    t = jnp.dot(h.astype(BF16), w1_ref[...], preferred_element_type=F32)
    c_kv = _rms(t[:, :KV_LORA], lg_ref[...])
    kr = t[:, KV_LORA:KV_LORA + LANES] * ca_ref[...] + t[:, KV_LORA + LANES:] * sb_ref[...]
    ckb = c_kv.astype(BF16)
    kn = jnp.dot(ckb, wuk_ref[...], preferred_element_type=F32)
    k_ref[...] = (kn + jnp.concatenate([kr] * MLA_HEADS, axis=-1)).astype(BF16)
    v_ref[...] = jnp.dot(ckb, wuv_ref[...], preferred_element_type=F32).T.astype(BF16)


def _kv_call(x2, mod, ng, w1, lg, wuk, wuv, ca, sb, S):
    T, D = x2.shape
    tm = TM_TOK
    nb = S // tm
    HP = MLA_HEADS * HEAD_PAD
    HV = MLA_HEADS * V_HEAD
    return pl.pallas_call(
        _kv_kernel,
        out_shape=(jax.ShapeDtypeStruct((T, HP), BF16), jax.ShapeDtypeStruct((HV, T), BF16)),
        grid=(T // tm,),
        in_specs=[pl.BlockSpec((tm, D), lambda i: (i, 0)),
                  pl.BlockSpec((None, 2, D), lambda i: (i // nb, 0, 0)),
                  _const_spec((1, D)), _const_spec(w1.shape), _const_spec((1, KV_LORA)),
                  _const_spec(wuk.shape), _const_spec(wuv.shape),
                  pl.BlockSpec((tm, LANES), lambda i: (i, 0)), pl.BlockSpec((tm, LANES), lambda i: (i, 0))],
        out_specs=(pl.BlockSpec((tm, HP), lambda i: (i, 0)), pl.BlockSpec((HV, tm), lambda i: (0, i))),
        compiler_params=_cparams(("arbitrary",)),
        name="shared_kv",
    )(x2, mod, ng, w1, lg, wuk, wuv, ca, sb)


def _q_kernel(x_ref, mod_ref, ng_ref, wdq_ref, lg_ref, wa_ref, wb_ref, ca_ref, sb_ref, q_ref):
    x = x_ref[...]
    mod = mod_ref[...]
    sh1, sc1 = mod[0:1], mod[1:2]
    h = _rms(x, ng_ref[...]) * (1.0 + sc1) + sh1
    ql = _rms(jnp.dot(h.astype(BF16), wdq_ref[...], preferred_element_type=F32), lg_ref[...])
    qlb = ql.astype(BF16)
    qa = jnp.dot(qlb, wa_ref[...], preferred_element_type=F32)
    qb = jnp.dot(qlb, wb_ref[...], preferred_element_type=F32)
    ca = jnp.concatenate([ca_ref[...]] * MLA_HEADS, axis=-1)
    sb = jnp.concatenate([sb_ref[...]] * MLA_HEADS, axis=-1)
    q_ref[...] = ((qa * ca + qb * sb) * (ATTN_SCALE * LOG2E)).T.astype(BF16)


def _q_call(x2, mod, ng, wdq, lg, wa, wb, ca, sb, S):
    T, D = x2.shape
    tm = TM_TOK
    nb = S // tm
    HP = MLA_HEADS * HEAD_PAD
    return pl.pallas_call(
        _q_kernel,
        out_shape=jax.ShapeDtypeStruct((HP, T), BF16),
        grid=(T // tm,),
        in_specs=[pl.BlockSpec((tm, D), lambda i: (i, 0)),
                  pl.BlockSpec((None, 6, D), lambda i: (i // nb, 0, 0)),
                  _const_spec((1, D)), _const_spec(wdq.shape), _const_spec((1, Q_LORA)),
                  _const_spec(wa.shape), _const_spec(wb.shape),
                  pl.BlockSpec((tm, LANES), lambda i: (i, 0)), pl.BlockSpec((tm, LANES), lambda i: (i, 0))],
        out_specs=pl.BlockSpec((HP, tm), lambda i: (0, i)),
        compiler_params=_cparams(("arbitrary",)),
        name="mla_queries",
    )(x2, mod, ng, wdq, lg, wa, wb, ca, sb)


def _attn_kernel(nfull_ref, nhi_ref, qt_ref, k_ref, vt_ref, pq_ref, pk_ref, o_ref, *, tq, tk, nq):
    b = pl.program_id(0)
    qi = pl.program_id(2)
    nfull = nfull_ref[b * nq + qi]
    nhi = nhi_ref[b * nq + qi]
    pos_q = pq_ref[...]

    def step(kc, carry, masked, tk):
        nstate = 2 * (tq // QSUB_ATT)
        nstream = nstate * (tk // KSUB_ATT)
        state = list(carry)

        def split(sidx):
            kb, sid = divmod(sidx, nstate)
            hh, qs = divmod(sid, tq // QSUB_ATT)
            return pl.multiple_of(kc * tk + kb * KSUB_ATT, KSUB_ATT), sid, hh, qs

        def scores(sidx):
            off, _, hh, qs = split(sidx)
            k = k_ref[pl.ds(off, KSUB_ATT), hh * HEAD_PAD:(hh + 1) * HEAD_PAD]
            qt = qt_ref[hh * HEAD_PAD:(hh + 1) * HEAD_PAD, qs * QSUB_ATT:(qs + 1) * QSUB_ATT]
            return jnp.dot(k, qt, preferred_element_type=F32)

        def update(sidx, st):
            off, sid, hh, qs = split(sidx)
            m, acc = state[sid]
            if masked:
                keep = pk_ref[pl.ds(off, KSUB_ATT), :] <= pos_q[:, qs * QSUB_ATT:(qs + 1) * QSUB_ATT]
                st = jnp.where(keep, st, NEG_INF)
            m_new = jnp.maximum(m, jnp.max(st, axis=0, keepdims=True))
            p = jnp.exp2((st - m_new).astype(BF16))
            alpha = jnp.exp2(m - m_new)
            vt = jnp.concatenate([vt_ref[hh * V_HEAD:(hh + 1) * V_HEAD, pl.ds(off, KSUB_ATT)],
                                  jnp.ones((ONES_ROWS, KSUB_ATT), BF16)], axis=0)
            state[sid] = (m_new, alpha * acc + jnp.dot(vt, p, preferred_element_type=F32))

        st_next = scores(0)
        for sidx in range(nstream):
            st_cur = st_next
            if sidx + 1 < nstream:
                st_next = scores(sidx + 1)
            update(sidx, st_cur)
        return tuple(state)

    nqs = tq // QSUB_ATT
    one = (jnp.full((1, QSUB_ATT), NEG_INF, F32), jnp.zeros((V_HEAD + ONES_ROWS, QSUB_ATT), F32))
    per = tk // KSUB_ATT
    nwide = nfull // per
    carry = lax.fori_loop(0, nwide, functools.partial(step, masked=False, tk=tk), (one,) * (2 * nqs))
    carry = lax.fori_loop(nwide * per, nfull, functools.partial(step, masked=False, tk=KSUB_ATT), carry)
    fin = lax.fori_loop(nfull, nhi, functools.partial(step, masked=True, tk=KSUB_ATT), carry)
    heads = [jnp.concatenate([fin[hh * nqs + qs][1][:V_HEAD] / fin[hh * nqs + qs][1][V_HEAD:V_HEAD + 1]
                              for qs in range(nqs)], axis=1) for hh in range(2)]
    o_ref[...] = jnp.concatenate(heads, axis=0).T.astype(BF16)


def _attn_call(nfull, nhi, qt, k, vt, pos_col, pos_row, B, S):
    tq, tk = TQ_ATT, TK_ATT
    nq = S // tq
    T = B * S
    kern = functools.partial(_attn_kernel, tq=tq, tk=tk, nq=nq)
    grid_spec = pltpu.PrefetchScalarGridSpec(
        num_scalar_prefetch=2,
        grid=(B, MLA_HEADS // 2, nq),
        in_specs=[pl.BlockSpec((2 * HEAD_PAD, tq), lambda b, h, i, *_: (h, b * nq + i)),
                  pl.BlockSpec((S, 2 * HEAD_PAD), lambda b, h, i, *_: (b, h)),
                  pl.BlockSpec((2 * V_HEAD, S), lambda b, h, i, *_: (h, b)),
                  pl.BlockSpec((None, 1, tq), lambda b, h, i, *_: (b, 0, i)),
                  pl.BlockSpec((S, 1), lambda b, h, i, *_: (b, 0))],
        out_specs=pl.BlockSpec((tq, 2 * V_HEAD), lambda b, h, i, *_: (b * nq + i, h)),
    )
    return pl.pallas_call(
        kern,
        out_shape=jax.ShapeDtypeStruct((T, MLA_HEADS * V_HEAD), BF16),
        grid_spec=grid_spec,
        compiler_params=_cparams(("arbitrary", "arbitrary", "arbitrary")),
        name="mla_attention",
    )(nfull, nhi, qt, k, vt, pos_row, pos_col)


def _oproj_kernel(x_ref, o_ref, mod_ref, w_ref, out_ref):
    g1 = mod_ref[...][2:3]
    y = jnp.dot(o_ref[...], w_ref[...], preferred_element_type=F32)
    out_ref[...] = x_ref[...] + g1 * y


def _oproj_call(x2, o, mod, w_o, S):
    T, D = x2.shape
    tm = TM_TOK
    nb = S // tm
    return pl.pallas_call(
        _oproj_kernel,
        out_shape=jax.ShapeDtypeStruct((T, D), F32),
        grid=(T // tm,),
        in_specs=[pl.BlockSpec((tm, D), lambda i: (i, 0)),
                  pl.BlockSpec((tm, o.shape[1]), lambda i: (i, 0)),
                  pl.BlockSpec((None, 6, D), lambda i: (i // nb, 0, 0)),
                  _const_spec(w_o.shape)],
        out_specs=pl.BlockSpec((tm, D), lambda i: (i, 0)),
        compiler_params=_cparams(("arbitrary",)),
        name="mla_out_proj",
    )(x2, o, mod, w_o)


TOP_ROWS = 3 * SUBLANES
SCORE_BLOCKS_PER_ITER = 4


def _merge_sort_network(n):
    pairs, p = [], 1
    while p < n:
        k = p
        while k >= 1:
            for j in range(k % p, n - k, 2 * k):
                for i in range(min(k, n - j - k)):
                    if (i + j) // (2 * p) == (i + j + k) // (2 * p):
                        pairs.append((i + j, i + j + k))
            k //= 2
        p *= 2
    return pairs


def _largest_rows(x, count):
    v = [x[r * SUBLANES:(r + 1) * SUBLANES, :] for r in range(x.shape[0] // SUBLANES)]
    size = 1 << (len(v) - 1).bit_length()
    v = v + [None] * (size - len(v))
    for i, j in _merge_sort_network(size):
        if v[j] is None:
            continue
        if v[i] is None:
            v[i], v[j] = v[j], None
        else:
            v[i], v[j] = jnp.maximum(v[i], v[j]), jnp.minimum(v[i], v[j])
    v = [t for t in v if t is not None]
    tops = []
    for j in range(count):
        m = jnp.max(v[0], axis=0, keepdims=True)
        tops.append(m)
        if j + 1 < count:
            won = v[0] == m
            depth = min(len(v), count - j)
            v = [jnp.where(won, v[r + 1] if r + 1 < len(v) else NEG_INF, v[r]) for r in range(depth)]
    return tops


def _top17_rows(x):
    pad = jnp.full((TOP_ROWS - PEER_TOPK - 1, x.shape[1]), NEG_INF, F32)
    return jnp.concatenate(_largest_rows(x, PEER_TOPK + 1) + [pad], axis=0)


def _peer_kernel(x_ref, mod_ref, ng_ref, wqt_ref, keys_ref, u_ref, vt_ref, fg_ref, o_ref,
                 h_s, st_s, thr_s, e1_s, e2_s, wa_s, acc_s, *, tm, ec, final_norm):
    ntc = tm // LANES
    e = pl.program_id(1)
    ne = pl.num_programs(1)

    @pl.when(e == 0)
    def _scores():
        x = x_ref[...]
        mod = mod_ref[...]
        sh2, sc2 = mod[3:4], mod[4:5]
        ht = (_rms(x, ng_ref[...]) * (1.0 + sc2) + sh2).T.astype(BF16)
        h_s[...] = ht
        qtb = jnp.dot(wqt_ref[...], ht, preferred_element_type=F32).astype(BF16)
        for hp in range(2 * PEER_HEADS):
            st = jnp.dot(keys_ref[hp], qtb[hp * PEER_HALF:(hp + 1) * PEER_HALF, :],
                         preferred_element_type=F32)
            for tc in range(ntc):
                st_s[tc, hp] = st[:, tc * LANES:(tc + 1) * LANES]

        def score_block(i):
            tc = i // PEER_HEADS
            hd = i % PEER_HEADS
            s1 = st_s[tc, 2 * hd]
            s2 = st_s[tc, 2 * hd + 1]
            a = _top17_rows(s1)
            b = _top17_rows(s2)
            cands = [a[0:1] + b]
            for r in range(1, SUBLANES):
                cands.append(a[r:r + 1] + b[0:SUBLANES])
            cands.append(a[SUBLANES:TOP_ROWS] + b[0:1])
            cand = jnp.concatenate(cands, axis=0)
            c16, c17 = _largest_rows(cand, PEER_TOPK + 1)[PEER_TOPK - 1:]
            tau = 0.5 * (c16 + c17)
            m1, m2 = a[0:1], b[0:1]
            zsum = jnp.sum(jnp.where(cand >= tau, jnp.exp(cand - (m1 + m2)), 0.0), axis=0, keepdims=True)
            thr_s[tc, hd] = jnp.exp(tau - s1 - m2)
            e1_s[tc, hd] = jnp.exp(s1 - m1) * (GELU_OUT_SCALE / zsum)
            e2_s[tc, hd] = jnp.exp(s2 - m2)

        def score_blocks(g, _):
            for bi in range(SCORE_BLOCKS_PER_ITER):
                score_block(g * SCORE_BLOCKS_PER_ITER + bi)
            return 0

        lax.fori_loop(0, ntc * PEER_HEADS // SCORE_BLOCKS_PER_ITER, score_blocks, 0)
        acc_s[...] = jnp.zeros(acc_s.shape, F32)

    at = jnp.dot(u_ref[...], h_s[...], preferred_element_type=F32)
    gact = at * (1.0 + lax.erf(at))
    ngrp = ec // LANES
    for g8 in range(ngrp // SUBLANES):
        base = pl.multiple_of(e * ngrp + g8 * SUBLANES, SUBLANES)
        for tc in range(ntc):
            thr8 = [thr_s[tc, hd, pl.ds(base, SUBLANES), :] for hd in range(PEER_HEADS)]
            e18 = [e1_s[tc, hd, pl.ds(base, SUBLANES), :] for hd in range(PEER_HEADS)]
            for j in range(SUBLANES):
                w = None
                for hd in range(PEER_HEADS):
                    e2 = e2_s[tc, hd]
                    term = jnp.where(e2 >= thr8[hd][j:j + 1, :], e2, 0.0) * e18[hd][j:j + 1, :]
                    w = term if w is None else w + term
                r0 = (g8 * SUBLANES + j) * LANES
                wa_s[r0:r0 + LANES, tc * LANES:(tc + 1) * LANES] = (
                    w * gact[r0:r0 + LANES, tc * LANES:(tc + 1) * LANES]).astype(BF16)
    acc_s[...] += jnp.dot(vt_ref[...], wa_s[...], preferred_element_type=F32)

    @pl.when(e == ne - 1)
    def _finish():
        g2 = mod_ref[...][5:6]
        xn = x_ref[...] + g2 * acc_s[...].T
        if final_norm:
            xn = _rms(xn, fg_ref[...])
        o_ref[...] = xn


def _peer_call(x2, mod, ng, wqt, keys, u, vt, fg, S, final_norm):
    T, D = x2.shape
    E = u.shape[0]
    tm, ec = TM_PEER, EC_PEER
    nb = S // tm
    ntc = tm // LANES
    kern = functools.partial(_peer_kernel, tm=tm, ec=ec, final_norm=final_norm)
    return pl.pallas_call(
        kern,
        out_shape=jax.ShapeDtypeStruct((T, D), F32),
        grid=(T // tm, E // ec),
        in_specs=[pl.BlockSpec((tm, D), lambda i, e: (i, 0)),
                  pl.BlockSpec((None, 6, D), lambda i, e: (i // nb, 0, 0)),
                  _const_spec((1, D)), _const_spec(wqt.shape), _const_spec(keys.shape),
                  pl.BlockSpec((ec, D), lambda i, e: (e, 0)),
                  pl.BlockSpec((D, ec), lambda i, e: (0, e)),
                  _const_spec((1, D))],
        out_specs=pl.BlockSpec((tm, D), lambda i, e: (i, 0)),
        scratch_shapes=[pltpu.VMEM((D, tm), BF16),
                        pltpu.VMEM((ntc, 2 * PEER_HEADS, N_KEYS, LANES), F32),
                        pltpu.VMEM((ntc, PEER_HEADS, N_KEYS, LANES), F32),
                        pltpu.VMEM((ntc, PEER_HEADS, N_KEYS, LANES), F32),
                        pltpu.VMEM((ntc, PEER_HEADS, N_KEYS, LANES), F32),
                        pltpu.VMEM((ec, tm), BF16),
                        pltpu.VMEM((D, tm), F32)],
        compiler_params=_cparams(("arbitrary", "arbitrary")),
        name="peer_layer",
    )(x2, mod, ng, wqt, keys, u, vt, fg)


def _head_pad_cols(w_nope, w_rope, n_in):
    zeros = jnp.zeros((n_in, MLA_HEADS, HEAD_PAD - QK_NOPE - QK_ROPE), w_nope.dtype)
    return jnp.concatenate([w_nope, w_rope, zeros], axis=-1).reshape(n_in, MLA_HEADS * HEAD_PAD)


def _swap_halves(w):
    half = QK_ROPE // 2
    return jnp.concatenate([w[..., half:], w[..., :half]], axis=-1)


def kernel(x, c, positions, ada_w, ada_b, norm_mix_g, norm_ffn_g, lru_w_in, lru_conv_w, lru_conv_b, lru_wa, lru_ba, lru_wx, lru_bx, lru_lambda, lru_w_out, kv_ada_w, kv_ada_b, kv_norm_g, mla_w_dkv, mla_w_kr, mla_kv_latent_g, mla_w_uk, mla_w_uv, mla_w_dq, mla_q_latent_g, mla_w_uq, mla_w_o, peer_w_q, peer_sub_keys, peer_u, peer_v, final_g):
    B, S, D = x.shape
    T = B * S
    assert D == D_MODEL and S % TM_TOK == 0 and S % TS_LRU == 0 and S % TM_PEER == 0
    assert S % TQ_ATT == 0 and S % TK_ATT == 0

    c_pad = jnp.zeros((SUBLANES, D), F32).at[:B].set(c)
    mod_all = _mod_call(c_pad, ada_w, ada_b[:, None, :], 1536)
    mod_all = mod_all[:, :B].reshape(DEPTH, B, 6, D)
    mod_kv = _mod_call(c_pad, kv_ada_w[None], kv_ada_b[None, None, :], 1024)[0, :B].reshape(B, 2, D)

    half = QK_ROPE // 2
    freqs = ROPE_THETA ** (-jnp.arange(half, dtype=F32) / half)
    freq_lane = jnp.zeros((1, LANES), F32).at[0, QK_NOPE:QK_NOPE + QK_ROPE].set(jnp.concatenate([freqs, freqs]))
    pos_col = positions.reshape(T, 1)
    ca, sb = _rope_call(pos_col, freq_lane)

    nq, nk = S // TQ_ATT, S // KSUB_ATT
    pq = positions.reshape(B, nq, TQ_ATT)
    pk = positions.reshape(B, nk, KSUB_ATT)
    qmin, qmax = pq.min(-1), pq.max(-1)
    kmin, kmax = pk.min(-1), pk.max(-1)
    needed = kmin[:, None, :] <= qmax[:, :, None]
    nhi = jnp.max(jnp.where(needed, jnp.arange(1, nk + 1, dtype=jnp.int32), 0), axis=-1)
    full = kmax[:, None, :] <= qmin[:, :, None]
    nfull = jnp.min(jnp.where(full, nk, jnp.arange(nk, dtype=jnp.int32)), axis=-1)
    nfull = jnp.minimum(nfull, nhi).astype(jnp.int32).reshape(-1)
    nhi = nhi.astype(jnp.int32).reshape(-1)
    pos_row = positions.reshape(B, 1, S)

    def peer_layer(x2, l, final_norm):
        wqt = peer_w_q[l].T.astype(BF16)
        keys = peer_sub_keys[l].reshape(2 * PEER_HEADS, N_KEYS, PEER_HALF).astype(BF16)
        u = (peer_u[l] * (1.0 / math.sqrt(2.0))).astype(BF16)
        vt = peer_v[l].T.astype(BF16)
        return _peer_call(x2, mod_all[l], norm_ffn_g[l][None], wqt, keys, u, vt, final_g[None], S, final_norm)

    for l in range(N_A_LAYERS):
        x = _lru_call(x, mod_all[l], norm_mix_g[l][None], lru_w_in[l].astype(BF16), lru_conv_w[l],
                      lru_conv_b[l][None], lru_wa[l].astype(BF16), lru_ba[l][None], lru_wx[l].astype(BF16),
                      lru_bx[l][None], lru_lambda[l][None], lru_w_out[l].astype(BF16))
        x = peer_layer(x.reshape(T, D), l, False).reshape(B, S, D)

    x2 = x.reshape(T, D)

    zc = jnp.zeros((D, QK_NOPE), F32)
    zt = jnp.zeros((D, HEAD_PAD - QK_NOPE - QK_ROPE), F32)
    w1 = jnp.concatenate([mla_w_dkv, zc, mla_w_kr, zt, zc, _swap_halves(mla_w_kr), zt], axis=-1).astype(BF16)
    wuk = jnp.concatenate([mla_w_uk.reshape(KV_LORA, MLA_HEADS, QK_NOPE),
                           jnp.zeros((KV_LORA, MLA_HEADS, HEAD_PAD - QK_NOPE), F32)], axis=-1)
    wuk = wuk.reshape(KV_LORA, MLA_HEADS * HEAD_PAD).astype(BF16)
    k_all, v_all = _kv_call(x2, mod_kv, kv_norm_g[None], w1, mla_kv_latent_g[None], wuk,
                            mla_w_uv.astype(BF16), ca, sb, S)

    for j in range(DEPTH - N_A_LAYERS):
        l = N_A_LAYERS + j
        wq = mla_w_uq[j].reshape(Q_LORA, MLA_HEADS, QK_NOPE + QK_ROPE)
        w_nope, w_rope = wq[..., :QK_NOPE], wq[..., QK_NOPE:]
        wa = _head_pad_cols(w_nope, w_rope, Q_LORA).astype(BF16)
        wb = _head_pad_cols(jnp.zeros_like(w_nope), _swap_halves(w_rope), Q_LORA).astype(BF16)
        q = _q_call(x2, mod_all[l], norm_mix_g[l][None], mla_w_dq[j].astype(BF16), mla_q_latent_g[j][None],
                    wa, wb, ca, sb, S)
        o = _attn_call(nfull, nhi, q, k_all, v_all, pos_col, pos_row, B, S)
        x2 = _oproj_call(x2, o, mod_all[l], mla_w_o[j].astype(BF16), S)
        x2 = peer_layer(x2, l, l == DEPTH - 1)

    return x2.reshape(B, S, D)
```

```python
import functools
import math

import jax
import jax.numpy as jnp
from jax import lax
from jax.experimental import pallas as pl
from jax.experimental.pallas import tpu as pltpu

F32 = jnp.float32
BF16 = jnp.bfloat16

D_MODEL = 1024
DEPTH = 4
N_A_LAYERS = DEPTH // 2
LRU_HEADS = 4
LRU_BLOCK = D_MODEL // LRU_HEADS
CONV_WIDTH = 4
LRU_C = 8.0
MLA_HEADS = 16
QK_NOPE = 64
QK_ROPE = 32
V_HEAD = 64
Q_LORA = 384
KV_LORA = 256
ROPE_THETA = 10000.0
ATTN_SCALE = 1.0 / math.sqrt(QK_NOPE + QK_ROPE)
PEER_HEADS = 8
N_KEYS = 128
PEER_TOPK = 16
PEER_HALF = 128
RMS_EPS = 1e-6
NEG_INF = -1e30
LOG2E = math.log2(math.e)
GELU_OUT_SCALE = 0.5 * math.sqrt(2.0)

LANES = 128
SUBLANES = 8
HEAD_PAD = LANES
VMEM_LIMIT = 56 * 1024 * 1024

TS_LRU = 256
TM_TOK = 512
TQ_ATT = 512
TK_ATT = 2048
KSUB_ATT = 512
QSUB_ATT = 256
ONES_ROWS = 16
TM_PEER = 512
EC_PEER = 2048


def _cparams(sem):
    return pltpu.CompilerParams(dimension_semantics=sem, vmem_limit_bytes=VMEM_LIMIT)


def _rms(x, g):
    return x * lax.rsqrt(jnp.mean(x * x, axis=-1, keepdims=True) + RMS_EPS) * g


def _gelu(x):
    return 0.5 * x * (1.0 + lax.erf(x * (1.0 / math.sqrt(2.0))))


def _expm1(y):
    series = y * (1.0 + 0.5 * y * (1.0 + (1.0 / 3.0) * y * (1.0 + 0.25 * y * (1.0 + 0.2 * y))))
    return jnp.where(jnp.abs(y) < 0.05, series, jnp.exp(y) - 1.0)


def _const_spec(shape):
    nd = len(shape)
    return pl.BlockSpec(shape, lambda *_: (0,) * nd)


def _mod_kernel(c_ref, w_ref, b_ref, o_ref):
    c = c_ref[...]
    sc = c * jax.nn.sigmoid(c)
    o_ref[...] = jnp.dot(sc.astype(BF16), w_ref[...].astype(BF16),
                         preferred_element_type=F32) + b_ref[...]


def _mod_call(c_pad, w, b, tn):
    L, D, N = w.shape
    return pl.pallas_call(
        _mod_kernel,
        out_shape=jax.ShapeDtypeStruct((L, SUBLANES, N), F32),
        grid=(L, N // tn),
        in_specs=[pl.BlockSpec((SUBLANES, D), lambda l, n: (0, 0)),
                  pl.BlockSpec((None, D, tn), lambda l, n: (l, 0, n)),
                  pl.BlockSpec((None, 1, tn), lambda l, n: (l, 0, n))],
        out_specs=pl.BlockSpec((None, SUBLANES, tn), lambda l, n: (l, 0, n)),
        compiler_params=_cparams(("arbitrary", "arbitrary")),
        name="adaln_mod",
    )(c_pad, w, b)


def _rope_kernel(pos_ref, freq_ref, ca_ref, sb_ref):
    ang = pos_ref[...].astype(F32) * freq_ref[...]
    lane = lax.broadcasted_iota(jnp.int32, ang.shape, 1)
    cos, sin = jnp.cos(ang), jnp.sin(ang)
    half = QK_ROPE // 2
    ca_ref[...] = jnp.where(lane < QK_NOPE, 1.0, jnp.where(lane < QK_NOPE + QK_ROPE, cos, 0.0))
    sb_ref[...] = jnp.where(lane < QK_NOPE, 0.0,
                            jnp.where(lane < QK_NOPE + half, -sin,
                                      jnp.where(lane < QK_NOPE + QK_ROPE, sin, 0.0)))


def _rope_call(pos_col, freq_lane):
    T = pos_col.shape[0]
    tm = TM_TOK
    spec = pl.BlockSpec((tm, LANES), lambda i: (i, 0))
    return pl.pallas_call(
        _rope_kernel,
        out_shape=(jax.ShapeDtypeStruct((T, LANES), F32),) * 2,
        grid=(T // tm,),
        in_specs=[pl.BlockSpec((tm, 1), lambda i: (i, 0)), _const_spec((1, LANES))],
        out_specs=(spec, spec),
        compiler_params=_cparams(("arbitrary",)),
        name="rope_tables",
    )(pos_col, freq_lane)


def _shift_rows(x, d, fill):
    n = x.shape[0]
    if d % SUBLANES == 0:
        return jnp.concatenate([jnp.full((d, x.shape[1]), fill, x.dtype), x[:n - d]], axis=0)
    rolled = pltpu.roll(x, d, 0)
    row = lax.broadcasted_iota(jnp.int32, x.shape, 0)
    return jnp.where(row < d, fill, rolled)


def _lru_kernel(x_ref, mod_ref, ng_ref, win_ref, cw_ref, cb_ref, wa_ref, ba_ref, wx_ref, bx_ref,
                lam_ref, wout_ref, o_ref, rec_s, h_s, *, ts):
    D = D_MODEL

    @pl.when(pl.program_id(1) == 0)
    def _():
        rec_s[0:SUBLANES, :] = jnp.zeros((SUBLANES, D), F32)
        h_s[...] = jnp.zeros((SUBLANES, D), F32)

    x = x_ref[...]
    mod = mod_ref[...]
    sh1, sc1, g1 = mod[0:1], mod[1:2], mod[2:3]
    h = _rms(x, ng_ref[...]) * (1.0 + sc1) + sh1
    u = jnp.dot(h.astype(BF16), win_ref[...], preferred_element_type=F32)
    gate = _gelu(u[:, :D])
    rec = u[:, D:]

    rec_s[SUBLANES:ts + SUBLANES, :] = rec
    cw = cw_ref[...]
    xc = (cb_ref[...] + rec * cw[3:4]
          + rec_s[SUBLANES - 1:ts + SUBLANES - 1, :] * cw[2:3]
          + rec_s[SUBLANES - 2:ts + SUBLANES - 2, :] * cw[1:2]
          + rec_s[SUBLANES - 3:ts + SUBLANES - 3, :] * cw[0:1])
    rec_s[0:SUBLANES, :] = rec[ts - SUBLANES:ts, :]

    xcb = xc.astype(BF16)
    r_pre = jnp.concatenate(
        [jnp.dot(xcb[:, k * LRU_BLOCK:(k + 1) * LRU_BLOCK], wa_ref[k], preferred_element_type=F32)
         for k in range(LRU_HEADS)], axis=-1)
    i_pre = jnp.concatenate(
        [jnp.dot(xcb[:, k * LRU_BLOCK:(k + 1) * LRU_BLOCK], wx_ref[k], preferred_element_type=F32)
         for k in range(LRU_HEADS)], axis=-1)
    r = jax.nn.sigmoid(r_pre + ba_ref[...])
    gi = jax.nn.sigmoid(i_pre + bx_ref[...])
    z = -lam_ref[...]
    softplus = jnp.maximum(z, 0.0) + jnp.log1p(jnp.exp(-jnp.abs(z)))
    log_a = (-LRU_C) * r * softplus
    a = jnp.exp(log_a)
    b = jnp.sqrt(-_expm1(2.0 * log_a)) * (gi * xc)

    d = 1
    while d < ts:
        a_sh = _shift_rows(a, d, 1.0)
        b_sh = _shift_rows(b, d, 0.0)
        b = a * b_sh + b
        a = a * a_sh
        d *= 2
    h0 = h_s[SUBLANES - 1:SUBLANES, :]
    hs = a * h0 + b
    h_s[...] = hs[ts - SUBLANES:ts, :]

    y = jnp.dot((gate * hs).astype(BF16), wout_ref[...], preferred_element_type=F32)
    o_ref[...] = x + g1 * y


def _lru_call(x, mod, ng, w_in, cw, cb, wa, ba, wx, bx, lam, w_out):
    B, S, D = x.shape
    ts = TS_LRU
    kern = functools.partial(_lru_kernel, ts=ts)
    return pl.pallas_call(
        kern,
        out_shape=jax.ShapeDtypeStruct((B, S, D), F32),
        grid=(B, S // ts),
        in_specs=[pl.BlockSpec((None, ts, D), lambda b, s: (b, s, 0)),
                  pl.BlockSpec((None, 6, D), lambda b, s: (b, 0, 0)),
                  _const_spec((1, D)), _const_spec((D, 2 * D)), _const_spec((CONV_WIDTH, D)),
                  _const_spec((1, D)), _const_spec((LRU_HEADS, LRU_BLOCK, LRU_BLOCK)), _const_spec((1, D)),
                  _const_spec((LRU_HEADS, LRU_BLOCK, LRU_BLOCK)), _const_spec((1, D)), _const_spec((1, D)),
                  _const_spec((D, D))],
        out_specs=pl.BlockSpec((None, ts, D), lambda b, s: (b, s, 0)),
        scratch_shapes=[pltpu.VMEM((ts + SUBLANES, D), F32), pltpu.VMEM((SUBLANES, D), F32)],
        compiler_params=_cparams(("arbitrary", "arbitrary")),
        name="rglru_layer",
    )(x, mod, ng, w_in, cw, cb, wa, ba, wx, bx, lam, w_out)


def _kv_kernel(x_ref, mod_ref, ng_ref, w1_ref, lg_ref, wuk_ref, wuv_ref, ca_ref, sb_ref, k_ref, v_ref):
    x = x_ref[...]
    mod = mod_ref[...]
    shift, scale = mod[0:1], mod[1:2]
    h = _rms(x, ng_ref[...]) * (1.0 + scale) + shift
    t = jnp.dot(h.astype(BF16), w1_ref[...], preferred_element_type=F32)
    c_kv = _rms(t[:, :KV_LORA], lg_ref[...])
    kr = t[:, KV_LORA:KV_LORA + LANES] * ca_ref[...] + t[:, KV_LORA + LANES:] * sb_ref[...]
    ckb = c_kv.astype(BF16)
    kn = jnp.dot(ckb, wuk_ref[...], preferred_element_type=F32)
    k_ref[...] = (kn + jnp.concatenate([kr] * MLA_HEADS, axis=-1)).astype(BF16)
    v_ref[...] = jnp.dot(ckb, wuv_ref[...], preferred_element_type=F32).T.astype(BF16)


def _kv_call(x2, mod, ng, w1, lg, wuk, wuv, ca, sb, S):
    T, D = x2.shape
    tm = TM_TOK
    nb = S // tm
    HP = MLA_HEADS * HEAD_PAD
    HV = MLA_HEADS * V_HEAD
    return pl.pallas_call(
        _kv_kernel,
        out_shape=(jax.ShapeDtypeStruct((T, HP), BF16), jax.ShapeDtypeStruct((HV, T), BF16)),
        grid=(T // tm,),
        in_specs=[pl.BlockSpec((tm, D), lambda i: (i, 0)),
                  pl.BlockSpec((None, 2, D), lambda i: (i // nb, 0, 0)),
                  _const_spec((1, D)), _const_spec(w1.shape), _const_spec((1, KV_LORA)),
                  _const_spec(wuk.shape), _const_spec(wuv.shape),
                  pl.BlockSpec((tm, LANES), lambda i: (i, 0)), pl.BlockSpec((tm, LANES), lambda i: (i, 0))],
        out_specs=(pl.BlockSpec((tm, HP), lambda i: (i, 0)), pl.BlockSpec((HV, tm), lambda i: (0, i))),
        compiler_params=_cparams(("arbitrary",)),
        name="shared_kv",
    )(x2, mod, ng, w1, lg, wuk, wuv, ca, sb)


def _q_kernel(x_ref, mod_ref, ng_ref, wdq_ref, lg_ref, wa_ref, wb_ref, ca_ref, sb_ref, q_ref):
    x = x_ref[...]
    mod = mod_ref[...]
    sh1, sc1 = mod[0:1], mod[1:2]
    h = _rms(x, ng_ref[...]) * (1.0 + sc1) + sh1
    ql = _rms(jnp.dot(h.astype(BF16), wdq_ref[...], preferred_element_type=F32), lg_ref[...])
    qlb = ql.astype(BF16)
    qa = jnp.dot(qlb, wa_ref[...], preferred_element_type=F32)
    qb = jnp.dot(qlb, wb_ref[...], preferred_element_type=F32)
    ca = jnp.concatenate([ca_ref[...]] * MLA_HEADS, axis=-1)
    sb = jnp.concatenate([sb_ref[...]] * MLA_HEADS, axis=-1)
    q_ref[...] = ((qa * ca + qb * sb) * (ATTN_SCALE * LOG2E)).T.astype(BF16)


def _q_call(x2, mod, ng, wdq, lg, wa, wb, ca, sb, S):
    T, D = x2.shape
    tm = TM_TOK
    nb = S // tm
    HP = MLA_HEADS * HEAD_PAD
    return pl.pallas_call(
        _q_kernel,
        out_shape=jax.ShapeDtypeStruct((HP, T), BF16),
        grid=(T // tm,),
        in_specs=[pl.BlockSpec((tm, D), lambda i: (i, 0)),
                  pl.BlockSpec((None, 6, D), lambda i: (i // nb, 0, 0)),
                  _const_spec((1, D)), _const_spec(wdq.shape), _const_spec((1, Q_LORA)),
                  _const_spec(wa.shape), _const_spec(wb.shape),
                  pl.BlockSpec((tm, LANES), lambda i: (i, 0)), pl.BlockSpec((tm, LANES), lambda i: (i, 0))],
        out_specs=pl.BlockSpec((HP, tm), lambda i: (0, i)),
        compiler_params=_cparams(("arbitrary",)),
        name="mla_queries",
    )(x2, mod, ng, wdq, lg, wa, wb, ca, sb)


def _attn_kernel(nfull_ref, nhi_ref, qt_ref, k_ref, vt_ref, pq_ref, pk_ref, o_ref, *, tq, tk, nq):
    b = pl.program_id(0)
    qi = pl.program_id(2)
    nfull = nfull_ref[b * nq + qi]
    nhi = nhi_ref[b * nq + qi]
    pos_q = pq_ref[...]

    def step(kc, carry, masked, tk):
        nstate = 2 * (tq // QSUB_ATT)
        nstream = nstate * (tk // KSUB_ATT)
        state = list(carry)

        def split(sidx):
            kb, sid = divmod(sidx, nstate)
            hh, qs = divmod(sid, tq // QSUB_ATT)
            return pl.multiple_of(kc * tk + kb * KSUB_ATT, KSUB_ATT), sid, hh, qs

        def scores(sidx):
            off, _, hh, qs = split(sidx)
            k = k_ref[pl.ds(off, KSUB_ATT), hh * HEAD_PAD:(hh + 1) * HEAD_PAD]
            qt = qt_ref[hh * HEAD_PAD:(hh + 1) * HEAD_PAD, qs * QSUB_ATT:(qs + 1) * QSUB_ATT]
            return jnp.dot(k, qt, preferred_element_type=F32)

        def update(sidx, st):
            off, sid, hh, qs = split(sidx)
            m, acc = state[sid]
            if masked:
                keep = pk_ref[pl.ds(off, KSUB_ATT), :] <= pos_q[:, qs * QSUB_ATT:(qs + 1) * QSUB_ATT]
                st = jnp.where(keep, st, NEG_INF)
            m_new = jnp.maximum(m, jnp.max(st, axis=0, keepdims=True))
            p = jnp.exp2((st - m_new).astype(BF16))
            alpha = jnp.exp2(m - m_new)
            vt = jnp.concatenate([vt_ref[hh * V_HEAD:(hh + 1) * V_HEAD, pl.ds(off, KSUB_ATT)],
                                  jnp.ones((ONES_ROWS, KSUB_ATT), BF16)], axis=0)
            state[sid] = (m_new, alpha * acc + jnp.dot(vt, p, preferred_element_type=F32))

        st_next = scores(0)
        for sidx in range(nstream):
            st_cur = st_next
            if sidx + 1 < nstream:
                st_next = scores(sidx + 1)
            update(sidx, st_cur)
        return tuple(state)

    nqs = tq // QSUB_ATT
    one = (jnp.full((1, QSUB_ATT), NEG_INF, F32), jnp.zeros((V_HEAD + ONES_ROWS, QSUB_ATT), F32))
    per = tk // KSUB_ATT
    nwide = nfull // per
    carry = lax.fori_loop(0, nwide, functools.partial(step, masked=False, tk=tk), (one,) * (2 * nqs))
    carry = lax.fori_loop(nwide * per, nfull, functools.partial(step, masked=False, tk=KSUB_ATT), carry)
    fin = lax.fori_loop(nfull, nhi, functools.partial(step, masked=True, tk=KSUB_ATT), carry)
    heads = [jnp.concatenate([fin[hh * nqs + qs][1][:V_HEAD] / fin[hh * nqs + qs][1][V_HEAD:V_HEAD + 1]
                              for qs in range(nqs)], axis=1) for hh in range(2)]
    o_ref[...] = jnp.concatenate(heads, axis=0).T.astype(BF16)


def _attn_call(nfull, nhi, qt, k, vt, pos_col, pos_row, B, S):
    tq, tk = TQ_ATT, TK_ATT
    nq = S // tq
    T = B * S
    kern = functools.partial(_attn_kernel, tq=tq, tk=tk, nq=nq)
    grid_spec = pltpu.PrefetchScalarGridSpec(
        num_scalar_prefetch=2,
        grid=(B, MLA_HEADS // 2, nq),
        in_specs=[pl.BlockSpec((2 * HEAD_PAD, tq), lambda b, h, i, *_: (h, b * nq + i)),
                  pl.BlockSpec((S, 2 * HEAD_PAD), lambda b, h, i, *_: (b, h)),
                  pl.BlockSpec((2 * V_HEAD, S), lambda b, h, i, *_: (h, b)),
                  pl.BlockSpec((None, 1, tq), lambda b, h, i, *_: (b, 0, i)),
                  pl.BlockSpec((S, 1), lambda b, h, i, *_: (b, 0))],
        out_specs=pl.BlockSpec((tq, 2 * V_HEAD), lambda b, h, i, *_: (b * nq + i, h)),
    )
    return pl.pallas_call(
        kern,
        out_shape=jax.ShapeDtypeStruct((T, MLA_HEADS * V_HEAD), BF16),
        grid_spec=grid_spec,
        compiler_params=_cparams(("arbitrary", "arbitrary", "arbitrary")),
        name="mla_attention",
    )(nfull, nhi, qt, k, vt, pos_row, pos_col)


def _oproj_kernel(x_ref, o_ref, mod_ref, w_ref, out_ref):
    g1 = mod_ref[...][2:3]
    y = jnp.dot(o_ref[...], w_ref[...], preferred_element_type=F32)
    out_ref[...] = x_ref[...] + g1 * y


def _oproj_call(x2, o, mod, w_o, S):
    T, D = x2.shape
    tm = TM_TOK
    nb = S // tm
    return pl.pallas_call(
        _oproj_kernel,
        out_shape=jax.ShapeDtypeStruct((T, D), F32),
        grid=(T // tm,),
        in_specs=[pl.BlockSpec((tm, D), lambda i: (i, 0)),
                  pl.BlockSpec((tm, o.shape[1]), lambda i: (i, 0)),
                  pl.BlockSpec((None, 6, D), lambda i: (i // nb, 0, 0)),
                  _const_spec(w_o.shape)],
        out_specs=pl.BlockSpec((tm, D), lambda i: (i, 0)),
        compiler_params=_cparams(("arbitrary",)),
        name="mla_out_proj",
    )(x2, o, mod, w_o)


TOP_ROWS = 3 * SUBLANES
SCORE_BLOCKS_PER_ITER = 4


def _merge_sort_network(n):
    pairs, p = [], 1
    while p < n:
        k = p
        while k >= 1:
            for j in range(k % p, n - k, 2 * k):
                for i in range(min(k, n - j - k)):
                    if (i + j) // (2 * p) == (i + j + k) // (2 * p):
                        pairs.append((i + j, i + j + k))
            k //= 2
        p *= 2
    return pairs


def _largest_rows(x, count):
    v = [x[r * SUBLANES:(r + 1) * SUBLANES, :] for r in range(x.shape[0] // SUBLANES)]
    size = 1 << (len(v) - 1).bit_length()
    v = v + [None] * (size - len(v))
    for i, j in _merge_sort_network(size):
        if v[j] is None:
            continue
        if v[i] is None:
            v[i], v[j] = v[j], None
        else:
            v[i], v[j] = jnp.maximum(v[i], v[j]), jnp.minimum(v[i], v[j])
    v = [t for t in v if t is not None]
    tops = []
    for j in range(count):
        m = jnp.max(v[0], axis=0, keepdims=True)
        tops.append(m)
        if j + 1 < count:
            won = v[0] == m
            depth = min(len(v), count - j)
            v = [jnp.where(won, v[r + 1] if r + 1 < len(v) else NEG_INF, v[r]) for r in range(depth)]
    return tops


def _top17_rows(x):
    pad = jnp.full((TOP_ROWS - PEER_TOPK - 1, x.shape[1]), NEG_INF, F32)
    return jnp.concatenate(_largest_rows(x, PEER_TOPK + 1) + [pad], axis=0)


def _peer_kernel(x_ref, mod_ref, ng_ref, wqt_ref, keys_ref, u_ref, vt_ref, fg_ref, o_ref,
                 h_s, st_s, thr_s, e1_s, e2_s, wa_s, acc_s, *, tm, ec, final_norm):
    ntc = tm // LANES
    e = pl.program_id(1)
    ne = pl.num_programs(1)

    @pl.when(e == 0)
    def _scores():
        x = x_ref[...]
        mod = mod_ref[...]
        sh2, sc2 = mod[3:4], mod[4:5]
        ht = (_rms(x, ng_ref[...]) * (1.0 + sc2) + sh2).T.astype(BF16)
        h_s[...] = ht
        qtb = jnp.dot(wqt_ref[...], ht, preferred_element_type=F32).astype(BF16)
        for hp in range(2 * PEER_HEADS):
            st = jnp.dot(keys_ref[hp], qtb[hp * PEER_HALF:(hp + 1) * PEER_HALF, :],
                         preferred_element_type=F32)
            for tc in range(ntc):
                st_s[tc, hp] = st[:, tc * LANES:(tc + 1) * LANES]

        def score_block(i):
            tc = i // PEER_HEADS
            hd = i % PEER_HEADS
            s1 = st_s[tc, 2 * hd]
            s2 = st_s[tc, 2 * hd + 1]
            a = _top17_rows(s1)
            b = _top17_rows(s2)
            cands = [a[0:1] + b]
            for r in range(1, SUBLANES):
                cands.append(a[r:r + 1] + b[0:SUBLANES])
            cands.append(a[SUBLANES:TOP_ROWS] + b[0:1])
            cand = jnp.concatenate(cands, axis=0)
            c16, c17 = _largest_rows(cand, PEER_TOPK + 1)[PEER_TOPK - 1:]
            tau = 0.5 * (c16 + c17)
            m1, m2 = a[0:1], b[0:1]
            zsum = jnp.sum(jnp.where(cand >= tau, jnp.exp(cand - (m1 + m2)), 0.0), axis=0, keepdims=True)
            thr_s[tc, hd] = jnp.exp(tau - s1 - m2)
            e1_s[tc, hd] = jnp.exp(s1 - m1) * (GELU_OUT_SCALE / zsum)
            e2_s[tc, hd] = jnp.exp(s2 - m2)

        def score_blocks(g, _):
            for bi in range(SCORE_BLOCKS_PER_ITER):
                score_block(g * SCORE_BLOCKS_PER_ITER + bi)
            return 0

        lax.fori_loop(0, ntc * PEER_HEADS // SCORE_BLOCKS_PER_ITER, score_blocks, 0)
        acc_s[...] = jnp.zeros(acc_s.shape, F32)

    at = jnp.dot(u_ref[...], h_s[...], preferred_element_type=F32)
    gact = at * (1.0 + lax.erf(at))
    ngrp = ec // LANES
    for g8 in range(ngrp // SUBLANES):
        base = pl.multiple_of(e * ngrp + g8 * SUBLANES, SUBLANES)
        for tc in range(ntc):
            thr8 = [thr_s[tc, hd, pl.ds(base, SUBLANES), :] for hd in range(PEER_HEADS)]
            e18 = [e1_s[tc, hd, pl.ds(base, SUBLANES), :] for hd in range(PEER_HEADS)]
            for j in range(SUBLANES):
                w = None
                for hd in range(PEER_HEADS):
                    e2 = e2_s[tc, hd]
                    term = jnp.where(e2 >= thr8[hd][j:j + 1, :], e2, 0.0) * e18[hd][j:j + 1, :]
                    w = term if w is None else w + term
                r0 = (g8 * SUBLANES + j) * LANES
                wa_s[r0:r0 + LANES, tc * LANES:(tc + 1) * LANES] = (
                    w * gact[r0:r0 + LANES, tc * LANES:(tc + 1) * LANES]).astype(BF16)
    acc_s[...] += jnp.dot(vt_ref[...], wa_s[...], preferred_element_type=F32)

    @pl.when(e == ne - 1)
    def _finish():
        g2 = mod_ref[...][5:6]
        xn = x_ref[...] + g2 * acc_s[...].T
        if final_norm:
            xn = _rms(xn, fg_ref[...])
        o_ref[...] = xn


def _peer_call(x2, mod, ng, wqt, keys, u, vt, fg, S, final_norm):
    T, D = x2.shape
    E = u.shape[0]
    tm, ec = TM_PEER, EC_PEER
    nb = S // tm
    ntc = tm // LANES
    kern = functools.partial(_peer_kernel, tm=tm, ec=ec, final_norm=final_norm)
    return pl.pallas_call(
        kern,
        out_shape=jax.ShapeDtypeStruct((T, D), F32),
        grid=(T // tm, E // ec),
        in_specs=[pl.BlockSpec((tm, D), lambda i, e: (i, 0)),
                  pl.BlockSpec((None, 6, D), lambda i, e: (i // nb, 0, 0)),
                  _const_spec((1, D)), _const_spec(wqt.shape), _const_spec(keys.shape),
                  pl.BlockSpec((ec, D), lambda i, e: (e, 0)),
                  pl.BlockSpec((D, ec), lambda i, e: (0, e)),
                  _const_spec((1, D))],
        out_specs=pl.BlockSpec((tm, D), lambda i, e: (i, 0)),
        scratch_shapes=[pltpu.VMEM((D, tm), BF16),
                        pltpu.VMEM((ntc, 2 * PEER_HEADS, N_KEYS, LANES), F32),
                        pltpu.VMEM((ntc, PEER_HEADS, N_KEYS, LANES), F32),
                        pltpu.VMEM((ntc, PEER_HEADS, N_KEYS, LANES), F32),
                        pltpu.VMEM((ntc, PEER_HEADS, N_KEYS, LANES), F32),
                        pltpu.VMEM((ec, tm), BF16),
                        pltpu.VMEM((D, tm), F32)],
        compiler_params=_cparams(("arbitrary", "arbitrary")),
        name="peer_layer",
    )(x2, mod, ng, wqt, keys, u, vt, fg)


def _head_pad_cols(w_nope, w_rope, n_in):
    zeros = jnp.zeros((n_in, MLA_HEADS, HEAD_PAD - QK_NOPE - QK_ROPE), w_nope.dtype)
    return jnp.concatenate([w_nope, w_rope, zeros], axis=-1).reshape(n_in, MLA_HEADS * HEAD_PAD)


def _swap_halves(w):
    half = QK_ROPE // 2
    return jnp.concatenate([w[..., half:], w[..., :half]], axis=-1)


def kernel(x, c, positions, ada_w, ada_b, norm_mix_g, norm_ffn_g, lru_w_in, lru_conv_w, lru_conv_b, lru_wa, lru_ba, lru_wx, lru_bx, lru_lambda, lru_w_out, kv_ada_w, kv_ada_b, kv_norm_g, mla_w_dkv, mla_w_kr, mla_kv_latent_g, mla_w_uk, mla_w_uv, mla_w_dq, mla_q_latent_g, mla_w_uq, mla_w_o, peer_w_q, peer_sub_keys, peer_u, peer_v, final_g):
    B, S, D = x.shape
    T = B * S
    assert D == D_MODEL and S % TM_TOK == 0 and S % TS_LRU == 0 and S % TM_PEER == 0
    assert S % TQ_ATT == 0 and S % TK_ATT == 0

    c_pad = jnp.zeros((SUBLANES, D), F32).at[:B].set(c)
    mod_all = _mod_call(c_pad, ada_w, ada_b[:, None, :], 1536)
    mod_all = mod_all[:, :B].reshape(DEPTH, B, 6, D)
    mod_kv = _mod_call(c_pad, kv_ada_w[None], kv_ada_b[None, None, :], 1024)[0, :B].reshape(B, 2, D)

    half = QK_ROPE // 2
    freqs = ROPE_THETA ** (-jnp.arange(half, dtype=F32) / half)
    freq_lane = jnp.zeros((1, LANES), F32).at[0, QK_NOPE:QK_NOPE + QK_ROPE].set(jnp.concatenate([freqs, freqs]))
    pos_col = positions.reshape(T, 1)
    ca, sb = _rope_call(pos_col, freq_lane)

    nq, nk = S // TQ_ATT, S // KSUB_ATT
    pq = positions.reshape(B, nq, TQ_ATT)
    pk = positions.reshape(B, nk, KSUB_ATT)
    qmin, qmax = pq.min(-1), pq.max(-1)
    kmin, kmax = pk.min(-1), pk.max(-1)
    needed = kmin[:, None, :] <= qmax[:, :, None]
    nhi = jnp.max(jnp.where(needed, jnp.arange(1, nk + 1, dtype=jnp.int32), 0), axis=-1)
    full = kmax[:, None, :] <= qmin[:, :, None]
    nfull = jnp.min(jnp.where(full, nk, jnp.arange(nk, dtype=jnp.int32)), axis=-1)
    nfull = jnp.minimum(nfull, nhi).astype(jnp.int32).reshape(-1)
    nhi = nhi.astype(jnp.int32).reshape(-1)
    pos_row = positions.reshape(B, 1, S)

    def peer_layer(x2, l, final_norm):
        wqt = peer_w_q[l].T.astype(BF16)
        keys = peer_sub_keys[l].reshape(2 * PEER_HEADS, N_KEYS, PEER_HALF).astype(BF16)
        u = (peer_u[l] * (1.0 / math.sqrt(2.0))).astype(BF16)
        vt = peer_v[l].T.astype(BF16)
        return _peer_call(x2, mod_all[l], norm_ffn_g[l][None], wqt, keys, u, vt, final_g[None], S, final_norm)

    for l in range(N_A_LAYERS):
        x = _lru_call(x, mod_all[l], norm_mix_g[l][None], lru_w_in[l].astype(BF16), lru_conv_w[l],
                      lru_conv_b[l][None], lru_wa[l].astype(BF16), lru_ba[l][None], lru_wx[l].astype(BF16),
                      lru_bx[l][None], lru_lambda[l][None], lru_w_out[l].astype(BF16))
        x = peer_layer(x.reshape(T, D), l, False).reshape(B, S, D)

    x2 = x.reshape(T, D)

    zc = jnp.zeros((D, QK_NOPE), F32)
    zt = jnp.zeros((D, HEAD_PAD - QK_NOPE - QK_ROPE), F32)
    w1 = jnp.concatenate([mla_w_dkv, zc, mla_w_kr, zt, zc, _swap_halves(mla_w_kr), zt], axis=-1).astype(BF16)
    wuk = jnp.concatenate([mla_w_uk.reshape(KV_LORA, MLA_HEADS, QK_NOPE),
                           jnp.zeros((KV_LORA, MLA_HEADS, HEAD_PAD - QK_NOPE), F32)], axis=-1)
    wuk = wuk.reshape(KV_LORA, MLA_HEADS * HEAD_PAD).astype(BF16)
    k_all, v_all = _kv_call(x2, mod_kv, kv_norm_g[None], w1, mla_kv_latent_g[None], wuk,
                            mla_w_uv.astype(BF16), ca, sb, S)

    for j in range(DEPTH - N_A_LAYERS):
        l = N_A_LAYERS + j
        wq = mla_w_uq[j].reshape(Q_LORA, MLA_HEADS, QK_NOPE + QK_ROPE)
        w_nope, w_rope = wq[..., :QK_NOPE], wq[..., QK_NOPE:]
        wa = _head_pad_cols(w_nope, w_rope, Q_LORA).astype(BF16)
        wb = _head_pad_cols(jnp.zeros_like(w_nope), _swap_halves(w_rope), Q_LORA).astype(BF16)
        q = _q_call(x2, mod_all[l], norm_mix_g[l][None], mla_w_dq[j].astype(BF16), mla_q_latent_g[j][None],
                    wa, wb, ca, sb, S)
        o = _attn_call(nfull, nhi, q, k_all, v_all, pos_col, pos_row, B, S)
        x2 = _oproj_call(x2, o, mod_all[l], mla_w_o[j].astype(BF16), S)
        x2 = peer_layer(x2, l, l == DEPTH - 1)

    return x2.reshape(B, S, D)
```

```python
import functools
import math

import jax
import jax.numpy as jnp
from jax import lax
from jax.experimental import pallas as pl
from jax.experimental.pallas import tpu as pltpu

F32 = jnp.float32
BF16 = jnp.bfloat16

D_MODEL = 1024
DEPTH = 4
N_A_LAYERS = DEPTH // 2
LRU_HEADS = 4
LRU_BLOCK = D_MODEL // LRU_HEADS
CONV_WIDTH = 4
LRU_C = 8.0
MLA_HEADS = 16
QK_NOPE = 64
QK_ROPE = 32
V_HEAD = 64
Q_LORA = 384
KV_LORA = 256
ROPE_THETA = 10000.0
ATTN_SCALE = 1.0 / math.sqrt(QK_NOPE + QK_ROPE)
PEER_HEADS = 8
N_KEYS = 128
PEER_TOPK = 16
PEER_HALF = 128
RMS_EPS = 1e-6
NEG_INF = -1e30
LOG2E = math.log2(math.e)
GELU_OUT_SCALE = 0.5 * math.sqrt(2.0)

LANES = 128
SUBLANES = 8
HEAD_PAD = LANES
VMEM_LIMIT = 56 * 1024 * 1024

TS_LRU = 256
TM_TOK = 512
TQ_ATT = 512
TK_ATT = 2048
KSUB_ATT = 512
QSUB_ATT = 256
SCORE_LOOKAHEAD = 4
ONES_ROWS = 16
TM_PEER = 512
EC_PEER = 2048


def _cparams(sem):
    return pltpu.CompilerParams(dimension_semantics=sem, vmem_limit_bytes=VMEM_LIMIT)


def _rms(x, g):
    return x * lax.rsqrt(jnp.mean(x * x, axis=-1, keepdims=True) + RMS_EPS) * g


def _gelu(x):
    return 0.5 * x * (1.0 + lax.erf(x * (1.0 / math.sqrt(2.0))))


def _expm1(y):
    series = y * (1.0 + 0.5 * y * (1.0 + (1.0 / 3.0) * y * (1.0 + 0.25 * y * (1.0 + 0.2 * y))))
    return jnp.where(jnp.abs(y) < 0.05, series, jnp.exp(y) - 1.0)


def _const_spec(shape):
    nd = len(shape)
    return pl.BlockSpec(shape, lambda *_: (0,) * nd)


def _mod_kernel(c_ref, w_ref, b_ref, o_ref):
    c = c_ref[...]
    sc = c * jax.nn.sigmoid(c)
    o_ref[...] = jnp.dot(sc.astype(BF16), w_ref[...].astype(BF16),
                         preferred_element_type=F32) + b_ref[...]


def _mod_call(c_pad, w, b, tn):
    L, D, N = w.shape
    return pl.pallas_call(
        _mod_kernel,
        out_shape=jax.ShapeDtypeStruct((L, SUBLANES, N), F32),
        grid=(L, N // tn),
        in_specs=[pl.BlockSpec((SUBLANES, D), lambda l, n: (0, 0)),
                  pl.BlockSpec((None, D, tn), lambda l, n: (l, 0, n)),
                  pl.BlockSpec((None, 1, tn), lambda l, n: (l, 0, n))],
        out_specs=pl.BlockSpec((None, SUBLANES, tn), lambda l, n: (l, 0, n)),
        compiler_params=_cparams(("arbitrary", "arbitrary")),
        name="adaln_mod",
    )(c_pad, w, b)


def _rope_kernel(pos_ref, freq_ref, ca_ref, sb_ref):
    ang = pos_ref[...].astype(F32) * freq_ref[...]
    lane = lax.broadcasted_iota(jnp.int32, ang.shape, 1)
    cos, sin = jnp.cos(ang), jnp.sin(ang)
    half = QK_ROPE // 2
    ca_ref[...] = jnp.where(lane < QK_NOPE, 1.0, jnp.where(lane < QK_NOPE + QK_ROPE, cos, 0.0))
    sb_ref[...] = jnp.where(lane < QK_NOPE, 0.0,
                            jnp.where(lane < QK_NOPE + half, -sin,
                                      jnp.where(lane < QK_NOPE + QK_ROPE, sin, 0.0)))


def _rope_call(pos_col, freq_lane):
    T = pos_col.shape[0]
    tm = TM_TOK
    spec = pl.BlockSpec((tm, LANES), lambda i: (i, 0))
    return pl.pallas_call(
        _rope_kernel,
        out_shape=(jax.ShapeDtypeStruct((T, LANES), F32),) * 2,
        grid=(T // tm,),
        in_specs=[pl.BlockSpec((tm, 1), lambda i: (i, 0)), _const_spec((1, LANES))],
        out_specs=(spec, spec),
        compiler_params=_cparams(("arbitrary",)),
        name="rope_tables",
    )(pos_col, freq_lane)


def _shift_rows(x, d, fill):
    n = x.shape[0]
    if d % SUBLANES == 0:
        return jnp.concatenate([jnp.full((d, x.shape[1]), fill, x.dtype), x[:n - d]], axis=0)
    rolled = pltpu.roll(x, d, 0)
    row = lax.broadcasted_iota(jnp.int32, x.shape, 0)
    return jnp.where(row < d, fill, rolled)


def _lru_kernel(x_ref, mod_ref, ng_ref, win_ref, cw_ref, cb_ref, wa_ref, ba_ref, wx_ref, bx_ref,
                lam_ref, wout_ref, o_ref, rec_s, h_s, *, ts):
    D = D_MODEL

    @pl.when(pl.program_id(1) == 0)
    def _():
        rec_s[0:SUBLANES, :] = jnp.zeros((SUBLANES, D), F32)
        h_s[...] = jnp.zeros((SUBLANES, D), F32)

    x = x_ref[...]
    mod = mod_ref[...]
    sh1, sc1, g1 = mod[0:1], mod[1:2], mod[2:3]
    h = _rms(x, ng_ref[...]) * (1.0 + sc1) + sh1
    u = jnp.dot(h.astype(BF16), win_ref[...], preferred_element_type=F32)
    gate = _gelu(u[:, :D])
    rec = u[:, D:]

    rec_s[SUBLANES:ts + SUBLANES, :] = rec
    cw = cw_ref[...]
    xc = (cb_ref[...] + rec * cw[3:4]
          + rec_s[SUBLANES - 1:ts + SUBLANES - 1, :] * cw[2:3]
          + rec_s[SUBLANES - 2:ts + SUBLANES - 2, :] * cw[1:2]
          + rec_s[SUBLANES - 3:ts + SUBLANES - 3, :] * cw[0:1])
    rec_s[0:SUBLANES, :] = rec[ts - SUBLANES:ts, :]

    xcb = xc.astype(BF16)
    r_pre = jnp.concatenate(
        [jnp.dot(xcb[:, k * LRU_BLOCK:(k + 1) * LRU_BLOCK], wa_ref[k], preferred_element_type=F32)
         for k in range(LRU_HEADS)], axis=-1)
    i_pre = jnp.concatenate(
        [jnp.dot(xcb[:, k * LRU_BLOCK:(k + 1) * LRU_BLOCK], wx_ref[k], preferred_element_type=F32)
         for k in range(LRU_HEADS)], axis=-1)
    r = jax.nn.sigmoid(r_pre + ba_ref[...])
    gi = jax.nn.sigmoid(i_pre + bx_ref[...])
    z = -lam_ref[...]
    softplus = jnp.maximum(z, 0.0) + jnp.log1p(jnp.exp(-jnp.abs(z)))
    log_a = (-LRU_C) * r * softplus
    a = jnp.exp(log_a)
    b = jnp.sqrt(-_expm1(2.0 * log_a)) * (gi * xc)

    d = 1
    while d < ts:
        a_sh = _shift_rows(a, d, 1.0)
        b_sh = _shift_rows(b, d, 0.0)
        b = a * b_sh + b
        a = a * a_sh
        d *= 2
    h0 = h_s[SUBLANES - 1:SUBLANES, :]
    hs = a * h0 + b
    h_s[...] = hs[ts - SUBLANES:ts, :]

    y = jnp.dot((gate * hs).astype(BF16), wout_ref[...], preferred_element_type=F32)
    o_ref[...] = x + g1 * y


def _lru_call(x, mod, ng, w_in, cw, cb, wa, ba, wx, bx, lam, w_out):
    B, S, D = x.shape
    ts = TS_LRU
    kern = functools.partial(_lru_kernel, ts=ts)
    return pl.pallas_call(
        kern,
        out_shape=jax.ShapeDtypeStruct((B, S, D), F32),
        grid=(B, S // ts),
        in_specs=[pl.BlockSpec((None, ts, D), lambda b, s: (b, s, 0)),
                  pl.BlockSpec((None, 6, D), lambda b, s: (b, 0, 0)),
                  _const_spec((1, D)), _const_spec((D, 2 * D)), _const_spec((CONV_WIDTH, D)),
                  _const_spec((1, D)), _const_spec((LRU_HEADS, LRU_BLOCK, LRU_BLOCK)), _const_spec((1, D)),
                  _const_spec((LRU_HEADS, LRU_BLOCK, LRU_BLOCK)), _const_spec((1, D)), _const_spec((1, D)),
                  _const_spec((D, D))],
        out_specs=pl.BlockSpec((None, ts, D), lambda b, s: (b, s, 0)),
        scratch_shapes=[pltpu.VMEM((ts + SUBLANES, D), F32), pltpu.VMEM((SUBLANES, D), F32)],
        compiler_params=_cparams(("arbitrary", "arbitrary")),
        name="rglru_layer",
    )(x, mod, ng, w_in, cw, cb, wa, ba, wx, bx, lam, w_out)


def _kv_kernel(x_ref, mod_ref, ng_ref, w1_ref, lg_ref, wuk_ref, wuv_ref, ca_ref, sb_ref, k_ref, v_ref):
    x = x_ref[...]
    mod = mod_ref[...]
    shift, scale = mod[0:1], mod[1:2]
    h = _rms(x, ng_ref[...]) * (1.0 + scale) + shift
    t = jnp.dot(h.astype(BF16), w1_ref[...], preferred_element_type=F32)
    c_kv = _rms(t[:, :KV_LORA], lg_ref[...])
    kr = t[:, KV_LORA:KV_LORA + LANES] * ca_ref[...] + t[:, KV_LORA + LANES:] * sb_ref[...]
    ckb = c_kv.astype(BF16)
    kn = jnp.dot(ckb, wuk_ref[...], preferred_element_type=F32)
    k_ref[...] = (kn + jnp.concatenate([kr] * MLA_HEADS, axis=-1)).astype(BF16)
    v_ref[...] = jnp.dot(ckb, wuv_ref[...], preferred_element_type=F32).T.astype(BF16)


def _kv_call(x2, mod, ng, w1, lg, wuk, wuv, ca, sb, S):
    T, D = x2.shape
    tm = TM_TOK
    nb = S // tm
    HP = MLA_HEADS * HEAD_PAD
    HV = MLA_HEADS * V_HEAD
    return pl.pallas_call(
        _kv_kernel,
        out_shape=(jax.ShapeDtypeStruct((T, HP), BF16), jax.ShapeDtypeStruct((HV, T), BF16)),
        grid=(T // tm,),
        in_specs=[pl.BlockSpec((tm, D), lambda i: (i, 0)),
                  pl.BlockSpec((None, 2, D), lambda i: (i // nb, 0, 0)),
                  _const_spec((1, D)), _const_spec(w1.shape), _const_spec((1, KV_LORA)),
                  _const_spec(wuk.shape), _const_spec(wuv.shape),
                  pl.BlockSpec((tm, LANES), lambda i: (i, 0)), pl.BlockSpec((tm, LANES), lambda i: (i, 0))],
        out_specs=(pl.BlockSpec((tm, HP), lambda i: (i, 0)), pl.BlockSpec((HV, tm), lambda i: (0, i))),
        compiler_params=_cparams(("arbitrary",)),
        name="shared_kv",
    )(x2, mod, ng, w1, lg, wuk, wuv, ca, sb)


def _q_kernel(x_ref, mod_ref, ng_ref, wdq_ref, lg_ref, wa_ref, wb_ref, ca_ref, sb_ref, q_ref):
    x = x_ref[...]
    mod = mod_ref[...]
    sh1, sc1 = mod[0:1], mod[1:2]
    h = _rms(x, ng_ref[...]) * (1.0 + sc1) + sh1
    ql = _rms(jnp.dot(h.astype(BF16), wdq_ref[...], preferred_element_type=F32), lg_ref[...])
    qlb = ql.astype(BF16)
    qa = jnp.dot(qlb, wa_ref[...], preferred_element_type=F32)
    qb = jnp.dot(qlb, wb_ref[...], preferred_element_type=F32)
    ca = jnp.concatenate([ca_ref[...]] * MLA_HEADS, axis=-1)
    sb = jnp.concatenate([sb_ref[...]] * MLA_HEADS, axis=-1)
    q_ref[...] = ((qa * ca + qb * sb) * (ATTN_SCALE * LOG2E)).T.astype(BF16)


def _q_call(x2, mod, ng, wdq, lg, wa, wb, ca, sb, S):
    T, D = x2.shape
    tm = TM_TOK
    nb = S // tm
    HP = MLA_HEADS * HEAD_PAD
    return pl.pallas_call(
        _q_kernel,
        out_shape=jax.ShapeDtypeStruct((HP, T), BF16),
        grid=(T // tm,),
        in_specs=[pl.BlockSpec((tm, D), lambda i: (i, 0)),
                  pl.BlockSpec((None, 6, D), lambda i: (i // nb, 0, 0)),
                  _const_spec((1, D)), _const_spec(wdq.shape), _const_spec((1, Q_LORA)),
                  _const_spec(wa.shape), _const_spec(wb.shape),
                  pl.BlockSpec((tm, LANES), lambda i: (i, 0)), pl.BlockSpec((tm, LANES), lambda i: (i, 0))],
        out_specs=pl.BlockSpec((HP, tm), lambda i: (0, i)),
        compiler_params=_cparams(("arbitrary",)),
        name="mla_queries",
    )(x2, mod, ng, wdq, lg, wa, wb, ca, sb)


def _col_max(x):
    while x.shape[0] > SUBLANES:
        h = x.shape[0] // 2
        x = jnp.maximum(x[:h], x[h:])
    return jnp.max(x, axis=0, keepdims=True)


def _attn_kernel(nfull_ref, nhi_ref, qt_ref, k_ref, vt_ref, pq_ref, pk_ref, o_ref, *, tq, tk, nq):
    b = pl.program_id(0)
    qi = pl.program_id(2)
    nfull = nfull_ref[b * nq + qi]
    nhi = nhi_ref[b * nq + qi]
    pos_q = pq_ref[...]

    def step(kc, carry, masked, tk):
        nstate = 2 * (tq // QSUB_ATT)
        nstream = nstate * (tk // KSUB_ATT)
        state = list(carry)

        def split(sidx):
            kb, sid = divmod(sidx, nstate)
            hh, qs = divmod(sid, tq // QSUB_ATT)
            return pl.multiple_of(kc * tk + kb * KSUB_ATT, KSUB_ATT), sid, hh, qs

        def scores(sidx):
            off, _, hh, qs = split(sidx)
            k = k_ref[pl.ds(off, KSUB_ATT), hh * HEAD_PAD:(hh + 1) * HEAD_PAD]
            qt = qt_ref[hh * HEAD_PAD:(hh + 1) * HEAD_PAD, qs * QSUB_ATT:(qs + 1) * QSUB_ATT]
            return jnp.dot(k, qt, preferred_element_type=F32)

        def update(sidx, st):
            off, sid, hh, qs = split(sidx)
            m, acc = state[sid]
            if masked:
                keep = pk_ref[pl.ds(off, KSUB_ATT), :] <= pos_q[:, qs * QSUB_ATT:(qs + 1) * QSUB_ATT]
                st = jnp.where(keep, st, NEG_INF)
            m_new = jnp.maximum(m, _col_max(st))
            p = jnp.exp2((st - m_new).astype(BF16))
            alpha = jnp.exp2(m - m_new)
            vt = jnp.concatenate([vt_ref[hh * V_HEAD:(hh + 1) * V_HEAD, pl.ds(off, KSUB_ATT)],
                                  jnp.ones((ONES_ROWS, KSUB_ATT), BF16)], axis=0)
            state[sid] = (m_new, alpha * acc + jnp.dot(vt, p, preferred_element_type=F32))

        ahead = [scores(s) for s in range(min(SCORE_LOOKAHEAD, nstream))]
        for sidx in range(nstream):
            st_cur = ahead.pop(0)
            if sidx + SCORE_LOOKAHEAD < nstream:
                ahead.append(scores(sidx + SCORE_LOOKAHEAD))
            update(sidx, st_cur)
        return tuple(state)

    nqs = tq // QSUB_ATT
    one = (jnp.full((1, QSUB_ATT), NEG_INF, F32), jnp.zeros((V_HEAD + ONES_ROWS, QSUB_ATT), F32))
    per = tk // KSUB_ATT
    nwide = nfull // per
    carry = lax.fori_loop(0, nwide, functools.partial(step, masked=False, tk=tk), (one,) * (2 * nqs))
    carry = lax.fori_loop(nwide * per, nfull, functools.partial(step, masked=False, tk=KSUB_ATT), carry)
    fin = lax.fori_loop(nfull, nhi, functools.partial(step, masked=True, tk=KSUB_ATT), carry)
    heads = [jnp.concatenate([fin[hh * nqs + qs][1][:V_HEAD] / fin[hh * nqs + qs][1][V_HEAD:V_HEAD + 1]
                              for qs in range(nqs)], axis=1) for hh in range(2)]
    o_ref[...] = jnp.concatenate(heads, axis=0).T.astype(BF16)


def _attn_call(nfull, nhi, qt, k, vt, pos_col, pos_row, B, S):
    tq, tk = TQ_ATT, TK_ATT
    nq = S // tq
    T = B * S
    kern = functools.partial(_attn_kernel, tq=tq, tk=tk, nq=nq)
    grid_spec = pltpu.PrefetchScalarGridSpec(
        num_scalar_prefetch=2,
        grid=(B, MLA_HEADS // 2, nq),
        in_specs=[pl.BlockSpec((2 * HEAD_PAD, tq), lambda b, h, i, *_: (h, b * nq + i)),
                  pl.BlockSpec((S, 2 * HEAD_PAD), lambda b, h, i, *_: (b, h)),
                  pl.BlockSpec((2 * V_HEAD, S), lambda b, h, i, *_: (h, b)),
                  pl.BlockSpec((None, 1, tq), lambda b, h, i, *_: (b, 0, i)),
                  pl.BlockSpec((S, 1), lambda b, h, i, *_: (b, 0))],
        out_specs=pl.BlockSpec((tq, 2 * V_HEAD), lambda b, h, i, *_: (b * nq + i, h)),
    )
    return pl.pallas_call(
        kern,
        out_shape=jax.ShapeDtypeStruct((T, MLA_HEADS * V_HEAD), BF16),
        grid_spec=grid_spec,
        compiler_params=_cparams(("arbitrary", "arbitrary", "arbitrary")),
        name="mla_attention",
    )(nfull, nhi, qt, k, vt, pos_row, pos_col)


def _oproj_kernel(x_ref, o_ref, mod_ref, w_ref, out_ref):
    g1 = mod_ref[...][2:3]
    y = jnp.dot(o_ref[...], w_ref[...], preferred_element_type=F32)
    out_ref[...] = x_ref[...] + g1 * y


def _oproj_call(x2, o, mod, w_o, S):
    T, D = x2.shape
    tm = TM_TOK
    nb = S // tm
    return pl.pallas_call(
        _oproj_kernel,
        out_shape=jax.ShapeDtypeStruct((T, D), F32),
        grid=(T // tm,),
        in_specs=[pl.BlockSpec((tm, D), lambda i: (i, 0)),
                  pl.BlockSpec((tm, o.shape[1]), lambda i: (i, 0)),
                  pl.BlockSpec((None, 6, D), lambda i: (i // nb, 0, 0)),
                  _const_spec(w_o.shape)],
        out_specs=pl.BlockSpec((tm, D), lambda i: (i, 0)),
        compiler_params=_cparams(("arbitrary",)),
        name="mla_out_proj",
    )(x2, o, mod, w_o)


TOP_ROWS = 3 * SUBLANES
SCORE_BLOCKS_PER_ITER = 4


def _merge_sort_network(n):
    pairs, p = [], 1
    while p < n:
        k = p
        while k >= 1:
            for j in range(k % p, n - k, 2 * k):
                for i in range(min(k, n - j - k)):
                    if (i + j) // (2 * p) == (i + j + k) // (2 * p):
                        pairs.append((i + j, i + j + k))
            k //= 2
        p *= 2
    return pairs


def _largest_rows(x, count):
    v = [x[r * SUBLANES:(r + 1) * SUBLANES, :] for r in range(x.shape[0] // SUBLANES)]
    size = 1 << (len(v) - 1).bit_length()
    v = v + [None] * (size - len(v))
    for i, j in _merge_sort_network(size):
        if v[j] is None:
            continue
        if v[i] is None:
            v[i], v[j] = v[j], None
        else:
            v[i], v[j] = jnp.maximum(v[i], v[j]), jnp.minimum(v[i], v[j])
    v = [t for t in v if t is not None]
    tops = []
    for j in range(count):
        m = jnp.max(v[0], axis=0, keepdims=True)
        tops.append(m)
        if j + 1 < count:
            won = v[0] == m
            depth = min(len(v), count - j)
            v = [jnp.where(won, v[r + 1] if r + 1 < len(v) else NEG_INF, v[r]) for r in range(depth)]
    return tops


def _top17_rows(x):
    pad = jnp.full((TOP_ROWS - PEER_TOPK - 1, x.shape[1]), NEG_INF, F32)
    return jnp.concatenate(_largest_rows(x, PEER_TOPK + 1) + [pad], axis=0)


def _peer_kernel(x_ref, mod_ref, ng_ref, wqt_ref, keys_ref, u_ref, vt_ref, fg_ref, o_ref,
                 h_s, st_s, thr_s, e1_s, e2_s, wa_s, acc_s, *, tm, ec, final_norm):
    ntc = tm // LANES
    e = pl.program_id(1)
    ne = pl.num_programs(1)

    @pl.when(e == 0)
    def _scores():
        x = x_ref[...]
        mod = mod_ref[...]
        sh2, sc2 = mod[3:4], mod[4:5]
        ht = (_rms(x, ng_ref[...]) * (1.0 + sc2) + sh2).T.astype(BF16)
        h_s[...] = ht
        qtb = jnp.dot(wqt_ref[...], ht, preferred_element_type=F32).astype(BF16)
        for hp in range(2 * PEER_HEADS):
            st = jnp.dot(keys_ref[hp], qtb[hp * PEER_HALF:(hp + 1) * PEER_HALF, :],
                         preferred_element_type=F32)
            for tc in range(ntc):
                st_s[tc, hp] = st[:, tc * LANES:(tc + 1) * LANES]

        def score_block(i):
            tc = i // PEER_HEADS
            hd = i % PEER_HEADS
            s1 = st_s[tc, 2 * hd]
            s2 = st_s[tc, 2 * hd + 1]
            a = _top17_rows(s1)
            b = _top17_rows(s2)
            cands = [a[0:1] + b]
            for r in range(1, SUBLANES):
                cands.append(a[r:r + 1] + b[0:SUBLANES])
            cands.append(a[SUBLANES:TOP_ROWS] + b[0:1])
            cand = jnp.concatenate(cands, axis=0)
            c16, c17 = _largest_rows(cand, PEER_TOPK + 1)[PEER_TOPK - 1:]
            tau = 0.5 * (c16 + c17)
            m1, m2 = a[0:1], b[0:1]
            zsum = jnp.sum(jnp.where(cand >= tau, jnp.exp(cand - (m1 + m2)), 0.0), axis=0, keepdims=True)
            thr_s[tc, hd] = jnp.exp(tau - s1 - m2)
            e1_s[tc, hd] = jnp.exp(s1 - m1) * (GELU_OUT_SCALE / zsum)
            e2_s[tc, hd] = jnp.exp(s2 - m2)

        def score_blocks(g, _):
            for bi in range(SCORE_BLOCKS_PER_ITER):
                score_block(g * SCORE_BLOCKS_PER_ITER + bi)
            return 0

        lax.fori_loop(0, ntc * PEER_HEADS // SCORE_BLOCKS_PER_ITER, score_blocks, 0)
        acc_s[...] = jnp.zeros(acc_s.shape, F32)

    at = jnp.dot(u_ref[...], h_s[...], preferred_element_type=F32)
    gact = at * (1.0 + lax.erf(at))
    ngrp = ec // LANES
    for g8 in range(ngrp // SUBLANES):
        base = pl.multiple_of(e * ngrp + g8 * SUBLANES, SUBLANES)
        for tc in range(ntc):
            thr8 = [thr_s[tc, hd, pl.ds(base, SUBLANES), :] for hd in range(PEER_HEADS)]
            e18 = [e1_s[tc, hd, pl.ds(base, SUBLANES), :] for hd in range(PEER_HEADS)]
            for j in range(SUBLANES):
                w = None
                for hd in range(PEER_HEADS):
                    e2 = e2_s[tc, hd]
                    term = jnp.where(e2 >= thr8[hd][j:j + 1, :], e2, 0.0) * e18[hd][j:j + 1, :]
                    w = term if w is None else w + term
                r0 = (g8 * SUBLANES + j) * LANES
                wa_s[r0:r0 + LANES, tc * LANES:(tc + 1) * LANES] = (
                    w * gact[r0:r0 + LANES, tc * LANES:(tc + 1) * LANES]).astype(BF16)
    acc_s[...] += jnp.dot(vt_ref[...], wa_s[...], preferred_element_type=F32)

    @pl.when(e == ne - 1)
    def _finish():
        g2 = mod_ref[...][5:6]
        xn = x_ref[...] + g2 * acc_s[...].T
        if final_norm:
            xn = _rms(xn, fg_ref[...])
        o_ref[...] = xn


def _peer_call(x2, mod, ng, wqt, keys, u, vt, fg, S, final_norm):
    T, D = x2.shape
    E = u.shape[0]
    tm, ec = TM_PEER, EC_PEER
    nb = S // tm
    ntc = tm // LANES
    kern = functools.partial(_peer_kernel, tm=tm, ec=ec, final_norm=final_norm)
    return pl.pallas_call(
        kern,
        out_shape=jax.ShapeDtypeStruct((T, D), F32),
        grid=(T // tm, E // ec),
        in_specs=[pl.BlockSpec((tm, D), lambda i, e: (i, 0)),
                  pl.BlockSpec((None, 6, D), lambda i, e: (i // nb, 0, 0)),
                  _const_spec((1, D)), _const_spec(wqt.shape), _const_spec(keys.shape),
                  pl.BlockSpec((ec, D), lambda i, e: (e, 0)),
                  pl.BlockSpec((D, ec), lambda i, e: (0, e)),
                  _const_spec((1, D))],
        out_specs=pl.BlockSpec((tm, D), lambda i, e: (i, 0)),
        scratch_shapes=[pltpu.VMEM((D, tm), BF16),
                        pltpu.VMEM((ntc, 2 * PEER_HEADS, N_KEYS, LANES), F32),
                        pltpu.VMEM((ntc, PEER_HEADS, N_KEYS, LANES), F32),
                        pltpu.VMEM((ntc, PEER_HEADS, N_KEYS, LANES), F32),
                        pltpu.VMEM((ntc, PEER_HEADS, N_KEYS, LANES), F32),
                        pltpu.VMEM((ec, tm), BF16),
                        pltpu.VMEM((D, tm), F32)],
        compiler_params=_cparams(("arbitrary", "arbitrary")),
        name="peer_layer",
    )(x2, mod, ng, wqt, keys, u, vt, fg)


def _head_pad_cols(w_nope, w_rope, n_in):
    zeros = jnp.zeros((n_in, MLA_HEADS, HEAD_PAD - QK_NOPE - QK_ROPE), w_nope.dtype)
    return jnp.concatenate([w_nope, w_rope, zeros], axis=-1).reshape(n_in, MLA_HEADS * HEAD_PAD)


def _swap_halves(w):
    half = QK_ROPE // 2
    return jnp.concatenate([w[..., half:], w[..., :half]], axis=-1)


def kernel(x, c, positions, ada_w, ada_b, norm_mix_g, norm_ffn_g, lru_w_in, lru_conv_w, lru_conv_b, lru_wa, lru_ba, lru_wx, lru_bx, lru_lambda, lru_w_out, kv_ada_w, kv_ada_b, kv_norm_g, mla_w_dkv, mla_w_kr, mla_kv_latent_g, mla_w_uk, mla_w_uv, mla_w_dq, mla_q_latent_g, mla_w_uq, mla_w_o, peer_w_q, peer_sub_keys, peer_u, peer_v, final_g):
    B, S, D = x.shape
    T = B * S
    assert D == D_MODEL and S % TM_TOK == 0 and S % TS_LRU == 0 and S % TM_PEER == 0
    assert S % TQ_ATT == 0 and S % TK_ATT == 0

    c_pad = jnp.zeros((SUBLANES, D), F32).at[:B].set(c)
    mod_all = _mod_call(c_pad, ada_w, ada_b[:, None, :], 1536)
    mod_all = mod_all[:, :B].reshape(DEPTH, B, 6, D)
    mod_kv = _mod_call(c_pad, kv_ada_w[None], kv_ada_b[None, None, :], 1024)[0, :B].reshape(B, 2, D)

    half = QK_ROPE // 2
    freqs = ROPE_THETA ** (-jnp.arange(half, dtype=F32) / half)
    freq_lane = jnp.zeros((1, LANES), F32).at[0, QK_NOPE:QK_NOPE + QK_ROPE].set(jnp.concatenate([freqs, freqs]))
    pos_col = positions.reshape(T, 1)
    ca, sb = _rope_call(pos_col, freq_lane)

    nq, nk = S // TQ_ATT, S // KSUB_ATT
    pq = positions.reshape(B, nq, TQ_ATT)
    pk = positions.reshape(B, nk, KSUB_ATT)
    qmin, qmax = pq.min(-1), pq.max(-1)
    kmin, kmax = pk.min(-1), pk.max(-1)
    needed = kmin[:, None, :] <= qmax[:, :, None]
    nhi = jnp.max(jnp.where(needed, jnp.arange(1, nk + 1, dtype=jnp.int32), 0), axis=-1)
    full = kmax[:, None, :] <= qmin[:, :, None]
    nfull = jnp.min(jnp.where(full, nk, jnp.arange(nk, dtype=jnp.int32)), axis=-1)
    nfull = jnp.minimum(nfull, nhi).astype(jnp.int32).reshape(-1)
    nhi = nhi.astype(jnp.int32).reshape(-1)
    pos_row = positions.reshape(B, 1, S)

    def peer_layer(x2, l, final_norm):
        wqt = peer_w_q[l].T.astype(BF16)
        keys = peer_sub_keys[l].reshape(2 * PEER_HEADS, N_KEYS, PEER_HALF).astype(BF16)
        u = (peer_u[l] * (1.0 / math.sqrt(2.0))).astype(BF16)
        vt = peer_v[l].T.astype(BF16)
        return _peer_call(x2, mod_all[l], norm_ffn_g[l][None], wqt, keys, u, vt, final_g[None], S, final_norm)

    for l in range(N_A_LAYERS):
        x = _lru_call(x, mod_all[l], norm_mix_g[l][None], lru_w_in[l].astype(BF16), lru_conv_w[l],
                      lru_conv_b[l][None], lru_wa[l].astype(BF16), lru_ba[l][None], lru_wx[l].astype(BF16),
                      lru_bx[l][None], lru_lambda[l][None], lru_w_out[l].astype(BF16))
        x = peer_layer(x.reshape(T, D), l, False).reshape(B, S, D)

    x2 = x.reshape(T, D)

    zc = jnp.zeros((D, QK_NOPE), F32)
    zt = jnp.zeros((D, HEAD_PAD - QK_NOPE - QK_ROPE), F32)
    w1 = jnp.concatenate([mla_w_dkv, zc, mla_w_kr, zt, zc, _swap_halves(mla_w_kr), zt], axis=-1).astype(BF16)
    wuk = jnp.concatenate([mla_w_uk.reshape(KV_LORA, MLA_HEADS, QK_NOPE),
                           jnp.zeros((KV_LORA, MLA_HEADS, HEAD_PAD - QK_NOPE), F32)], axis=-1)
    wuk = wuk.reshape(KV_LORA, MLA_HEADS * HEAD_PAD).astype(BF16)
    k_all, v_all = _kv_call(x2, mod_kv, kv_norm_g[None], w1, mla_kv_latent_g[None], wuk,
                            mla_w_uv.astype(BF16), ca, sb, S)

    for j in range(DEPTH - N_A_LAYERS):
        l = N_A_LAYERS + j
        wq = mla_w_uq[j].reshape(Q_LORA, MLA_HEADS, QK_NOPE + QK_ROPE)
        w_nope, w_rope = wq[..., :QK_NOPE], wq[..., QK_NOPE:]
        wa = _head_pad_cols(w_nope, w_rope, Q_LORA).astype(BF16)
        wb = _head_pad_cols(jnp.zeros_like(w_nope), _swap_halves(w_rope), Q_LORA).astype(BF16)
        q = _q_call(x2, mod_all[l], norm_mix_g[l][None], mla_w_dq[j].astype(BF16), mla_q_latent_g[j][None],
                    wa, wb, ca, sb, S)
        o = _attn_call(nfull, nhi, q, k_all, v_all, pos_col, pos_row, B, S)
        x2 = _oproj_call(x2, o, mod_all[l], mla_w_o[j].astype(BF16), S)
        x2 = peer_layer(x2, l, l == DEPTH - 1)

    return x2.reshape(B, S, D)
```

```python
import functools
import math

import jax
import jax.numpy as jnp
from jax import lax
from jax.experimental import pallas as pl
from jax.experimental.pallas import tpu as pltpu

F32 = jnp.float32
BF16 = jnp.bfloat16

D_MODEL = 1024
DEPTH = 4
N_A_LAYERS = DEPTH // 2
LRU_HEADS = 4
LRU_BLOCK = D_MODEL // LRU_HEADS
CONV_WIDTH = 4
LRU_C = 8.0
MLA_HEADS = 16
QK_NOPE = 64
QK_ROPE = 32
V_HEAD = 64
Q_LORA = 384
KV_LORA = 256
ROPE_THETA = 10000.0
ATTN_SCALE = 1.0 / math.sqrt(QK_NOPE + QK_ROPE)
PEER_HEADS = 8
N_KEYS = 128
PEER_TOPK = 16
PEER_HALF = 128
RMS_EPS = 1e-6
NEG_INF = -1e30
LOG2E = math.log2(math.e)
GELU_OUT_SCALE = 0.5 * math.sqrt(2.0)

LANES = 128
SUBLANES = 8
HEAD_PAD = LANES
VMEM_LIMIT = 56 * 1024 * 1024

TS_LRU = 256
TM_TOK = 512
TQ_ATT = 512
TK_ATT = 2048
KSUB_ATT = 512
QSUB_ATT = 256
SCORE_LOOKAHEAD = 4
ONES_ROWS = 16
TM_PEER = 512
EC_PEER = 2048
PIECE_PEER = 512
PIECE_LOOKAHEAD = 2


def _cparams(sem):
    return pltpu.CompilerParams(dimension_semantics=sem, vmem_limit_bytes=VMEM_LIMIT)


def _rms(x, g):
    return x * lax.rsqrt(jnp.mean(x * x, axis=-1, keepdims=True) + RMS_EPS) * g


def _gelu(x):
    return 0.5 * x * (1.0 + lax.erf(x * (1.0 / math.sqrt(2.0))))


def _expm1(y):
    series = y * (1.0 + 0.5 * y * (1.0 + (1.0 / 3.0) * y * (1.0 + 0.25 * y * (1.0 + 0.2 * y))))
    return jnp.where(jnp.abs(y) < 0.05, series, jnp.exp(y) - 1.0)


def _const_spec(shape):
    nd = len(shape)
    return pl.BlockSpec(shape, lambda *_: (0,) * nd)


def _mod_kernel(c_ref, w_ref, b_ref, o_ref):
    c = c_ref[...]
    sc = c * jax.nn.sigmoid(c)
    o_ref[...] = jnp.dot(sc.astype(BF16), w_ref[...].astype(BF16),
                         preferred_element_type=F32) + b_ref[...]


def _mod_call(c_pad, w, b, tn):
    L, D, N = w.shape
    return pl.pallas_call(
        _mod_kernel,
        out_shape=jax.ShapeDtypeStruct((L, SUBLANES, N), F32),
        grid=(L, N // tn),
        in_specs=[pl.BlockSpec((SUBLANES, D), lambda l, n: (0, 0)),
                  pl.BlockSpec((None, D, tn), lambda l, n: (l, 0, n)),
                  pl.BlockSpec((None, 1, tn), lambda l, n: (l, 0, n))],
        out_specs=pl.BlockSpec((None, SUBLANES, tn), lambda l, n: (l, 0, n)),
        compiler_params=_cparams(("arbitrary", "arbitrary")),
        name="adaln_mod",
    )(c_pad, w, b)


def _rope_kernel(pos_ref, freq_ref, ca_ref, sb_ref):
    ang = pos_ref[...].astype(F32) * freq_ref[...]
    lane = lax.broadcasted_iota(jnp.int32, ang.shape, 1)
    cos, sin = jnp.cos(ang), jnp.sin(ang)
    half = QK_ROPE // 2
    ca_ref[...] = jnp.where(lane < QK_NOPE, 1.0, jnp.where(lane < QK_NOPE + QK_ROPE, cos, 0.0))
    sb_ref[...] = jnp.where(lane < QK_NOPE, 0.0,
                            jnp.where(lane < QK_NOPE + half, -sin,
                                      jnp.where(lane < QK_NOPE + QK_ROPE, sin, 0.0)))


def _rope_call(pos_col, freq_lane):
    T = pos_col.shape[0]
    tm = TM_TOK
    spec = pl.BlockSpec((tm, LANES), lambda i: (i, 0))
    return pl.pallas_call(
        _rope_kernel,
        out_shape=(jax.ShapeDtypeStruct((T, LANES), F32),) * 2,
        grid=(T // tm,),
        in_specs=[pl.BlockSpec((tm, 1), lambda i: (i, 0)), _const_spec((1, LANES))],
        out_specs=(spec, spec),
        compiler_params=_cparams(("arbitrary",)),
        name="rope_tables",
    )(pos_col, freq_lane)


def _shift_rows(x, d, fill):
    n = x.shape[0]
    if d % SUBLANES == 0:
        return jnp.concatenate([jnp.full((d, x.shape[1]), fill, x.dtype), x[:n - d]], axis=0)
    rolled = pltpu.roll(x, d, 0)
    row = lax.broadcasted_iota(jnp.int32, x.shape, 0)
    return jnp.where(row < d, fill, rolled)


def _lru_kernel(x_ref, mod_ref, ng_ref, win_ref, cw_ref, cb_ref, wa_ref, ba_ref, wx_ref, bx_ref,
                lam_ref, wout_ref, o_ref, rec_s, h_s, *, ts):
    D = D_MODEL

    @pl.when(pl.program_id(1) == 0)
    def _():
        rec_s[0:SUBLANES, :] = jnp.zeros((SUBLANES, D), F32)
        h_s[...] = jnp.zeros((SUBLANES, D), F32)

    x = x_ref[...]
    mod = mod_ref[...]
    sh1, sc1, g1 = mod[0:1], mod[1:2], mod[2:3]
    h = _rms(x, ng_ref[...]) * (1.0 + sc1) + sh1
    u = jnp.dot(h.astype(BF16), win_ref[...], preferred_element_type=F32)
    gate = _gelu(u[:, :D])
    rec = u[:, D:]

    rec_s[SUBLANES:ts + SUBLANES, :] = rec
    cw = cw_ref[...]
    xc = (cb_ref[...] + rec * cw[3:4]
          + rec_s[SUBLANES - 1:ts + SUBLANES - 1, :] * cw[2:3]
          + rec_s[SUBLANES - 2:ts + SUBLANES - 2, :] * cw[1:2]
          + rec_s[SUBLANES - 3:ts + SUBLANES - 3, :] * cw[0:1])
    rec_s[0:SUBLANES, :] = rec[ts - SUBLANES:ts, :]

    xcb = xc.astype(BF16)
    r_pre = jnp.concatenate(
        [jnp.dot(xcb[:, k * LRU_BLOCK:(k + 1) * LRU_BLOCK], wa_ref[k], preferred_element_type=F32)
         for k in range(LRU_HEADS)], axis=-1)
    i_pre = jnp.concatenate(
        [jnp.dot(xcb[:, k * LRU_BLOCK:(k + 1) * LRU_BLOCK], wx_ref[k], preferred_element_type=F32)
         for k in range(LRU_HEADS)], axis=-1)
    r = jax.nn.sigmoid(r_pre + ba_ref[...])
    gi = jax.nn.sigmoid(i_pre + bx_ref[...])
    z = -lam_ref[...]
    softplus = jnp.maximum(z, 0.0) + jnp.log1p(jnp.exp(-jnp.abs(z)))
    log_a = (-LRU_C) * r * softplus
    a = jnp.exp(log_a)
    b = jnp.sqrt(-_expm1(2.0 * log_a)) * (gi * xc)

    d = 1
    while d < ts:
        a_sh = _shift_rows(a, d, 1.0)
        b_sh = _shift_rows(b, d, 0.0)
        b = a * b_sh + b
        a = a * a_sh
        d *= 2
    h0 = h_s[SUBLANES - 1:SUBLANES, :]
    hs = a * h0 + b
    h_s[...] = hs[ts - SUBLANES:ts, :]

    y = jnp.dot((gate * hs).astype(BF16), wout_ref[...], preferred_element_type=F32)
    o_ref[...] = x + g1 * y


def _lru_call(x, mod, ng, w_in, cw, cb, wa, ba, wx, bx, lam, w_out):
    B, S, D = x.shape
    ts = TS_LRU
    kern = functools.partial(_lru_kernel, ts=ts)
    return pl.pallas_call(
        kern,
        out_shape=jax.ShapeDtypeStruct((B, S, D), F32),
        grid=(B, S // ts),
        in_specs=[pl.BlockSpec((None, ts, D), lambda b, s: (b, s, 0)),
                  pl.BlockSpec((None, 6, D), lambda b, s: (b, 0, 0)),
                  _const_spec((1, D)), _const_spec((D, 2 * D)), _const_spec((CONV_WIDTH, D)),
                  _const_spec((1, D)), _const_spec((LRU_HEADS, LRU_BLOCK, LRU_BLOCK)), _const_spec((1, D)),
                  _const_spec((LRU_HEADS, LRU_BLOCK, LRU_BLOCK)), _const_spec((1, D)), _const_spec((1, D)),
                  _const_spec((D, D))],
        out_specs=pl.BlockSpec((None, ts, D), lambda b, s: (b, s, 0)),
        scratch_shapes=[pltpu.VMEM((ts + SUBLANES, D), F32), pltpu.VMEM((SUBLANES, D), F32)],
        compiler_params=_cparams(("arbitrary", "arbitrary")),
        name="rglru_layer",
    )(x, mod, ng, w_in, cw, cb, wa, ba, wx, bx, lam, w_out)


def _kv_kernel(x_ref, mod_ref, ng_ref, w1_ref, lg_ref, wuk_ref, wuv_ref, ca_ref, sb_ref, k_ref, v_ref):
    x = x_ref[...]
    mod = mod_ref[...]
    shift, scale = mod[0:1], mod[1:2]
    h = _rms(x, ng_ref[...]) * (1.0 + scale) + shift
    t = jnp.dot(h.astype(BF16), w1_ref[...], preferred_element_type=F32)
    c_kv = _rms(t[:, :KV_LORA], lg_ref[...])
    kr = t[:, KV_LORA:KV_LORA + LANES] * ca_ref[...] + t[:, KV_LORA + LANES:] * sb_ref[...]
    ckb = c_kv.astype(BF16)
    kn = jnp.dot(ckb, wuk_ref[...], preferred_element_type=F32)
    k_ref[...] = (kn + jnp.concatenate([kr] * MLA_HEADS, axis=-1)).astype(BF16)
    v_ref[...] = jnp.dot(ckb, wuv_ref[...], preferred_element_type=F32).T.astype(BF16)


def _kv_call(x2, mod, ng, w1, lg, wuk, wuv, ca, sb, S):
    T, D = x2.shape
    tm = TM_TOK
    nb = S // tm
    HP = MLA_HEADS * HEAD_PAD
    HV = MLA_HEADS * V_HEAD
    return pl.pallas_call(
        _kv_kernel,
        out_shape=(jax.ShapeDtypeStruct((T, HP), BF16), jax.ShapeDtypeStruct((HV, T), BF16)),
        grid=(T // tm,),
        in_specs=[pl.BlockSpec((tm, D), lambda i: (i, 0)),
                  pl.BlockSpec((None, 2, D), lambda i: (i // nb, 0, 0)),
                  _const_spec((1, D)), _const_spec(w1.shape), _const_spec((1, KV_LORA)),
                  _const_spec(wuk.shape), _const_spec(wuv.shape),
                  pl.BlockSpec((tm, LANES), lambda i: (i, 0)), pl.BlockSpec((tm, LANES), lambda i: (i, 0))],
        out_specs=(pl.BlockSpec((tm, HP), lambda i: (i, 0)), pl.BlockSpec((HV, tm), lambda i: (0, i))),
        compiler_params=_cparams(("arbitrary",)),
        name="shared_kv",
    )(x2, mod, ng, w1, lg, wuk, wuv, ca, sb)


def _q_kernel(x_ref, mod_ref, ng_ref, wdq_ref, lg_ref, wa_ref, wb_ref, ca_ref, sb_ref, q_ref):
    x = x_ref[...]
    mod = mod_ref[...]
    sh1, sc1 = mod[0:1], mod[1:2]
    h = _rms(x, ng_ref[...]) * (1.0 + sc1) + sh1
    ql = _rms(jnp.dot(h.astype(BF16), wdq_ref[...], preferred_element_type=F32), lg_ref[...])
    qlb = ql.astype(BF16)
    qa = jnp.dot(qlb, wa_ref[...], preferred_element_type=F32)
    qb = jnp.dot(qlb, wb_ref[...], preferred_element_type=F32)
    ca = jnp.concatenate([ca_ref[...]] * MLA_HEADS, axis=-1)
    sb = jnp.concatenate([sb_ref[...]] * MLA_HEADS, axis=-1)
    q_ref[...] = ((qa * ca + qb * sb) * (ATTN_SCALE * LOG2E)).T.astype(BF16)


def _q_call(x2, mod, ng, wdq, lg, wa, wb, ca, sb, S):
    T, D = x2.shape
    tm = TM_TOK
    nb = S // tm
    HP = MLA_HEADS * HEAD_PAD
    return pl.pallas_call(
        _q_kernel,
        out_shape=jax.ShapeDtypeStruct((HP, T), BF16),
        grid=(T // tm,),
        in_specs=[pl.BlockSpec((tm, D), lambda i: (i, 0)),
                  pl.BlockSpec((None, 6, D), lambda i: (i // nb, 0, 0)),
                  _const_spec((1, D)), _const_spec(wdq.shape), _const_spec((1, Q_LORA)),
                  _const_spec(wa.shape), _const_spec(wb.shape),
                  pl.BlockSpec((tm, LANES), lambda i: (i, 0)), pl.BlockSpec((tm, LANES), lambda i: (i, 0))],
        out_specs=pl.BlockSpec((HP, tm), lambda i: (0, i)),
        compiler_params=_cparams(("arbitrary",)),
        name="mla_queries",
    )(x2, mod, ng, wdq, lg, wa, wb, ca, sb)


def _col_max(x):
    while x.shape[0] > SUBLANES:
        h = x.shape[0] // 2
        x = jnp.maximum(x[:h], x[h:])
    return jnp.max(x, axis=0, keepdims=True)


def _attn_kernel(nfull_ref, nhi_ref, qt_ref, k_ref, vt_ref, pq_ref, pk_ref, o_ref, *, tq, tk, nq):
    b = pl.program_id(0)
    qi = pl.program_id(2)
    nfull = nfull_ref[b * nq + qi]
    nhi = nhi_ref[b * nq + qi]
    pos_q = pq_ref[...]

    def step(kc, carry, masked, tk):
        nstate = 2 * (tq // QSUB_ATT)
        nstream = nstate * (tk // KSUB_ATT)
        state = list(carry)

        def split(sidx):
            kb, sid = divmod(sidx, nstate)
            hh, qs = divmod(sid, tq // QSUB_ATT)
            return pl.multiple_of(kc * tk + kb * KSUB_ATT, KSUB_ATT), sid, hh, qs

        def scores(sidx):
            off, _, hh, qs = split(sidx)
            k = k_ref[pl.ds(off, KSUB_ATT), hh * HEAD_PAD:(hh + 1) * HEAD_PAD]
            qt = qt_ref[hh * HEAD_PAD:(hh + 1) * HEAD_PAD, qs * QSUB_ATT:(qs + 1) * QSUB_ATT]
            return jnp.dot(k, qt, preferred_element_type=F32)

        def update(sidx, st):
            off, sid, hh, qs = split(sidx)
            m, acc = state[sid]
            if masked:
                keep = pk_ref[pl.ds(off, KSUB_ATT), :] <= pos_q[:, qs * QSUB_ATT:(qs + 1) * QSUB_ATT]
                st = jnp.where(keep, st, NEG_INF)
            m_new = jnp.maximum(m, _col_max(st))
            p = jnp.exp2((st - m_new).astype(BF16))
            alpha = jnp.exp2(m - m_new)
            vt = jnp.concatenate([vt_ref[hh * V_HEAD:(hh + 1) * V_HEAD, pl.ds(off, KSUB_ATT)],
                                  jnp.ones((ONES_ROWS, KSUB_ATT), BF16)], axis=0)
            state[sid] = (m_new, alpha * acc + jnp.dot(vt, p, preferred_element_type=F32))

        ahead = [scores(s) for s in range(min(SCORE_LOOKAHEAD, nstream))]
        for sidx in range(nstream):
            st_cur = ahead.pop(0)
            if sidx + SCORE_LOOKAHEAD < nstream:
                ahead.append(scores(sidx + SCORE_LOOKAHEAD))
            update(sidx, st_cur)
        return tuple(state)

    nqs = tq // QSUB_ATT
    one = (jnp.full((1, QSUB_ATT), NEG_INF, F32), jnp.zeros((V_HEAD + ONES_ROWS, QSUB_ATT), F32))
    per = tk // KSUB_ATT
    nwide = nfull // per
    carry = lax.fori_loop(0, nwide, functools.partial(step, masked=False, tk=tk), (one,) * (2 * nqs))
    carry = lax.fori_loop(nwide * per, nfull, functools.partial(step, masked=False, tk=KSUB_ATT), carry)
    fin = lax.fori_loop(nfull, nhi, functools.partial(step, masked=True, tk=KSUB_ATT), carry)
    heads = [jnp.concatenate([fin[hh * nqs + qs][1][:V_HEAD] / fin[hh * nqs + qs][1][V_HEAD:V_HEAD + 1]
                              for qs in range(nqs)], axis=1) for hh in range(2)]
    o_ref[...] = jnp.concatenate(heads, axis=0).T.astype(BF16)


def _attn_call(nfull, nhi, qt, k, vt, pos_col, pos_row, B, S):
    tq, tk = TQ_ATT, TK_ATT
    nq = S // tq
    T = B * S
    kern = functools.partial(_attn_kernel, tq=tq, tk=tk, nq=nq)
    grid_spec = pltpu.PrefetchScalarGridSpec(
        num_scalar_prefetch=2,
        grid=(B, MLA_HEADS // 2, nq),
        in_specs=[pl.BlockSpec((2 * HEAD_PAD, tq), lambda b, h, i, *_: (h, b * nq + i)),
                  pl.BlockSpec((S, 2 * HEAD_PAD), lambda b, h, i, *_: (b, h)),
                  pl.BlockSpec((2 * V_HEAD, S), lambda b, h, i, *_: (h, b)),
                  pl.BlockSpec((None, 1, tq), lambda b, h, i, *_: (b, 0, i)),
                  pl.BlockSpec((S, 1), lambda b, h, i, *_: (b, 0))],
        out_specs=pl.BlockSpec((tq, 2 * V_HEAD), lambda b, h, i, *_: (b * nq + i, h)),
    )
    return pl.pallas_call(
        kern,
        out_shape=jax.ShapeDtypeStruct((T, MLA_HEADS * V_HEAD), BF16),
        grid_spec=grid_spec,
        compiler_params=_cparams(("arbitrary", "arbitrary", "arbitrary")),
        name="mla_attention",
    )(nfull, nhi, qt, k, vt, pos_row, pos_col)


def _oproj_kernel(x_ref, o_ref, mod_ref, w_ref, out_ref):
    g1 = mod_ref[...][2:3]
    y = jnp.dot(o_ref[...], w_ref[...], preferred_element_type=F32)
    out_ref[...] = x_ref[...] + g1 * y


def _oproj_call(x2, o, mod, w_o, S):
    T, D = x2.shape
    tm = TM_TOK
    nb = S // tm
    return pl.pallas_call(
        _oproj_kernel,
        out_shape=jax.ShapeDtypeStruct((T, D), F32),
        grid=(T // tm,),
        in_specs=[pl.BlockSpec((tm, D), lambda i: (i, 0)),
                  pl.BlockSpec((tm, o.shape[1]), lambda i: (i, 0)),
                  pl.BlockSpec((None, 6, D), lambda i: (i // nb, 0, 0)),
                  _const_spec(w_o.shape)],
        out_specs=pl.BlockSpec((tm, D), lambda i: (i, 0)),
        compiler_params=_cparams(("arbitrary",)),
        name="mla_out_proj",
    )(x2, o, mod, w_o)


TOP_ROWS = 3 * SUBLANES
SCORE_BLOCKS_PER_ITER = 4


def _merge_sort_network(n):
    pairs, p = [], 1
    while p < n:
        k = p
        while k >= 1:
            for j in range(k % p, n - k, 2 * k):
                for i in range(min(k, n - j - k)):
                    if (i + j) // (2 * p) == (i + j + k) // (2 * p):
                        pairs.append((i + j, i + j + k))
            k //= 2
        p *= 2
    return pairs


def _largest_rows(x, count):
    v = [x[r * SUBLANES:(r + 1) * SUBLANES, :] for r in range(x.shape[0] // SUBLANES)]
    size = 1 << (len(v) - 1).bit_length()
    v = v + [None] * (size - len(v))
    for i, j in _merge_sort_network(size):
        if v[j] is None:
            continue
        if v[i] is None:
            v[i], v[j] = v[j], None
        else:
            v[i], v[j] = jnp.maximum(v[i], v[j]), jnp.minimum(v[i], v[j])
    v = [t for t in v if t is not None]
    tops = []
    for j in range(count):
        m = jnp.max(v[0], axis=0, keepdims=True)
        tops.append(m)
        if j + 1 < count:
            won = v[0] == m
            depth = min(len(v), count - j)
            v = [jnp.where(won, v[r + 1] if r + 1 < len(v) else NEG_INF, v[r]) for r in range(depth)]
    return tops


def _top17_rows(x):
    pad = jnp.full((TOP_ROWS - PEER_TOPK - 1, x.shape[1]), NEG_INF, F32)
    return jnp.concatenate(_largest_rows(x, PEER_TOPK + 1) + [pad], axis=0)


def _peer_kernel(x_ref, mod_ref, ng_ref, wqt_ref, keys_ref, u_ref, vt_ref, fg_ref, o_ref,
                 h_s, st_s, thr_s, e1_s, e2_s, wa_s, acc_s, *, tm, ec, final_norm):
    ntc = tm // LANES
    e = pl.program_id(1)
    ne = pl.num_programs(1)

    @pl.when(e == 0)
    def _scores():
        x = x_ref[...]
        mod = mod_ref[...]
        sh2, sc2 = mod[3:4], mod[4:5]
        ht = (_rms(x, ng_ref[...]) * (1.0 + sc2) + sh2).T.astype(BF16)
        h_s[...] = ht
        qtb = jnp.dot(wqt_ref[...], ht, preferred_element_type=F32).astype(BF16)
        for hp in range(2 * PEER_HEADS):
            st = jnp.dot(keys_ref[hp], qtb[hp * PEER_HALF:(hp + 1) * PEER_HALF, :],
                         preferred_element_type=F32)
            for tc in range(ntc):
                st_s[tc, hp] = st[:, tc * LANES:(tc + 1) * LANES]

        def score_block(i):
            tc = i // PEER_HEADS
            hd = i % PEER_HEADS
            s1 = st_s[tc, 2 * hd]
            s2 = st_s[tc, 2 * hd + 1]
            a = _top17_rows(s1)
            b = _top17_rows(s2)
            cands = [a[0:1] + b]
            for r in range(1, SUBLANES):
                cands.append(a[r:r + 1] + b[0:SUBLANES])
            cands.append(a[SUBLANES:TOP_ROWS] + b[0:1])
            cand = jnp.concatenate(cands, axis=0)
            c16, c17 = _largest_rows(cand, PEER_TOPK + 1)[PEER_TOPK - 1:]
            tau = 0.5 * (c16 + c17)
            m1, m2 = a[0:1], b[0:1]
            zsum = jnp.sum(jnp.where(cand >= tau, jnp.exp(cand - (m1 + m2)), 0.0), axis=0, keepdims=True)
            thr_s[tc, hd] = jnp.exp(tau - s1 - m2)
            e1_s[tc, hd] = jnp.exp(s1 - m1) * (GELU_OUT_SCALE / zsum)
            e2_s[tc, hd] = jnp.exp(s2 - m2)

        def score_blocks(g, _):
            for bi in range(SCORE_BLOCKS_PER_ITER):
                score_block(g * SCORE_BLOCKS_PER_ITER + bi)
            return 0

        lax.fori_loop(0, ntc * PEER_HEADS // SCORE_BLOCKS_PER_ITER, score_blocks, 0)
        acc_s[...] = jnp.zeros(acc_s.shape, F32)

    ngrp = ec // LANES
    key_rows = {}

    def key_row(ref, r, tc, hd):
        g8, j = divmod(r, SUBLANES)
        if (id(ref), g8, tc, hd) not in key_rows:
            base = pl.multiple_of(e * ngrp + g8 * SUBLANES, SUBLANES)
            key_rows[(id(ref), g8, tc, hd)] = ref[tc, hd, pl.ds(base, SUBLANES), :]
        return key_rows[(id(ref), g8, tc, hd)][j:j + 1, :]

    rows_per_piece = PIECE_PEER // LANES

    def activations(p):
        return jnp.dot(u_ref[p * PIECE_PEER:(p + 1) * PIECE_PEER, :], h_s[...], preferred_element_type=F32)

    def weigh(p, at):
        gact = at * (1.0 + lax.erf(at))
        for rl in range(rows_per_piece):
            r = p * rows_per_piece + rl
            for tc in range(ntc):
                w = None
                for hd in range(PEER_HEADS):
                    e2 = e2_s[tc, hd]
                    term = jnp.where(e2 >= key_row(thr_s, r, tc, hd), e2, 0.0) * key_row(e1_s, r, tc, hd)
                    w = term if w is None else w + term
                wa_s[r * LANES:(r + 1) * LANES, tc * LANES:(tc + 1) * LANES] = (
                    w * gact[rl * LANES:(rl + 1) * LANES, tc * LANES:(tc + 1) * LANES]).astype(BF16)

    def values(p):
        cols = slice(p * PIECE_PEER, (p + 1) * PIECE_PEER)
        acc_s[...] += jnp.dot(vt_ref[:, cols], wa_s[cols, :], preferred_element_type=F32)

    npiece = ec // PIECE_PEER
    ahead = [activations(p) for p in range(min(PIECE_LOOKAHEAD, npiece))]
    for p in range(npiece):
        at = ahead.pop(0)
        if p + PIECE_LOOKAHEAD < npiece:
            ahead.append(activations(p + PIECE_LOOKAHEAD))
        weigh(p, at)
        values(p)

    @pl.when(e == ne - 1)
    def _finish():
        g2 = mod_ref[...][5:6]
        xn = x_ref[...] + g2 * acc_s[...].T
        if final_norm:
            xn = _rms(xn, fg_ref[...])
        o_ref[...] = xn


def _peer_call(x2, mod, ng, wqt, keys, u, vt, fg, S, final_norm):
    T, D = x2.shape
    E = u.shape[0]
    tm, ec = TM_PEER, EC_PEER
    nb = S // tm
    ntc = tm // LANES
    kern = functools.partial(_peer_kernel, tm=tm, ec=ec, final_norm=final_norm)
    return pl.pallas_call(
        kern,
        out_shape=jax.ShapeDtypeStruct((T, D), F32),
        grid=(T // tm, E // ec),
        in_specs=[pl.BlockSpec((tm, D), lambda i, e: (i, 0)),
                  pl.BlockSpec((None, 6, D), lambda i, e: (i // nb, 0, 0)),
                  _const_spec((1, D)), _const_spec(wqt.shape), _const_spec(keys.shape),
                  pl.BlockSpec((ec, D), lambda i, e: (e, 0)),
                  pl.BlockSpec((D, ec), lambda i, e: (0, e)),
                  _const_spec((1, D))],
        out_specs=pl.BlockSpec((tm, D), lambda i, e: (i, 0)),
        scratch_shapes=[pltpu.VMEM((D, tm), BF16),
                        pltpu.VMEM((ntc, 2 * PEER_HEADS, N_KEYS, LANES), F32),
                        pltpu.VMEM((ntc, PEER_HEADS, N_KEYS, LANES), F32),
                        pltpu.VMEM((ntc, PEER_HEADS, N_KEYS, LANES), F32),
                        pltpu.VMEM((ntc, PEER_HEADS, N_KEYS, LANES), F32),
                        pltpu.VMEM((ec, tm), BF16),
                        pltpu.VMEM((D, tm), F32)],
        compiler_params=_cparams(("arbitrary", "arbitrary")),
        name="peer_layer",
    )(x2, mod, ng, wqt, keys, u, vt, fg)


def _head_pad_cols(w_nope, w_rope, n_in):
    zeros = jnp.zeros((n_in, MLA_HEADS, HEAD_PAD - QK_NOPE - QK_ROPE), w_nope.dtype)
    return jnp.concatenate([w_nope, w_rope, zeros], axis=-1).reshape(n_in, MLA_HEADS * HEAD_PAD)


def _swap_halves(w):
    half = QK_ROPE // 2
    return jnp.concatenate([w[..., half:], w[..., :half]], axis=-1)


def kernel(x, c, positions, ada_w, ada_b, norm_mix_g, norm_ffn_g, lru_w_in, lru_conv_w, lru_conv_b, lru_wa, lru_ba, lru_wx, lru_bx, lru_lambda, lru_w_out, kv_ada_w, kv_ada_b, kv_norm_g, mla_w_dkv, mla_w_kr, mla_kv_latent_g, mla_w_uk, mla_w_uv, mla_w_dq, mla_q_latent_g, mla_w_uq, mla_w_o, peer_w_q, peer_sub_keys, peer_u, peer_v, final_g):
    B, S, D = x.shape
    T = B * S
    assert D == D_MODEL and S % TM_TOK == 0 and S % TS_LRU == 0 and S % TM_PEER == 0
    assert S % TQ_ATT == 0 and S % TK_ATT == 0

    c_pad = jnp.zeros((SUBLANES, D), F32).at[:B].set(c)
    mod_all = _mod_call(c_pad, ada_w, ada_b[:, None, :], 1536)
    mod_all = mod_all[:, :B].reshape(DEPTH, B, 6, D)
    mod_kv = _mod_call(c_pad, kv_ada_w[None], kv_ada_b[None, None, :], 1024)[0, :B].reshape(B, 2, D)

    half = QK_ROPE // 2
    freqs = ROPE_THETA ** (-jnp.arange(half, dtype=F32) / half)
    freq_lane = jnp.zeros((1, LANES), F32).at[0, QK_NOPE:QK_NOPE + QK_ROPE].set(jnp.concatenate([freqs, freqs]))
    pos_col = positions.reshape(T, 1)
    ca, sb = _rope_call(pos_col, freq_lane)

    nq, nk = S // TQ_ATT, S // KSUB_ATT
    pq = positions.reshape(B, nq, TQ_ATT)
    pk = positions.reshape(B, nk, KSUB_ATT)
    qmin, qmax = pq.min(-1), pq.max(-1)
    kmin, kmax = pk.min(-1), pk.max(-1)
    needed = kmin[:, None, :] <= qmax[:, :, None]
    nhi = jnp.max(jnp.where(needed, jnp.arange(1, nk + 1, dtype=jnp.int32), 0), axis=-1)
    full = kmax[:, None, :] <= qmin[:, :, None]
    nfull = jnp.min(jnp.where(full, nk, jnp.arange(nk, dtype=jnp.int32)), axis=-1)
    nfull = jnp.minimum(nfull, nhi).astype(jnp.int32).reshape(-1)
    nhi = nhi.astype(jnp.int32).reshape(-1)
    pos_row = positions.reshape(B, 1, S)

    def peer_layer(x2, l, final_norm):
        wqt = peer_w_q[l].T.astype(BF16)
        keys = peer_sub_keys[l].reshape(2 * PEER_HEADS, N_KEYS, PEER_HALF).astype(BF16)
        u = (peer_u[l] * (1.0 / math.sqrt(2.0))).astype(BF16)
        vt = peer_v[l].T.astype(BF16)
        return _peer_call(x2, mod_all[l], norm_ffn_g[l][None], wqt, keys, u, vt, final_g[None], S, final_norm)

    for l in range(N_A_LAYERS):
        x = _lru_call(x, mod_all[l], norm_mix_g[l][None], lru_w_in[l].astype(BF16), lru_conv_w[l],
                      lru_conv_b[l][None], lru_wa[l].astype(BF16), lru_ba[l][None], lru_wx[l].astype(BF16),
                      lru_bx[l][None], lru_lambda[l][None], lru_w_out[l].astype(BF16))
        x = peer_layer(x.reshape(T, D), l, False).reshape(B, S, D)

    x2 = x.reshape(T, D)

    zc = jnp.zeros((D, QK_NOPE), F32)
    zt = jnp.zeros((D, HEAD_PAD - QK_NOPE - QK_ROPE), F32)
    w1 = jnp.concatenate([mla_w_dkv, zc, mla_w_kr, zt, zc, _swap_halves(mla_w_kr), zt], axis=-1).astype(BF16)
    wuk = jnp.concatenate([mla_w_uk.reshape(KV_LORA, MLA_HEADS, QK_NOPE),
                           jnp.zeros((KV_LORA, MLA_HEADS, HEAD_PAD - QK_NOPE), F32)], axis=-1)
    wuk = wuk.reshape(KV_LORA, MLA_HEADS * HEAD_PAD).astype(BF16)
    k_all, v_all = _kv_call(x2, mod_kv, kv_norm_g[None], w1, mla_kv_latent_g[None], wuk,
                            mla_w_uv.astype(BF16), ca, sb, S)

    for j in range(DEPTH - N_A_LAYERS):
        l = N_A_LAYERS + j
        wq = mla_w_uq[j].reshape(Q_LORA, MLA_HEADS, QK_NOPE + QK_ROPE)
        w_nope, w_rope = wq[..., :QK_NOPE], wq[..., QK_NOPE:]
        wa = _head_pad_cols(w_nope, w_rope, Q_LORA).astype(BF16)
        wb = _head_pad_cols(jnp.zeros_like(w_nope), _swap_halves(w_rope), Q_LORA).astype(BF16)
        q = _q_call(x2, mod_all[l], norm_mix_g[l][None], mla_w_dq[j].astype(BF16), mla_q_latent_g[j][None],
                    wa, wb, ca, sb, S)
        o = _attn_call(nfull, nhi, q, k_all, v_all, pos_col, pos_row, B, S)
        x2 = _oproj_call(x2, o, mod_all[l], mla_w_o[j].astype(BF16), S)
        x2 = peer_layer(x2, l, l == DEPTH - 1)

    return x2.reshape(B, S, D)
```

```python
import functools
import math

import jax
import jax.numpy as jnp
from jax import lax
from jax.experimental import pallas as pl
from jax.experimental.pallas import tpu as pltpu

F32 = jnp.float32
BF16 = jnp.bfloat16

D_MODEL = 1024
DEPTH = 4
N_A_LAYERS = DEPTH // 2
LRU_HEADS = 4
LRU_BLOCK = D_MODEL // LRU_HEADS
CONV_WIDTH = 4
LRU_C = 8.0
MLA_HEADS = 16
QK_NOPE = 64
QK_ROPE = 32
V_HEAD = 64
Q_LORA = 384
KV_LORA = 256
ROPE_THETA = 10000.0
ATTN_SCALE = 1.0 / math.sqrt(QK_NOPE + QK_ROPE)
PEER_HEADS = 8
N_KEYS = 128
PEER_TOPK = 16
PEER_HALF = 128
RMS_EPS = 1e-6
NEG_INF = -1e30
LOG2E = math.log2(math.e)
GELU_OUT_SCALE = 0.5 * math.sqrt(2.0)

LANES = 128
SUBLANES = 8
HEAD_PAD = LANES
VMEM_LIMIT = 56 * 1024 * 1024

TS_LRU = 256
TM_TOK = 512
TQ_ATT = 512
TK_ATT = 2048
KSUB_ATT = 512
QSUB_ATT = 256
SCORE_LOOKAHEAD = 4
ONES_ROWS = 16
TM_PEER = 512
EC_PEER = 2048
PIECE_PEER = 256
PIECE_LOOKAHEAD = 4


def _cparams(sem):
    return pltpu.CompilerParams(dimension_semantics=sem, vmem_limit_bytes=VMEM_LIMIT)


def _rms(x, g):
    return x * lax.rsqrt(jnp.mean(x * x, axis=-1, keepdims=True) + RMS_EPS) * g


def _gelu(x):
    return 0.5 * x * (1.0 + lax.erf(x * (1.0 / math.sqrt(2.0))))


def _expm1(y):
    series = y * (1.0 + 0.5 * y * (1.0 + (1.0 / 3.0) * y * (1.0 + 0.25 * y * (1.0 + 0.2 * y))))
    return jnp.where(jnp.abs(y) < 0.05, series, jnp.exp(y) - 1.0)


def _const_spec(shape):
    nd = len(shape)
    return pl.BlockSpec(shape, lambda *_: (0,) * nd)


def _mod_kernel(c_ref, w_ref, b_ref, o_ref):
    c = c_ref[...]
    sc = c * jax.nn.sigmoid(c)
    o_ref[...] = jnp.dot(sc.astype(BF16), w_ref[...].astype(BF16),
                         preferred_element_type=F32) + b_ref[...]


def _mod_call(c_pad, w, b, tn):
    L, D, N = w.shape
    return pl.pallas_call(
        _mod_kernel,
        out_shape=jax.ShapeDtypeStruct((L, SUBLANES, N), F32),
        grid=(L, N // tn),
        in_specs=[pl.BlockSpec((SUBLANES, D), lambda l, n: (0, 0)),
                  pl.BlockSpec((None, D, tn), lambda l, n: (l, 0, n)),
                  pl.BlockSpec((None, 1, tn), lambda l, n: (l, 0, n))],
        out_specs=pl.BlockSpec((None, SUBLANES, tn), lambda l, n: (l, 0, n)),
        compiler_params=_cparams(("arbitrary", "arbitrary")),
        name="adaln_mod",
    )(c_pad, w, b)


def _rope_kernel(pos_ref, freq_ref, ca_ref, sb_ref):
    ang = pos_ref[...].astype(F32) * freq_ref[...]
    lane = lax.broadcasted_iota(jnp.int32, ang.shape, 1)
    cos, sin = jnp.cos(ang), jnp.sin(ang)
    half = QK_ROPE // 2
    ca_ref[...] = jnp.where(lane < QK_NOPE, 1.0, jnp.where(lane < QK_NOPE + QK_ROPE, cos, 0.0))
    sb_ref[...] = jnp.where(lane < QK_NOPE, 0.0,
                            jnp.where(lane < QK_NOPE + half, -sin,
                                      jnp.where(lane < QK_NOPE + QK_ROPE, sin, 0.0)))


def _rope_call(pos_col, freq_lane):
    T = pos_col.shape[0]
    tm = TM_TOK
    spec = pl.BlockSpec((tm, LANES), lambda i: (i, 0))
    return pl.pallas_call(
        _rope_kernel,
        out_shape=(jax.ShapeDtypeStruct((T, LANES), F32),) * 2,
        grid=(T // tm,),
        in_specs=[pl.BlockSpec((tm, 1), lambda i: (i, 0)), _const_spec((1, LANES))],
        out_specs=(spec, spec),
        compiler_params=_cparams(("arbitrary",)),
        name="rope_tables",
    )(pos_col, freq_lane)


def _shift_rows(x, d, fill):
    n = x.shape[0]
    if d % SUBLANES == 0:
        return jnp.concatenate([jnp.full((d, x.shape[1]), fill, x.dtype), x[:n - d]], axis=0)
    rolled = pltpu.roll(x, d, 0)
    row = lax.broadcasted_iota(jnp.int32, x.shape, 0)
    return jnp.where(row < d, fill, rolled)


def _lru_kernel(x_ref, mod_ref, ng_ref, win_ref, cw_ref, cb_ref, wa_ref, ba_ref, wx_ref, bx_ref,
                lam_ref, wout_ref, o_ref, rec_s, h_s, *, ts):
    D = D_MODEL

    @pl.when(pl.program_id(1) == 0)
    def _():
        rec_s[0:SUBLANES, :] = jnp.zeros((SUBLANES, D), F32)
        h_s[...] = jnp.zeros((SUBLANES, D), F32)

    x = x_ref[...]
    mod = mod_ref[...]
    sh1, sc1, g1 = mod[0:1], mod[1:2], mod[2:3]
    h = _rms(x, ng_ref[...]) * (1.0 + sc1) + sh1
    u = jnp.dot(h.astype(BF16), win_ref[...], preferred_element_type=F32)
    gate = _gelu(u[:, :D])
    rec = u[:, D:]

    rec_s[SUBLANES:ts + SUBLANES, :] = rec
    cw = cw_ref[...]
    xc = (cb_ref[...] + rec * cw[3:4]
          + rec_s[SUBLANES - 1:ts + SUBLANES - 1, :] * cw[2:3]
          + rec_s[SUBLANES - 2:ts + SUBLANES - 2, :] * cw[1:2]
          + rec_s[SUBLANES - 3:ts + SUBLANES - 3, :] * cw[0:1])
    rec_s[0:SUBLANES, :] = rec[ts - SUBLANES:ts, :]

    xcb = xc.astype(BF16)
    r_pre = jnp.concatenate(
        [jnp.dot(xcb[:, k * LRU_BLOCK:(k + 1) * LRU_BLOCK], wa_ref[k], preferred_element_type=F32)
         for k in range(LRU_HEADS)], axis=-1)
    i_pre = jnp.concatenate(
        [jnp.dot(xcb[:, k * LRU_BLOCK:(k + 1) * LRU_BLOCK], wx_ref[k], preferred_element_type=F32)
         for k in range(LRU_HEADS)], axis=-1)
    r = jax.nn.sigmoid(r_pre + ba_ref[...])
    gi = jax.nn.sigmoid(i_pre + bx_ref[...])
    z = -lam_ref[...]
    softplus = jnp.maximum(z, 0.0) + jnp.log1p(jnp.exp(-jnp.abs(z)))
    log_a = (-LRU_C) * r * softplus
    a = jnp.exp(log_a)
    b = jnp.sqrt(-_expm1(2.0 * log_a)) * (gi * xc)

    d = 1
    while d < ts:
        a_sh = _shift_rows(a, d, 1.0)
        b_sh = _shift_rows(b, d, 0.0)
        b = a * b_sh + b
        a = a * a_sh
        d *= 2
    h0 = h_s[SUBLANES - 1:SUBLANES, :]
    hs = a * h0 + b
    h_s[...] = hs[ts - SUBLANES:ts, :]

    y = jnp.dot((gate * hs).astype(BF16), wout_ref[...], preferred_element_type=F32)
    o_ref[...] = x + g1 * y


def _lru_call(x, mod, ng, w_in, cw, cb, wa, ba, wx, bx, lam, w_out):
    B, S, D = x.shape
    ts = TS_LRU
    kern = functools.partial(_lru_kernel, ts=ts)
    return pl.pallas_call(
        kern,
        out_shape=jax.ShapeDtypeStruct((B, S, D), F32),
        grid=(B, S // ts),
        in_specs=[pl.BlockSpec((None, ts, D), lambda b, s: (b, s, 0)),
                  pl.BlockSpec((None, 6, D), lambda b, s: (b, 0, 0)),
                  _const_spec((1, D)), _const_spec((D, 2 * D)), _const_spec((CONV_WIDTH, D)),
                  _const_spec((1, D)), _const_spec((LRU_HEADS, LRU_BLOCK, LRU_BLOCK)), _const_spec((1, D)),
                  _const_spec((LRU_HEADS, LRU_BLOCK, LRU_BLOCK)), _const_spec((1, D)), _const_spec((1, D)),
                  _const_spec((D, D))],
        out_specs=pl.BlockSpec((None, ts, D), lambda b, s: (b, s, 0)),
        scratch_shapes=[pltpu.VMEM((ts + SUBLANES, D), F32), pltpu.VMEM((SUBLANES, D), F32)],
        compiler_params=_cparams(("arbitrary", "arbitrary")),
        name="rglru_layer",
    )(x, mod, ng, w_in, cw, cb, wa, ba, wx, bx, lam, w_out)


def _kv_kernel(x_ref, mod_ref, ng_ref, w1_ref, lg_ref, wuk_ref, wuv_ref, ca_ref, sb_ref, k_ref, v_ref):
    x = x_ref[...]
    mod = mod_ref[...]
    shift, scale = mod[0:1], mod[1:2]
    h = _rms(x, ng_ref[...]) * (1.0 + scale) + shift
    t = jnp.dot(h.astype(BF16), w1_ref[...], preferred_element_type=F32)
    c_kv = _rms(t[:, :KV_LORA], lg_ref[...])
    kr = t[:, KV_LORA:KV_LORA + LANES] * ca_ref[...] + t[:, KV_LORA + LANES:] * sb_ref[...]
    ckb = c_kv.astype(BF16)
    kn = jnp.dot(ckb, wuk_ref[...], preferred_element_type=F32)
    k_ref[...] = (kn + jnp.concatenate([kr] * MLA_HEADS, axis=-1)).astype(BF16)
    v_ref[...] = jnp.dot(ckb, wuv_ref[...], preferred_element_type=F32).T.astype(BF16)


def _kv_call(x2, mod, ng, w1, lg, wuk, wuv, ca, sb, S):
    T, D = x2.shape
    tm = TM_TOK
    nb = S // tm
    HP = MLA_HEADS * HEAD_PAD
    HV = MLA_HEADS * V_HEAD
    return pl.pallas_call(
        _kv_kernel,
        out_shape=(jax.ShapeDtypeStruct((T, HP), BF16), jax.ShapeDtypeStruct((HV, T), BF16)),
        grid=(T // tm,),
        in_specs=[pl.BlockSpec((tm, D), lambda i: (i, 0)),
                  pl.BlockSpec((None, 2, D), lambda i: (i // nb, 0, 0)),
                  _const_spec((1, D)), _const_spec(w1.shape), _const_spec((1, KV_LORA)),
                  _const_spec(wuk.shape), _const_spec(wuv.shape),
                  pl.BlockSpec((tm, LANES), lambda i: (i, 0)), pl.BlockSpec((tm, LANES), lambda i: (i, 0))],
        out_specs=(pl.BlockSpec((tm, HP), lambda i: (i, 0)), pl.BlockSpec((HV, tm), lambda i: (0, i))),
        compiler_params=_cparams(("arbitrary",)),
        name="shared_kv",
    )(x2, mod, ng, w1, lg, wuk, wuv, ca, sb)


def _q_kernel(x_ref, mod_ref, ng_ref, wdq_ref, lg_ref, wa_ref, wb_ref, ca_ref, sb_ref, q_ref):
    x = x_ref[...]
    mod = mod_ref[...]
    sh1, sc1 = mod[0:1], mod[1:2]
    h = _rms(x, ng_ref[...]) * (1.0 + sc1) + sh1
    ql = _rms(jnp.dot(h.astype(BF16), wdq_ref[...], preferred_element_type=F32), lg_ref[...])
    qlb = ql.astype(BF16)
    qa = jnp.dot(qlb, wa_ref[...], preferred_element_type=F32)
    qb = jnp.dot(qlb, wb_ref[...], preferred_element_type=F32)
    ca = jnp.concatenate([ca_ref[...]] * MLA_HEADS, axis=-1)
    sb = jnp.concatenate([sb_ref[...]] * MLA_HEADS, axis=-1)
    q_ref[...] = ((qa * ca + qb * sb) * (ATTN_SCALE * LOG2E)).T.astype(BF16)


def _q_call(x2, mod, ng, wdq, lg, wa, wb, ca, sb, S):
    T, D = x2.shape
    tm = TM_TOK
    nb = S // tm
    HP = MLA_HEADS * HEAD_PAD
    return pl.pallas_call(
        _q_kernel,
        out_shape=jax.ShapeDtypeStruct((HP, T), BF16),
        grid=(T // tm,),
        in_specs=[pl.BlockSpec((tm, D), lambda i: (i, 0)),
                  pl.BlockSpec((None, 6, D), lambda i: (i // nb, 0, 0)),
                  _const_spec((1, D)), _const_spec(wdq.shape), _const_spec((1, Q_LORA)),
                  _const_spec(wa.shape), _const_spec(wb.shape),
                  pl.BlockSpec((tm, LANES), lambda i: (i, 0)), pl.BlockSpec((tm, LANES), lambda i: (i, 0))],
        out_specs=pl.BlockSpec((HP, tm), lambda i: (0, i)),
        compiler_params=_cparams(("arbitrary",)),
        name="mla_queries",
    )(x2, mod, ng, wdq, lg, wa, wb, ca, sb)


def _col_max(x):
    while x.shape[0] > SUBLANES:
        h = x.shape[0] // 2
        x = jnp.maximum(x[:h], x[h:])
    return jnp.max(x, axis=0, keepdims=True)


def _attn_kernel(nfull_ref, nhi_ref, qt_ref, k_ref, vt_ref, pq_ref, pk_ref, o_ref, *, tq, tk, nq):
    b = pl.program_id(0)
    qi = pl.program_id(2)
    nfull = nfull_ref[b * nq + qi]
    nhi = nhi_ref[b * nq + qi]
    pos_q = pq_ref[...]

    def step(kc, carry, masked, tk):
        nstate = 2 * (tq // QSUB_ATT)
        nstream = nstate * (tk // KSUB_ATT)
        state = list(carry)

        def split(sidx):
            kb, sid = divmod(sidx, nstate)
            hh, qs = divmod(sid, tq // QSUB_ATT)
            return pl.multiple_of(kc * tk + kb * KSUB_ATT, KSUB_ATT), sid, hh, qs

        def scores(sidx):
            off, _, hh, qs = split(sidx)
            k = k_ref[pl.ds(off, KSUB_ATT), hh * HEAD_PAD:(hh + 1) * HEAD_PAD]
            qt = qt_ref[hh * HEAD_PAD:(hh + 1) * HEAD_PAD, qs * QSUB_ATT:(qs + 1) * QSUB_ATT]
            return jnp.dot(k, qt, preferred_element_type=F32)

        def update(sidx, st):
            off, sid, hh, qs = split(sidx)
            m, acc = state[sid]
            if masked:
                keep = pk_ref[pl.ds(off, KSUB_ATT), :] <= pos_q[:, qs * QSUB_ATT:(qs + 1) * QSUB_ATT]
                st = jnp.where(keep, st, NEG_INF)
            m_new = jnp.maximum(m, _col_max(st))
            p = jnp.exp2((st - m_new).astype(BF16))
            alpha = jnp.exp2(m - m_new)
            vt = jnp.concatenate([vt_ref[hh * V_HEAD:(hh + 1) * V_HEAD, pl.ds(off, KSUB_ATT)],
                                  jnp.ones((ONES_ROWS, KSUB_ATT), BF16)], axis=0)
            state[sid] = (m_new, alpha * acc + jnp.dot(vt, p, preferred_element_type=F32))

        ahead = [scores(s) for s in range(min(SCORE_LOOKAHEAD, nstream))]
        for sidx in range(nstream):
            st_cur = ahead.pop(0)
            if sidx + SCORE_LOOKAHEAD < nstream:
                ahead.append(scores(sidx + SCORE_LOOKAHEAD))
            update(sidx, st_cur)
        return tuple(state)

    nqs = tq // QSUB_ATT
    one = (jnp.full((1, QSUB_ATT), NEG_INF, F32), jnp.zeros((V_HEAD + ONES_ROWS, QSUB_ATT), F32))
    per = tk // KSUB_ATT
    nwide = nfull // per
    carry = lax.fori_loop(0, nwide, functools.partial(step, masked=False, tk=tk), (one,) * (2 * nqs))
    carry = lax.fori_loop(nwide * per, nfull, functools.partial(step, masked=False, tk=KSUB_ATT), carry)
    fin = lax.fori_loop(nfull, nhi, functools.partial(step, masked=True, tk=KSUB_ATT), carry)
    heads = [jnp.concatenate([fin[hh * nqs + qs][1][:V_HEAD] / fin[hh * nqs + qs][1][V_HEAD:V_HEAD + 1]
                              for qs in range(nqs)], axis=1) for hh in range(2)]
    o_ref[...] = jnp.concatenate(heads, axis=0).T.astype(BF16)


def _attn_call(nfull, nhi, qt, k, vt, pos_col, pos_row, B, S):
    tq, tk = TQ_ATT, TK_ATT
    nq = S // tq
    T = B * S
    kern = functools.partial(_attn_kernel, tq=tq, tk=tk, nq=nq)
    grid_spec = pltpu.PrefetchScalarGridSpec(
        num_scalar_prefetch=2,
        grid=(B, MLA_HEADS // 2, nq),
        in_specs=[pl.BlockSpec((2 * HEAD_PAD, tq), lambda b, h, i, *_: (h, b * nq + i)),
                  pl.BlockSpec((S, 2 * HEAD_PAD), lambda b, h, i, *_: (b, h)),
                  pl.BlockSpec((2 * V_HEAD, S), lambda b, h, i, *_: (h, b)),
                  pl.BlockSpec((None, 1, tq), lambda b, h, i, *_: (b, 0, i)),
                  pl.BlockSpec((S, 1), lambda b, h, i, *_: (b, 0))],
        out_specs=pl.BlockSpec((tq, 2 * V_HEAD), lambda b, h, i, *_: (b * nq + i, h)),
    )
    return pl.pallas_call(
        kern,
        out_shape=jax.ShapeDtypeStruct((T, MLA_HEADS * V_HEAD), BF16),
        grid_spec=grid_spec,
        compiler_params=_cparams(("arbitrary", "arbitrary", "arbitrary")),
        name="mla_attention",
    )(nfull, nhi, qt, k, vt, pos_row, pos_col)


def _oproj_kernel(x_ref, o_ref, mod_ref, w_ref, out_ref):
    g1 = mod_ref[...][2:3]
    y = jnp.dot(o_ref[...], w_ref[...], preferred_element_type=F32)
    out_ref[...] = x_ref[...] + g1 * y


def _oproj_call(x2, o, mod, w_o, S):
    T, D = x2.shape
    tm = TM_TOK
    nb = S // tm
    return pl.pallas_call(
        _oproj_kernel,
        out_shape=jax.ShapeDtypeStruct((T, D), F32),
        grid=(T // tm,),
        in_specs=[pl.BlockSpec((tm, D), lambda i: (i, 0)),
                  pl.BlockSpec((tm, o.shape[1]), lambda i: (i, 0)),
                  pl.BlockSpec((None, 6, D), lambda i: (i // nb, 0, 0)),
                  _const_spec(w_o.shape)],
        out_specs=pl.BlockSpec((tm, D), lambda i: (i, 0)),
        compiler_params=_cparams(("arbitrary",)),
        name="mla_out_proj",
    )(x2, o, mod, w_o)


TOP_ROWS = 3 * SUBLANES
SCORE_BLOCKS_PER_ITER = 4


def _merge_sort_network(n):
    pairs, p = [], 1
    while p < n:
        k = p
        while k >= 1:
            for j in range(k % p, n - k, 2 * k):
                for i in range(min(k, n - j - k)):
                    if (i + j) // (2 * p) == (i + j + k) // (2 * p):
                        pairs.append((i + j, i + j + k))
            k //= 2
        p *= 2
    return pairs


def _largest_rows(x, count):
    v = [x[r * SUBLANES:(r + 1) * SUBLANES, :] for r in range(x.shape[0] // SUBLANES)]
    size = 1 << (len(v) - 1).bit_length()
    v = v + [None] * (size - len(v))
    for i, j in _merge_sort_network(size):
        if v[j] is None:
            continue
        if v[i] is None:
            v[i], v[j] = v[j], None
        else:
            v[i], v[j] = jnp.maximum(v[i], v[j]), jnp.minimum(v[i], v[j])
    v = [t for t in v if t is not None]
    tops = []
    for j in range(count):
        m = jnp.max(v[0], axis=0, keepdims=True)
        tops.append(m)
        if j + 1 < count:
            won = v[0] == m
            depth = min(len(v), count - j)
            v = [jnp.where(won, v[r + 1] if r + 1 < len(v) else NEG_INF, v[r]) for r in range(depth)]
    return tops


def _top17_rows(x):
    pad = jnp.full((TOP_ROWS - PEER_TOPK - 1, x.shape[1]), NEG_INF, F32)
    return jnp.concatenate(_largest_rows(x, PEER_TOPK + 1) + [pad], axis=0)


def _peer_kernel(x_ref, mod_ref, ng_ref, wqt_ref, keys_ref, u_ref, vt_ref, fg_ref, o_ref,
                 h_s, st_s, thr_s, e1_s, e2_s, wa_s, acc_s, *, tm, ec, final_norm):
    ntc = tm // LANES
    e = pl.program_id(1)
    ne = pl.num_programs(1)

    @pl.when(e == 0)
    def _scores():
        x = x_ref[...]
        mod = mod_ref[...]
        sh2, sc2 = mod[3:4], mod[4:5]
        ht = (_rms(x, ng_ref[...]) * (1.0 + sc2) + sh2).T.astype(BF16)
        h_s[...] = ht
        qtb = jnp.dot(wqt_ref[...], ht, preferred_element_type=F32).astype(BF16)
        for hp in range(2 * PEER_HEADS):
            st = jnp.dot(keys_ref[hp], qtb[hp * PEER_HALF:(hp + 1) * PEER_HALF, :],
                         preferred_element_type=F32)
            for tc in range(ntc):
                st_s[tc, hp] = st[:, tc * LANES:(tc + 1) * LANES]

        def score_block(i):
            tc = i // PEER_HEADS
            hd = i % PEER_HEADS
            s1 = st_s[tc, 2 * hd]
            s2 = st_s[tc, 2 * hd + 1]
            a = _top17_rows(s1)
            b = _top17_rows(s2)
            cands = [a[0:1] + b]
            for r in range(1, SUBLANES):
                cands.append(a[r:r + 1] + b[0:SUBLANES])
            cands.append(a[SUBLANES:TOP_ROWS] + b[0:1])
            cand = jnp.concatenate(cands, axis=0)
            c16, c17 = _largest_rows(cand, PEER_TOPK + 1)[PEER_TOPK - 1:]
            tau = 0.5 * (c16 + c17)
            m1, m2 = a[0:1], b[0:1]
            zsum = jnp.sum(jnp.where(cand >= tau, jnp.exp(cand - (m1 + m2)), 0.0), axis=0, keepdims=True)
            thr_s[tc, hd] = jnp.exp(tau - s1 - m2)
            e1_s[tc, hd] = jnp.exp(s1 - m1) * (GELU_OUT_SCALE / zsum)
            e2_s[tc, hd] = jnp.exp(s2 - m2)

        def score_blocks(g, _):
            for bi in range(SCORE_BLOCKS_PER_ITER):
                score_block(g * SCORE_BLOCKS_PER_ITER + bi)
            return 0

        lax.fori_loop(0, ntc * PEER_HEADS // SCORE_BLOCKS_PER_ITER, score_blocks, 0)
        acc_s[...] = jnp.zeros(acc_s.shape, F32)

    ngrp = ec // LANES
    key_rows = {}

    def key_row(ref, r, tc, hd):
        g8, j = divmod(r, SUBLANES)
        if (id(ref), g8, tc, hd) not in key_rows:
            base = pl.multiple_of(e * ngrp + g8 * SUBLANES, SUBLANES)
            key_rows[(id(ref), g8, tc, hd)] = ref[tc, hd, pl.ds(base, SUBLANES), :]
        return key_rows[(id(ref), g8, tc, hd)][j:j + 1, :]

    rows_per_piece = PIECE_PEER // LANES

    def activations(p):
        return jnp.dot(u_ref[p * PIECE_PEER:(p + 1) * PIECE_PEER, :], h_s[...], preferred_element_type=F32)

    def weigh(p, at):
        gact = at * (1.0 + lax.erf(at))
        for rl in range(rows_per_piece):
            r = p * rows_per_piece + rl
            for tc in range(ntc):
                w = None
                for hd in range(PEER_HEADS):
                    e2 = e2_s[tc, hd]
                    term = jnp.where(e2 >= key_row(thr_s, r, tc, hd), e2, 0.0) * key_row(e1_s, r, tc, hd)
                    w = term if w is None else w + term
                wa_s[r * LANES:(r + 1) * LANES, tc * LANES:(tc + 1) * LANES] = (
                    w * gact[rl * LANES:(rl + 1) * LANES, tc * LANES:(tc + 1) * LANES]).astype(BF16)

    def values(p):
        cols = slice(p * PIECE_PEER, (p + 1) * PIECE_PEER)
        acc_s[...] += jnp.dot(vt_ref[:, cols], wa_s[cols, :], preferred_element_type=F32)

    npiece = ec // PIECE_PEER
    ahead = [activations(p) for p in range(min(PIECE_LOOKAHEAD, npiece))]
    for p in range(npiece):
        at = ahead.pop(0)
        if p + PIECE_LOOKAHEAD < npiece:
            ahead.append(activations(p + PIECE_LOOKAHEAD))
        weigh(p, at)
        values(p)

    @pl.when(e == ne - 1)
    def _finish():
        g2 = mod_ref[...][5:6]
        xn = x_ref[...] + g2 * acc_s[...].T
        if final_norm:
            xn = _rms(xn, fg_ref[...])
        o_ref[...] = xn


def _peer_call(x2, mod, ng, wqt, keys, u, vt, fg, S, final_norm):
    T, D = x2.shape
    E = u.shape[0]
    tm, ec = TM_PEER, EC_PEER
    nb = S // tm
    ntc = tm // LANES
    kern = functools.partial(_peer_kernel, tm=tm, ec=ec, final_norm=final_norm)
    return pl.pallas_call(
        kern,
        out_shape=jax.ShapeDtypeStruct((T, D), F32),
        grid=(T // tm, E // ec),
        in_specs=[pl.BlockSpec((tm, D), lambda i, e: (i, 0)),
                  pl.BlockSpec((None, 6, D), lambda i, e: (i // nb, 0, 0)),
                  _const_spec((1, D)), _const_spec(wqt.shape), _const_spec(keys.shape),
                  pl.BlockSpec((ec, D), lambda i, e: (e, 0)),
                  pl.BlockSpec((D, ec), lambda i, e: (0, e)),
                  _const_spec((1, D))],
        out_specs=pl.BlockSpec((tm, D), lambda i, e: (i, 0)),
        scratch_shapes=[pltpu.VMEM((D, tm), BF16),
                        pltpu.VMEM((ntc, 2 * PEER_HEADS, N_KEYS, LANES), F32),
                        pltpu.VMEM((ntc, PEER_HEADS, N_KEYS, LANES), F32),
                        pltpu.VMEM((ntc, PEER_HEADS, N_KEYS, LANES), F32),
                        pltpu.VMEM((ntc, PEER_HEADS, N_KEYS, LANES), F32),
                        pltpu.VMEM((ec, tm), BF16),
                        pltpu.VMEM((D, tm), F32)],
        compiler_params=_cparams(("arbitrary", "arbitrary")),
        name="peer_layer",
    )(x2, mod, ng, wqt, keys, u, vt, fg)


def _head_pad_cols(w_nope, w_rope, n_in):
    zeros = jnp.zeros((n_in, MLA_HEADS, HEAD_PAD - QK_NOPE - QK_ROPE), w_nope.dtype)
    return jnp.concatenate([w_nope, w_rope, zeros], axis=-1).reshape(n_in, MLA_HEADS * HEAD_PAD)


def _swap_halves(w):
    half = QK_ROPE // 2
    return jnp.concatenate([w[..., half:], w[..., :half]], axis=-1)


def kernel(x, c, positions, ada_w, ada_b, norm_mix_g, norm_ffn_g, lru_w_in, lru_conv_w, lru_conv_b, lru_wa, lru_ba, lru_wx, lru_bx, lru_lambda, lru_w_out, kv_ada_w, kv_ada_b, kv_norm_g, mla_w_dkv, mla_w_kr, mla_kv_latent_g, mla_w_uk, mla_w_uv, mla_w_dq, mla_q_latent_g, mla_w_uq, mla_w_o, peer_w_q, peer_sub_keys, peer_u, peer_v, final_g):
    B, S, D = x.shape
    T = B * S
    assert D == D_MODEL and S % TM_TOK == 0 and S % TS_LRU == 0 and S % TM_PEER == 0
    assert S % TQ_ATT == 0 and S % TK_ATT == 0

    c_pad = jnp.zeros((SUBLANES, D), F32).at[:B].set(c)
    mod_all = _mod_call(c_pad, ada_w, ada_b[:, None, :], 1536)
    mod_all = mod_all[:, :B].reshape(DEPTH, B, 6, D)
    mod_kv = _mod_call(c_pad, kv_ada_w[None], kv_ada_b[None, None, :], 1024)[0, :B].reshape(B, 2, D)

    half = QK_ROPE // 2
    freqs = ROPE_THETA ** (-jnp.arange(half, dtype=F32) / half)
    freq_lane = jnp.zeros((1, LANES), F32).at[0, QK_NOPE:QK_NOPE + QK_ROPE].set(jnp.concatenate([freqs, freqs]))
    pos_col = positions.reshape(T, 1)
    ca, sb = _rope_call(pos_col, freq_lane)

    nq, nk = S // TQ_ATT, S // KSUB_ATT
    pq = positions.reshape(B, nq, TQ_ATT)
    pk = positions.reshape(B, nk, KSUB_ATT)
    qmin, qmax = pq.min(-1), pq.max(-1)
    kmin, kmax = pk.min(-1), pk.max(-1)
    needed = kmin[:, None, :] <= qmax[:, :, None]
    nhi = jnp.max(jnp.where(needed, jnp.arange(1, nk + 1, dtype=jnp.int32), 0), axis=-1)
    full = kmax[:, None, :] <= qmin[:, :, None]
    nfull = jnp.min(jnp.where(full, nk, jnp.arange(nk, dtype=jnp.int32)), axis=-1)
    nfull = jnp.minimum(nfull, nhi).astype(jnp.int32).reshape(-1)
    nhi = nhi.astype(jnp.int32).reshape(-1)
    pos_row = positions.reshape(B, 1, S)

    def peer_layer(x2, l, final_norm):
        wqt = peer_w_q[l].T.astype(BF16)
        keys = peer_sub_keys[l].reshape(2 * PEER_HEADS, N_KEYS, PEER_HALF).astype(BF16)
        u = (peer_u[l] * (1.0 / math.sqrt(2.0))).astype(BF16)
        vt = peer_v[l].T.astype(BF16)
        return _peer_call(x2, mod_all[l], norm_ffn_g[l][None], wqt, keys, u, vt, final_g[None], S, final_norm)

    for l in range(N_A_LAYERS):
        x = _lru_call(x, mod_all[l], norm_mix_g[l][None], lru_w_in[l].astype(BF16), lru_conv_w[l],
                      lru_conv_b[l][None], lru_wa[l].astype(BF16), lru_ba[l][None], lru_wx[l].astype(BF16),
                      lru_bx[l][None], lru_lambda[l][None], lru_w_out[l].astype(BF16))
        x = peer_layer(x.reshape(T, D), l, False).reshape(B, S, D)

    x2 = x.reshape(T, D)

    zc = jnp.zeros((D, QK_NOPE), F32)
    zt = jnp.zeros((D, HEAD_PAD - QK_NOPE - QK_ROPE), F32)
    w1 = jnp.concatenate([mla_w_dkv, zc, mla_w_kr, zt, zc, _swap_halves(mla_w_kr), zt], axis=-1).astype(BF16)
    wuk = jnp.concatenate([mla_w_uk.reshape(KV_LORA, MLA_HEADS, QK_NOPE),
                           jnp.zeros((KV_LORA, MLA_HEADS, HEAD_PAD - QK_NOPE), F32)], axis=-1)
    wuk = wuk.reshape(KV_LORA, MLA_HEADS * HEAD_PAD).astype(BF16)
    k_all, v_all = _kv_call(x2, mod_kv, kv_norm_g[None], w1, mla_kv_latent_g[None], wuk,
                            mla_w_uv.astype(BF16), ca, sb, S)

    for j in range(DEPTH - N_A_LAYERS):
        l = N_A_LAYERS + j
        wq = mla_w_uq[j].reshape(Q_LORA, MLA_HEADS, QK_NOPE + QK_ROPE)
        w_nope, w_rope = wq[..., :QK_NOPE], wq[..., QK_NOPE:]
        wa = _head_pad_cols(w_nope, w_rope, Q_LORA).astype(BF16)
        wb = _head_pad_cols(jnp.zeros_like(w_nope), _swap_halves(w_rope), Q_LORA).astype(BF16)
        q = _q_call(x2, mod_all[l], norm_mix_g[l][None], mla_w_dq[j].astype(BF16), mla_q_latent_g[j][None],
                    wa, wb, ca, sb, S)
        o = _attn_call(nfull, nhi, q, k_all, v_all, pos_col, pos_row, B, S)
        x2 = _oproj_call(x2, o, mod_all[l], mla_w_o[j].astype(BF16), S)
        x2 = peer_layer(x2, l, l == DEPTH - 1)

    return x2.reshape(B, S, D)
```

```python
import functools
import math

import jax
import jax.numpy as jnp
from jax import lax
from jax.experimental import pallas as pl
from jax.experimental.pallas import tpu as pltpu

F32 = jnp.float32
BF16 = jnp.bfloat16

D_MODEL = 1024
DEPTH = 4
N_A_LAYERS = DEPTH // 2
LRU_HEADS = 4
LRU_BLOCK = D_MODEL // LRU_HEADS
CONV_WIDTH = 4
LRU_C = 8.0
MLA_HEADS = 16
QK_NOPE = 64
QK_ROPE = 32
V_HEAD = 64
Q_LORA = 384
KV_LORA = 256
ROPE_THETA = 10000.0
ATTN_SCALE = 1.0 / math.sqrt(QK_NOPE + QK_ROPE)
PEER_HEADS = 8
N_KEYS = 128
PEER_TOPK = 16
PEER_HALF = 128
RMS_EPS = 1e-6
NEG_INF = -1e30
LOG2E = math.log2(math.e)
GELU_OUT_SCALE = 0.5 * math.sqrt(2.0)

LANES = 128
SUBLANES = 8
HEAD_PAD = LANES
VMEM_LIMIT = 56 * 1024 * 1024

TS_LRU = 256
TM_TOK = 512
TQ_ATT = 512
TK_ATT = 2048
KSUB_ATT = 512
QSUB_ATT = 256
SCORE_LOOKAHEAD = 4
ONES_ROWS = 16
TM_PEER = 512
EC_PEER = 2048
PIECE_PEER = 256
PIECE_LOOKAHEAD = 4


def _cparams(sem):
    return pltpu.CompilerParams(dimension_semantics=sem, vmem_limit_bytes=VMEM_LIMIT)


def _rms(x, g):
    return x * lax.rsqrt(jnp.mean(x * x, axis=-1, keepdims=True) + RMS_EPS) * g


def _gelu(x):
    return 0.5 * x * (1.0 + lax.erf(x * (1.0 / math.sqrt(2.0))))


def _expm1(y):
    series = y * (1.0 + 0.5 * y * (1.0 + (1.0 / 3.0) * y * (1.0 + 0.25 * y * (1.0 + 0.2 * y))))
    return jnp.where(jnp.abs(y) < 0.05, series, jnp.exp(y) - 1.0)


def _const_spec(shape):
    nd = len(shape)
    return pl.BlockSpec(shape, lambda *_: (0,) * nd)


def _mod_kernel(c_ref, w_ref, b_ref, o_ref):
    c = c_ref[...]
    sc = c * jax.nn.sigmoid(c)
    o_ref[...] = jnp.dot(sc.astype(BF16), w_ref[...].astype(BF16),
                         preferred_element_type=F32) + b_ref[...]


def _mod_call(c_pad, w, b, tn):
    L, D, N = w.shape
    return pl.pallas_call(
        _mod_kernel,
        out_shape=jax.ShapeDtypeStruct((L, SUBLANES, N), F32),
        grid=(L, N // tn),
        in_specs=[pl.BlockSpec((SUBLANES, D), lambda l, n: (0, 0)),
                  pl.BlockSpec((None, D, tn), lambda l, n: (l, 0, n)),
                  pl.BlockSpec((None, 1, tn), lambda l, n: (l, 0, n))],
        out_specs=pl.BlockSpec((None, SUBLANES, tn), lambda l, n: (l, 0, n)),
        compiler_params=_cparams(("arbitrary", "arbitrary")),
        name="adaln_mod",
    )(c_pad, w, b)


def _rope_kernel(pos_ref, freq_ref, ca_ref, sb_ref):
    ang = pos_ref[...].astype(F32) * freq_ref[...]
    lane = lax.broadcasted_iota(jnp.int32, ang.shape, 1)
    cos, sin = jnp.cos(ang), jnp.sin(ang)
    half = QK_ROPE // 2
    ca_ref[...] = jnp.where(lane < QK_NOPE, 1.0, jnp.where(lane < QK_NOPE + QK_ROPE, cos, 0.0))
    sb_ref[...] = jnp.where(lane < QK_NOPE, 0.0,
                            jnp.where(lane < QK_NOPE + half, -sin,
                                      jnp.where(lane < QK_NOPE + QK_ROPE, sin, 0.0)))


def _rope_call(pos_col, freq_lane):
    T = pos_col.shape[0]
    tm = TM_TOK
    spec = pl.BlockSpec((tm, LANES), lambda i: (i, 0))
    return pl.pallas_call(
        _rope_kernel,
        out_shape=(jax.ShapeDtypeStruct((T, LANES), F32),) * 2,
        grid=(T // tm,),
        in_specs=[pl.BlockSpec((tm, 1), lambda i: (i, 0)), _const_spec((1, LANES))],
        out_specs=(spec, spec),
        compiler_params=_cparams(("arbitrary",)),
        name="rope_tables",
    )(pos_col, freq_lane)


def _shift_rows(x, d, fill):
    n = x.shape[0]
    if d % SUBLANES == 0:
        return jnp.concatenate([jnp.full((d, x.shape[1]), fill, x.dtype), x[:n - d]], axis=0)
    rolled = pltpu.roll(x, d, 0)
    row = lax.broadcasted_iota(jnp.int32, x.shape, 0)
    return jnp.where(row < d, fill, rolled)


def _lru_kernel(x_ref, mod_ref, ng_ref, win_ref, cw_ref, cb_ref, wa_ref, ba_ref, wx_ref, bx_ref,
                lam_ref, wout_ref, o_ref, rec_s, h_s, *, ts):
    D = D_MODEL

    @pl.when(pl.program_id(1) == 0)
    def _():
        rec_s[0:SUBLANES, :] = jnp.zeros((SUBLANES, D), F32)
        h_s[...] = jnp.zeros((SUBLANES, D), F32)

    x = x_ref[...]
    mod = mod_ref[...]
    sh1, sc1, g1 = mod[0:1], mod[1:2], mod[2:3]
    h = _rms(x, ng_ref[...]) * (1.0 + sc1) + sh1
    u = jnp.dot(h.astype(BF16), win_ref[...], preferred_element_type=F32)
    gate = _gelu(u[:, :D])
    rec = u[:, D:]

    rec_s[SUBLANES:ts + SUBLANES, :] = rec
    cw = cw_ref[...]
    xc = (cb_ref[...] + rec * cw[3:4]
          + rec_s[SUBLANES - 1:ts + SUBLANES - 1, :] * cw[2:3]
          + rec_s[SUBLANES - 2:ts + SUBLANES - 2, :] * cw[1:2]
          + rec_s[SUBLANES - 3:ts + SUBLANES - 3, :] * cw[0:1])
    rec_s[0:SUBLANES, :] = rec[ts - SUBLANES:ts, :]

    xcb = xc.astype(BF16)
    r_pre = jnp.concatenate(
        [jnp.dot(xcb[:, k * LRU_BLOCK:(k + 1) * LRU_BLOCK], wa_ref[k], preferred_element_type=F32)
         for k in range(LRU_HEADS)], axis=-1)
    i_pre = jnp.concatenate(
        [jnp.dot(xcb[:, k * LRU_BLOCK:(k + 1) * LRU_BLOCK], wx_ref[k], preferred_element_type=F32)
         for k in range(LRU_HEADS)], axis=-1)
    r = jax.nn.sigmoid(r_pre + ba_ref[...])
    gi = jax.nn.sigmoid(i_pre + bx_ref[...])
    z = -lam_ref[...]
    softplus = jnp.maximum(z, 0.0) + jnp.log1p(jnp.exp(-jnp.abs(z)))
    log_a = (-LRU_C) * r * softplus
    a = jnp.exp(log_a)
    b = jnp.sqrt(-_expm1(2.0 * log_a)) * (gi * xc)

    d = 1
    while d < ts:
        a_sh = _shift_rows(a, d, 1.0)
        b_sh = _shift_rows(b, d, 0.0)
        b = a * b_sh + b
        a = a * a_sh
        d *= 2
    h0 = h_s[SUBLANES - 1:SUBLANES, :]
    hs = a * h0 + b
    h_s[...] = hs[ts - SUBLANES:ts, :]

    y = jnp.dot((gate * hs).astype(BF16), wout_ref[...], preferred_element_type=F32)
    o_ref[...] = x + g1 * y


def _lru_call(x, mod, ng, w_in, cw, cb, wa, ba, wx, bx, lam, w_out):
    B, S, D = x.shape
    ts = TS_LRU
    kern = functools.partial(_lru_kernel, ts=ts)
    return pl.pallas_call(
        kern,
        out_shape=jax.ShapeDtypeStruct((B, S, D), F32),
        grid=(B, S // ts),
        in_specs=[pl.BlockSpec((None, ts, D), lambda b, s: (b, s, 0)),
                  pl.BlockSpec((None, 6, D), lambda b, s: (b, 0, 0)),
                  _const_spec((1, D)), _const_spec((D, 2 * D)), _const_spec((CONV_WIDTH, D)),
                  _const_spec((1, D)), _const_spec((LRU_HEADS, LRU_BLOCK, LRU_BLOCK)), _const_spec((1, D)),
                  _const_spec((LRU_HEADS, LRU_BLOCK, LRU_BLOCK)), _const_spec((1, D)), _const_spec((1, D)),
                  _const_spec((D, D))],
        out_specs=pl.BlockSpec((None, ts, D), lambda b, s: (b, s, 0)),
        scratch_shapes=[pltpu.VMEM((ts + SUBLANES, D), F32), pltpu.VMEM((SUBLANES, D), F32)],
        compiler_params=_cparams(("arbitrary", "arbitrary")),
        name="rglru_layer",
    )(x, mod, ng, w_in, cw, cb, wa, ba, wx, bx, lam, w_out)


def _kv_kernel(x_ref, mod_ref, ng_ref, w1_ref, lg_ref, wuk_ref, wuv_ref, ca_ref, sb_ref, k_ref, v_ref):
    x = x_ref[...]
    mod = mod_ref[...]
    shift, scale = mod[0:1], mod[1:2]
    h = _rms(x, ng_ref[...]) * (1.0 + scale) + shift
    t = jnp.dot(h.astype(BF16), w1_ref[...], preferred_element_type=F32)
    c_kv = _rms(t[:, :KV_LORA], lg_ref[...])
    kr = t[:, KV_LORA:KV_LORA + LANES] * ca_ref[...] + t[:, KV_LORA + LANES:] * sb_ref[...]
    ckb = c_kv.astype(BF16)
    kn = jnp.dot(ckb, wuk_ref[...], preferred_element_type=F32)
    k_ref[...] = (kn + jnp.concatenate([kr] * MLA_HEADS, axis=-1)).astype(BF16)
    v_ref[...] = jnp.dot(ckb, wuv_ref[...], preferred_element_type=F32).T.astype(BF16)


def _kv_call(x2, mod, ng, w1, lg, wuk, wuv, ca, sb, S):
    T, D = x2.shape
    tm = TM_TOK
    nb = S // tm
    HP = MLA_HEADS * HEAD_PAD
    HV = MLA_HEADS * V_HEAD
    return pl.pallas_call(
        _kv_kernel,
        out_shape=(jax.ShapeDtypeStruct((T, HP), BF16), jax.ShapeDtypeStruct((HV, T), BF16)),
        grid=(T // tm,),
        in_specs=[pl.BlockSpec((tm, D), lambda i: (i, 0)),
                  pl.BlockSpec((None, 2, D), lambda i: (i // nb, 0, 0)),
                  _const_spec((1, D)), _const_spec(w1.shape), _const_spec((1, KV_LORA)),
                  _const_spec(wuk.shape), _const_spec(wuv.shape),
                  pl.BlockSpec((tm, LANES), lambda i: (i, 0)), pl.BlockSpec((tm, LANES), lambda i: (i, 0))],
        out_specs=(pl.BlockSpec((tm, HP), lambda i: (i, 0)), pl.BlockSpec((HV, tm), lambda i: (0, i))),
        compiler_params=_cparams(("arbitrary",)),
        name="shared_kv",
    )(x2, mod, ng, w1, lg, wuk, wuv, ca, sb)


def _q_kernel(x_ref, mod_ref, ng_ref, wdq_ref, lg_ref, wa_ref, wb_ref, ca_ref, sb_ref, q_ref):
    x = x_ref[...]
    mod = mod_ref[...]
    sh1, sc1 = mod[0:1], mod[1:2]
    h = _rms(x, ng_ref[...]) * (1.0 + sc1) + sh1
    ql = _rms(jnp.dot(h.astype(BF16), wdq_ref[...], preferred_element_type=F32), lg_ref[...])
    qlb = ql.astype(BF16)
    qa = jnp.dot(qlb, wa_ref[...], preferred_element_type=F32)
    qb = jnp.dot(qlb, wb_ref[...], preferred_element_type=F32)
    ca = jnp.concatenate([ca_ref[...]] * MLA_HEADS, axis=-1)
    sb = jnp.concatenate([sb_ref[...]] * MLA_HEADS, axis=-1)
    q_ref[...] = ((qa * ca + qb * sb) * (ATTN_SCALE * LOG2E)).T.astype(BF16)


def _q_call(x2, mod, ng, wdq, lg, wa, wb, ca, sb, S):
    T, D = x2.shape
    tm = TM_TOK
    nb = S // tm
    HP = MLA_HEADS * HEAD_PAD
    return pl.pallas_call(
        _q_kernel,
        out_shape=jax.ShapeDtypeStruct((HP, T), BF16),
        grid=(T // tm,),
        in_specs=[pl.BlockSpec((tm, D), lambda i: (i, 0)),
                  pl.BlockSpec((None, 6, D), lambda i: (i // nb, 0, 0)),
                  _const_spec((1, D)), _const_spec(wdq.shape), _const_spec((1, Q_LORA)),
                  _const_spec(wa.shape), _const_spec(wb.shape),
                  pl.BlockSpec((tm, LANES), lambda i: (i, 0)), pl.BlockSpec((tm, LANES), lambda i: (i, 0))],
        out_specs=pl.BlockSpec((HP, tm), lambda i: (0, i)),
        compiler_params=_cparams(("arbitrary",)),
        name="mla_queries",
    )(x2, mod, ng, wdq, lg, wa, wb, ca, sb)


def _col_max(x):
    while x.shape[0] > SUBLANES:
        h = x.shape[0] // 2
        x = jnp.maximum(x[:h], x[h:])
    return jnp.max(x, axis=0, keepdims=True)


def _attn_kernel(nfull_ref, nhi_ref, qt_ref, k_ref, vt_ref, pq_ref, pk_ref, o_ref, *, tq, tk, nq):
    b = pl.program_id(0)
    qi = pl.program_id(2)
    nfull = nfull_ref[b * nq + qi]
    nhi = nhi_ref[b * nq + qi]
    pos_q = pq_ref[...]

    def step(kc, carry, masked, tk):
        nstate = 2 * (tq // QSUB_ATT)
        nstream = nstate * (tk // KSUB_ATT)
        state = list(carry)

        def split(sidx):
            kb, sid = divmod(sidx, nstate)
            hh, qs = divmod(sid, tq // QSUB_ATT)
            return pl.multiple_of(kc * tk + kb * KSUB_ATT, KSUB_ATT), sid, hh, qs

        def scores(sidx):
            off, _, hh, qs = split(sidx)
            k = k_ref[pl.ds(off, KSUB_ATT), hh * HEAD_PAD:(hh + 1) * HEAD_PAD]
            qt = qt_ref[hh * HEAD_PAD:(hh + 1) * HEAD_PAD, qs * QSUB_ATT:(qs + 1) * QSUB_ATT]
            return jnp.dot(k, qt, preferred_element_type=F32)

        def update(sidx, st):
            off, sid, hh, qs = split(sidx)
            m, acc = state[sid]
            if masked:
                keep = pk_ref[pl.ds(off, KSUB_ATT), :] <= pos_q[:, qs * QSUB_ATT:(qs + 1) * QSUB_ATT]
                st = jnp.where(keep, st, NEG_INF)
            m_new = jnp.maximum(m, _col_max(st))
            p = jnp.exp2((st - m_new).astype(BF16))
            alpha = jnp.exp2(m - m_new)
            vt = jnp.concatenate([vt_ref[hh * V_HEAD:(hh + 1) * V_HEAD, pl.ds(off, KSUB_ATT)],
                                  jnp.ones((ONES_ROWS, KSUB_ATT), BF16)], axis=0)
            state[sid] = (m_new, alpha * acc + jnp.dot(vt, p, preferred_element_type=F32))

        ahead = [scores(s) for s in range(min(SCORE_LOOKAHEAD, nstream))]
        for sidx in range(nstream):
            st_cur = ahead.pop(0)
            if sidx + SCORE_LOOKAHEAD < nstream:
                ahead.append(scores(sidx + SCORE_LOOKAHEAD))
            update(sidx, st_cur)
        return tuple(state)

    nqs = tq // QSUB_ATT
    one = (jnp.full((1, QSUB_ATT), NEG_INF, F32), jnp.zeros((V_HEAD + ONES_ROWS, QSUB_ATT), F32))
    per = tk // KSUB_ATT
    nwide = nfull // per
    carry = lax.fori_loop(0, nwide, functools.partial(step, masked=False, tk=tk), (one,) * (2 * nqs))
    npair = (nhi - nwide * per) // 2
    carry = lax.fori_loop(nwide * (per // 2), nwide * (per // 2) + npair,
                          functools.partial(step, masked=True, tk=2 * KSUB_ATT), carry)
    fin = lax.fori_loop(nwide * per + 2 * npair, nhi, functools.partial(step, masked=True, tk=KSUB_ATT), carry)
    heads = [jnp.concatenate([fin[hh * nqs + qs][1][:V_HEAD] / fin[hh * nqs + qs][1][V_HEAD:V_HEAD + 1]
                              for qs in range(nqs)], axis=1) for hh in range(2)]
    o_ref[...] = jnp.concatenate(heads, axis=0).T.astype(BF16)


def _attn_call(nfull, nhi, qt, k, vt, pos_col, pos_row, B, S):
    tq, tk = TQ_ATT, TK_ATT
    nq = S // tq
    T = B * S
    kern = functools.partial(_attn_kernel, tq=tq, tk=tk, nq=nq)
    grid_spec = pltpu.PrefetchScalarGridSpec(
        num_scalar_prefetch=2,
        grid=(B, MLA_HEADS // 2, nq),
        in_specs=[pl.BlockSpec((2 * HEAD_PAD, tq), lambda b, h, i, *_: (h, b * nq + i)),
                  pl.BlockSpec((S, 2 * HEAD_PAD), lambda b, h, i, *_: (b, h)),
                  pl.BlockSpec((2 * V_HEAD, S), lambda b, h, i, *_: (h, b)),
                  pl.BlockSpec((None, 1, tq), lambda b, h, i, *_: (b, 0, i)),
                  pl.BlockSpec((S, 1), lambda b, h, i, *_: (b, 0))],
        out_specs=pl.BlockSpec((tq, 2 * V_HEAD), lambda b, h, i, *_: (b * nq + i, h)),
    )
    return pl.pallas_call(
        kern,
        out_shape=jax.ShapeDtypeStruct((T, MLA_HEADS * V_HEAD), BF16),
        grid_spec=grid_spec,
        compiler_params=_cparams(("arbitrary", "arbitrary", "arbitrary")),
        name="mla_attention",
    )(nfull, nhi, qt, k, vt, pos_row, pos_col)


def _oproj_kernel(x_ref, o_ref, mod_ref, w_ref, out_ref):
    g1 = mod_ref[...][2:3]
    y = jnp.dot(o_ref[...], w_ref[...], preferred_element_type=F32)
    out_ref[...] = x_ref[...] + g1 * y


def _oproj_call(x2, o, mod, w_o, S):
    T, D = x2.shape
    tm = TM_TOK
    nb = S // tm
    return pl.pallas_call(
        _oproj_kernel,
        out_shape=jax.ShapeDtypeStruct((T, D), F32),
        grid=(T // tm,),
        in_specs=[pl.BlockSpec((tm, D), lambda i: (i, 0)),
                  pl.BlockSpec((tm, o.shape[1]), lambda i: (i, 0)),
                  pl.BlockSpec((None, 6, D), lambda i: (i // nb, 0, 0)),
                  _const_spec(w_o.shape)],
        out_specs=pl.BlockSpec((tm, D), lambda i: (i, 0)),
        compiler_params=_cparams(("arbitrary",)),
        name="mla_out_proj",
    )(x2, o, mod, w_o)


TOP_ROWS = 3 * SUBLANES
SCORE_BLOCKS_PER_ITER = 4


def _merge_sort_network(n):
    pairs, p = [], 1
    while p < n:
        k = p
        while k >= 1:
            for j in range(k % p, n - k, 2 * k):
                for i in range(min(k, n - j - k)):
                    if (i + j) // (2 * p) == (i + j + k) // (2 * p):
                        pairs.append((i + j, i + j + k))
            k //= 2
        p *= 2
    return pairs


def _largest_rows(x, count):
    v = [x[r * SUBLANES:(r + 1) * SUBLANES, :] for r in range(x.shape[0] // SUBLANES)]
    size = 1 << (len(v) - 1).bit_length()
    v = v + [None] * (size - len(v))
    for i, j in _merge_sort_network(size):
        if v[j] is None:
            continue
        if v[i] is None:
            v[i], v[j] = v[j], None
        else:
            v[i], v[j] = jnp.maximum(v[i], v[j]), jnp.minimum(v[i], v[j])
    v = [t for t in v if t is not None]
    tops = []
    for j in range(count):
        m = jnp.max(v[0], axis=0, keepdims=True)
        tops.append(m)
        if j + 1 < count:
            won = v[0] == m
            depth = min(len(v), count - j)
            v = [jnp.where(won, v[r + 1] if r + 1 < len(v) else NEG_INF, v[r]) for r in range(depth)]
    return tops


def _top17_rows(x):
    pad = jnp.full((TOP_ROWS - PEER_TOPK - 1, x.shape[1]), NEG_INF, F32)
    return jnp.concatenate(_largest_rows(x, PEER_TOPK + 1) + [pad], axis=0)


def _peer_kernel(x_ref, mod_ref, ng_ref, wqt_ref, keys_ref, u_ref, vt_ref, fg_ref, o_ref,
                 h_s, st_s, thr_s, e1_s, e2_s, wa_s, acc_s, *, tm, ec, final_norm):
    ntc = tm // LANES
    e = pl.program_id(1)
    ne = pl.num_programs(1)

    @pl.when(e == 0)
    def _scores():
        x = x_ref[...]
        mod = mod_ref[...]
        sh2, sc2 = mod[3:4], mod[4:5]
        ht = (_rms(x, ng_ref[...]) * (1.0 + sc2) + sh2).T.astype(BF16)
        h_s[...] = ht
        qtb = jnp.dot(wqt_ref[...], ht, preferred_element_type=F32).astype(BF16)
        for hp in range(2 * PEER_HEADS):
            st = jnp.dot(keys_ref[hp], qtb[hp * PEER_HALF:(hp + 1) * PEER_HALF, :],
                         preferred_element_type=F32)
            for tc in range(ntc):
                st_s[tc, hp] = st[:, tc * LANES:(tc + 1) * LANES]

        def score_block(i):
            tc = i // PEER_HEADS
            hd = i % PEER_HEADS
            s1 = st_s[tc, 2 * hd]
            s2 = st_s[tc, 2 * hd + 1]
            a = _top17_rows(s1)
            b = _top17_rows(s2)
            cands = [a[0:1] + b]
            for r in range(1, SUBLANES):
                cands.append(a[r:r + 1] + b[0:SUBLANES])
            cands.append(a[SUBLANES:TOP_ROWS] + b[0:1])
            cand = jnp.concatenate(cands, axis=0)
            c16, c17 = _largest_rows(cand, PEER_TOPK + 1)[PEER_TOPK - 1:]
            tau = 0.5 * (c16 + c17)
            m1, m2 = a[0:1], b[0:1]
            zsum = jnp.sum(jnp.where(cand >= tau, jnp.exp(cand - (m1 + m2)), 0.0), axis=0, keepdims=True)
            thr_s[tc, hd] = jnp.exp(tau - s1 - m2)
            e1_s[tc, hd] = jnp.exp(s1 - m1) * (GELU_OUT_SCALE / zsum)
            e2_s[tc, hd] = jnp.exp(s2 - m2)

        def score_blocks(g, _):
            for bi in range(SCORE_BLOCKS_PER_ITER):
                score_block(g * SCORE_BLOCKS_PER_ITER + bi)
            return 0

        lax.fori_loop(0, ntc * PEER_HEADS // SCORE_BLOCKS_PER_ITER, score_blocks, 0)
        acc_s[...] = jnp.zeros(acc_s.shape, F32)

    ngrp = ec // LANES
    key_rows = {}

    def key_row(ref, r, tc, hd):
        g8, j = divmod(r, SUBLANES)
        if (id(ref), g8, tc, hd) not in key_rows:
            base = pl.multiple_of(e * ngrp + g8 * SUBLANES, SUBLANES)
            key_rows[(id(ref), g8, tc, hd)] = ref[tc, hd, pl.ds(base, SUBLANES), :]
        return key_rows[(id(ref), g8, tc, hd)][j:j + 1, :]

    rows_per_piece = PIECE_PEER // LANES

    def activations(p):
        return jnp.dot(u_ref[p * PIECE_PEER:(p + 1) * PIECE_PEER, :], h_s[...], preferred_element_type=F32)

    def weigh(p, at):
        gact = at * (1.0 + lax.erf(at))
        for rl in range(rows_per_piece):
            r = p * rows_per_piece + rl
            for tc in range(ntc):
                w = None
                for hd in range(PEER_HEADS):
                    e2 = e2_s[tc, hd]
                    term = jnp.where(e2 >= key_row(thr_s, r, tc, hd), e2, 0.0) * key_row(e1_s, r, tc, hd)
                    w = term if w is None else w + term
                wa_s[r * LANES:(r + 1) * LANES, tc * LANES:(tc + 1) * LANES] = (
                    w * gact[rl * LANES:(rl + 1) * LANES, tc * LANES:(tc + 1) * LANES]).astype(BF16)

    def values(p):
        cols = slice(p * PIECE_PEER, (p + 1) * PIECE_PEER)
        acc_s[...] += jnp.dot(vt_ref[:, cols], wa_s[cols, :], preferred_element_type=F32)

    npiece = ec // PIECE_PEER
    ahead = [activations(p) for p in range(min(PIECE_LOOKAHEAD, npiece))]
    for p in range(npiece):
        at = ahead.pop(0)
        if p + PIECE_LOOKAHEAD < npiece:
            ahead.append(activations(p + PIECE_LOOKAHEAD))
        weigh(p, at)
        values(p)

    @pl.when(e == ne - 1)
    def _finish():
        g2 = mod_ref[...][5:6]
        xn = x_ref[...] + g2 * acc_s[...].T
        if final_norm:
            xn = _rms(xn, fg_ref[...])
        o_ref[...] = xn


def _peer_call(x2, mod, ng, wqt, keys, u, vt, fg, S, final_norm):
    T, D = x2.shape
    E = u.shape[0]
    tm, ec = TM_PEER, EC_PEER
    nb = S // tm
    ntc = tm // LANES
    kern = functools.partial(_peer_kernel, tm=tm, ec=ec, final_norm=final_norm)
    return pl.pallas_call(
        kern,
        out_shape=jax.ShapeDtypeStruct((T, D), F32),
        grid=(T // tm, E // ec),
        in_specs=[pl.BlockSpec((tm, D), lambda i, e: (i, 0)),
                  pl.BlockSpec((None, 6, D), lambda i, e: (i // nb, 0, 0)),
                  _const_spec((1, D)), _const_spec(wqt.shape), _const_spec(keys.shape),
                  pl.BlockSpec((ec, D), lambda i, e: (e, 0)),
                  pl.BlockSpec((D, ec), lambda i, e: (0, e)),
                  _const_spec((1, D))],
        out_specs=pl.BlockSpec((tm, D), lambda i, e: (i, 0)),
        scratch_shapes=[pltpu.VMEM((D, tm), BF16),
                        pltpu.VMEM((ntc, 2 * PEER_HEADS, N_KEYS, LANES), F32),
                        pltpu.VMEM((ntc, PEER_HEADS, N_KEYS, LANES), F32),
                        pltpu.VMEM((ntc, PEER_HEADS, N_KEYS, LANES), F32),
                        pltpu.VMEM((ntc, PEER_HEADS, N_KEYS, LANES), F32),
                        pltpu.VMEM((ec, tm), BF16),
                        pltpu.VMEM((D, tm), F32)],
        compiler_params=_cparams(("arbitrary", "arbitrary")),
        name="peer_layer",
    )(x2, mod, ng, wqt, keys, u, vt, fg)


def _head_pad_cols(w_nope, w_rope, n_in):
    zeros = jnp.zeros((n_in, MLA_HEADS, HEAD_PAD - QK_NOPE - QK_ROPE), w_nope.dtype)
    return jnp.concatenate([w_nope, w_rope, zeros], axis=-1).reshape(n_in, MLA_HEADS * HEAD_PAD)


def _swap_halves(w):
    half = QK_ROPE // 2
    return jnp.concatenate([w[..., half:], w[..., :half]], axis=-1)


def kernel(x, c, positions, ada_w, ada_b, norm_mix_g, norm_ffn_g, lru_w_in, lru_conv_w, lru_conv_b, lru_wa, lru_ba, lru_wx, lru_bx, lru_lambda, lru_w_out, kv_ada_w, kv_ada_b, kv_norm_g, mla_w_dkv, mla_w_kr, mla_kv_latent_g, mla_w_uk, mla_w_uv, mla_w_dq, mla_q_latent_g, mla_w_uq, mla_w_o, peer_w_q, peer_sub_keys, peer_u, peer_v, final_g):
    B, S, D = x.shape
    T = B * S
    assert D == D_MODEL and S % TM_TOK == 0 and S % TS_LRU == 0 and S % TM_PEER == 0
    assert S % TQ_ATT == 0 and S % TK_ATT == 0

    c_pad = jnp.zeros((SUBLANES, D), F32).at[:B].set(c)
    mod_all = _mod_call(c_pad, ada_w, ada_b[:, None, :], 1536)
    mod_all = mod_all[:, :B].reshape(DEPTH, B, 6, D)
    mod_kv = _mod_call(c_pad, kv_ada_w[None], kv_ada_b[None, None, :], 1024)[0, :B].reshape(B, 2, D)

    half = QK_ROPE // 2
    freqs = ROPE_THETA ** (-jnp.arange(half, dtype=F32) / half)
    freq_lane = jnp.zeros((1, LANES), F32).at[0, QK_NOPE:QK_NOPE + QK_ROPE].set(jnp.concatenate([freqs, freqs]))
    pos_col = positions.reshape(T, 1)
    ca, sb = _rope_call(pos_col, freq_lane)

    nq, nk = S // TQ_ATT, S // KSUB_ATT
    pq = positions.reshape(B, nq, TQ_ATT)
    pk = positions.reshape(B, nk, KSUB_ATT)
    qmin, qmax = pq.min(-1), pq.max(-1)
    kmin, kmax = pk.min(-1), pk.max(-1)
    needed = kmin[:, None, :] <= qmax[:, :, None]
    nhi = jnp.max(jnp.where(needed, jnp.arange(1, nk + 1, dtype=jnp.int32), 0), axis=-1)
    full = kmax[:, None, :] <= qmin[:, :, None]
    nfull = jnp.min(jnp.where(full, nk, jnp.arange(nk, dtype=jnp.int32)), axis=-1)
    nfull = jnp.minimum(nfull, nhi).astype(jnp.int32).reshape(-1)
    nhi = nhi.astype(jnp.int32).reshape(-1)
    pos_row = positions.reshape(B, 1, S)

    def peer_layer(x2, l, final_norm):
        wqt = peer_w_q[l].T.astype(BF16)
        keys = peer_sub_keys[l].reshape(2 * PEER_HEADS, N_KEYS, PEER_HALF).astype(BF16)
        u = (peer_u[l] * (1.0 / math.sqrt(2.0))).astype(BF16)
        vt = peer_v[l].T.astype(BF16)
        return _peer_call(x2, mod_all[l], norm_ffn_g[l][None], wqt, keys, u, vt, final_g[None], S, final_norm)

    for l in range(N_A_LAYERS):
        x = _lru_call(x, mod_all[l], norm_mix_g[l][None], lru_w_in[l].astype(BF16), lru_conv_w[l],
                      lru_conv_b[l][None], lru_wa[l].astype(BF16), lru_ba[l][None], lru_wx[l].astype(BF16),
                      lru_bx[l][None], lru_lambda[l][None], lru_w_out[l].astype(BF16))
        x = peer_layer(x.reshape(T, D), l, False).reshape(B, S, D)

    x2 = x.reshape(T, D)

    zc = jnp.zeros((D, QK_NOPE), F32)
    zt = jnp.zeros((D, HEAD_PAD - QK_NOPE - QK_ROPE), F32)
    w1 = jnp.concatenate([mla_w_dkv, zc, mla_w_kr, zt, zc, _swap_halves(mla_w_kr), zt], axis=-1).astype(BF16)
    wuk = jnp.concatenate([mla_w_uk.reshape(KV_LORA, MLA_HEADS, QK_NOPE),
                           jnp.zeros((KV_LORA, MLA_HEADS, HEAD_PAD - QK_NOPE), F32)], axis=-1)
    wuk = wuk.reshape(KV_LORA, MLA_HEADS * HEAD_PAD).astype(BF16)
    k_all, v_all = _kv_call(x2, mod_kv, kv_norm_g[None], w1, mla_kv_latent_g[None], wuk,
                            mla_w_uv.astype(BF16), ca, sb, S)

    for j in range(DEPTH - N_A_LAYERS):
        l = N_A_LAYERS + j
        wq = mla_w_uq[j].reshape(Q_LORA, MLA_HEADS, QK_NOPE + QK_ROPE)
        w_nope, w_rope = wq[..., :QK_NOPE], wq[..., QK_NOPE:]
        wa = _head_pad_cols(w_nope, w_rope, Q_LORA).astype(BF16)
        wb = _head_pad_cols(jnp.zeros_like(w_nope), _swap_halves(w_rope), Q_LORA).astype(BF16)
        q = _q_call(x2, mod_all[l], norm_mix_g[l][None], mla_w_dq[j].astype(BF16), mla_q_latent_g[j][None],
                    wa, wb, ca, sb, S)
        o = _attn_call(nfull, nhi, q, k_all, v_all, pos_col, pos_row, B, S)
        x2 = _oproj_call(x2, o, mod_all[l], mla_w_o[j].astype(BF16), S)
        x2 = peer_layer(x2, l, l == DEPTH - 1)

    return x2.reshape(B, S, D)
```

```python
import functools
import math

import jax
import jax.numpy as jnp
from jax import lax
from jax.experimental import pallas as pl
from jax.experimental.pallas import tpu as pltpu

F32 = jnp.float32
BF16 = jnp.bfloat16

D_MODEL = 1024
DEPTH = 4
N_A_LAYERS = DEPTH // 2
LRU_HEADS = 4
LRU_BLOCK = D_MODEL // LRU_HEADS
CONV_WIDTH = 4
LRU_C = 8.0
MLA_HEADS = 16
QK_NOPE = 64
QK_ROPE = 32
V_HEAD = 64
Q_LORA = 384
KV_LORA = 256
ROPE_THETA = 10000.0
ATTN_SCALE = 1.0 / math.sqrt(QK_NOPE + QK_ROPE)
PEER_HEADS = 8
N_KEYS = 128
PEER_TOPK = 16
PEER_HALF = 128
RMS_EPS = 1e-6
NEG_INF = -1e30
LOG2E = math.log2(math.e)
GELU_OUT_SCALE = 0.5 * math.sqrt(2.0)

LANES = 128
SUBLANES = 8
HEAD_PAD = LANES
VMEM_LIMIT = 56 * 1024 * 1024

TS_LRU = 256
TM_TOK = 512
TQ_ATT = 1024
TK_ATT = 2048
KSUB_ATT = 512
QSUB_ATT = 256
SCORE_LOOKAHEAD = 4
ONES_ROWS = 16
TM_PEER = 512
EC_PEER = 2048
PIECE_PEER = 256
PIECE_LOOKAHEAD = 4


def _cparams(sem):
    return pltpu.CompilerParams(dimension_semantics=sem, vmem_limit_bytes=VMEM_LIMIT)


def _rms(x, g):
    return x * lax.rsqrt(jnp.mean(x * x, axis=-1, keepdims=True) + RMS_EPS) * g


def _gelu(x):
    return 0.5 * x * (1.0 + lax.erf(x * (1.0 / math.sqrt(2.0))))


def _expm1(y):
    series = y * (1.0 + 0.5 * y * (1.0 + (1.0 / 3.0) * y * (1.0 + 0.25 * y * (1.0 + 0.2 * y))))
    return jnp.where(jnp.abs(y) < 0.05, series, jnp.exp(y) - 1.0)


def _const_spec(shape):
    nd = len(shape)
    return pl.BlockSpec(shape, lambda *_: (0,) * nd)


def _mod_kernel(c_ref, w_ref, b_ref, o_ref):
    c = c_ref[...]
    sc = c * jax.nn.sigmoid(c)
    o_ref[...] = jnp.dot(sc.astype(BF16), w_ref[...].astype(BF16),
                         preferred_element_type=F32) + b_ref[...]


def _mod_call(c_pad, w, b, tn):
    L, D, N = w.shape
    return pl.pallas_call(
        _mod_kernel,
        out_shape=jax.ShapeDtypeStruct((L, SUBLANES, N), F32),
        grid=(L, N // tn),
        in_specs=[pl.BlockSpec((SUBLANES, D), lambda l, n: (0, 0)),
                  pl.BlockSpec((None, D, tn), lambda l, n: (l, 0, n)),
                  pl.BlockSpec((None, 1, tn), lambda l, n: (l, 0, n))],
        out_specs=pl.BlockSpec((None, SUBLANES, tn), lambda l, n: (l, 0, n)),
        compiler_params=_cparams(("arbitrary", "arbitrary")),
        name="adaln_mod",
    )(c_pad, w, b)


def _rope_kernel(pos_ref, freq_ref, ca_ref, sb_ref):
    ang = pos_ref[...].astype(F32) * freq_ref[...]
    lane = lax.broadcasted_iota(jnp.int32, ang.shape, 1)
    cos, sin = jnp.cos(ang), jnp.sin(ang)
    half = QK_ROPE // 2
    ca_ref[...] = jnp.where(lane < QK_NOPE, 1.0, jnp.where(lane < QK_NOPE + QK_ROPE, cos, 0.0))
    sb_ref[...] = jnp.where(lane < QK_NOPE, 0.0,
                            jnp.where(lane < QK_NOPE + half, -sin,
                                      jnp.where(lane < QK_NOPE + QK_ROPE, sin, 0.0)))


def _rope_call(pos_col, freq_lane):
    T = pos_col.shape[0]
    tm = TM_TOK
    spec = pl.BlockSpec((tm, LANES), lambda i: (i, 0))
    return pl.pallas_call(
        _rope_kernel,
        out_shape=(jax.ShapeDtypeStruct((T, LANES), F32),) * 2,
        grid=(T // tm,),
        in_specs=[pl.BlockSpec((tm, 1), lambda i: (i, 0)), _const_spec((1, LANES))],
        out_specs=(spec, spec),
        compiler_params=_cparams(("arbitrary",)),
        name="rope_tables",
    )(pos_col, freq_lane)


def _shift_rows(x, d, fill):
    n = x.shape[0]
    if d % SUBLANES == 0:
        return jnp.concatenate([jnp.full((d, x.shape[1]), fill, x.dtype), x[:n - d]], axis=0)
    rolled = pltpu.roll(x, d, 0)
    row = lax.broadcasted_iota(jnp.int32, x.shape, 0)
    return jnp.where(row < d, fill, rolled)


def _lru_kernel(x_ref, mod_ref, ng_ref, win_ref, cw_ref, cb_ref, wa_ref, ba_ref, wx_ref, bx_ref,
                lam_ref, wout_ref, o_ref, rec_s, h_s, *, ts):
    D = D_MODEL

    @pl.when(pl.program_id(1) == 0)
    def _():
        rec_s[0:SUBLANES, :] = jnp.zeros((SUBLANES, D), F32)
        h_s[...] = jnp.zeros((SUBLANES, D), F32)

    x = x_ref[...]
    mod = mod_ref[...]
    sh1, sc1, g1 = mod[0:1], mod[1:2], mod[2:3]
    h = _rms(x, ng_ref[...]) * (1.0 + sc1) + sh1
    u = jnp.dot(h.astype(BF16), win_ref[...], preferred_element_type=F32)
    gate = _gelu(u[:, :D])
    rec = u[:, D:]

    rec_s[SUBLANES:ts + SUBLANES, :] = rec
    cw = cw_ref[...]
    xc = (cb_ref[...] + rec * cw[3:4]
          + rec_s[SUBLANES - 1:ts + SUBLANES - 1, :] * cw[2:3]
          + rec_s[SUBLANES - 2:ts + SUBLANES - 2, :] * cw[1:2]
          + rec_s[SUBLANES - 3:ts + SUBLANES - 3, :] * cw[0:1])
    rec_s[0:SUBLANES, :] = rec[ts - SUBLANES:ts, :]

    xcb = xc.astype(BF16)
    r_pre = jnp.concatenate(
        [jnp.dot(xcb[:, k * LRU_BLOCK:(k + 1) * LRU_BLOCK], wa_ref[k], preferred_element_type=F32)
         for k in range(LRU_HEADS)], axis=-1)
    i_pre = jnp.concatenate(
        [jnp.dot(xcb[:, k * LRU_BLOCK:(k + 1) * LRU_BLOCK], wx_ref[k], preferred_element_type=F32)
         for k in range(LRU_HEADS)], axis=-1)
    r = jax.nn.sigmoid(r_pre + ba_ref[...])
    gi = jax.nn.sigmoid(i_pre + bx_ref[...])
    z = -lam_ref[...]
    softplus = jnp.maximum(z, 0.0) + jnp.log1p(jnp.exp(-jnp.abs(z)))
    log_a = (-LRU_C) * r * softplus
    a = jnp.exp(log_a)
    b = jnp.sqrt(-_expm1(2.0 * log_a)) * (gi * xc)

    d = 1
    while d < ts:
        a_sh = _shift_rows(a, d, 1.0)
        b_sh = _shift_rows(b, d, 0.0)
        b = a * b_sh + b
        a = a * a_sh
        d *= 2
    h0 = h_s[SUBLANES - 1:SUBLANES, :]
    hs = a * h0 + b
    h_s[...] = hs[ts - SUBLANES:ts, :]

    y = jnp.dot((gate * hs).astype(BF16), wout_ref[...], preferred_element_type=F32)
    o_ref[...] = x + g1 * y


def _lru_call(x, mod, ng, w_in, cw, cb, wa, ba, wx, bx, lam, w_out):
    B, S, D = x.shape
    ts = TS_LRU
    kern = functools.partial(_lru_kernel, ts=ts)
    return pl.pallas_call(
        kern,
        out_shape=jax.ShapeDtypeStruct((B, S, D), F32),
        grid=(B, S // ts),
        in_specs=[pl.BlockSpec((None, ts, D), lambda b, s: (b, s, 0)),
                  pl.BlockSpec((None, 6, D), lambda b, s: (b, 0, 0)),
                  _const_spec((1, D)), _const_spec((D, 2 * D)), _const_spec((CONV_WIDTH, D)),
                  _const_spec((1, D)), _const_spec((LRU_HEADS, LRU_BLOCK, LRU_BLOCK)), _const_spec((1, D)),
                  _const_spec((LRU_HEADS, LRU_BLOCK, LRU_BLOCK)), _const_spec((1, D)), _const_spec((1, D)),
                  _const_spec((D, D))],
        out_specs=pl.BlockSpec((None, ts, D), lambda b, s: (b, s, 0)),
        scratch_shapes=[pltpu.VMEM((ts + SUBLANES, D), F32), pltpu.VMEM((SUBLANES, D), F32)],
        compiler_params=_cparams(("arbitrary", "arbitrary")),
        name="rglru_layer",
    )(x, mod, ng, w_in, cw, cb, wa, ba, wx, bx, lam, w_out)


def _kv_kernel(x_ref, mod_ref, ng_ref, w1_ref, lg_ref, wuk_ref, wuv_ref, ca_ref, sb_ref, k_ref, v_ref):
    x = x_ref[...]
    mod = mod_ref[...]
    shift, scale = mod[0:1], mod[1:2]
    h = _rms(x, ng_ref[...]) * (1.0 + scale) + shift
    t = jnp.dot(h.astype(BF16), w1_ref[...], preferred_element_type=F32)
    c_kv = _rms(t[:, :KV_LORA], lg_ref[...])
    kr = t[:, KV_LORA:KV_LORA + LANES] * ca_ref[...] + t[:, KV_LORA + LANES:] * sb_ref[...]
    ckb = c_kv.astype(BF16)
    kn = jnp.dot(ckb, wuk_ref[...], preferred_element_type=F32)
    k_ref[...] = (kn + jnp.concatenate([kr] * MLA_HEADS, axis=-1)).astype(BF16)
    v_ref[...] = jnp.dot(ckb, wuv_ref[...], preferred_element_type=F32).T.astype(BF16)


def _kv_call(x2, mod, ng, w1, lg, wuk, wuv, ca, sb, S):
    T, D = x2.shape
    tm = TM_TOK
    nb = S // tm
    HP = MLA_HEADS * HEAD_PAD
    HV = MLA_HEADS * V_HEAD
    return pl.pallas_call(
        _kv_kernel,
        out_shape=(jax.ShapeDtypeStruct((T, HP), BF16), jax.ShapeDtypeStruct((HV, T), BF16)),
        grid=(T // tm,),
        in_specs=[pl.BlockSpec((tm, D), lambda i: (i, 0)),
                  pl.BlockSpec((None, 2, D), lambda i: (i // nb, 0, 0)),
                  _const_spec((1, D)), _const_spec(w1.shape), _const_spec((1, KV_LORA)),
                  _const_spec(wuk.shape), _const_spec(wuv.shape),
                  pl.BlockSpec((tm, LANES), lambda i: (i, 0)), pl.BlockSpec((tm, LANES), lambda i: (i, 0))],
        out_specs=(pl.BlockSpec((tm, HP), lambda i: (i, 0)), pl.BlockSpec((HV, tm), lambda i: (0, i))),
        compiler_params=_cparams(("arbitrary",)),
        name="shared_kv",
    )(x2, mod, ng, w1, lg, wuk, wuv, ca, sb)


def _q_kernel(x_ref, mod_ref, ng_ref, wdq_ref, lg_ref, wa_ref, wb_ref, ca_ref, sb_ref, q_ref):
    x = x_ref[...]
    mod = mod_ref[...]
    sh1, sc1 = mod[0:1], mod[1:2]
    h = _rms(x, ng_ref[...]) * (1.0 + sc1) + sh1
    ql = _rms(jnp.dot(h.astype(BF16), wdq_ref[...], preferred_element_type=F32), lg_ref[...])
    qlb = ql.astype(BF16)
    qa = jnp.dot(qlb, wa_ref[...], preferred_element_type=F32)
    qb = jnp.dot(qlb, wb_ref[...], preferred_element_type=F32)
    ca = jnp.concatenate([ca_ref[...]] * MLA_HEADS, axis=-1)
    sb = jnp.concatenate([sb_ref[...]] * MLA_HEADS, axis=-1)
    q_ref[...] = ((qa * ca + qb * sb) * (ATTN_SCALE * LOG2E)).T.astype(BF16)


def _q_call(x2, mod, ng, wdq, lg, wa, wb, ca, sb, S):
    T, D = x2.shape
    tm = TM_TOK
    nb = S // tm
    HP = MLA_HEADS * HEAD_PAD
    return pl.pallas_call(
        _q_kernel,
        out_shape=jax.ShapeDtypeStruct((HP, T), BF16),
        grid=(T // tm,),
        in_specs=[pl.BlockSpec((tm, D), lambda i: (i, 0)),
                  pl.BlockSpec((None, 6, D), lambda i: (i // nb, 0, 0)),
                  _const_spec((1, D)), _const_spec(wdq.shape), _const_spec((1, Q_LORA)),
                  _const_spec(wa.shape), _const_spec(wb.shape),
                  pl.BlockSpec((tm, LANES), lambda i: (i, 0)), pl.BlockSpec((tm, LANES), lambda i: (i, 0))],
        out_specs=pl.BlockSpec((HP, tm), lambda i: (0, i)),
        compiler_params=_cparams(("arbitrary",)),
        name="mla_queries",
    )(x2, mod, ng, wdq, lg, wa, wb, ca, sb)


def _col_max(x):
    while x.shape[0] > SUBLANES:
        h = x.shape[0] // 2
        x = jnp.maximum(x[:h], x[h:])
    return jnp.max(x, axis=0, keepdims=True)


def _attn_kernel(nfull_ref, nhi_ref, qt_ref, k_ref, vt_ref, pq_ref, pk_ref, o_ref, *, tq, tk, nq):
    b = pl.program_id(0)
    qi = pl.program_id(2)
    nfull = nfull_ref[b * nq + qi]
    nhi = nhi_ref[b * nq + qi]
    pos_q = pq_ref[...]

    def step(kc, carry, masked, tk):
        nstate = 2 * (tq // QSUB_ATT)
        nstream = nstate * (tk // KSUB_ATT)
        state = list(carry)

        def split(sidx):
            kb, sid = divmod(sidx, nstate)
            hh, qs = divmod(sid, tq // QSUB_ATT)
            return pl.multiple_of(kc * tk + kb * KSUB_ATT, KSUB_ATT), sid, hh, qs

        def scores(sidx):
            off, _, hh, qs = split(sidx)
            k = k_ref[pl.ds(off, KSUB_ATT), hh * HEAD_PAD:(hh + 1) * HEAD_PAD]
            qt = qt_ref[hh * HEAD_PAD:(hh + 1) * HEAD_PAD, qs * QSUB_ATT:(qs + 1) * QSUB_ATT]
            return jnp.dot(k, qt, preferred_element_type=F32)

        def update(sidx, st):
            off, sid, hh, qs = split(sidx)
            m, acc = state[sid]
            if masked:
                keep = pk_ref[pl.ds(off, KSUB_ATT), :] <= pos_q[:, qs * QSUB_ATT:(qs + 1) * QSUB_ATT]
                st = jnp.where(keep, st, NEG_INF)
            m_new = jnp.maximum(m, _col_max(st))
            p = jnp.exp2((st - m_new).astype(BF16))
            alpha = jnp.exp2(m - m_new)
            vt = jnp.concatenate([vt_ref[hh * V_HEAD:(hh + 1) * V_HEAD, pl.ds(off, KSUB_ATT)],
                                  jnp.ones((ONES_ROWS, KSUB_ATT), BF16)], axis=0)
            state[sid] = (m_new, alpha * acc + jnp.dot(vt, p, preferred_element_type=F32))

        ahead = [scores(s) for s in range(min(SCORE_LOOKAHEAD, nstream))]
        for sidx in range(nstream):
            st_cur = ahead.pop(0)
            if sidx + SCORE_LOOKAHEAD < nstream:
                ahead.append(scores(sidx + SCORE_LOOKAHEAD))
            update(sidx, st_cur)
        return tuple(state)

    nqs = tq // QSUB_ATT
    one = (jnp.full((1, QSUB_ATT), NEG_INF, F32), jnp.zeros((V_HEAD + ONES_ROWS, QSUB_ATT), F32))
    per = tk // KSUB_ATT
    nwide = nfull // per
    carry = lax.fori_loop(0, nwide, functools.partial(step, masked=False, tk=tk), (one,) * (2 * nqs))
    npair = (nhi - nwide * per) // 2
    carry = lax.fori_loop(nwide * (per // 2), nwide * (per // 2) + npair,
                          functools.partial(step, masked=True, tk=2 * KSUB_ATT), carry)
    fin = lax.fori_loop(nwide * per + 2 * npair, nhi, functools.partial(step, masked=True, tk=KSUB_ATT), carry)
    heads = [jnp.concatenate([fin[hh * nqs + qs][1][:V_HEAD] / fin[hh * nqs + qs][1][V_HEAD:V_HEAD + 1]
                              for qs in range(nqs)], axis=1) for hh in range(2)]
    o_ref[...] = jnp.concatenate(heads, axis=0).T.astype(BF16)


def _attn_call(nfull, nhi, qt, k, vt, pos_col, pos_row, B, S):
    tq, tk = TQ_ATT, TK_ATT
    nq = S // tq
    T = B * S
    kern = functools.partial(_attn_kernel, tq=tq, tk=tk, nq=nq)
    grid_spec = pltpu.PrefetchScalarGridSpec(
        num_scalar_prefetch=2,
        grid=(B, MLA_HEADS // 2, nq),
        in_specs=[pl.BlockSpec((2 * HEAD_PAD, tq), lambda b, h, i, *_: (h, b * nq + i)),
                  pl.BlockSpec((S, 2 * HEAD_PAD), lambda b, h, i, *_: (b, h)),
                  pl.BlockSpec((2 * V_HEAD, S), lambda b, h, i, *_: (h, b)),
                  pl.BlockSpec((None, 1, tq), lambda b, h, i, *_: (b, 0, i)),
                  pl.BlockSpec((S, 1), lambda b, h, i, *_: (b, 0))],
        out_specs=pl.BlockSpec((tq, 2 * V_HEAD), lambda b, h, i, *_: (b * nq + i, h)),
    )
    return pl.pallas_call(
        kern,
        out_shape=jax.ShapeDtypeStruct((T, MLA_HEADS * V_HEAD), BF16),
        grid_spec=grid_spec,
        compiler_params=_cparams(("arbitrary", "arbitrary", "arbitrary")),
        name="mla_attention",
    )(nfull, nhi, qt, k, vt, pos_row, pos_col)


def _oproj_kernel(x_ref, o_ref, mod_ref, w_ref, out_ref):
    g1 = mod_ref[...][2:3]
    y = jnp.dot(o_ref[...], w_ref[...], preferred_element_type=F32)
    out_ref[...] = x_ref[...] + g1 * y


def _oproj_call(x2, o, mod, w_o, S):
    T, D = x2.shape
    tm = TM_TOK
    nb = S // tm
    return pl.pallas_call(
        _oproj_kernel,
        out_shape=jax.ShapeDtypeStruct((T, D), F32),
        grid=(T // tm,),
        in_specs=[pl.BlockSpec((tm, D), lambda i: (i, 0)),
                  pl.BlockSpec((tm, o.shape[1]), lambda i: (i, 0)),
                  pl.BlockSpec((None, 6, D), lambda i: (i // nb, 0, 0)),
                  _const_spec(w_o.shape)],
        out_specs=pl.BlockSpec((tm, D), lambda i: (i, 0)),
        compiler_params=_cparams(("arbitrary",)),
        name="mla_out_proj",
    )(x2, o, mod, w_o)


TOP_ROWS = 3 * SUBLANES
SCORE_BLOCKS_PER_ITER = 4


def _merge_sort_network(n):
    pairs, p = [], 1
    while p < n:
        k = p
        while k >= 1:
            for j in range(k % p, n - k, 2 * k):
                for i in range(min(k, n - j - k)):
                    if (i + j) // (2 * p) == (i + j + k) // (2 * p):
                        pairs.append((i + j, i + j + k))
            k //= 2
        p *= 2
    return pairs


def _largest_rows(x, count):
    v = [x[r * SUBLANES:(r + 1) * SUBLANES, :] for r in range(x.shape[0] // SUBLANES)]
    size = 1 << (len(v) - 1).bit_length()
    v = v + [None] * (size - len(v))
    for i, j in _merge_sort_network(size):
        if v[j] is None:
            continue
        if v[i] is None:
            v[i], v[j] = v[j], None
        else:
            v[i], v[j] = jnp.maximum(v[i], v[j]), jnp.minimum(v[i], v[j])
    v = [t for t in v if t is not None]
    tops = []
    for j in range(count):
        m = jnp.max(v[0], axis=0, keepdims=True)
        tops.append(m)
        if j + 1 < count:
            won = v[0] == m
            depth = min(len(v), count - j)
            v = [jnp.where(won, v[r + 1] if r + 1 < len(v) else NEG_INF, v[r]) for r in range(depth)]
    return tops


def _top17_rows(x):
    pad = jnp.full((TOP_ROWS - PEER_TOPK - 1, x.shape[1]), NEG_INF, F32)
    return jnp.concatenate(_largest_rows(x, PEER_TOPK + 1) + [pad], axis=0)


def _peer_kernel(x_ref, mod_ref, ng_ref, wqt_ref, keys_ref, u_ref, vt_ref, fg_ref, o_ref,
                 h_s, st_s, thr_s, e1_s, e2_s, wa_s, acc_s, *, tm, ec, final_norm):
    ntc = tm // LANES
    e = pl.program_id(1)
    ne = pl.num_programs(1)

    @pl.when(e == 0)
    def _scores():
        x = x_ref[...]
        mod = mod_ref[...]
        sh2, sc2 = mod[3:4], mod[4:5]
        ht = (_rms(x, ng_ref[...]) * (1.0 + sc2) + sh2).T.astype(BF16)
        h_s[...] = ht
        qtb = jnp.dot(wqt_ref[...], ht, preferred_element_type=F32).astype(BF16)
        for hp in range(2 * PEER_HEADS):
            st = jnp.dot(keys_ref[hp], qtb[hp * PEER_HALF:(hp + 1) * PEER_HALF, :],
                         preferred_element_type=F32)
            for tc in range(ntc):
                st_s[tc, hp] = st[:, tc * LANES:(tc + 1) * LANES]

        def score_block(i):
            tc = i // PEER_HEADS
            hd = i % PEER_HEADS
            s1 = st_s[tc, 2 * hd]
            s2 = st_s[tc, 2 * hd + 1]
            a = _top17_rows(s1)
            b = _top17_rows(s2)
            cands = [a[0:1] + b]
            for r in range(1, SUBLANES):
                cands.append(a[r:r + 1] + b[0:SUBLANES])
            cands.append(a[SUBLANES:TOP_ROWS] + b[0:1])
            cand = jnp.concatenate(cands, axis=0)
            c16, c17 = _largest_rows(cand, PEER_TOPK + 1)[PEER_TOPK - 1:]
            tau = 0.5 * (c16 + c17)
            m1, m2 = a[0:1], b[0:1]
            zsum = jnp.sum(jnp.where(cand >= tau, jnp.exp(cand - (m1 + m2)), 0.0), axis=0, keepdims=True)
            thr_s[tc, hd] = jnp.exp(tau - s1 - m2)
            e1_s[tc, hd] = jnp.exp(s1 - m1) * (GELU_OUT_SCALE / zsum)
            e2_s[tc, hd] = jnp.exp(s2 - m2)

        def score_blocks(g, _):
            for bi in range(SCORE_BLOCKS_PER_ITER):
                score_block(g * SCORE_BLOCKS_PER_ITER + bi)
            return 0

        lax.fori_loop(0, ntc * PEER_HEADS // SCORE_BLOCKS_PER_ITER, score_blocks, 0)
        acc_s[...] = jnp.zeros(acc_s.shape, F32)

    ngrp = ec // LANES
    key_rows = {}

    def key_row(ref, r, tc, hd):
        g8, j = divmod(r, SUBLANES)
        if (id(ref), g8, tc, hd) not in key_rows:
            base = pl.multiple_of(e * ngrp + g8 * SUBLANES, SUBLANES)
            key_rows[(id(ref), g8, tc, hd)] = ref[tc, hd, pl.ds(base, SUBLANES), :]
        return key_rows[(id(ref), g8, tc, hd)][j:j + 1, :]

    rows_per_piece = PIECE_PEER // LANES

    def activations(p):
        return jnp.dot(u_ref[p * PIECE_PEER:(p + 1) * PIECE_PEER, :], h_s[...], preferred_element_type=F32)

    def weigh(p, at):
        gact = at * (1.0 + lax.erf(at))
        for rl in range(rows_per_piece):
            r = p * rows_per_piece + rl
            for tc in range(ntc):
                w = None
                for hd in range(PEER_HEADS):
                    e2 = e2_s[tc, hd]
                    term = jnp.where(e2 >= key_row(thr_s, r, tc, hd), e2, 0.0) * key_row(e1_s, r, tc, hd)
                    w = term if w is None else w + term
                wa_s[r * LANES:(r + 1) * LANES, tc * LANES:(tc + 1) * LANES] = (
                    w * gact[rl * LANES:(rl + 1) * LANES, tc * LANES:(tc + 1) * LANES]).astype(BF16)

    def values(p):
        cols = slice(p * PIECE_PEER, (p + 1) * PIECE_PEER)
        acc_s[...] += jnp.dot(vt_ref[:, cols], wa_s[cols, :], preferred_element_type=F32)

    npiece = ec // PIECE_PEER
    ahead = [activations(p) for p in range(min(PIECE_LOOKAHEAD, npiece))]
    for p in range(npiece):
        at = ahead.pop(0)
        if p + PIECE_LOOKAHEAD < npiece:
            ahead.append(activations(p + PIECE_LOOKAHEAD))
        weigh(p, at)
        values(p)

    @pl.when(e == ne - 1)
    def _finish():
        g2 = mod_ref[...][5:6]
        xn = x_ref[...] + g2 * acc_s[...].T
        if final_norm:
            xn = _rms(xn, fg_ref[...])
        o_ref[...] = xn


def _peer_call(x2, mod, ng, wqt, keys, u, vt, fg, S, final_norm):
    T, D = x2.shape
    E = u.shape[0]
    tm, ec = TM_PEER, EC_PEER
    nb = S // tm
    ntc = tm // LANES
    kern = functools.partial(_peer_kernel, tm=tm, ec=ec, final_norm=final_norm)
    return pl.pallas_call(
        kern,
        out_shape=jax.ShapeDtypeStruct((T, D), F32),
        grid=(T // tm, E // ec),
        in_specs=[pl.BlockSpec((tm, D), lambda i, e: (i, 0)),
                  pl.BlockSpec((None, 6, D), lambda i, e: (i // nb, 0, 0)),
                  _const_spec((1, D)), _const_spec(wqt.shape), _const_spec(keys.shape),
                  pl.BlockSpec((ec, D), lambda i, e: (e, 0)),
                  pl.BlockSpec((D, ec), lambda i, e: (0, e)),
                  _const_spec((1, D))],
        out_specs=pl.BlockSpec((tm, D), lambda i, e: (i, 0)),
        scratch_shapes=[pltpu.VMEM((D, tm), BF16),
                        pltpu.VMEM((ntc, 2 * PEER_HEADS, N_KEYS, LANES), F32),
                        pltpu.VMEM((ntc, PEER_HEADS, N_KEYS, LANES), F32),
                        pltpu.VMEM((ntc, PEER_HEADS, N_KEYS, LANES), F32),
                        pltpu.VMEM((ntc, PEER_HEADS, N_KEYS, LANES), F32),
                        pltpu.VMEM((ec, tm), BF16),
                        pltpu.VMEM((D, tm), F32)],
        compiler_params=_cparams(("arbitrary", "arbitrary")),
        name="peer_layer",
    )(x2, mod, ng, wqt, keys, u, vt, fg)


def _head_pad_cols(w_nope, w_rope, n_in):
    zeros = jnp.zeros((n_in, MLA_HEADS, HEAD_PAD - QK_NOPE - QK_ROPE), w_nope.dtype)
    return jnp.concatenate([w_nope, w_rope, zeros], axis=-1).reshape(n_in, MLA_HEADS * HEAD_PAD)


def _swap_halves(w):
    half = QK_ROPE // 2
    return jnp.concatenate([w[..., half:], w[..., :half]], axis=-1)


def kernel(x, c, positions, ada_w, ada_b, norm_mix_g, norm_ffn_g, lru_w_in, lru_conv_w, lru_conv_b, lru_wa, lru_ba, lru_wx, lru_bx, lru_lambda, lru_w_out, kv_ada_w, kv_ada_b, kv_norm_g, mla_w_dkv, mla_w_kr, mla_kv_latent_g, mla_w_uk, mla_w_uv, mla_w_dq, mla_q_latent_g, mla_w_uq, mla_w_o, peer_w_q, peer_sub_keys, peer_u, peer_v, final_g):
    B, S, D = x.shape
    T = B * S
    assert D == D_MODEL and S % TM_TOK == 0 and S % TS_LRU == 0 and S % TM_PEER == 0
    assert S % TQ_ATT == 0 and S % TK_ATT == 0

    c_pad = jnp.zeros((SUBLANES, D), F32).at[:B].set(c)
    mod_all = _mod_call(c_pad, ada_w, ada_b[:, None, :], 1536)
    mod_all = mod_all[:, :B].reshape(DEPTH, B, 6, D)
    mod_kv = _mod_call(c_pad, kv_ada_w[None], kv_ada_b[None, None, :], 1024)[0, :B].reshape(B, 2, D)

    half = QK_ROPE // 2
    freqs = ROPE_THETA ** (-jnp.arange(half, dtype=F32) / half)
    freq_lane = jnp.zeros((1, LANES), F32).at[0, QK_NOPE:QK_NOPE + QK_ROPE].set(jnp.concatenate([freqs, freqs]))
    pos_col = positions.reshape(T, 1)
    ca, sb = _rope_call(pos_col, freq_lane)

    nq, nk = S // TQ_ATT, S // KSUB_ATT
    pq = positions.reshape(B, nq, TQ_ATT)
    pk = positions.reshape(B, nk, KSUB_ATT)
    qmin, qmax = pq.min(-1), pq.max(-1)
    kmin, kmax = pk.min(-1), pk.max(-1)
    needed = kmin[:, None, :] <= qmax[:, :, None]
    nhi = jnp.max(jnp.where(needed, jnp.arange(1, nk + 1, dtype=jnp.int32), 0), axis=-1)
    full = kmax[:, None, :] <= qmin[:, :, None]
    nfull = jnp.min(jnp.where(full, nk, jnp.arange(nk, dtype=jnp.int32)), axis=-1)
    nfull = jnp.minimum(nfull, nhi).astype(jnp.int32).reshape(-1)
    nhi = nhi.astype(jnp.int32).reshape(-1)
    pos_row = positions.reshape(B, 1, S)

    def peer_layer(x2, l, final_norm):
        wqt = peer_w_q[l].T.astype(BF16)
        keys = peer_sub_keys[l].reshape(2 * PEER_HEADS, N_KEYS, PEER_HALF).astype(BF16)
        u = (peer_u[l] * (1.0 / math.sqrt(2.0))).astype(BF16)
        vt = peer_v[l].T.astype(BF16)
        return _peer_call(x2, mod_all[l], norm_ffn_g[l][None], wqt, keys, u, vt, final_g[None], S, final_norm)

    for l in range(N_A_LAYERS):
        x = _lru_call(x, mod_all[l], norm_mix_g[l][None], lru_w_in[l].astype(BF16), lru_conv_w[l],
                      lru_conv_b[l][None], lru_wa[l].astype(BF16), lru_ba[l][None], lru_wx[l].astype(BF16),
                      lru_bx[l][None], lru_lambda[l][None], lru_w_out[l].astype(BF16))
        x = peer_layer(x.reshape(T, D), l, False).reshape(B, S, D)

    x2 = x.reshape(T, D)

    zc = jnp.zeros((D, QK_NOPE), F32)
    zt = jnp.zeros((D, HEAD_PAD - QK_NOPE - QK_ROPE), F32)
    w1 = jnp.concatenate([mla_w_dkv, zc, mla_w_kr, zt, zc, _swap_halves(mla_w_kr), zt], axis=-1).astype(BF16)
    wuk = jnp.concatenate([mla_w_uk.reshape(KV_LORA, MLA_HEADS, QK_NOPE),
                           jnp.zeros((KV_LORA, MLA_HEADS, HEAD_PAD - QK_NOPE), F32)], axis=-1)
    wuk = wuk.reshape(KV_LORA, MLA_HEADS * HEAD_PAD).astype(BF16)
    k_all, v_all = _kv_call(x2, mod_kv, kv_norm_g[None], w1, mla_kv_latent_g[None], wuk,
                            mla_w_uv.astype(BF16), ca, sb, S)

    for j in range(DEPTH - N_A_LAYERS):
        l = N_A_LAYERS + j
        wq = mla_w_uq[j].reshape(Q_LORA, MLA_HEADS, QK_NOPE + QK_ROPE)
        w_nope, w_rope = wq[..., :QK_NOPE], wq[..., QK_NOPE:]
        wa = _head_pad_cols(w_nope, w_rope, Q_LORA).astype(BF16)
        wb = _head_pad_cols(jnp.zeros_like(w_nope), _swap_halves(w_rope), Q_LORA).astype(BF16)
        q = _q_call(x2, mod_all[l], norm_mix_g[l][None], mla_w_dq[j].astype(BF16), mla_q_latent_g[j][None],
                    wa, wb, ca, sb, S)
        o = _attn_call(nfull, nhi, q, k_all, v_all, pos_col, pos_row, B, S)
        x2 = _oproj_call(x2, o, mod_all[l], mla_w_o[j].astype(BF16), S)
        x2 = peer_layer(x2, l, l == DEPTH - 1)

    return x2.reshape(B, S, D)
```

```python
import functools
import math

import jax
import jax.numpy as jnp
from jax import lax
from jax.experimental import pallas as pl
from jax.experimental.pallas import tpu as pltpu

F32 = jnp.float32
BF16 = jnp.bfloat16

D_MODEL = 1024
DEPTH = 4
N_A_LAYERS = DEPTH // 2
LRU_HEADS = 4
LRU_BLOCK = D_MODEL // LRU_HEADS
CONV_WIDTH = 4
LRU_C = 8.0
MLA_HEADS = 16
QK_NOPE = 64
QK_ROPE = 32
V_HEAD = 64
Q_LORA = 384
KV_LORA = 256
ROPE_THETA = 10000.0
ATTN_SCALE = 1.0 / math.sqrt(QK_NOPE + QK_ROPE)
PEER_HEADS = 8
N_KEYS = 128
PEER_TOPK = 16
PEER_HALF = 128
RMS_EPS = 1e-6
NEG_INF = -1e30
LOG2E = math.log2(math.e)
GELU_OUT_SCALE = 0.5 * math.sqrt(2.0)

LANES = 128
SUBLANES = 8
HEAD_PAD = LANES
VMEM_LIMIT = 56 * 1024 * 1024

TS_LRU = 256
TM_TOK = 512
TQ_ATT = 1024
TK_ATT = 2048
KSUB_ATT = 512
QSUB_ATT = 256
SCORE_LOOKAHEAD = 4
ONES_ROWS = 16
TM_PEER = 512
EC_PEER = 2048
PIECE_PEER = 256
PIECE_LOOKAHEAD = 4


def _cparams(sem):
    return pltpu.CompilerParams(dimension_semantics=sem, vmem_limit_bytes=VMEM_LIMIT)


def _rms(x, g):
    return x * lax.rsqrt(jnp.mean(x * x, axis=-1, keepdims=True) + RMS_EPS) * g


def _gelu(x):
    return 0.5 * x * (1.0 + lax.erf(x * (1.0 / math.sqrt(2.0))))


def _expm1(y):
    series = y * (1.0 + 0.5 * y * (1.0 + (1.0 / 3.0) * y * (1.0 + 0.25 * y * (1.0 + 0.2 * y))))
    return jnp.where(jnp.abs(y) < 0.05, series, jnp.exp(y) - 1.0)


def _const_spec(shape):
    nd = len(shape)
    return pl.BlockSpec(shape, lambda *_: (0,) * nd)


def _mod_kernel(c_ref, w_ref, b_ref, o_ref):
    c = c_ref[...]
    sc = c * jax.nn.sigmoid(c)
    o_ref[...] = jnp.dot(sc.astype(BF16), w_ref[...].astype(BF16),
                         preferred_element_type=F32) + b_ref[...]


def _mod_call(c_pad, w, b, tn):
    L, D, N = w.shape
    return pl.pallas_call(
        _mod_kernel,
        out_shape=jax.ShapeDtypeStruct((L, SUBLANES, N), F32),
        grid=(L, N // tn),
        in_specs=[pl.BlockSpec((SUBLANES, D), lambda l, n: (0, 0)),
                  pl.BlockSpec((None, D, tn), lambda l, n: (l, 0, n)),
                  pl.BlockSpec((None, 1, tn), lambda l, n: (l, 0, n))],
        out_specs=pl.BlockSpec((None, SUBLANES, tn), lambda l, n: (l, 0, n)),
        compiler_params=_cparams(("arbitrary", "arbitrary")),
        name="adaln_mod",
    )(c_pad, w, b)


def _rope_kernel(pos_ref, freq_ref, ca_ref, sb_ref):
    ang = pos_ref[...].astype(F32) * freq_ref[...]
    lane = lax.broadcasted_iota(jnp.int32, ang.shape, 1)
    cos, sin = jnp.cos(ang), jnp.sin(ang)
    half = QK_ROPE // 2
    ca_ref[...] = jnp.where(lane < QK_NOPE, 1.0, jnp.where(lane < QK_NOPE + QK_ROPE, cos, 0.0))
    sb_ref[...] = jnp.where(lane < QK_NOPE, 0.0,
                            jnp.where(lane < QK_NOPE + half, -sin,
                                      jnp.where(lane < QK_NOPE + QK_ROPE, sin, 0.0)))


def _rope_call(pos_col, freq_lane):
    T = pos_col.shape[0]
    tm = TM_TOK
    spec = pl.BlockSpec((tm, LANES), lambda i: (i, 0))
    return pl.pallas_call(
        _rope_kernel,
        out_shape=(jax.ShapeDtypeStruct((T, LANES), F32),) * 2,
        grid=(T // tm,),
        in_specs=[pl.BlockSpec((tm, 1), lambda i: (i, 0)), _const_spec((1, LANES))],
        out_specs=(spec, spec),
        compiler_params=_cparams(("arbitrary",)),
        name="rope_tables",
    )(pos_col, freq_lane)


def _lru_kernel(x_ref, mod_ref, ng_ref, win_ref, cw_ref, cb_ref, wa_ref, ba_ref, wx_ref, bx_ref,
                lam_ref, wout_ref, o_ref, rec_s, h_s, *, ts):
    D = D_MODEL

    @pl.when(pl.program_id(1) == 0)
    def _():
        rec_s[0:SUBLANES, :] = jnp.zeros((SUBLANES, D), F32)
        h_s[...] = jnp.zeros((SUBLANES, D), F32)

    x = x_ref[...]
    mod = mod_ref[...]
    sh1, sc1, g1 = mod[0:1], mod[1:2], mod[2:3]
    h = _rms(x, ng_ref[...]) * (1.0 + sc1) + sh1
    u = jnp.dot(h.astype(BF16), win_ref[...], preferred_element_type=F32)
    gate = _gelu(u[:, :D])
    rec = u[:, D:]

    rec_s[SUBLANES:ts + SUBLANES, :] = rec
    cw = cw_ref[...]
    xc = (cb_ref[...] + rec * cw[3:4]
          + rec_s[SUBLANES - 1:ts + SUBLANES - 1, :] * cw[2:3]
          + rec_s[SUBLANES - 2:ts + SUBLANES - 2, :] * cw[1:2]
          + rec_s[SUBLANES - 3:ts + SUBLANES - 3, :] * cw[0:1])
    rec_s[0:SUBLANES, :] = rec[ts - SUBLANES:ts, :]

    xcb = xc.astype(BF16)
    r_pre = jnp.concatenate(
        [jnp.dot(xcb[:, k * LRU_BLOCK:(k + 1) * LRU_BLOCK], wa_ref[k], preferred_element_type=F32)
         for k in range(LRU_HEADS)], axis=-1)
    i_pre = jnp.concatenate(
        [jnp.dot(xcb[:, k * LRU_BLOCK:(k + 1) * LRU_BLOCK], wx_ref[k], preferred_element_type=F32)
         for k in range(LRU_HEADS)], axis=-1)
    r = jax.nn.sigmoid(r_pre + ba_ref[...])
    gi = jax.nn.sigmoid(i_pre + bx_ref[...])
    z = -lam_ref[...]
    softplus = jnp.maximum(z, 0.0) + jnp.log1p(jnp.exp(-jnp.abs(z)))
    log_a = (-LRU_C) * r * softplus
    a = jnp.exp(log_a)
    b = jnp.sqrt(-_expm1(2.0 * log_a)) * (gi * xc)

    row = lax.broadcasted_iota(jnp.int32, a.shape, 0)
    d = 1
    while d < SUBLANES:
        inside = (row & (SUBLANES - 1)) >= d
        a_sh = jnp.where(inside, pltpu.roll(a, d, 0), 1.0)
        b_sh = jnp.where(inside, pltpu.roll(b, d, 0), 0.0)
        b = a * b_sh + b
        a = a * a_sh
        d *= 2
    h_prev = h_s[SUBLANES - 1:SUBLANES, :]
    tiles = []
    for k in range(ts // SUBLANES):
        rows = slice(k * SUBLANES, (k + 1) * SUBLANES)
        h_tile = a[rows, :] * h_prev + b[rows, :]
        tiles.append(h_tile)
        h_prev = h_tile[SUBLANES - 1:SUBLANES, :]
    hs = jnp.concatenate(tiles, axis=0)
    h_s[...] = tiles[-1]

    y = jnp.dot((gate * hs).astype(BF16), wout_ref[...], preferred_element_type=F32)
    o_ref[...] = x + g1 * y


def _lru_call(x, mod, ng, w_in, cw, cb, wa, ba, wx, bx, lam, w_out):
    B, S, D = x.shape
    ts = TS_LRU
    kern = functools.partial(_lru_kernel, ts=ts)
    return pl.pallas_call(
        kern,
        out_shape=jax.ShapeDtypeStruct((B, S, D), F32),
        grid=(B, S // ts),
        in_specs=[pl.BlockSpec((None, ts, D), lambda b, s: (b, s, 0)),
                  pl.BlockSpec((None, 6, D), lambda b, s: (b, 0, 0)),
                  _const_spec((1, D)), _const_spec((D, 2 * D)), _const_spec((CONV_WIDTH, D)),
                  _const_spec((1, D)), _const_spec((LRU_HEADS, LRU_BLOCK, LRU_BLOCK)), _const_spec((1, D)),
                  _const_spec((LRU_HEADS, LRU_BLOCK, LRU_BLOCK)), _const_spec((1, D)), _const_spec((1, D)),
                  _const_spec((D, D))],
        out_specs=pl.BlockSpec((None, ts, D), lambda b, s: (b, s, 0)),
        scratch_shapes=[pltpu.VMEM((ts + SUBLANES, D), F32), pltpu.VMEM((SUBLANES, D), F32)],
        compiler_params=_cparams(("arbitrary", "arbitrary")),
        name="rglru_layer",
    )(x, mod, ng, w_in, cw, cb, wa, ba, wx, bx, lam, w_out)


def _kv_kernel(x_ref, mod_ref, ng_ref, w1_ref, lg_ref, wuk_ref, wuv_ref, ca_ref, sb_ref, k_ref, v_ref):
    x = x_ref[...]
    mod = mod_ref[...]
    shift, scale = mod[0:1], mod[1:2]
    h = _rms(x, ng_ref[...]) * (1.0 + scale) + shift
    t = jnp.dot(h.astype(BF16), w1_ref[...], preferred_element_type=F32)
    c_kv = _rms(t[:, :KV_LORA], lg_ref[...])
    kr = t[:, KV_LORA:KV_LORA + LANES] * ca_ref[...] + t[:, KV_LORA + LANES:] * sb_ref[...]
    ckb = c_kv.astype(BF16)
    kn = jnp.dot(ckb, wuk_ref[...], preferred_element_type=F32)
    k_ref[...] = (kn + jnp.concatenate([kr] * MLA_HEADS, axis=-1)).astype(BF16)
    v_ref[...] = jnp.dot(ckb, wuv_ref[...], preferred_element_type=F32).T.astype(BF16)


def _kv_call(x2, mod, ng, w1, lg, wuk, wuv, ca, sb, S):
    T, D = x2.shape
    tm = TM_TOK
    nb = S // tm
    HP = MLA_HEADS * HEAD_PAD
    HV = MLA_HEADS * V_HEAD
    return pl.pallas_call(
        _kv_kernel,
        out_shape=(jax.ShapeDtypeStruct((T, HP), BF16), jax.ShapeDtypeStruct((HV, T), BF16)),
        grid=(T // tm,),
        in_specs=[pl.BlockSpec((tm, D), lambda i: (i, 0)),
                  pl.BlockSpec((None, 2, D), lambda i: (i // nb, 0, 0)),
                  _const_spec((1, D)), _const_spec(w1.shape), _const_spec((1, KV_LORA)),
                  _const_spec(wuk.shape), _const_spec(wuv.shape),
                  pl.BlockSpec((tm, LANES), lambda i: (i, 0)), pl.BlockSpec((tm, LANES), lambda i: (i, 0))],
        out_specs=(pl.BlockSpec((tm, HP), lambda i: (i, 0)), pl.BlockSpec((HV, tm), lambda i: (0, i))),
        compiler_params=_cparams(("arbitrary",)),
        name="shared_kv",
    )(x2, mod, ng, w1, lg, wuk, wuv, ca, sb)


def _q_kernel(x_ref, mod_ref, ng_ref, wdq_ref, lg_ref, wa_ref, wb_ref, ca_ref, sb_ref, q_ref):
    x = x_ref[...]
    mod = mod_ref[...]
    sh1, sc1 = mod[0:1], mod[1:2]
    h = _rms(x, ng_ref[...]) * (1.0 + sc1) + sh1
    ql = _rms(jnp.dot(h.astype(BF16), wdq_ref[...], preferred_element_type=F32), lg_ref[...])
    qlb = ql.astype(BF16)
    qa = jnp.dot(qlb, wa_ref[...], preferred_element_type=F32)
    qb = jnp.dot(qlb, wb_ref[...], preferred_element_type=F32)
    ca = jnp.concatenate([ca_ref[...]] * MLA_HEADS, axis=-1)
    sb = jnp.concatenate([sb_ref[...]] * MLA_HEADS, axis=-1)
    q_ref[...] = ((qa * ca + qb * sb) * (ATTN_SCALE * LOG2E)).T.astype(BF16)


def _q_call(x2, mod, ng, wdq, lg, wa, wb, ca, sb, S):
    T, D = x2.shape
    tm = TM_TOK
    nb = S // tm
    HP = MLA_HEADS * HEAD_PAD
    return pl.pallas_call(
        _q_kernel,
        out_shape=jax.ShapeDtypeStruct((HP, T), BF16),
        grid=(T // tm,),
        in_specs=[pl.BlockSpec((tm, D), lambda i: (i, 0)),
                  pl.BlockSpec((None, 6, D), lambda i: (i // nb, 0, 0)),
                  _const_spec((1, D)), _const_spec(wdq.shape), _const_spec((1, Q_LORA)),
                  _const_spec(wa.shape), _const_spec(wb.shape),
                  pl.BlockSpec((tm, LANES), lambda i: (i, 0)), pl.BlockSpec((tm, LANES), lambda i: (i, 0))],
        out_specs=pl.BlockSpec((HP, tm), lambda i: (0, i)),
        compiler_params=_cparams(("arbitrary",)),
        name="mla_queries",
    )(x2, mod, ng, wdq, lg, wa, wb, ca, sb)


def _col_max(x):
    while x.shape[0] > SUBLANES:
        h = x.shape[0] // 2
        x = jnp.maximum(x[:h], x[h:])
    return jnp.max(x, axis=0, keepdims=True)


def _attn_kernel(nfull_ref, nhi_ref, qt_ref, k_ref, vt_ref, pq_ref, pk_ref, o_ref, *, tq, tk, nq):
    b = pl.program_id(0)
    qi = pl.program_id(2)
    nfull = nfull_ref[b * nq + qi]
    nhi = nhi_ref[b * nq + qi]
    pos_q = pq_ref[...]

    def step(kc, carry, masked, tk):
        nstate = 2 * (tq // QSUB_ATT)
        nstream = nstate * (tk // KSUB_ATT)
        state = list(carry)

        def split(sidx):
            kb, sid = divmod(sidx, nstate)
            hh, qs = divmod(sid, tq // QSUB_ATT)
            return pl.multiple_of(kc * tk + kb * KSUB_ATT, KSUB_ATT), sid, hh, qs

        def scores(sidx):
            off, _, hh, qs = split(sidx)
            k = k_ref[pl.ds(off, KSUB_ATT), hh * HEAD_PAD:(hh + 1) * HEAD_PAD]
            qt = qt_ref[hh * HEAD_PAD:(hh + 1) * HEAD_PAD, qs * QSUB_ATT:(qs + 1) * QSUB_ATT]
            return jnp.dot(k, qt, preferred_element_type=F32)

        def update(sidx, st):
            off, sid, hh, qs = split(sidx)
            m, acc = state[sid]
            if masked:
                keep = pk_ref[pl.ds(off, KSUB_ATT), :] <= pos_q[:, qs * QSUB_ATT:(qs + 1) * QSUB_ATT]
                st = jnp.where(keep, st, NEG_INF)
            m_new = jnp.maximum(m, _col_max(st))
            p = jnp.exp2((st - m_new).astype(BF16))
            alpha = jnp.exp2(m - m_new)
            vt = jnp.concatenate([vt_ref[hh * V_HEAD:(hh + 1) * V_HEAD, pl.ds(off, KSUB_ATT)],
                                  jnp.ones((ONES_ROWS, KSUB_ATT), BF16)], axis=0)
            state[sid] = (m_new, alpha * acc + jnp.dot(vt, p, preferred_element_type=F32))

        ahead = [scores(s) for s in range(min(SCORE_LOOKAHEAD, nstream))]
        for sidx in range(nstream):
            st_cur = ahead.pop(0)
            if sidx + SCORE_LOOKAHEAD < nstream:
                ahead.append(scores(sidx + SCORE_LOOKAHEAD))
            update(sidx, st_cur)
        return tuple(state)

    nqs = tq // QSUB_ATT
    one = (jnp.full((1, QSUB_ATT), NEG_INF, F32), jnp.zeros((V_HEAD + ONES_ROWS, QSUB_ATT), F32))
    per = tk // KSUB_ATT
    nwide = nfull // per
    carry = lax.fori_loop(0, nwide, functools.partial(step, masked=False, tk=tk), (one,) * (2 * nqs))
    npair = (nhi - nwide * per) // 2
    carry = lax.fori_loop(nwide * (per // 2), nwide * (per // 2) + npair,
                          functools.partial(step, masked=True, tk=2 * KSUB_ATT), carry)
    fin = lax.fori_loop(nwide * per + 2 * npair, nhi, functools.partial(step, masked=True, tk=KSUB_ATT), carry)
    heads = [jnp.concatenate([fin[hh * nqs + qs][1][:V_HEAD] / fin[hh * nqs + qs][1][V_HEAD:V_HEAD + 1]
                              for qs in range(nqs)], axis=1) for hh in range(2)]
    o_ref[...] = jnp.concatenate(heads, axis=0).T.astype(BF16)


def _attn_call(nfull, nhi, qt, k, vt, pos_col, pos_row, B, S):
    tq, tk = TQ_ATT, TK_ATT
    nq = S // tq
    T = B * S
    kern = functools.partial(_attn_kernel, tq=tq, tk=tk, nq=nq)
    grid_spec = pltpu.PrefetchScalarGridSpec(
        num_scalar_prefetch=2,
        grid=(B, MLA_HEADS // 2, nq),
        in_specs=[pl.BlockSpec((2 * HEAD_PAD, tq), lambda b, h, i, *_: (h, b * nq + i)),
                  pl.BlockSpec((S, 2 * HEAD_PAD), lambda b, h, i, *_: (b, h)),
                  pl.BlockSpec((2 * V_HEAD, S), lambda b, h, i, *_: (h, b)),
                  pl.BlockSpec((None, 1, tq), lambda b, h, i, *_: (b, 0, i)),
                  pl.BlockSpec((S, 1), lambda b, h, i, *_: (b, 0))],
        out_specs=pl.BlockSpec((tq, 2 * V_HEAD), lambda b, h, i, *_: (b * nq + i, h)),
    )
    return pl.pallas_call(
        kern,
        out_shape=jax.ShapeDtypeStruct((T, MLA_HEADS * V_HEAD), BF16),
        grid_spec=grid_spec,
        compiler_params=_cparams(("arbitrary", "arbitrary", "arbitrary")),
        name="mla_attention",
    )(nfull, nhi, qt, k, vt, pos_row, pos_col)


def _oproj_kernel(x_ref, o_ref, mod_ref, w_ref, out_ref):
    g1 = mod_ref[...][2:3]
    y = jnp.dot(o_ref[...], w_ref[...], preferred_element_type=F32)
    out_ref[...] = x_ref[...] + g1 * y


def _oproj_call(x2, o, mod, w_o, S):
    T, D = x2.shape
    tm = TM_TOK
    nb = S // tm
    return pl.pallas_call(
        _oproj_kernel,
        out_shape=jax.ShapeDtypeStruct((T, D), F32),
        grid=(T // tm,),
        in_specs=[pl.BlockSpec((tm, D), lambda i: (i, 0)),
                  pl.BlockSpec((tm, o.shape[1]), lambda i: (i, 0)),
                  pl.BlockSpec((None, 6, D), lambda i: (i // nb, 0, 0)),
                  _const_spec(w_o.shape)],
        out_specs=pl.BlockSpec((tm, D), lambda i: (i, 0)),
        compiler_params=_cparams(("arbitrary",)),
        name="mla_out_proj",
    )(x2, o, mod, w_o)


TOP_ROWS = 3 * SUBLANES
SCORE_BLOCKS_PER_ITER = 4


def _merge_sort_network(n):
    pairs, p = [], 1
    while p < n:
        k = p
        while k >= 1:
            for j in range(k % p, n - k, 2 * k):
                for i in range(min(k, n - j - k)):
                    if (i + j) // (2 * p) == (i + j + k) // (2 * p):
                        pairs.append((i + j, i + j + k))
            k //= 2
        p *= 2
    return pairs


def _largest_rows(x, count):
    v = [x[r * SUBLANES:(r + 1) * SUBLANES, :] for r in range(x.shape[0] // SUBLANES)]
    size = 1 << (len(v) - 1).bit_length()
    v = v + [None] * (size - len(v))
    for i, j in _merge_sort_network(size):
        if v[j] is None:
            continue
        if v[i] is None:
            v[i], v[j] = v[j], None
        else:
            v[i], v[j] = jnp.maximum(v[i], v[j]), jnp.minimum(v[i], v[j])
    v = [t for t in v if t is not None]
    tops = []
    for j in range(count):
        m = jnp.max(v[0], axis=0, keepdims=True)
        tops.append(m)
        if j + 1 < count:
            won = v[0] == m
            depth = min(len(v), count - j)
            v = [jnp.where(won, v[r + 1] if r + 1 < len(v) else NEG_INF, v[r]) for r in range(depth)]
    return tops


def _top17_rows(x):
    pad = jnp.full((TOP_ROWS - PEER_TOPK - 1, x.shape[1]), NEG_INF, F32)
    return jnp.concatenate(_largest_rows(x, PEER_TOPK + 1) + [pad], axis=0)


def _peer_kernel(x_ref, mod_ref, ng_ref, wqt_ref, keys_ref, u_ref, vt_ref, fg_ref, o_ref,
                 h_s, st_s, thr_s, e1_s, e2_s, wa_s, acc_s, *, tm, ec, final_norm):
    ntc = tm // LANES
    e = pl.program_id(1)
    ne = pl.num_programs(1)

    @pl.when(e == 0)
    def _scores():
        x = x_ref[...]
        mod = mod_ref[...]
        sh2, sc2 = mod[3:4], mod[4:5]
        ht = (_rms(x, ng_ref[...]) * (1.0 + sc2) + sh2).T.astype(BF16)
        h_s[...] = ht
        qtb = jnp.dot(wqt_ref[...], ht, preferred_element_type=F32).astype(BF16)
        for hp in range(2 * PEER_HEADS):
            st = jnp.dot(keys_ref[hp], qtb[hp * PEER_HALF:(hp + 1) * PEER_HALF, :],
                         preferred_element_type=F32)
            for tc in range(ntc):
                st_s[tc, hp] = st[:, tc * LANES:(tc + 1) * LANES]

        def score_block(i):
            tc = i // PEER_HEADS
            hd = i % PEER_HEADS
            s1 = st_s[tc, 2 * hd]
            s2 = st_s[tc, 2 * hd + 1]
            a = _top17_rows(s1)
            b = _top17_rows(s2)
            cands = [a[0:1] + b]
            for r in range(1, SUBLANES):
                cands.append(a[r:r + 1] + b[0:SUBLANES])
            cands.append(a[SUBLANES:TOP_ROWS] + b[0:1])
            cand = jnp.concatenate(cands, axis=0)
            c16, c17 = _largest_rows(cand, PEER_TOPK + 1)[PEER_TOPK - 1:]
            tau = 0.5 * (c16 + c17)
            m1, m2 = a[0:1], b[0:1]
            zsum = jnp.sum(jnp.where(cand >= tau, jnp.exp(cand - (m1 + m2)), 0.0), axis=0, keepdims=True)
            thr_s[tc, hd] = jnp.exp(tau - s1 - m2)
            e1_s[tc, hd] = jnp.exp(s1 - m1) * (GELU_OUT_SCALE / zsum)
            e2_s[tc, hd] = jnp.exp(s2 - m2)

        def score_blocks(g, _):
            for bi in range(SCORE_BLOCKS_PER_ITER):
                score_block(g * SCORE_BLOCKS_PER_ITER + bi)
            return 0

        lax.fori_loop(0, ntc * PEER_HEADS // SCORE_BLOCKS_PER_ITER, score_blocks, 0)
        acc_s[...] = jnp.zeros(acc_s.shape, F32)

    ngrp = ec // LANES
    key_rows = {}

    def key_row(ref, r, tc, hd):
        g8, j = divmod(r, SUBLANES)
        if (id(ref), g8, tc, hd) not in key_rows:
            base = pl.multiple_of(e * ngrp + g8 * SUBLANES, SUBLANES)
            key_rows[(id(ref), g8, tc, hd)] = ref[tc, hd, pl.ds(base, SUBLANES), :]
        return key_rows[(id(ref), g8, tc, hd)][j:j + 1, :]

    rows_per_piece = PIECE_PEER // LANES

    def activations(p):
        return jnp.dot(u_ref[p * PIECE_PEER:(p + 1) * PIECE_PEER, :], h_s[...], preferred_element_type=F32)

    def weigh(p, at):
        gact = at * (1.0 + lax.erf(at))
        for rl in range(rows_per_piece):
            r = p * rows_per_piece + rl
            for tc in range(ntc):
                w = None
                for hd in range(PEER_HEADS):
                    e2 = e2_s[tc, hd]
                    term = jnp.where(e2 >= key_row(thr_s, r, tc, hd), e2, 0.0) * key_row(e1_s, r, tc, hd)
                    w = term if w is None else w + term
                wa_s[r * LANES:(r + 1) * LANES, tc * LANES:(tc + 1) * LANES] = (
                    w * gact[rl * LANES:(rl + 1) * LANES, tc * LANES:(tc + 1) * LANES]).astype(BF16)

    def values(p):
        cols = slice(p * PIECE_PEER, (p + 1) * PIECE_PEER)
        acc_s[...] += jnp.dot(vt_ref[:, cols], wa_s[cols, :], preferred_element_type=F32)

    npiece = ec // PIECE_PEER
    ahead = [activations(p) for p in range(min(PIECE_LOOKAHEAD, npiece))]
    for p in range(npiece):
        at = ahead.pop(0)
        if p + PIECE_LOOKAHEAD < npiece:
            ahead.append(activations(p + PIECE_LOOKAHEAD))
        weigh(p, at)
        values(p)

    @pl.when(e == ne - 1)
    def _finish():
        g2 = mod_ref[...][5:6]
        xn = x_ref[...] + g2 * acc_s[...].T
        if final_norm:
            xn = _rms(xn, fg_ref[...])
        o_ref[...] = xn


def _peer_call(x2, mod, ng, wqt, keys, u, vt, fg, S, final_norm):
    T, D = x2.shape
    E = u.shape[0]
    tm, ec = TM_PEER, EC_PEER
    nb = S // tm
    ntc = tm // LANES
    kern = functools.partial(_peer_kernel, tm=tm, ec=ec, final_norm=final_norm)
    return pl.pallas_call(
        kern,
        out_shape=jax.ShapeDtypeStruct((T, D), F32),
        grid=(T // tm, E // ec),
        in_specs=[pl.BlockSpec((tm, D), lambda i, e: (i, 0)),
                  pl.BlockSpec((None, 6, D), lambda i, e: (i // nb, 0, 0)),
                  _const_spec((1, D)), _const_spec(wqt.shape), _const_spec(keys.shape),
                  pl.BlockSpec((ec, D), lambda i, e: (e, 0)),
                  pl.BlockSpec((D, ec), lambda i, e: (0, e)),
                  _const_spec((1, D))],
        out_specs=pl.BlockSpec((tm, D), lambda i, e: (i, 0)),
        scratch_shapes=[pltpu.VMEM((D, tm), BF16),
                        pltpu.VMEM((ntc, 2 * PEER_HEADS, N_KEYS, LANES), F32),
                        pltpu.VMEM((ntc, PEER_HEADS, N_KEYS, LANES), F32),
                        pltpu.VMEM((ntc, PEER_HEADS, N_KEYS, LANES), F32),
                        pltpu.VMEM((ntc, PEER_HEADS, N_KEYS, LANES), F32),
                        pltpu.VMEM((ec, tm), BF16),
                        pltpu.VMEM((D, tm), F32)],
        compiler_params=_cparams(("arbitrary", "arbitrary")),
        name="peer_layer",
    )(x2, mod, ng, wqt, keys, u, vt, fg)


def _head_pad_cols(w_nope, w_rope, n_in):
    zeros = jnp.zeros((n_in, MLA_HEADS, HEAD_PAD - QK_NOPE - QK_ROPE), w_nope.dtype)
    return jnp.concatenate([w_nope, w_rope, zeros], axis=-1).reshape(n_in, MLA_HEADS * HEAD_PAD)


def _swap_halves(w):
    half = QK_ROPE // 2
    return jnp.concatenate([w[..., half:], w[..., :half]], axis=-1)


def kernel(x, c, positions, ada_w, ada_b, norm_mix_g, norm_ffn_g, lru_w_in, lru_conv_w, lru_conv_b, lru_wa, lru_ba, lru_wx, lru_bx, lru_lambda, lru_w_out, kv_ada_w, kv_ada_b, kv_norm_g, mla_w_dkv, mla_w_kr, mla_kv_latent_g, mla_w_uk, mla_w_uv, mla_w_dq, mla_q_latent_g, mla_w_uq, mla_w_o, peer_w_q, peer_sub_keys, peer_u, peer_v, final_g):
    B, S, D = x.shape
    T = B * S
    assert D == D_MODEL and S % TM_TOK == 0 and S % TS_LRU == 0 and S % TM_PEER == 0
    assert S % TQ_ATT == 0 and S % TK_ATT == 0

    c_pad = jnp.zeros((SUBLANES, D), F32).at[:B].set(c)
    mod_all = _mod_call(c_pad, ada_w, ada_b[:, None, :], 1536)
    mod_all = mod_all[:, :B].reshape(DEPTH, B, 6, D)
    mod_kv = _mod_call(c_pad, kv_ada_w[None], kv_ada_b[None, None, :], 1024)[0, :B].reshape(B, 2, D)

    half = QK_ROPE // 2
    freqs = ROPE_THETA ** (-jnp.arange(half, dtype=F32) / half)
    freq_lane = jnp.zeros((1, LANES), F32).at[0, QK_NOPE:QK_NOPE + QK_ROPE].set(jnp.concatenate([freqs, freqs]))
    pos_col = positions.reshape(T, 1)
    ca, sb = _rope_call(pos_col, freq_lane)

    nq, nk = S // TQ_ATT, S // KSUB_ATT
    pq = positions.reshape(B, nq, TQ_ATT)
    pk = positions.reshape(B, nk, KSUB_ATT)
    qmin, qmax = pq.min(-1), pq.max(-1)
    kmin, kmax = pk.min(-1), pk.max(-1)
    needed = kmin[:, None, :] <= qmax[:, :, None]
    nhi = jnp.max(jnp.where(needed, jnp.arange(1, nk + 1, dtype=jnp.int32), 0), axis=-1)
    full = kmax[:, None, :] <= qmin[:, :, None]
    nfull = jnp.min(jnp.where(full, nk, jnp.arange(nk, dtype=jnp.int32)), axis=-1)
    nfull = jnp.minimum(nfull, nhi).astype(jnp.int32).reshape(-1)
    nhi = nhi.astype(jnp.int32).reshape(-1)
    pos_row = positions.reshape(B, 1, S)

    def peer_layer(x2, l, final_norm):
        wqt = peer_w_q[l].T.astype(BF16)
        keys = peer_sub_keys[l].reshape(2 * PEER_HEADS, N_KEYS, PEER_HALF).astype(BF16)
        u = (peer_u[l] * (1.0 / math.sqrt(2.0))).astype(BF16)
        vt = peer_v[l].T.astype(BF16)
        return _peer_call(x2, mod_all[l], norm_ffn_g[l][None], wqt, keys, u, vt, final_g[None], S, final_norm)

    for l in range(N_A_LAYERS):
        x = _lru_call(x, mod_all[l], norm_mix_g[l][None], lru_w_in[l].astype(BF16), lru_conv_w[l],
                      lru_conv_b[l][None], lru_wa[l].astype(BF16), lru_ba[l][None], lru_wx[l].astype(BF16),
                      lru_bx[l][None], lru_lambda[l][None], lru_w_out[l].astype(BF16))
        x = peer_layer(x.reshape(T, D), l, False).reshape(B, S, D)

    x2 = x.reshape(T, D)

    zc = jnp.zeros((D, QK_NOPE), F32)
    zt = jnp.zeros((D, HEAD_PAD - QK_NOPE - QK_ROPE), F32)
    w1 = jnp.concatenate([mla_w_dkv, zc, mla_w_kr, zt, zc, _swap_halves(mla_w_kr), zt], axis=-1).astype(BF16)
    wuk = jnp.concatenate([mla_w_uk.reshape(KV_LORA, MLA_HEADS, QK_NOPE),
                           jnp.zeros((KV_LORA, MLA_HEADS, HEAD_PAD - QK_NOPE), F32)], axis=-1)
    wuk = wuk.reshape(KV_LORA, MLA_HEADS * HEAD_PAD).astype(BF16)
    k_all, v_all = _kv_call(x2, mod_kv, kv_norm_g[None], w1, mla_kv_latent_g[None], wuk,
                            mla_w_uv.astype(BF16), ca, sb, S)

    for j in range(DEPTH - N_A_LAYERS):
        l = N_A_LAYERS + j
        wq = mla_w_uq[j].reshape(Q_LORA, MLA_HEADS, QK_NOPE + QK_ROPE)
        w_nope, w_rope = wq[..., :QK_NOPE], wq[..., QK_NOPE:]
        wa = _head_pad_cols(w_nope, w_rope, Q_LORA).astype(BF16)
        wb = _head_pad_cols(jnp.zeros_like(w_nope), _swap_halves(w_rope), Q_LORA).astype(BF16)
        q = _q_call(x2, mod_all[l], norm_mix_g[l][None], mla_w_dq[j].astype(BF16), mla_q_latent_g[j][None],
                    wa, wb, ca, sb, S)
        o = _attn_call(nfull, nhi, q, k_all, v_all, pos_col, pos_row, B, S)
        x2 = _oproj_call(x2, o, mod_all[l], mla_w_o[j].astype(BF16), S)
        x2 = peer_layer(x2, l, l == DEPTH - 1)

    return x2.reshape(B, S, D)
```
